```python
import jax, jax.numpy as jnp
from jax import lax
import numpy as np

D_MODEL = 2048
BATCH = 8
SEQ = 2048
DEPTH = 1

GRID_W = 64
CTX_LEN = 256
D_MIX = D_MODEL
D_MLSTM = D_MIX // 2
D_CMLP = D_MIX - D_MLSTM
MLSTM_HEADS = 4
MLSTM_HD = D_MLSTM // MLSTM_HEADS
MLSTM_CHUNK = 128
CONV_K = 3
CMLP_GROUPS = 4
CMLP_GD = D_CMLP // CMLP_GROUPS
CMLP_CHUNK = 128
N_GROUPS = 4
EXPERTS_PER_GROUP = 8
N_EXPERTS = N_GROUPS * EXPERTS_PER_GROUP
TOP_K_INNER = 2
D_EXPERT = D_MODEL // 2
MOE_BLOCK = 128
N_MOD = 6
N_GATE_COLS = 4 * MLSTM_HEADS
D_IN = 4 * D_MLSTM + N_GATE_COLS + 2 * D_CMLP
DEEPNORM_ALPHA = (2.0 * DEPTH) ** 0.25
DEEPNORM_BETA = (8.0 * DEPTH) ** -0.25
LN_EPS = 1e-6

kernel_name = 'hybrid_mlstm_chunkmlp_hmoe_dit_block'

F32 = jnp.float32


def _layer_norm(x, w, b):
    xf = x.astype(F32)
    mu = xf.mean(-1, keepdims=True)
    var = jnp.mean(jnp.square(xf - mu), -1, keepdims=True)
    y = (xf - mu) * lax.rsqrt(var + LN_EPS)
    return (y * w.astype(F32) + b.astype(F32)).astype(x.dtype)


def _grid_pos_embed(rows):
    r = jnp.repeat(jnp.arange(rows, dtype=F32), GRID_W)
    col = jnp.tile(jnp.arange(GRID_W, dtype=F32), rows)
    quarter = D_MODEL // 4
    omega = 1.0 / (10000.0 ** (jnp.arange(quarter, dtype=F32) / quarter))
    ar = r[:, None] * omega
    ac = col[:, None] * omega
    return jnp.concatenate([jnp.sin(ar), jnp.cos(ar), jnp.sin(ac), jnp.cos(ac)], -1)


def _dwconv_centred(x, w, b):
    ch = x.shape[-1]
    y = lax.conv_general_dilated(x, w[:, None, :].astype(x.dtype), window_strides=(1,),
                                 padding=[(CONV_K // 2, CONV_K // 2)],
                                 dimension_numbers=('NWC', 'WIO', 'NWC'), feature_group_count=ch)
    return y + b.astype(x.dtype)


def _heads(t):
    bsz, n, _ = t.shape
    return t.reshape(bsz, n, MLSTM_HEADS, MLSTM_HD).transpose(0, 2, 1, 3)


def _stream_proj(h, w_in, conv_w, conv_b, gate_bias):
    bsz, n, _ = h.shape
    p = jnp.einsum('bld,de->ble', h, w_in)
    dm = D_MLSTM
    qk = jax.nn.silu(_dwconv_centred(p[..., :2 * dm], conv_w, conv_b))
    q = _heads(qk[..., :dm])
    k = _heads(qk[..., dm:]) * (MLSTM_HD ** -0.5)
    v = _heads(p[..., 2 * dm:3 * dm])
    o = p[..., 3 * dm:4 * dm]
    g = p[..., 4 * dm:4 * dm + N_GATE_COLS].astype(F32).reshape(bsz, n, 4, MLSTM_HEADS)
    g = (g + gate_bias.astype(F32)).transpose(2, 0, 3, 1)
    gates = (g[0], jax.nn.log_sigmoid(g[1]), g[2], jax.nn.log_sigmoid(g[3]))
    uv = jax.nn.gelu(p[..., 4 * dm + N_GATE_COLS:])
    return q, k, v, o, gates, uv[..., :D_CMLP], uv[..., D_CMLP:]


def _zero_state(bsz):
    return (jnp.zeros((bsz, MLSTM_HEADS, MLSTM_HD, MLSTM_HD), F32),
            jnp.zeros((bsz, MLSTM_HEADS, MLSTM_HD), F32),
            jnp.zeros((bsz, MLSTM_HEADS), F32))


def _chunk_states(k, v, log_i, log_f, state):
    bsz, nh, n, dh = k.shape
    nc = n // MLSTM_CHUNK
    kc = k.astype(F32).reshape(bsz, nh, nc, MLSTM_CHUNK, dh)
    vc = v.astype(F32).reshape(bsz, nh, nc, MLSTM_CHUNK, dh)
    li = log_i.reshape(bsz, nh, nc, MLSTM_CHUNK)
    b = lax.cumsum(log_f.reshape(bsz, nh, nc, MLSTM_CHUNK), axis=3)
    b_last = b[..., -1]
    a = b_last[..., None] - b + li
    m_loc = a.max(-1)
    w = jnp.exp(a - m_loc[..., None])
    c_loc = jnp.einsum('bhcsd,bhcse->bhcde', vc * w[..., None], kc)
    n_loc = jnp.einsum('bhcs,bhcse->bhce', w, kc)

    def step(carry, inp):
        cm, nv, m = carry
        cl, nl, ml, bl = inp
        m_new = jnp.maximum(bl + m, ml)
        s_old = jnp.exp(bl + m - m_new)
        s_loc = jnp.exp(ml - m_new)
        c_new = s_old[..., None, None] * cm + s_loc[..., None, None] * cl
        n_new = s_old[..., None] * nv + s_loc[..., None] * nl
        return (c_new, n_new, m_new), (cm, nv, m)

    xs = (jnp.moveaxis(c_loc, 2, 0), jnp.moveaxis(n_loc, 2, 0),
          jnp.moveaxis(m_loc, 2, 0), jnp.moveaxis(b_last, 2, 0))
    final, (c_in, n_in, m_in) = lax.scan(step, state, xs)
    c_in = jnp.moveaxis(c_in, 0, 2)
    n_in = jnp.moveaxis(n_in, 0, 2)
    m_in = jnp.moveaxis(m_in, 0, 2)
    return kc, vc, li, b, c_in, n_in, m_in, final


def _mlstm_core(q, k, v, log_i, log_f, state):
    kc, vc, li, b, c_in, n_in, m_in, final = _chunk_states(k, v, log_i, log_f, state)
    bsz, nh, nc, lc, dh = kc.shape
    qc = q.astype(F32).reshape(bsz, nh, nc, lc, dh)
    order = jnp.tril(jnp.ones((lc, lc), bool))
    dmat = jnp.where(order, b[..., :, None] - b[..., None, :] + li[..., None, :], -jnp.inf)
    inter = b + m_in[..., None]
    m_t = jnp.maximum(inter, dmat.max(-1))
    s_inter = jnp.exp(inter - m_t)
    pmat = jnp.exp(dmat - m_t[..., None]) * jnp.einsum('bhctd,bhcsd->bhcts', qc, kc)
    num = (jnp.einsum('bhcts,bhcsd->bhctd', pmat, vc)
           + s_inter[..., None] * jnp.einsum('bhcde,bhcte->bhctd', c_in, qc))
    den = pmat.sum(-1) + s_inter * jnp.einsum('bhce,bhcte->bhct', n_in, qc)
    h = num / jnp.maximum(jnp.abs(den), jnp.exp(-m_t))[..., None]
    return h.reshape(bsz, nh, nc * lc, dh), final


def _flip(t):
    return jnp.flip(t, axis=2)


def _mlstm_bidir(q, k, v, gates, init_f, init_b):
    li_f, lf_f, li_b, lf_b = gates
    h_f, fin_f = _mlstm_core(q, k, v, li_f, lf_f, init_f)
    h_b, fin_b = _mlstm_core(_flip(q), _flip(k), _flip(v), _flip(li_b), _flip(lf_b), init_b)
    return h_f + _flip(h_b), fin_f, fin_b


def _mlstm_bidir_final(k, v, gates, init):
    li_f, lf_f, li_b, lf_b = gates
    fin_f = _chunk_states(k, v, li_f, lf_f, init)[-1]
    fin_b = _chunk_states(_flip(k), _flip(v), _flip(li_b), _flip(lf_b), init)[-1]
    return fin_f, fin_b


def _chunk_mlp(u, vg, norm_w, w_s, b_s):
    bsz, n, _ = u.shape
    shp = (bsz, n // CMLP_CHUNK, CMLP_CHUNK, CMLP_GROUPS, CMLP_GD)
    vf = vg.astype(F32).reshape(shp)
    mu = vf.mean(-1, keepdims=True)
    var = jnp.mean(jnp.square(vf - mu), -1, keepdims=True)
    vn = ((vf - mu) * lax.rsqrt(var + LN_EPS) * norm_w.astype(F32).reshape(CMLP_GROUPS, CMLP_GD)).astype(u.dtype)
    s = jnp.einsum('gpq,bcqgd->bcpgd', w_s, vn) + b_s.T[:, :, None]
    return (u.reshape(shp) * s).reshape(bsz, n, D_CMLP)


def _mixer_out(h_m, o, u, vg, mlstm_norm_w, cmlp_norm_w, w_s, b_s, w_out):
    bsz, nh, n, dh = h_m.shape
    mu = h_m.mean(-1, keepdims=True)
    var = jnp.mean(jnp.square(h_m - mu), -1, keepdims=True)
    hn = (h_m - mu) * lax.rsqrt(var + LN_EPS) * mlstm_norm_w.astype(F32).reshape(nh, 1, dh)
    hn = hn.transpose(0, 2, 1, 3).reshape(bsz, n, D_MLSTM).astype(o.dtype)
    y_m = hn * jax.nn.sigmoid(o)
    y_c = _chunk_mlp(u, vg, cmlp_norm_w, w_s, b_s)
    return jnp.einsum('ble,ed->bld', jnp.concatenate([y_m, y_c], -1), w_out)


def _hier_moe(h, r1_w, r1_b, r2_w, r2_b, w_gate, w_up, w_down):
    lead = h.shape[:-1]
    t = h.reshape(-1, D_MODEL)
    n_tok = t.shape[0]
    logits1 = (t @ r1_w + r1_b).astype(F32)
    grp = jnp.argmax(logits1, -1)
    p_grp = jnp.take_along_axis(jax.nn.softmax(logits1, -1), grp[:, None], -1)[:, 0]
    logits2 = (t @ r2_w + r2_b).astype(F32).reshape(n_tok, N_GROUPS, EXPERTS_PER_GROUP)
    l2 = jnp.take_along_axis(logits2, grp[:, None, None], 1)[:, 0]
    top_v, top_i = lax.top_k(l2, TOP_K_INNER)
    gate = p_grp[:, None] * jax.nn.softmax(top_v, -1)
    expert = grp[:, None] * EXPERTS_PER_GROUP + top_i
    n_asg = n_tok * TOP_K_INNER
    e_flat = expert.reshape(-1).astype(jnp.int32)
    tok = jnp.repeat(jnp.arange(n_tok, dtype=jnp.int32), TOP_K_INNER)
    order = jnp.argsort(e_flat)
    e_s, tok_s, g_s = e_flat[order], tok[order], gate.reshape(-1)[order]
    counts = jnp.zeros((N_EXPERTS,), jnp.int32).at[e_flat].add(1)
    start = jnp.cumsum(counts) - counts
    padded = (counts + MOE_BLOCK - 1) // MOE_BLOCK * MOE_BLOCK
    pad_end = jnp.cumsum(padded)
    pad_start = pad_end - padded
    pos = pad_start[e_s] + (jnp.arange(n_asg, dtype=jnp.int32) - start[e_s])
    n_blk = (n_asg + MOE_BLOCK - 1) // MOE_BLOCK + N_EXPERTS
    xp = jnp.zeros((n_blk * MOE_BLOCK, D_MODEL), t.dtype).at[pos].set(t[tok_s])
    blk_e = jnp.minimum(jnp.searchsorted(pad_end, jnp.arange(n_blk, dtype=jnp.int32) * MOE_BLOCK, side='right'),
                        N_EXPERTS - 1)

    def expert_block(args):
        xb, e = args
        return (jax.nn.silu(xb @ w_gate[e]) * (xb @ w_up[e])) @ w_down[e]

    yp = lax.map(expert_block, (xp.reshape(n_blk, MOE_BLOCK, D_MODEL), blk_e))
    ys = yp.reshape(n_blk * MOE_BLOCK, D_MODEL)[pos] * g_s[:, None].astype(t.dtype)
    out = jax.ops.segment_sum(ys, tok_s, num_segments=n_tok)
    return out.reshape(*lead, D_MODEL)


def setup_inputs(seed: int = 0) -> dict:
    key = jax.random.key(seed)
    ks = jax.random.split(key, 32)
    dm = D_MODEL

    def nrm(k, shape, s):
        return jax.random.normal(k, shape, F32) * s

    f_bias = jnp.linspace(3.0, 6.0, MLSTM_HEADS, dtype=F32)
    i_bias = jnp.zeros((MLSTM_HEADS,), F32)
    gate_base = jnp.stack([i_bias, f_bias, i_bias, f_bias])
    return {
        'x': nrm(ks[0], (BATCH, SEQ, dm), 1.0),
        'c': nrm(ks[1], (BATCH, dm), 1.0),
        'ctx': nrm(ks[2], (BATCH, CTX_LEN, dm), 1.0),
        'c_ctx': nrm(ks[3], (dm,), 1.0),
        'w_mod': nrm(ks[4], (DEPTH, dm, N_MOD * dm), 0.5 * dm ** -0.5),
        'b_mod': nrm(ks[5], (DEPTH, N_MOD * dm), 0.01),
        'w_in': nrm(ks[6], (DEPTH, dm, D_IN), dm ** -0.5),
        'conv_w': nrm(ks[7], (DEPTH, CONV_K, 2 * D_MLSTM), CONV_K ** -0.5),
        'conv_b': nrm(ks[8], (DEPTH, 2 * D_MLSTM), 0.01),
        'gate_bias': gate_base[None] + nrm(ks[9], (DEPTH, 4, MLSTM_HEADS), 0.1),
        'mlstm_norm_w': 1.0 + nrm(ks[10], (DEPTH, D_MLSTM), 0.02),
        'cmlp_norm_w': 1.0 + nrm(ks[11], (DEPTH, D_CMLP), 0.02),
        'w_s': nrm(ks[12], (DEPTH, CMLP_GROUPS, CMLP_CHUNK, CMLP_CHUNK), CMLP_CHUNK ** -0.5),
        'b_s': 1.0 + nrm(ks[13], (DEPTH, CMLP_GROUPS, CMLP_CHUNK), 0.02),
        'w_out': nrm(ks[14], (DEPTH, D_MIX, dm), DEEPNORM_BETA * D_MIX ** -0.5),
        'ln1_w': 1.0 + nrm(ks[15], (DEPTH, dm), 0.02),
        'ln1_b': nrm(ks[16], (DEPTH, dm), 0.02),
        'router1_w': nrm(ks[17], (DEPTH, dm, N_GROUPS), dm ** -0.5),
        'router1_b': nrm(ks[18], (DEPTH, N_GROUPS), 0.01),
        'router2_w': nrm(ks[19], (DEPTH, dm, N_EXPERTS), dm ** -0.5),
        'router2_b': nrm(ks[20], (DEPTH, N_EXPERTS), 0.01),
        'w_gate': nrm(ks[21], (DEPTH, N_EXPERTS, dm, D_EXPERT), dm ** -0.5),
        'w_up': nrm(ks[22], (DEPTH, N_EXPERTS, dm, D_EXPERT), dm ** -0.5),
        'w_down': nrm(ks[23], (DEPTH, N_EXPERTS, D_EXPERT, dm), DEEPNORM_BETA * D_EXPERT ** -0.5),
        'ln2_w': 1.0 + nrm(ks[24], (DEPTH, dm), 0.02),
        'ln2_b': nrm(ks[25], (DEPTH, dm), 0.02),
    }


def reference(x, c, ctx, c_ctx, w_mod, b_mod, w_in, conv_w, conv_b, gate_bias,
              mlstm_norm_w, cmlp_norm_w, w_s, b_s, w_out, ln1_w, ln1_b,
              router1_w, router1_b, router2_w, router2_b, w_gate, w_up, w_down,
              ln2_w, ln2_b):
    bsz, n_lat, dm = x.shape
    rows = n_lat // GRID_W
    x = x + _grid_pos_embed(rows).astype(x.dtype)[None]
    for layer in range(DEPTH):
        last = layer == DEPTH - 1
        mod_x = (jax.nn.silu(c) @ w_mod[layer] + b_mod[layer]).reshape(bsz, N_MOD, 1, dm)
        mod_c = (jax.nn.silu(c_ctx) @ w_mod[layer] + b_mod[layer]).reshape(N_MOD, 1, dm)
        hx = x * (1 + mod_x[:, 1]) + mod_x[:, 0]
        hc = ctx * (1 + mod_c[1]) + mod_c[0]
        qx, kx, vx, ox, gx, ux, vgx = _stream_proj(hx, w_in[layer], conv_w[layer], conv_b[layer], gate_bias[layer])
        qc, kc, vc, oc, gc, uc, vgc = _stream_proj(hc, w_in[layer], conv_w[layer], conv_b[layer], gate_bias[layer])
        zero = _zero_state(bsz)
        if last:
            fin_f, fin_b = _mlstm_bidir_final(kc, vc, gc, zero)
        else:
            hm_c, fin_f, fin_b = _mlstm_bidir(qc, kc, vc, gc, zero, zero)
        hm_x, _, _ = _mlstm_bidir(qx, kx, vx, gx, fin_f, fin_b)
        y = _mixer_out(hm_x, ox, ux, vgx, mlstm_norm_w[layer], cmlp_norm_w[layer], w_s[layer], b_s[layer], w_out[layer])
        x = _layer_norm(DEEPNORM_ALPHA * x + mod_x[:, 2] * y, ln1_w[layer], ln1_b[layer])
        y = _hier_moe(x * (1 + mod_x[:, 4]) + mod_x[:, 3], router1_w[layer], router1_b[layer],
                      router2_w[layer], router2_b[layer], w_gate[layer], w_up[layer], w_down[layer])
        x = _layer_norm(DEEPNORM_ALPHA * x + mod_x[:, 5] * y, ln2_w[layer], ln2_b[layer])
        if not last:
            yc = _mixer_out(hm_c, oc, uc, vgc, mlstm_norm_w[layer], cmlp_norm_w[layer], w_s[layer], b_s[layer], w_out[layer])
            ctx = _layer_norm(DEEPNORM_ALPHA * ctx + mod_c[2] * yc, ln1_w[layer], ln1_b[layer])
            yc = _hier_moe(ctx * (1 + mod_c[4]) + mod_c[3], router1_w[layer], router1_b[layer],
                           router2_w[layer], router2_b[layer], w_gate[layer], w_up[layer], w_down[layer])
            ctx = _layer_norm(DEEPNORM_ALPHA * ctx + mod_c[5] * yc, ln2_w[layer], ln2_b[layer])
    return x
```

```python
import functools

import jax
import jax.numpy as jnp
from jax import lax
from jax.experimental import pallas as pl
from jax.experimental.pallas import tpu as pltpu

F32 = jnp.float32
BF16 = jnp.bfloat16

D_MODEL = 2048
GRID_W = 64
D_MLSTM = 1024
D_CMLP = 1024
HEADS = 4
HD = 256
CMLP_GROUPS = 4
CMLP_GD = 256
CMLP_CHUNK = 128
N_GROUPS = 4
EXPERTS_PER_GROUP = 8
N_EXPERTS = 32
D_EXPERT = 1024
N_MOD = 6
N_GATE_COLS = 16
DEEPNORM_ALPHA = 2.0 ** 0.25
LN_EPS = 1e-6

LANE = 128
MCHUNK = 256
MOE_ROWS = 256
VMEM_LIMIT = 56 * 1024 * 1024


def _cparams(sem):
    return pltpu.CompilerParams(dimension_semantics=sem, vmem_limit_bytes=VMEM_LIMIT)


def _resident(shape):
    nd = len(shape)
    return pl.BlockSpec(shape, lambda *_: (0,) * nd, pipeline_mode=pl.Buffered(1))


def _sigmoid(x):
    return 1.0 / (1.0 + jnp.exp(-x))


def _silu(x):
    return x * _sigmoid(x)


def _gelu_tanh(x):
    c = 0.7978845608028654
    return 0.5 * x * (1.0 + jnp.tanh(c * (x + 0.044715 * (x * x * x))))


def _layer_norm_rows(z, w, b):
    mu = jnp.mean(z, axis=-1, keepdims=True)
    zc = z - mu
    var = jnp.mean(zc * zc, axis=-1, keepdims=True)
    return zc * lax.rsqrt(var + LN_EPS) * w + b


def _mod_kernel(c_ref, w_ref, b_ref, o_ref):
    s = _silu(c_ref[...]).astype(BF16)
    o_ref[...] = jnp.dot(s, w_ref[...].astype(BF16), preferred_element_type=F32) + b_ref[...]


def _modulation(cc, w_mod, b_mod):
    rows, dm = cc.shape
    n = w_mod.shape[1]
    tn = 1024
    return pl.pallas_call(
        _mod_kernel,
        grid=(n // tn,),
        in_specs=[pl.BlockSpec((rows, dm), lambda j: (0, 0)),
                  pl.BlockSpec((dm, tn), lambda j: (0, j)),
                  pl.BlockSpec((1, tn), lambda j: (0, j))],
        out_specs=pl.BlockSpec((rows, tn), lambda j: (0, j)),
        out_shape=jax.ShapeDtypeStruct((rows, n), F32),
        compiler_params=_cparams(("arbitrary",)),
        name="modulation",
    )(cc, w_mod, b_mod.reshape(1, n))


def _proj_kernel(*refs, n_w, has_pe, gelu_flags, tn):
    x_ref = refs[0]
    k = 1
    pe_ref = None
    if has_pe:
        pe_ref = refs[k]
        k += 1
    m_ref = refs[k]
    k += 1
    w_refs = refs[k:k + n_w]
    o_refs = refs[k + n_w:k + 2 * n_w]
    hx_ref = refs[k + 2 * n_w]
    x = x_ref[...]
    if has_pe:
        x = x + pe_ref[...]
    m = m_ref[0]
    hx_ref[...] = (x * (1.0 + m[1:2, :]) + m[0:1, :]).astype(BF16)
    for w_ref, o_ref, use_gelu in zip(w_refs, o_refs, gelu_flags):
        n = w_ref.shape[1]
        step = min(tn, n)
        for j in range(0, n, step):
            acc = jnp.dot(hx_ref[...], w_ref[:, j:j + step], preferred_element_type=F32)
            if use_gelu:
                acc = _gelu_tanh(acc)
            o_ref[:, j:j + step] = acc.astype(o_ref.dtype)


def _projection(x2d, pe, mods3, mod_row_of_block, weights, out_dtypes, gelu_flags, bm):
    rows, dm = x2d.shape
    has_pe = pe is not None
    n_w = len(weights)
    in_specs = [pl.BlockSpec((bm, dm), lambda i: (i, 0))]
    args = [x2d]
    if has_pe:
        pe_blocks = pe.shape[0] // bm
        in_specs.append(pl.BlockSpec((bm, dm), lambda i: (i % pe_blocks, 0)))
        args.append(pe)
    in_specs.append(pl.BlockSpec((1, N_MOD, dm), lambda i: (mod_row_of_block(i), 0, 0)))
    args.append(mods3)
    for w in weights:
        in_specs.append(_resident(w.shape))
        args.append(w)
    out_specs = [pl.BlockSpec((bm, w.shape[1]), lambda i: (i, 0)) for w in weights]
    out_shape = [jax.ShapeDtypeStruct((rows, w.shape[1]), dt) for w, dt in zip(weights, out_dtypes)]
    kern = functools.partial(_proj_kernel, n_w=n_w, has_pe=has_pe, gelu_flags=tuple(gelu_flags), tn=1024)
    return pl.pallas_call(
        kern,
        grid=(rows // bm,),
        in_specs=in_specs,
        out_specs=out_specs,
        out_shape=out_shape,
        scratch_shapes=[pltpu.VMEM((bm, dm), BF16)],
        compiler_params=_cparams(("parallel",)),
        name="projection",
    )(*args)


def _gate_stats(g_x, g_c, gate_bias, bsz, seq, ctx_len):
    nc = seq // MCHUNK

    def split(g, n):
        g = g[..., :N_GATE_COLS].reshape(bsz, n, 4, HEADS) + gate_bias.astype(F32)
        g = g.transpose(2, 0, 3, 1)
        return g[0], jax.nn.log_sigmoid(g[1]), g[2], jax.nn.log_sigmoid(g[3])

    def chunk_stats(li, lf, reverse):
        if reverse:
            b = jnp.flip(jnp.cumsum(jnp.flip(lf, -1), -1), -1)
        else:
            b = jnp.cumsum(lf, -1)
        btot = jnp.sum(lf, -1)
        a = btot[..., None] - b + li
        m_loc = a.max(-1)
        r = li - b
        cm = lax.cummax(r, axis=r.ndim - 1, reverse=reverse)
        return b, btot, a, m_loc, r, cm

    gx = split(g_x.reshape(bsz, seq, -1), seq)
    gc = split(g_c.reshape(bsz, ctx_len, -1), ctx_len)
    rows_r, rows_so = [], []
    cols = {q: [] for q in ("w", "M", "si", "en")}
    ctx_w = []
    for d in range(2):
        li_c, lf_c = gc[2 * d][:, :, None, :], gc[2 * d + 1][:, :, None, :]
        _, btot_c, a_c, mloc_c, _, _ = chunk_stats(li_c, lf_c, bool(d))
        m_in = jnp.maximum(btot_c[:, :, 0], mloc_c[:, :, 0])
        ctx_w.append(jnp.exp(a_c[:, :, 0] - m_in[..., None]))
        li = gx[2 * d].reshape(bsz, HEADS, nc, MCHUNK)
        lf = gx[2 * d + 1].reshape(bsz, HEADS, nc, MCHUNK)
        b, btot, a, m_loc, r, cm = chunk_stats(li, lf, bool(d))
        per_c = {}
        order = range(nc) if d == 0 else range(nc - 1, -1, -1)
        for c in order:
            m_new = jnp.maximum(btot[:, :, c] + m_in, m_loc[:, :, c])
            s_old = jnp.exp(btot[:, :, c] + m_in - m_new)
            w = jnp.exp(a[:, :, c] - m_new[..., None])
            big_m = jnp.maximum(m_in[..., None], cm[:, :, c])
            s_int = jnp.exp(m_in[..., None] - big_m)
            e_neg = jnp.exp(-(b[:, :, c] + big_m))
            per_c[c] = (r[:, :, c], jnp.broadcast_to(s_old[..., None], w.shape), w, big_m, s_int, e_neg)
            m_in = m_new
        for c in range(nc):
            rr, so, w, big_m, s_int, e_neg = per_c[c]
            rows_r.append(rr)
            rows_so.append(so)
            cols["w"].append(w)
            cols["M"].append(big_m)
            cols["si"].append(s_int)
            cols["en"].append(e_neg)
    rowq = jnp.stack(rows_r + rows_so, axis=2)
    col_list = cols["w"] + cols["M"] + cols["si"] + cols["en"] + ctx_w
    colq = jnp.stack(col_list, axis=-1)
    colq = jnp.pad(colq, ((0, 0), (0, 0), (0, 0), (0, LANE - colq.shape[-1])))
    return rowq, colq


def _mlstm_kernel(q_ref, k_ref, v_ref, o_ref, kc_ref, vc_ref, cwq_ref, cbq_ref, cwk_ref, cbk_ref,
                  row_ref, col_ref, nw_ref, y_ref, q_s, k_s, kc_s, ct_s, n_s, *, nc):
    lc = MCHUNK

    def conv_silu(x, w, b):
        n = x.shape[0]
        rid = lax.broadcasted_iota(jnp.int32, x.shape, 0)
        xm = jnp.where(rid == 0, 0.0, pltpu.roll(x, 1, 0))
        xp = jnp.where(rid == n - 1, 0.0, pltpu.roll(x, n - 1, 0))
        return _silu(xm * w[0:1, :] + x * w[1:2, :] + xp * w[2:3, :] + b)

    k_scale = HD ** -0.5
    q_s[...] = conv_silu(q_ref[...].astype(F32), cwq_ref[...], cbq_ref[...]).astype(BF16)
    k_s[...] = (conv_silu(k_ref[...].astype(F32), cwk_ref[...], cbk_ref[...]) * k_scale).astype(BF16)
    kc_s[...] = (conv_silu(kc_ref[...].astype(F32), cwk_ref[...], cbk_ref[...]) * k_scale).astype(BF16)

    def col(j):
        return col_ref[0, 0, :, j:j + 1]

    def local_state(kk, vv, wcol):
        vw = (vv.astype(F32) * wcol).astype(BF16)
        ct = jnp.dot(kk.T, vw, preferred_element_type=F32)
        nn = jnp.sum(kk.astype(F32) * wcol, axis=0, keepdims=True)
        return ct, nn

    for d in range(2):
        ct, nn = local_state(kc_s[...], vc_ref[...], col(8 * nc + d))
        order = list(range(nc)) if d == 0 else list(range(nc - 1, -1, -1))
        for pos, c in enumerate(order):
            idx = d * nc + c
            ct_s[idx] = ct.astype(BF16)
            n_s[idx] = nn
            if pos == nc - 1:
                break
            sl = pl.ds(c * lc, lc)
            ctl, nl = local_state(k_s[sl, :], v_ref[sl, :], col(idx))
            s_old = row_ref[0, 0, 2 * nc + idx:2 * nc + idx + 1, :]
            ct = s_old * ct + ctl
            nn = s_old * nn + nl

    tid = lax.broadcasted_iota(jnp.int32, (lc, lc), 0)
    sid = lax.broadcasted_iota(jnp.int32, (lc, lc), 1)
    masks = (sid <= tid, sid >= tid)
    for c in range(nc):
        sl = pl.ds(c * lc, lc)
        q = q_s[sl, :]
        kk = k_s[sl, :]
        v = v_ref[sl, :]
        qf = q.astype(F32)
        s = lax.dot_general(q, kk, (((1,), (1,)), ((), ())), preferred_element_type=F32)
        h = None
        for d in range(2):
            idx = d * nc + c
            r = row_ref[0, 0, idx:idx + 1, :]
            big_m = col(2 * nc + idx)
            s_int = col(4 * nc + idx)
            e_neg = col(6 * nc + idx)
            p = jnp.where(masks[d], jnp.exp(r - big_m), 0.0) * s
            den = (jnp.sum(p, axis=-1, keepdims=True)
                   + s_int * jnp.sum(qf * n_s[idx], axis=-1, keepdims=True))
            num = (jnp.dot(p.astype(BF16), v, preferred_element_type=F32)
                   + s_int * jnp.dot(q, ct_s[idx], preferred_element_type=F32))
            hd = num / jnp.maximum(jnp.abs(den), e_neg)
            h = hd if h is None else h + hd
        mu = jnp.mean(h, axis=-1, keepdims=True)
        hc = h - mu
        var = jnp.mean(hc * hc, axis=-1, keepdims=True)
        hn = hc * lax.rsqrt(var + LN_EPS) * nw_ref[...]
        y_ref[sl, :] = (hn * _sigmoid(o_ref[sl, :].astype(F32))).astype(BF16)


def _mlstm(qkvo, kv_ctx, conv_w, conv_b, rowq, colq, norm_w, bsz, seq, ctx_len):
    nc = seq // MCHUNK
    hq = D_MLSTM // HD
    kern = functools.partial(_mlstm_kernel, nc=nc)
    seq_blk = lambda off: pl.BlockSpec((seq, HD), lambda b, h: (b, off + h))
    ctx_blk = lambda off: pl.BlockSpec((ctx_len, HD), lambda b, h: (b, off + h))
    return pl.pallas_call(
        kern,
        grid=(bsz, HEADS),
        in_specs=[seq_blk(0), seq_blk(hq), seq_blk(2 * hq), seq_blk(3 * hq),
                  ctx_blk(0), ctx_blk(hq),
                  pl.BlockSpec((3, HD), lambda b, h: (0, h)),
                  pl.BlockSpec((1, HD), lambda b, h: (0, h)),
                  pl.BlockSpec((3, HD), lambda b, h: (0, hq + h)),
                  pl.BlockSpec((1, HD), lambda b, h: (0, hq + h)),
                  pl.BlockSpec((1, 1, 4 * nc, MCHUNK), lambda b, h: (b, h, 0, 0)),
                  pl.BlockSpec((1, 1, MCHUNK, LANE), lambda b, h: (b, h, 0, 0)),
                  pl.BlockSpec((1, HD), lambda b, h: (0, h))],
        out_specs=pl.BlockSpec((seq, HD), lambda b, h: (b, h)),
        out_shape=jax.ShapeDtypeStruct((bsz * seq, D_MLSTM), BF16),
        scratch_shapes=[pltpu.VMEM((seq, HD), BF16), pltpu.VMEM((seq, HD), BF16),
                        pltpu.VMEM((ctx_len, HD), BF16),
                        pltpu.VMEM((2 * nc, HD, HD), BF16), pltpu.VMEM((2 * nc, 1, HD), F32)],
        compiler_params=_cparams(("parallel", "parallel")),
        name="mlstm",
    )(qkvo, qkvo, qkvo, qkvo, kv_ctx, kv_ctx, conv_w, conv_b, conv_w, conv_b, rowq, colq, norm_w)


def _out_kernel(ym_ref, uv_ref, x_ref, pe_ref, m_ref, cnw_ref, ws_ref, bs_ref, wout_ref,
                l1w_ref, l1b_ref, wr_ref, br_ref, x1_ref, h2_ref, lg_ref, y_s, *, bm):
    m = m_ref[0]
    y_s[:, :D_MLSTM] = ym_ref[...]
    for g in range(CMLP_GROUPS):
        gs = slice(g * CMLP_GD, (g + 1) * CMLP_GD)
        vg = uv_ref[:, D_CMLP + g * CMLP_GD:D_CMLP + (g + 1) * CMLP_GD].astype(F32)
        mu = jnp.mean(vg, axis=-1, keepdims=True)
        vc = vg - mu
        var = jnp.mean(vc * vc, axis=-1, keepdims=True)
        vn = (vc * lax.rsqrt(var + LN_EPS) * cnw_ref[:, gs]).astype(BF16)
        for p in range(bm // CMLP_CHUNK):
            ps = slice(p * CMLP_CHUNK, (p + 1) * CMLP_CHUNK)
            s = jnp.dot(ws_ref[g], vn[ps, :], preferred_element_type=F32) + bs_ref[:, g:g + 1]
            yc = uv_ref[ps, gs].astype(F32) * s
            y_s[ps, D_MLSTM + g * CMLP_GD:D_MLSTM + (g + 1) * CMLP_GD] = yc.astype(BF16)
    y = jnp.dot(y_s[...], wout_ref[...], preferred_element_type=F32)
    z = DEEPNORM_ALPHA * (x_ref[...] + pe_ref[...]) + m[2:3, :] * y
    x1 = _layer_norm_rows(z, l1w_ref[...], l1b_ref[...])
    x1_ref[...] = x1
    h2 = (x1 * (1.0 + m[4:5, :]) + m[3:4, :]).astype(BF16)
    h2_ref[...] = h2
    lg_ref[...] = jnp.dot(h2, wr_ref[...], preferred_element_type=F32) + br_ref[...]


def _mixer_out(ym, uv, x2d, pe, mods3, cnw, ws, bs_t, wout, l1w, l1b, wr, br, seq, bm):
    rows, dm = x2d.shape
    pe_blocks = seq // bm
    kern = functools.partial(_out_kernel, bm=bm)
    row_blk = lambda n: pl.BlockSpec((bm, n), lambda i: (i, 0))
    return pl.pallas_call(
        kern,
        grid=(rows // bm,),
        in_specs=[row_blk(D_MLSTM), row_blk(2 * D_CMLP), row_blk(dm),
                  pl.BlockSpec((bm, dm), lambda i: (i % pe_blocks, 0)),
                  pl.BlockSpec((1, N_MOD, dm), lambda i: (i // pe_blocks, 0, 0)),
                  _resident(cnw.shape), _resident(ws.shape), _resident(bs_t.shape),
                  _resident(wout.shape), _resident(l1w.shape), _resident(l1b.shape),
                  _resident(wr.shape), _resident(br.shape)],
        out_specs=[row_blk(dm), row_blk(dm), row_blk(LANE)],
        out_shape=[jax.ShapeDtypeStruct((rows, dm), F32),
                   jax.ShapeDtypeStruct((rows, dm), BF16),
                   jax.ShapeDtypeStruct((rows, LANE), F32)],
        scratch_shapes=[pltpu.VMEM((bm, D_MLSTM + D_CMLP), BF16)],
        compiler_params=_cparams(("parallel",)),
        name="mixer_out",
    )(ym, uv, x2d, pe, mods3, cnw, ws, bs_t, wout, l1w, l1b, wr, br)


def _moe_kernel(be_ref, nu_ref, x_ref, wg_ref, wu_ref, wd_ref, y_ref):
    del be_ref
    used = pl.program_id(0) < nu_ref[0]

    @pl.when(used)
    def _():
        x = x_ref[...]
        g = jnp.dot(x, wg_ref[0], preferred_element_type=F32)
        u = jnp.dot(x, wu_ref[0], preferred_element_type=F32)
        h = (_silu(g) * u).astype(BF16)
        y_ref[...] = jnp.dot(h, wd_ref[0], preferred_element_type=F32).astype(y_ref.dtype)

    @pl.when(jnp.logical_not(used))
    def _():
        y_ref[...] = jnp.zeros_like(y_ref)


def _experts(xp, blk_e, n_used, wg, wu, wd):
    rows, dm = xp.shape
    n_blk = rows // MOE_ROWS
    grid_spec = pltpu.PrefetchScalarGridSpec(
        num_scalar_prefetch=2,
        grid=(n_blk,),
        in_specs=[pl.BlockSpec((MOE_ROWS, dm), lambda i, be, nu: (i, 0)),
                  pl.BlockSpec((1, dm, D_EXPERT), lambda i, be, nu: (be[i], 0, 0)),
                  pl.BlockSpec((1, dm, D_EXPERT), lambda i, be, nu: (be[i], 0, 0)),
                  pl.BlockSpec((1, D_EXPERT, dm), lambda i, be, nu: (be[i], 0, 0))],
        out_specs=pl.BlockSpec((MOE_ROWS, dm), lambda i, be, nu: (i, 0)),
    )
    return pl.pallas_call(
        _moe_kernel,
        grid_spec=grid_spec,
        out_shape=jax.ShapeDtypeStruct((rows, dm), BF16),
        compiler_params=_cparams(("arbitrary",)),
        name="experts",
    )(blk_e, n_used, xp, wg, wu, wd)


def _final_kernel(x1_ref, y0_ref, y1_ref, g_ref, m_ref, w_ref, b_ref, o_ref):
    m = m_ref[0]
    y = (g_ref[:, 0:1] * y0_ref[...].astype(F32) + g_ref[:, 1:2] * y1_ref[...].astype(F32))
    z = DEEPNORM_ALPHA * x1_ref[...] + m[5:6, :] * y
    o_ref[...] = _layer_norm_rows(z, w_ref[...], b_ref[...])


def _final(x1, y0, y1, gates, mods3, w, b, seq, bm):
    rows, dm = x1.shape
    blocks_per_batch = seq // bm
    row_blk = lambda n: pl.BlockSpec((bm, n), lambda i: (i, 0))
    return pl.pallas_call(
        _final_kernel,
        grid=(rows // bm,),
        in_specs=[row_blk(dm), row_blk(dm), row_blk(dm), row_blk(LANE),
                  pl.BlockSpec((1, N_MOD, dm), lambda i: (i // blocks_per_batch, 0, 0)),
                  _resident(w.shape), _resident(b.shape)],
        out_specs=row_blk(dm),
        out_shape=jax.ShapeDtypeStruct((rows, dm), F32),
        compiler_params=_cparams(("parallel",)),
        name="final_ln",
    )(x1, y0, y1, gates, mods3, w, b)


def _route(logits, n_tok):
    logits1 = logits[:, :N_GROUPS]
    grp = jnp.argmax(logits1, -1)
    p_grp = jnp.take_along_axis(jax.nn.softmax(logits1, -1), grp[:, None], -1)[:, 0]
    logits2 = logits[:, N_GROUPS:N_GROUPS + N_EXPERTS].reshape(n_tok, N_GROUPS, EXPERTS_PER_GROUP)
    l2 = jnp.take_along_axis(logits2, grp[:, None, None], 1)[:, 0]
    top_v, top_i = lax.top_k(l2, 2)
    gate = p_grp[:, None] * jax.nn.softmax(top_v, -1)
    expert = (grp[:, None] * EXPERTS_PER_GROUP + top_i).astype(jnp.int32)
    e_flat = expert.reshape(-1)
    n_asg = e_flat.shape[0]
    onehot = (e_flat[:, None] == jnp.arange(N_EXPERTS, dtype=jnp.int32)[None, :]).astype(jnp.int32)
    csum = jnp.cumsum(onehot, axis=0)
    counts = csum[-1]
    rank = jnp.take_along_axis(csum, e_flat[:, None], 1)[:, 0] - 1
    padded = (counts + MOE_ROWS - 1) // MOE_ROWS * MOE_ROWS
    pad_end = jnp.cumsum(padded)
    pad_start = pad_end - padded
    pos = pad_start[e_flat] + rank
    n_blk = n_asg // MOE_ROWS + N_EXPERTS
    n_used = (pad_end[-1] // MOE_ROWS).astype(jnp.int32)
    blk_start = jnp.arange(n_blk, dtype=jnp.int32) * MOE_ROWS
    blk_e = jnp.minimum(jnp.searchsorted(pad_end, blk_start, side="right"), N_EXPERTS - 1).astype(jnp.int32)
    last_e = blk_e[jnp.maximum(n_used - 1, 0)]
    blk_e = jnp.where(jnp.arange(n_blk) < n_used, blk_e, last_e)
    row_tok = jnp.full((n_blk * MOE_ROWS,), n_tok, jnp.int32).at[pos].set(
        jnp.arange(n_asg, dtype=jnp.int32) // 2)
    return pos.reshape(n_tok, 2), gate, blk_e, n_used.reshape(1), row_tok


def _grid_pos_embed(rows):
    r = jnp.repeat(jnp.arange(rows, dtype=F32), GRID_W)
    col = jnp.tile(jnp.arange(GRID_W, dtype=F32), rows)
    quarter = D_MODEL // 4
    omega = 1.0 / (10000.0 ** (jnp.arange(quarter, dtype=F32) / quarter))
    ar = r[:, None] * omega
    ac = col[:, None] * omega
    return jnp.concatenate([jnp.sin(ar), jnp.cos(ar), jnp.sin(ac), jnp.cos(ac)], -1)


def kernel(x, c, ctx, c_ctx, w_mod, b_mod, w_in, conv_w, conv_b, gate_bias, mlstm_norm_w, cmlp_norm_w,
           w_s, b_s, w_out, ln1_w, ln1_b, router1_w, router1_b, router2_w, router2_b, w_gate, w_up,
           w_down, ln2_w, ln2_b):
    bsz, seq, dm = x.shape
    ctx_len = ctx.shape[1]
    n_tok = bsz * seq
    assert w_mod.shape[0] == 1 and dm == D_MODEL and seq % MCHUNK == 0 and ctx_len == MCHUNK
    pe = _grid_pos_embed(seq // GRID_W).astype(x.dtype)
    x2d = x.reshape(n_tok, dm)
    ctx2d = ctx.reshape(bsz * ctx_len, dm)

    mod_rows = 16
    cc = jnp.concatenate([c, c_ctx[None, :], jnp.zeros((mod_rows - bsz - 1, dm), c.dtype)], 0)
    mods3 = _modulation(cc, w_mod[0], b_mod[0]).reshape(mod_rows, N_MOD, dm)

    dq = D_MLSTM
    wi = w_in[0]
    w_qkvo = wi[:, :4 * dq].astype(BF16)
    w_g = jnp.pad(wi[:, 4 * dq:4 * dq + N_GATE_COLS], ((0, 0), (0, LANE - N_GATE_COLS))).astype(BF16)
    w_uv = wi[:, 4 * dq + N_GATE_COLS:].astype(BF16)
    bm_proj = 256
    blocks_per_seq = seq // bm_proj
    qkvo, g_x, uv = _projection(x2d, pe, mods3, lambda i: i // blocks_per_seq,
                                [w_qkvo, w_g, w_uv], [BF16, F32, BF16], [False, False, True], bm_proj)
    w_kv = wi[:, dq:3 * dq].astype(BF16)
    kv_c, g_c = _projection(ctx2d, None, mods3, lambda i: bsz, [w_kv, w_g], [BF16, F32],
                            [False, False], bm_proj)

    rowq, colq = _gate_stats(g_x, g_c, gate_bias[0], bsz, seq, ctx_len)
    ym = _mlstm(qkvo, kv_c, conv_w[0], conv_b[0].reshape(1, -1), rowq, colq,
                mlstm_norm_w[0].reshape(1, -1), bsz, seq, ctx_len)

    wr = jnp.pad(jnp.concatenate([router1_w[0], router2_w[0]], 1),
                 ((0, 0), (0, LANE - N_GROUPS - N_EXPERTS))).astype(BF16)
    br = jnp.pad(jnp.concatenate([router1_b[0], router2_b[0]], 0),
                 (0, LANE - N_GROUPS - N_EXPERTS)).reshape(1, LANE)
    x1, h2, logits = _mixer_out(ym, uv, x2d, pe, mods3, cmlp_norm_w[0].reshape(1, -1),
                                w_s[0].astype(BF16), b_s[0].T, w_out[0].astype(BF16),
                                ln1_w[0].reshape(1, -1), ln1_b[0].reshape(1, -1), wr, br, seq, 256)

    pos, gate, blk_e, n_used, row_tok = _route(logits, n_tok)
    h2_ext = jnp.concatenate([h2, jnp.zeros((1, dm), h2.dtype)], 0)
    xp = h2_ext[row_tok]
    yp = _experts(xp, blk_e, n_used, w_gate[0].astype(BF16), w_up[0].astype(BF16), w_down[0].astype(BF16))
    y0 = yp[pos[:, 0]]
    y1 = yp[pos[:, 1]]
    gates = jnp.pad(gate.astype(F32), ((0, 0), (0, LANE - 2)))
    out = _final(x1, y0, y1, gates, mods3, ln2_w[0].reshape(1, -1), ln2_b[0].reshape(1, -1), seq, 256)
    return out.reshape(bsz, seq, dm)
```

```python
import functools

import jax
import jax.numpy as jnp
from jax import lax
from jax.experimental import pallas as pl
from jax.experimental.pallas import tpu as pltpu

F32 = jnp.float32
BF16 = jnp.bfloat16

D_MODEL = 2048
GRID_W = 64
D_MLSTM = 1024
D_CMLP = 1024
HEADS = 4
HD = 256
CMLP_GROUPS = 4
CMLP_GD = 256
CMLP_CHUNK = 128
N_GROUPS = 4
EXPERTS_PER_GROUP = 8
N_EXPERTS = 32
D_EXPERT = 1024
N_MOD = 6
N_GATE_COLS = 16
DEEPNORM_ALPHA = 2.0 ** 0.25
LN_EPS = 1e-6

LANE = 128
MCHUNK = 256
MOE_ROWS = 256
MOE_FT = 256
VMEM_LIMIT = 56 * 1024 * 1024


def _cparams(sem):
    return pltpu.CompilerParams(dimension_semantics=sem, vmem_limit_bytes=VMEM_LIMIT)


def _resident(shape):
    nd = len(shape)
    return pl.BlockSpec(shape, lambda *_: (0,) * nd, pipeline_mode=pl.Buffered(1))


def _sigmoid(x):
    return 1.0 / (1.0 + jnp.exp(-x))


def _silu(x):
    return x * _sigmoid(x)


def _log_sigmoid(x):
    return jnp.minimum(x, 0.0) - jnp.log1p(jnp.exp(-jnp.abs(x)))


def _gelu_tanh(x):
    c = 0.7978845608028654
    return 0.5 * x * (1.0 + jnp.tanh(c * (x + 0.044715 * (x * x * x))))


def _layer_norm_rows(z, w, b):
    mu = jnp.mean(z, axis=-1, keepdims=True)
    zc = z - mu
    var = jnp.mean(zc * zc, axis=-1, keepdims=True)
    return zc * lax.rsqrt(var + LN_EPS) * w + b


def _mod_kernel(c_ref, w_ref, b_ref, o_ref):
    s = _silu(c_ref[...]).astype(BF16)
    o_ref[...] = jnp.dot(s, w_ref[...].astype(BF16), preferred_element_type=F32) + b_ref[...]


def _modulation(cc, w_mod, b_mod):
    rows, dm = cc.shape
    n = w_mod.shape[1]
    tn = 1024
    return pl.pallas_call(
        _mod_kernel,
        grid=(n // tn,),
        in_specs=[pl.BlockSpec((rows, dm), lambda j: (0, 0)),
                  pl.BlockSpec((dm, tn), lambda j: (0, j)),
                  pl.BlockSpec((1, tn), lambda j: (0, j))],
        out_specs=pl.BlockSpec((rows, tn), lambda j: (0, j)),
        out_shape=jax.ShapeDtypeStruct((rows, n), F32),
        compiler_params=_cparams(("arbitrary",)),
        name="modulation",
    )(cc, w_mod, b_mod.reshape(1, n))


def _proj_kernel(*refs, n_w, has_pe, gelu_flags, tn):
    x_ref = refs[0]
    k = 1
    pe_ref = None
    if has_pe:
        pe_ref = refs[k]
        k += 1
    m_ref = refs[k]
    k += 1
    w_refs = refs[k:k + n_w]
    o_refs = refs[k + n_w:k + 2 * n_w]
    hx_ref = refs[k + 2 * n_w]
    x = x_ref[...]
    if has_pe:
        x = x + pe_ref[...]
    m = m_ref[0]
    hx_ref[...] = (x * (1.0 + m[1:2, :]) + m[0:1, :]).astype(BF16)
    for w_ref, o_ref, use_gelu in zip(w_refs, o_refs, gelu_flags):
        n = w_ref.shape[1]
        step = min(tn, n)
        for j in range(0, n, step):
            acc = jnp.dot(hx_ref[...], w_ref[:, j:j + step], preferred_element_type=F32)
            if use_gelu:
                acc = _gelu_tanh(acc)
            o_ref[:, j:j + step] = acc.astype(o_ref.dtype)


def _projection(x2d, pe, mods3, mod_row_of_block, weights, out_dtypes, gelu_flags, bm):
    rows, dm = x2d.shape
    has_pe = pe is not None
    n_w = len(weights)
    in_specs = [pl.BlockSpec((bm, dm), lambda i: (i, 0))]
    args = [x2d]
    if has_pe:
        pe_blocks = pe.shape[0] // bm
        in_specs.append(pl.BlockSpec((bm, dm), lambda i: (i % pe_blocks, 0)))
        args.append(pe)
    in_specs.append(pl.BlockSpec((1, N_MOD, dm), lambda i: (mod_row_of_block(i), 0, 0)))
    args.append(mods3)
    for w in weights:
        in_specs.append(_resident(w.shape))
        args.append(w)
    out_specs = [pl.BlockSpec((bm, w.shape[1]), lambda i: (i, 0)) for w in weights]
    out_shape = [jax.ShapeDtypeStruct((rows, w.shape[1]), dt) for w, dt in zip(weights, out_dtypes)]
    kern = functools.partial(_proj_kernel, n_w=n_w, has_pe=has_pe, gelu_flags=tuple(gelu_flags), tn=1024)
    return pl.pallas_call(
        kern,
        grid=(rows // bm,),
        in_specs=in_specs,
        out_specs=out_specs,
        out_shape=out_shape,
        scratch_shapes=[pltpu.VMEM((bm, dm), BF16)],
        compiler_params=_cparams(("parallel",)),
        name="projection",
    )(*args)


def _gate_kernel(lic_ref, lfc_ref, lir_ref, lfr_ref, row_ref, col_ref, *, nc):
    li = lic_ref[0, 0]
    lf = _log_sigmoid(lfc_ref[0, 0])
    length = li.shape[0]
    tid = lax.broadcasted_iota(jnp.int32, li.shape, 0)
    lane = lax.broadcasted_iota(jnp.int32, li.shape, 1)
    lane1 = lane[0:1, :]
    fwd = (lane < nc) | (lane == 2 * nc)

    def scan_sublanes(x, op, fill):
        p = x
        s = x
        k = 1
        while k < length:
            p = op(p, jnp.where(tid >= k, pltpu.roll(p, k, 0), fill))
            s = op(s, jnp.where(tid < length - k, pltpu.roll(s, length - k, 0), fill))
            k *= 2
        return jnp.where(fwd, p, s)

    b = scan_sublanes(lf, jnp.add, 0.0)
    btot = jnp.sum(lf, axis=0, keepdims=True)
    a = btot - b + li
    m_loc = jnp.max(a, axis=0, keepdims=True)
    r = li - b
    cm = scan_sublanes(r, jnp.maximum, -jnp.inf)

    m_ctx = jnp.maximum(btot, m_loc)
    m_in = jnp.where(lane1 == 0, pltpu.roll(m_ctx, LANE - 2 * nc, 1), pltpu.roll(m_ctx, LANE - 2, 1))
    for k in range(nc - 1):
        m_new = jnp.maximum(btot + m_in, m_loc)
        m_in = jnp.where(lane1 == k + 1, pltpu.roll(m_new, 1, 1),
                         jnp.where(lane1 == 2 * nc - 2 - k, pltpu.roll(m_new, LANE - 1, 1), m_in))
    is_ctx = lane1 >= 2 * nc
    m_in = jnp.where(is_ctx, 0.0, m_in)
    m_new = jnp.maximum(btot + m_in, m_loc)
    s_old = jnp.broadcast_to(jnp.exp(btot + m_in - m_new), li.shape)
    w = jnp.exp(a - m_new)
    big_m = jnp.maximum(m_in, cm)
    s_int = jnp.exp(m_in - big_m)
    e_neg = jnp.exp(-(b + big_m))
    g = 2 * nc
    col_ref[0, 0] = jnp.where(
        lane < g, w, jnp.where(
            lane < 2 * g, pltpu.roll(big_m, g, 1), jnp.where(
                lane < 3 * g, pltpu.roll(s_int, 2 * g, 1), jnp.where(
                    lane < 4 * g, pltpu.roll(e_neg, 3 * g, 1), jnp.where(
                        lane < 4 * g + 2, pltpu.roll(w, 3 * g, 1), pltpu.roll(s_old, 4 * g + 2, 1))))))

    lir = lir_ref[0, 0]
    lfr = _log_sigmoid(lfr_ref[0, 0])
    width = lir.shape[1]
    rid = lax.broadcasted_iota(jnp.int32, lir.shape, 0)
    pid = lax.broadcasted_iota(jnp.int32, lir.shape, 1)
    p = lfr
    s = lfr
    k = 1
    while k < width:
        p = p + jnp.where(pid >= k, pltpu.roll(p, k, 1), 0.0)
        s = s + jnp.where(pid < width - k, pltpu.roll(s, width - k, 1), 0.0)
        k *= 2
    row_ref[0, 0] = lir - jnp.where(rid < nc, p, s)


def _gate_stats(g_x, g_c, gate_bias, bsz, seq, ctx_len):
    nc = seq // MCHUNK
    gb = gate_bias.astype(F32)
    gx = g_x[:, :N_GATE_COLS].reshape(bsz, nc, MCHUNK, 2, 2, HEADS) + gb.reshape(2, 2, HEADS)
    gc = g_c[:, :N_GATE_COLS].reshape(bsz, ctx_len, 2, 2, HEADS) + gb.reshape(2, 2, HEADS)
    col_x = gx.transpose(4, 0, 5, 2, 3, 1).reshape(2, bsz, HEADS, MCHUNK, 2 * nc)
    col_c = gc.transpose(3, 0, 4, 1, 2)
    col = jnp.concatenate([col_x, col_c], -1)
    col = jnp.pad(col, ((0, 0),) * 4 + ((0, LANE - col.shape[-1]),))
    row = gx.transpose(4, 0, 5, 3, 1, 2).reshape(2, bsz, HEADS, 2 * nc, MCHUNK)
    blk_c = pl.BlockSpec((1, 1, MCHUNK, LANE), lambda b, h: (b, h, 0, 0))
    blk_r = pl.BlockSpec((1, 1, 2 * nc, MCHUNK), lambda b, h: (b, h, 0, 0))
    return pl.pallas_call(
        functools.partial(_gate_kernel, nc=nc),
        grid=(bsz, HEADS),
        in_specs=[blk_c, blk_c, blk_r, blk_r],
        out_specs=[blk_r, blk_c],
        out_shape=[jax.ShapeDtypeStruct((bsz, HEADS, 2 * nc, MCHUNK), F32),
                   jax.ShapeDtypeStruct((bsz, HEADS, MCHUNK, LANE), F32)],
        compiler_params=_cparams(("parallel", "parallel")),
        name="gate_stats",
    )(col[0], col[1], row[0], row[1])


def _mlstm_kernel(q_ref, k_ref, v_ref, o_ref, kc_ref, vc_ref, cwq_ref, cbq_ref, cwk_ref, cbk_ref,
                  row_ref, col_ref, nw_ref, y_ref, q_s, k_s, kc_s, ct_s, n_s, *, nc):
    lc = MCHUNK

    def conv_silu(x, w, b):
        n = x.shape[0]
        rid = lax.broadcasted_iota(jnp.int32, x.shape, 0)
        xm = jnp.where(rid == 0, 0.0, pltpu.roll(x, 1, 0))
        xp = jnp.where(rid == n - 1, 0.0, pltpu.roll(x, n - 1, 0))
        return _silu(xm * w[0:1, :] + x * w[1:2, :] + xp * w[2:3, :] + b)

    k_scale = HD ** -0.5
    q_s[...] = conv_silu(q_ref[...].astype(F32), cwq_ref[...], cbq_ref[...]).astype(BF16)
    k_s[...] = (conv_silu(k_ref[...].astype(F32), cwk_ref[...], cbk_ref[...]) * k_scale).astype(BF16)
    kc_s[...] = (conv_silu(kc_ref[...].astype(F32), cwk_ref[...], cbk_ref[...]) * k_scale).astype(BF16)

    def col(j):
        return col_ref[0, 0, :, j:j + 1]

    def local_state(kk, vv, wcol):
        vw = (vv.astype(F32) * wcol).astype(BF16)
        ct = jnp.dot(kk.T, vw, preferred_element_type=F32)
        nn = jnp.sum(kk.astype(F32) * wcol, axis=0, keepdims=True)
        return ct, nn

    for d in range(2):
        ct, nn = local_state(kc_s[...], vc_ref[...], col(8 * nc + d))
        order = list(range(nc)) if d == 0 else list(range(nc - 1, -1, -1))
        for pos, c in enumerate(order):
            idx = d * nc + c
            ct_s[idx] = ct.astype(BF16)
            n_s[idx] = nn
            if pos == nc - 1:
                break
            sl = pl.ds(c * lc, lc)
            ctl, nl = local_state(k_s[sl, :], v_ref[sl, :], col(idx))
            s_old = col_ref[0, 0, 0:1, 8 * nc + 2 + idx:8 * nc + 3 + idx]
            ct = s_old * ct + ctl
            nn = s_old * nn + nl

    tid = lax.broadcasted_iota(jnp.int32, (lc, lc), 0)
    sid = lax.broadcasted_iota(jnp.int32, (lc, lc), 1)
    masks = (sid <= tid, sid >= tid)
    for c in range(nc):
        sl = pl.ds(c * lc, lc)
        q = q_s[sl, :]
        kk = k_s[sl, :]
        v = v_ref[sl, :]
        qf = q.astype(F32)
        s = lax.dot_general(q, kk, (((1,), (1,)), ((), ())), preferred_element_type=F32)
        h = None
        for d in range(2):
            idx = d * nc + c
            r = row_ref[0, 0, idx:idx + 1, :]
            big_m = col(2 * nc + idx)
            s_int = col(4 * nc + idx)
            e_neg = col(6 * nc + idx)
            p = jnp.where(masks[d], jnp.exp(r - big_m), 0.0) * s
            den = (jnp.sum(p, axis=-1, keepdims=True)
                   + s_int * jnp.sum(qf * n_s[idx], axis=-1, keepdims=True))
            num = (jnp.dot(p.astype(BF16), v, preferred_element_type=F32)
                   + s_int * jnp.dot(q, ct_s[idx], preferred_element_type=F32))
            hd = num / jnp.maximum(jnp.abs(den), e_neg)
            h = hd if h is None else h + hd
        mu = jnp.mean(h, axis=-1, keepdims=True)
        hc = h - mu
        var = jnp.mean(hc * hc, axis=-1, keepdims=True)
        hn = hc * lax.rsqrt(var + LN_EPS) * nw_ref[...]
        y_ref[sl, :] = (hn * _sigmoid(o_ref[sl, :].astype(F32))).astype(BF16)


def _mlstm(qkvo, kv_ctx, conv_w, conv_b, rowq, colq, norm_w, bsz, seq, ctx_len):
    nc = seq // MCHUNK
    hq = D_MLSTM // HD
    kern = functools.partial(_mlstm_kernel, nc=nc)
    seq_blk = lambda off: pl.BlockSpec((seq, HD), lambda b, h: (b, off + h))
    ctx_blk = lambda off: pl.BlockSpec((ctx_len, HD), lambda b, h: (b, off + h))
    return pl.pallas_call(
        kern,
        grid=(bsz, HEADS),
        in_specs=[seq_blk(0), seq_blk(hq), seq_blk(2 * hq), seq_blk(3 * hq),
                  ctx_blk(0), ctx_blk(hq),
                  pl.BlockSpec((3, HD), lambda b, h: (0, h)),
                  pl.BlockSpec((1, HD), lambda b, h: (0, h)),
                  pl.BlockSpec((3, HD), lambda b, h: (0, hq + h)),
                  pl.BlockSpec((1, HD), lambda b, h: (0, hq + h)),
                  pl.BlockSpec((1, 1, 2 * nc, MCHUNK), lambda b, h: (b, h, 0, 0)),
                  pl.BlockSpec((1, 1, MCHUNK, LANE), lambda b, h: (b, h, 0, 0)),
                  pl.BlockSpec((1, HD), lambda b, h: (0, h))],
        out_specs=pl.BlockSpec((seq, HD), lambda b, h: (b, h)),
        out_shape=jax.ShapeDtypeStruct((bsz * seq, D_MLSTM), BF16),
        scratch_shapes=[pltpu.VMEM((seq, HD), BF16), pltpu.VMEM((seq, HD), BF16),
                        pltpu.VMEM((ctx_len, HD), BF16),
                        pltpu.VMEM((2 * nc, HD, HD), BF16), pltpu.VMEM((2 * nc, 1, HD), F32)],
        compiler_params=_cparams(("parallel", "parallel")),
        name="mlstm",
    )(qkvo, qkvo, qkvo, qkvo, kv_ctx, kv_ctx, conv_w, conv_b, conv_w, conv_b, rowq, colq, norm_w)


def _out_kernel(ym_ref, uv_ref, x_ref, pe_ref, m_ref, cnw_ref, ws_ref, bs_ref, wout_ref,
                l1w_ref, l1b_ref, wr_ref, br_ref, x1_ref, h2_ref, lg_ref, y_s, *, bm):
    m = m_ref[0]
    y_s[:, :D_MLSTM] = ym_ref[...]
    for g in range(CMLP_GROUPS):
        gs = slice(g * CMLP_GD, (g + 1) * CMLP_GD)
        vg = uv_ref[:, D_CMLP + g * CMLP_GD:D_CMLP + (g + 1) * CMLP_GD].astype(F32)
        mu = jnp.mean(vg, axis=-1, keepdims=True)
        vc = vg - mu
        var = jnp.mean(vc * vc, axis=-1, keepdims=True)
        vn = (vc * lax.rsqrt(var + LN_EPS) * cnw_ref[:, gs]).astype(BF16)
        for p in range(bm // CMLP_CHUNK):
            ps = slice(p * CMLP_CHUNK, (p + 1) * CMLP_CHUNK)
            s = jnp.dot(ws_ref[g], vn[ps, :], preferred_element_type=F32) + bs_ref[:, g:g + 1]
            yc = uv_ref[ps, gs].astype(F32) * s
            y_s[ps, D_MLSTM + g * CMLP_GD:D_MLSTM + (g + 1) * CMLP_GD] = yc.astype(BF16)
    y = jnp.dot(y_s[...], wout_ref[...], preferred_element_type=F32)
    z = DEEPNORM_ALPHA * (x_ref[...] + pe_ref[...]) + m[2:3, :] * y
    x1 = _layer_norm_rows(z, l1w_ref[...], l1b_ref[...])
    x1_ref[...] = x1
    h2 = (x1 * (1.0 + m[4:5, :]) + m[3:4, :]).astype(BF16)
    h2_ref[...] = h2
    lg_ref[...] = jnp.dot(h2, wr_ref[...], preferred_element_type=F32) + br_ref[...]


def _mixer_out(ym, uv, x2d, pe, mods3, cnw, ws, bs_t, wout, l1w, l1b, wr, br, seq, bm):
    rows, dm = x2d.shape
    pe_blocks = seq // bm
    kern = functools.partial(_out_kernel, bm=bm)
    row_blk = lambda n: pl.BlockSpec((bm, n), lambda i: (i, 0))
    return pl.pallas_call(
        kern,
        grid=(rows // bm,),
        in_specs=[row_blk(D_MLSTM), row_blk(2 * D_CMLP), row_blk(dm),
                  pl.BlockSpec((bm, dm), lambda i: (i % pe_blocks, 0)),
                  pl.BlockSpec((1, N_MOD, dm), lambda i: (i // pe_blocks, 0, 0)),
                  _resident(cnw.shape), _resident(ws.shape), _resident(bs_t.shape),
                  _resident(wout.shape), _resident(l1w.shape), _resident(l1b.shape),
                  _resident(wr.shape), _resident(br.shape)],
        out_specs=[row_blk(dm), row_blk(dm), row_blk(LANE)],
        out_shape=[jax.ShapeDtypeStruct((rows, dm), F32),
                   jax.ShapeDtypeStruct((rows, dm), BF16),
                   jax.ShapeDtypeStruct((rows, LANE), F32)],
        scratch_shapes=[pltpu.VMEM((bm, D_MLSTM + D_CMLP), BF16)],
        compiler_params=_cparams(("parallel",)),
        name="mixer_out",
    )(ym, uv, x2d, pe, mods3, cnw, ws, bs_t, wout, l1w, l1b, wr, br)


def _moe_kernel(slot_ref, comp_ref, cast_ref, cslot_ref, ne_ref, x_ref, wgf_ref, wuf_ref, wdf_ref,
                y_ref, wg_s, wu_s, wd_s, acc_s):
    del ne_ref
    s = pl.program_id(0)
    j = pl.program_id(1)
    last = pl.num_programs(1) - 1

    @pl.when(cast_ref[s] == 1)
    def _():
        cs = cslot_ref[s]
        wg_s[cs, j] = wgf_ref[0].astype(BF16)
        wu_s[cs, j] = wuf_ref[0].astype(BF16)
        wd_s[cs, j] = wdf_ref[0].astype(BF16)

    @pl.when(comp_ref[s] == 1)
    def _():
        sl = slot_ref[s]
        x = x_ref[...]
        g = jnp.dot(x, wg_s[sl, j], preferred_element_type=F32)
        u = jnp.dot(x, wu_s[sl, j], preferred_element_type=F32)
        h = (_silu(g) * u).astype(BF16)
        part = jnp.dot(h, wd_s[sl, j], preferred_element_type=F32)

        @pl.when(j == 0)
        def _():
            acc_s[...] = part

        @pl.when(j > 0)
        def _():
            acc_s[...] += part

        @pl.when(j == last)
        def _():
            y_ref[...] = acc_s[...].astype(y_ref.dtype)

    @pl.when(jnp.logical_and(comp_ref[s] == 0, j == last))
    def _():
        y_ref[...] = jnp.zeros_like(y_ref)


def _experts(xp, sched, wg, wu, wd):
    rows, dm = xp.shape
    n_blk = rows // MOE_ROWS
    nt = D_EXPERT // MOE_FT

    def tile(j, cast, s):
        return jnp.where(cast[s] == 1, j, nt - 1)

    def xmap(s, j, slot, comp, cast, cslot, ne):
        return (jnp.maximum(s - 1, 0), 0)

    grid_spec = pltpu.PrefetchScalarGridSpec(
        num_scalar_prefetch=5,
        grid=(n_blk + 1, nt),
        in_specs=[pl.BlockSpec((MOE_ROWS, dm), xmap),
                  pl.BlockSpec((1, dm, MOE_FT), lambda s, j, slot, comp, cast, cslot, ne: (ne[s], 0, tile(j, cast, s))),
                  pl.BlockSpec((1, dm, MOE_FT), lambda s, j, slot, comp, cast, cslot, ne: (ne[s], 0, tile(j, cast, s))),
                  pl.BlockSpec((1, MOE_FT, dm), lambda s, j, slot, comp, cast, cslot, ne: (ne[s], tile(j, cast, s), 0))],
        out_specs=pl.BlockSpec((MOE_ROWS, dm), xmap),
        scratch_shapes=[pltpu.VMEM((2, nt, dm, MOE_FT), BF16), pltpu.VMEM((2, nt, dm, MOE_FT), BF16),
                        pltpu.VMEM((2, nt, MOE_FT, dm), BF16), pltpu.VMEM((MOE_ROWS, dm), F32)],
    )
    return pl.pallas_call(
        _moe_kernel,
        grid_spec=grid_spec,
        out_shape=jax.ShapeDtypeStruct((rows, dm), BF16),
        compiler_params=_cparams(("arbitrary", "arbitrary")),
        name="experts",
    )(*sched, xp, wg, wu, wd)


def _final_kernel(x1_ref, y0_ref, y1_ref, g_ref, m_ref, w_ref, b_ref, o_ref):
    m = m_ref[0]
    y = (g_ref[:, 0:1] * y0_ref[...].astype(F32) + g_ref[:, 1:2] * y1_ref[...].astype(F32))
    z = DEEPNORM_ALPHA * x1_ref[...] + m[5:6, :] * y
    o_ref[...] = _layer_norm_rows(z, w_ref[...], b_ref[...])


def _final(x1, y0, y1, gates, mods3, w, b, seq, bm):
    rows, dm = x1.shape
    blocks_per_batch = seq // bm
    row_blk = lambda n: pl.BlockSpec((bm, n), lambda i: (i, 0))
    return pl.pallas_call(
        _final_kernel,
        grid=(rows // bm,),
        in_specs=[row_blk(dm), row_blk(dm), row_blk(dm), row_blk(LANE),
                  pl.BlockSpec((1, N_MOD, dm), lambda i: (i // blocks_per_batch, 0, 0)),
                  _resident(w.shape), _resident(b.shape)],
        out_specs=row_blk(dm),
        out_shape=jax.ShapeDtypeStruct((rows, dm), F32),
        compiler_params=_cparams(("parallel",)),
        name="final_ln",
    )(x1, y0, y1, gates, mods3, w, b)


def _route(logits, n_tok):
    logits1 = logits[:, :N_GROUPS]
    grp = jnp.argmax(logits1, -1).astype(jnp.int32)
    gsel = grp[:, None] == jnp.arange(N_GROUPS, dtype=jnp.int32)[None, :]
    p_grp = jnp.sum(jnp.where(gsel, jax.nn.softmax(logits1, -1), 0.0), -1)
    logits2 = logits[:, N_GROUPS:N_GROUPS + N_EXPERTS].reshape(n_tok, N_GROUPS, EXPERTS_PER_GROUP)
    l2 = jnp.sum(jnp.where(gsel[:, :, None], logits2, 0.0), 1)
    eidx = jnp.arange(EXPERTS_PER_GROUP, dtype=jnp.int32)[None, :]
    i0 = jnp.argmax(l2, -1).astype(jnp.int32)
    v0 = jnp.max(l2, -1)
    l2m = jnp.where(eidx == i0[:, None], -jnp.inf, l2)
    i1 = jnp.argmax(l2m, -1).astype(jnp.int32)
    v1 = jnp.max(l2m, -1)
    top_v = jnp.stack([v0, v1], -1)
    gate = p_grp[:, None] * jax.nn.softmax(top_v, -1)
    e_flat = (grp[:, None] * EXPERTS_PER_GROUP + jnp.stack([i0, i1], -1)).reshape(-1)
    n_asg = e_flat.shape[0]
    blk = 256
    onehot = (e_flat[:, None] == jnp.arange(N_EXPERTS, dtype=jnp.int32)[None, :])
    oh = onehot.astype(BF16).reshape(n_asg // blk, blk, N_EXPERTS)
    tri = (jnp.arange(blk)[:, None] >= jnp.arange(blk)[None, :]).astype(BF16)
    within = jnp.einsum("ts,bse->bte", tri, oh, preferred_element_type=F32)
    totals = within[:, -1, :]
    offs = jnp.cumsum(totals, axis=0) - totals
    csum = (within + offs[:, None, :]).reshape(n_asg, N_EXPERTS)
    counts = (offs[-1] + totals[-1]).astype(jnp.int32)
    rank = jnp.sum(jnp.where(onehot, csum, 0.0), -1).astype(jnp.int32) - 1
    nblk_e = (counts + MOE_ROWS - 1) // MOE_ROWS
    padded = nblk_e * MOE_ROWS
    pad_end = jnp.cumsum(padded)
    pad_start = pad_end - padded
    pos = jnp.sum(jnp.where(onehot, pad_start[None, :], 0), -1) + rank
    n_blk = n_asg // MOE_ROWS + N_EXPERTS
    n_used = pad_end[-1] // MOE_ROWS
    bidx = jnp.arange(n_blk, dtype=jnp.int32)
    blk_e = jnp.minimum(jnp.sum(pad_end[None, :] <= (bidx * MOE_ROWS)[:, None], -1), N_EXPERTS - 1).astype(jnp.int32)
    used = bidx < n_used
    has = nblk_e > 0
    visit_of_e = jnp.cumsum(has.astype(jnp.int32)) - 1
    n_visits = visit_of_e[-1] + 1
    e_of_visit = jnp.sort(jnp.where(has, jnp.arange(N_EXPERTS, dtype=jnp.int32), N_EXPERTS))
    first_e = jnp.minimum(e_of_visit[0], N_EXPERTS - 1)
    visit_b = visit_of_e[blk_e]
    prev_e = jnp.concatenate([jnp.full((1,), -1, jnp.int32), blk_e[:-1]])
    is_first = used & (blk_e != prev_e)
    nxt_visit = visit_b + 1
    do_cast = is_first & (nxt_visit < n_visits)
    nxt_e = e_of_visit[jnp.minimum(nxt_visit, N_EXPERTS - 1)]
    ne_b = jnp.where(do_cast, nxt_e, -1)
    ne_b = lax.cummax(jnp.concatenate([first_e[None], ne_b]), axis=0)
    one = jnp.ones((1,), jnp.int32)
    zero = jnp.zeros((1,), jnp.int32)
    slot = jnp.concatenate([zero, (visit_b % 2).astype(jnp.int32)])
    comp = jnp.concatenate([zero, used.astype(jnp.int32)])
    cast = jnp.concatenate([one, do_cast.astype(jnp.int32)])
    cslot = jnp.concatenate([zero, ((visit_b + 1) % 2).astype(jnp.int32)])
    sched = (slot, comp, cast, cslot, ne_b.astype(jnp.int32))
    row_tok = jnp.full((n_blk * MOE_ROWS,), n_tok, jnp.int32).at[pos].set(
        jnp.arange(n_asg, dtype=jnp.int32) // 2)
    return pos.reshape(n_tok, 2), gate, sched, row_tok


def _grid_pos_embed(rows):
    r = jnp.repeat(jnp.arange(rows, dtype=F32), GRID_W)
    col = jnp.tile(jnp.arange(GRID_W, dtype=F32), rows)
    quarter = D_MODEL // 4
    omega = 1.0 / (10000.0 ** (jnp.arange(quarter, dtype=F32) / quarter))
    ar = r[:, None] * omega
    ac = col[:, None] * omega
    return jnp.concatenate([jnp.sin(ar), jnp.cos(ar), jnp.sin(ac), jnp.cos(ac)], -1)


def kernel(x, c, ctx, c_ctx, w_mod, b_mod, w_in, conv_w, conv_b, gate_bias, mlstm_norm_w, cmlp_norm_w,
           w_s, b_s, w_out, ln1_w, ln1_b, router1_w, router1_b, router2_w, router2_b, w_gate, w_up,
           w_down, ln2_w, ln2_b):
    bsz, seq, dm = x.shape
    ctx_len = ctx.shape[1]
    n_tok = bsz * seq
    assert w_mod.shape[0] == 1 and dm == D_MODEL and seq % MCHUNK == 0 and ctx_len == MCHUNK
    pe = _grid_pos_embed(seq // GRID_W).astype(x.dtype)
    x2d = x.reshape(n_tok, dm)
    ctx2d = ctx.reshape(bsz * ctx_len, dm)

    mod_rows = 16
    cc = jnp.concatenate([c, c_ctx[None, :], jnp.zeros((mod_rows - bsz - 1, dm), c.dtype)], 0)
    mods3 = _modulation(cc, w_mod[0], b_mod[0]).reshape(mod_rows, N_MOD, dm)

    dq = D_MLSTM
    wi = w_in[0]
    w_qkvo = wi[:, :4 * dq].astype(BF16)
    w_g = jnp.pad(wi[:, 4 * dq:4 * dq + N_GATE_COLS], ((0, 0), (0, LANE - N_GATE_COLS))).astype(BF16)
    w_uv = wi[:, 4 * dq + N_GATE_COLS:].astype(BF16)
    bm_proj = 256
    blocks_per_seq = seq // bm_proj
    qkvo, g_x, uv = _projection(x2d, pe, mods3, lambda i: i // blocks_per_seq,
                                [w_qkvo, w_g, w_uv], [BF16, F32, BF16], [False, False, True], bm_proj)
    w_kv = wi[:, dq:3 * dq].astype(BF16)
    kv_c, g_c = _projection(ctx2d, None, mods3, lambda i: bsz, [w_kv, w_g], [BF16, F32],
                            [False, False], bm_proj)

    rowq, colq = _gate_stats(g_x, g_c, gate_bias[0], bsz, seq, ctx_len)
    ym = _mlstm(qkvo, kv_c, conv_w[0], conv_b[0].reshape(1, -1), rowq, colq,
                mlstm_norm_w[0].reshape(1, -1), bsz, seq, ctx_len)

    wr = jnp.pad(jnp.concatenate([router1_w[0], router2_w[0]], 1),
                 ((0, 0), (0, LANE - N_GROUPS - N_EXPERTS))).astype(BF16)
    br = jnp.pad(jnp.concatenate([router1_b[0], router2_b[0]], 0),
                 (0, LANE - N_GROUPS - N_EXPERTS)).reshape(1, LANE)
    x1, h2, logits = _mixer_out(ym, uv, x2d, pe, mods3, cmlp_norm_w[0].reshape(1, -1),
                                w_s[0].astype(BF16), b_s[0].T, w_out[0].astype(BF16),
                                ln1_w[0].reshape(1, -1), ln1_b[0].reshape(1, -1), wr, br, seq, 256)

    pos, gate, sched, row_tok = _route(logits, n_tok)
    h2_ext = jnp.concatenate([h2, jnp.zeros((1, dm), h2.dtype)], 0)
    xp = h2_ext[row_tok]
    yp = _experts(xp, sched, w_gate[0], w_up[0], w_down[0])
    y0 = yp[pos[:, 0]]
    y1 = yp[pos[:, 1]]
    gates = jnp.pad(gate.astype(F32), ((0, 0), (0, LANE - 2)))
    out = _final(x1, y0, y1, gates, mods3, ln2_w[0].reshape(1, -1), ln2_b[0].reshape(1, -1), seq, 256)
    return out.reshape(bsz, seq, dm)
```

```python
import functools

import jax
import jax.numpy as jnp
from jax import lax
from jax.experimental import pallas as pl
from jax.experimental.pallas import tpu as pltpu

F32 = jnp.float32
BF16 = jnp.bfloat16

D_MODEL = 2048
GRID_W = 64
D_MLSTM = 1024
D_CMLP = 1024
HEADS = 4
HD = 256
CMLP_GROUPS = 4
CMLP_GD = 256
CMLP_CHUNK = 128
N_GROUPS = 4
EXPERTS_PER_GROUP = 8
N_EXPERTS = 32
D_EXPERT = 1024
N_MOD = 6
N_GATE_COLS = 16
DEEPNORM_ALPHA = 2.0 ** 0.25
LN_EPS = 1e-6

LANE = 128
SUBLANES = 8
MCHUNK = 256
MOE_ROWS = 256
MOE_FT = 256
VMEM_LIMIT = 56 * 1024 * 1024


def _cparams(sem):
    return pltpu.CompilerParams(dimension_semantics=sem, vmem_limit_bytes=VMEM_LIMIT)


def _resident(shape):
    nd = len(shape)
    return pl.BlockSpec(shape, lambda *_: (0,) * nd, pipeline_mode=pl.Buffered(1))


def _sigmoid(x):
    return 1.0 / (1.0 + jnp.exp(-x))


def _silu(x):
    return x * _sigmoid(x)


def _log_sigmoid(x):
    return jnp.minimum(x, 0.0) - jnp.log1p(jnp.exp(-jnp.abs(x)))


def _gelu_tanh(x):
    c = 0.7978845608028654
    return 0.5 * x * (1.0 + jnp.tanh(c * (x + 0.044715 * (x * x * x))))


def _layer_norm_rows(z, w, b):
    mu = jnp.mean(z, axis=-1, keepdims=True)
    zc = z - mu
    var = jnp.mean(zc * zc, axis=-1, keepdims=True)
    return zc * lax.rsqrt(var + LN_EPS) * w + b


def _mod_kernel(c_ref, w_ref, b_ref, o_ref):
    s = _silu(c_ref[...]).astype(BF16)
    o_ref[...] = jnp.dot(s, w_ref[...].astype(BF16), preferred_element_type=F32) + b_ref[...]


def _modulation(cc, w_mod, b_mod):
    rows, dm = cc.shape
    n = w_mod.shape[1]
    tn = 1024
    return pl.pallas_call(
        _mod_kernel,
        grid=(n // tn,),
        in_specs=[pl.BlockSpec((rows, dm), lambda j: (0, 0)),
                  pl.BlockSpec((dm, tn), lambda j: (0, j)),
                  pl.BlockSpec((1, tn), lambda j: (0, j))],
        out_specs=pl.BlockSpec((rows, tn), lambda j: (0, j)),
        out_shape=jax.ShapeDtypeStruct((rows, n), F32),
        compiler_params=_cparams(("arbitrary",)),
        name="modulation",
    )(cc, w_mod, b_mod.reshape(1, n))


def _proj_kernel(*refs, n_w, has_pe, gelu_flags, tn):
    x_ref = refs[0]
    k = 1
    pe_ref = None
    if has_pe:
        pe_ref = refs[k]
        k += 1
    m_ref = refs[k]
    k += 1
    w_refs = refs[k:k + n_w]
    o_refs = refs[k + n_w:k + 2 * n_w]
    hx_ref = refs[k + 2 * n_w]
    x = x_ref[...]
    if has_pe:
        x = x + pe_ref[...]
    m = m_ref[0]
    hx_ref[...] = (x * (1.0 + m[1:2, :]) + m[0:1, :]).astype(BF16)
    for w_ref, o_ref, use_gelu in zip(w_refs, o_refs, gelu_flags):
        n = w_ref.shape[1]
        step = min(tn, n)
        for j in range(0, n, step):
            acc = jnp.dot(hx_ref[...], w_ref[:, j:j + step], preferred_element_type=F32)
            if use_gelu:
                acc = _gelu_tanh(acc)
            o_ref[:, j:j + step] = acc.astype(o_ref.dtype)


def _projection(x2d, pe, mods3, mod_row_of_block, weights, out_dtypes, gelu_flags, bm):
    rows, dm = x2d.shape
    has_pe = pe is not None
    n_w = len(weights)
    in_specs = [pl.BlockSpec((bm, dm), lambda i: (i, 0))]
    args = [x2d]
    if has_pe:
        pe_blocks = pe.shape[0] // bm
        in_specs.append(pl.BlockSpec((bm, dm), lambda i: (i % pe_blocks, 0)))
        args.append(pe)
    in_specs.append(pl.BlockSpec((1, N_MOD, dm), lambda i: (mod_row_of_block(i), 0, 0)))
    args.append(mods3)
    for w in weights:
        in_specs.append(_resident(w.shape))
        args.append(w)
    out_specs = [pl.BlockSpec((bm, w.shape[1]), lambda i: (i, 0)) for w in weights]
    out_shape = [jax.ShapeDtypeStruct((rows, w.shape[1]), dt) for w, dt in zip(weights, out_dtypes)]
    kern = functools.partial(_proj_kernel, n_w=n_w, has_pe=has_pe, gelu_flags=tuple(gelu_flags), tn=1024)
    return pl.pallas_call(
        kern,
        grid=(rows // bm,),
        in_specs=in_specs,
        out_specs=out_specs,
        out_shape=out_shape,
        scratch_shapes=[pltpu.VMEM((bm, dm), BF16)],
        compiler_params=_cparams(("parallel",)),
        name="projection",
    )(*args)


def _gate_kernel(lic_ref, lfc_ref, lir_ref, lfr_ref, row_ref, col_ref, *, nc):
    li = lic_ref[0, 0]
    lf = _log_sigmoid(lfc_ref[0, 0])
    length = li.shape[0]
    tid = lax.broadcasted_iota(jnp.int32, li.shape, 0)
    lane = lax.broadcasted_iota(jnp.int32, li.shape, 1)
    lane1 = lane[0:1, :]
    fwd = (lane < nc) | (lane == 2 * nc)

    def scan_sublanes(x, op, fill):
        p = x
        s = x
        k = 1
        while k < length:
            p = op(p, jnp.where(tid >= k, pltpu.roll(p, k, 0), fill))
            s = op(s, jnp.where(tid < length - k, pltpu.roll(s, length - k, 0), fill))
            k *= 2
        return jnp.where(fwd, p, s)

    b = scan_sublanes(lf, jnp.add, 0.0)
    btot = jnp.sum(lf, axis=0, keepdims=True)
    a = btot - b + li
    m_loc = jnp.max(a, axis=0, keepdims=True)
    r = li - b
    cm = scan_sublanes(r, jnp.maximum, -jnp.inf)

    m_ctx = jnp.maximum(btot, m_loc)
    m_in = jnp.where(lane1 == 0, pltpu.roll(m_ctx, LANE - 2 * nc, 1), pltpu.roll(m_ctx, LANE - 2, 1))
    for k in range(nc - 1):
        m_new = jnp.maximum(btot + m_in, m_loc)
        m_in = jnp.where(lane1 == k + 1, pltpu.roll(m_new, 1, 1),
                         jnp.where(lane1 == 2 * nc - 2 - k, pltpu.roll(m_new, LANE - 1, 1), m_in))
    is_ctx = lane1 >= 2 * nc
    m_in = jnp.where(is_ctx, 0.0, m_in)
    m_new = jnp.maximum(btot + m_in, m_loc)
    s_old = jnp.broadcast_to(jnp.exp(btot + m_in - m_new), li.shape)
    w = jnp.exp(a - m_new)
    big_m = jnp.maximum(m_in, cm)
    s_int = jnp.exp(m_in - big_m)
    e_neg = jnp.exp(-(b + big_m))
    g = 2 * nc
    col_ref[0, 0] = jnp.where(
        lane < g, w, jnp.where(
            lane < 2 * g, pltpu.roll(big_m, g, 1), jnp.where(
                lane < 3 * g, pltpu.roll(s_int, 2 * g, 1), jnp.where(
                    lane < 4 * g, pltpu.roll(e_neg, 3 * g, 1), jnp.where(
                        lane < 4 * g + 2, pltpu.roll(w, 3 * g, 1), pltpu.roll(s_old, 4 * g + 2, 1))))))

    lir = lir_ref[0, 0]
    lfr = _log_sigmoid(lfr_ref[0, 0])
    width = lir.shape[1]
    rid = lax.broadcasted_iota(jnp.int32, lir.shape, 0)
    pid = lax.broadcasted_iota(jnp.int32, lir.shape, 1)
    p = lfr
    s = lfr
    k = 1
    while k < width:
        p = p + jnp.where(pid >= k, pltpu.roll(p, k, 1), 0.0)
        s = s + jnp.where(pid < width - k, pltpu.roll(s, width - k, 1), 0.0)
        k *= 2
    row_ref[0, 0] = lir - jnp.where(rid < nc, p, s)


def _gate_stats(g_x, g_c, gate_bias, bsz, seq, ctx_len):
    nc = seq // MCHUNK
    gb = gate_bias.astype(F32)
    gx = g_x[:, :N_GATE_COLS].reshape(bsz, nc, MCHUNK, 2, 2, HEADS) + gb.reshape(2, 2, HEADS)
    gc = g_c[:, :N_GATE_COLS].reshape(bsz, ctx_len, 2, 2, HEADS) + gb.reshape(2, 2, HEADS)
    col_x = gx.transpose(4, 0, 5, 2, 3, 1).reshape(2, bsz, HEADS, MCHUNK, 2 * nc)
    col_c = gc.transpose(3, 0, 4, 1, 2)
    col = jnp.concatenate([col_x, col_c], -1)
    col = jnp.pad(col, ((0, 0),) * 4 + ((0, LANE - col.shape[-1]),))
    row = gx.transpose(4, 0, 5, 3, 1, 2).reshape(2, bsz, HEADS, 2 * nc, MCHUNK)
    blk_c = pl.BlockSpec((1, 1, MCHUNK, LANE), lambda b, h: (b, h, 0, 0))
    blk_r = pl.BlockSpec((1, 1, 2 * nc, MCHUNK), lambda b, h: (b, h, 0, 0))
    return pl.pallas_call(
        functools.partial(_gate_kernel, nc=nc),
        grid=(bsz, HEADS),
        in_specs=[blk_c, blk_c, blk_r, blk_r],
        out_specs=[blk_r, blk_c],
        out_shape=[jax.ShapeDtypeStruct((bsz, HEADS, 2 * nc, MCHUNK), F32),
                   jax.ShapeDtypeStruct((bsz, HEADS, MCHUNK, LANE), F32)],
        compiler_params=_cparams(("parallel", "parallel")),
        name="gate_stats",
    )(col[0], col[1], row[0], row[1])


def _mlstm_kernel(q_ref, k_ref, v_ref, o_ref, kc_ref, vc_ref, cwq_ref, cbq_ref, cwk_ref, cbk_ref,
                  row_ref, col_ref, nw_ref, y_ref, q_s, k_s, kc_s, ct_s, n_s, *, nc):
    lc = MCHUNK

    def conv_silu(x, w, b):
        n = x.shape[0]
        rid = lax.broadcasted_iota(jnp.int32, x.shape, 0)
        xm = jnp.where(rid == 0, 0.0, pltpu.roll(x, 1, 0))
        xp = jnp.where(rid == n - 1, 0.0, pltpu.roll(x, n - 1, 0))
        return _silu(xm * w[0:1, :] + x * w[1:2, :] + xp * w[2:3, :] + b)

    k_scale = HD ** -0.5
    q_s[...] = conv_silu(q_ref[...].astype(F32), cwq_ref[...], cbq_ref[...]).astype(BF16)
    k_s[...] = (conv_silu(k_ref[...].astype(F32), cwk_ref[...], cbk_ref[...]) * k_scale).astype(BF16)
    kc_s[...] = (conv_silu(kc_ref[...].astype(F32), cwk_ref[...], cbk_ref[...]) * k_scale).astype(BF16)

    def col(j):
        return col_ref[0, 0, :, j:j + 1]

    def local_state(kk, vv, wcol):
        vw = (vv.astype(F32) * wcol).astype(BF16)
        ct = jnp.dot(kk.T, vw, preferred_element_type=F32)
        nn = jnp.sum(kk.astype(F32) * wcol, axis=0, keepdims=True)
        return ct, nn

    for d in range(2):
        ct, nn = local_state(kc_s[...], vc_ref[...], col(8 * nc + d))
        order = list(range(nc)) if d == 0 else list(range(nc - 1, -1, -1))
        for pos, c in enumerate(order):
            idx = d * nc + c
            ct_s[idx] = ct.astype(BF16)
            n_s[idx] = nn
            if pos == nc - 1:
                break
            sl = pl.ds(c * lc, lc)
            ctl, nl = local_state(k_s[sl, :], v_ref[sl, :], col(idx))
            s_old = col_ref[0, 0, 0:1, 8 * nc + 2 + idx:8 * nc + 3 + idx]
            ct = s_old * ct + ctl
            nn = s_old * nn + nl

    tid = lax.broadcasted_iota(jnp.int32, (lc, lc), 0)
    sid = lax.broadcasted_iota(jnp.int32, (lc, lc), 1)
    masks = (sid <= tid, sid >= tid)
    for c in range(nc):
        sl = pl.ds(c * lc, lc)
        q = q_s[sl, :]
        kk = k_s[sl, :]
        v = v_ref[sl, :]
        qf = q.astype(F32)
        s = lax.dot_general(q, kk, (((1,), (1,)), ((), ())), preferred_element_type=F32)
        h = None
        for d in range(2):
            idx = d * nc + c
            r = row_ref[0, 0, idx:idx + 1, :]
            big_m = col(2 * nc + idx)
            s_int = col(4 * nc + idx)
            e_neg = col(6 * nc + idx)
            p = jnp.where(masks[d], jnp.exp(r - big_m), 0.0) * s
            den = (jnp.sum(p, axis=-1, keepdims=True)
                   + s_int * jnp.sum(qf * n_s[idx], axis=-1, keepdims=True))
            num = (jnp.dot(p.astype(BF16), v, preferred_element_type=F32)
                   + s_int * jnp.dot(q, ct_s[idx], preferred_element_type=F32))
            hd = num / jnp.maximum(jnp.abs(den), e_neg)
            h = hd if h is None else h + hd
        mu = jnp.mean(h, axis=-1, keepdims=True)
        hc = h - mu
        var = jnp.mean(hc * hc, axis=-1, keepdims=True)
        hn = hc * lax.rsqrt(var + LN_EPS) * nw_ref[...]
        y_ref[sl, :] = (hn * _sigmoid(o_ref[sl, :].astype(F32))).astype(BF16)


def _mlstm(qkvo, kv_ctx, conv_w, conv_b, rowq, colq, norm_w, bsz, seq, ctx_len):
    nc = seq // MCHUNK
    hq = D_MLSTM // HD
    kern = functools.partial(_mlstm_kernel, nc=nc)
    seq_blk = lambda off: pl.BlockSpec((seq, HD), lambda b, h: (b, off + h))
    ctx_blk = lambda off: pl.BlockSpec((ctx_len, HD), lambda b, h: (b, off + h))
    return pl.pallas_call(
        kern,
        grid=(bsz, HEADS),
        in_specs=[seq_blk(0), seq_blk(hq), seq_blk(2 * hq), seq_blk(3 * hq),
                  ctx_blk(0), ctx_blk(hq),
                  pl.BlockSpec((3, HD), lambda b, h: (0, h)),
                  pl.BlockSpec((1, HD), lambda b, h: (0, h)),
                  pl.BlockSpec((3, HD), lambda b, h: (0, hq + h)),
                  pl.BlockSpec((1, HD), lambda b, h: (0, hq + h)),
                  pl.BlockSpec((1, 1, 2 * nc, MCHUNK), lambda b, h: (b, h, 0, 0)),
                  pl.BlockSpec((1, 1, MCHUNK, LANE), lambda b, h: (b, h, 0, 0)),
                  pl.BlockSpec((1, HD), lambda b, h: (0, h))],
        out_specs=pl.BlockSpec((seq, HD), lambda b, h: (b, h)),
        out_shape=jax.ShapeDtypeStruct((bsz * seq, D_MLSTM), BF16),
        scratch_shapes=[pltpu.VMEM((seq, HD), BF16), pltpu.VMEM((seq, HD), BF16),
                        pltpu.VMEM((ctx_len, HD), BF16),
                        pltpu.VMEM((2 * nc, HD, HD), BF16), pltpu.VMEM((2 * nc, 1, HD), F32)],
        compiler_params=_cparams(("parallel", "parallel")),
        name="mlstm",
    )(qkvo, qkvo, qkvo, qkvo, kv_ctx, kv_ctx, conv_w, conv_b, conv_w, conv_b, rowq, colq, norm_w)


def _out_kernel(ym_ref, uv_ref, x_ref, pe_ref, m_ref, cnw_ref, ws_ref, bs_ref, wout_ref,
                l1w_ref, l1b_ref, wr_ref, br_ref, x1_ref, h2_ref, lg_ref, y_s, *, bm):
    m = m_ref[0]
    y_s[:, :D_MLSTM] = ym_ref[...]
    for g in range(CMLP_GROUPS):
        gs = slice(g * CMLP_GD, (g + 1) * CMLP_GD)
        vg = uv_ref[:, D_CMLP + g * CMLP_GD:D_CMLP + (g + 1) * CMLP_GD].astype(F32)
        mu = jnp.mean(vg, axis=-1, keepdims=True)
        vc = vg - mu
        var = jnp.mean(vc * vc, axis=-1, keepdims=True)
        vn = (vc * lax.rsqrt(var + LN_EPS) * cnw_ref[:, gs]).astype(BF16)
        for p in range(bm // CMLP_CHUNK):
            ps = slice(p * CMLP_CHUNK, (p + 1) * CMLP_CHUNK)
            s = jnp.dot(ws_ref[g], vn[ps, :], preferred_element_type=F32) + bs_ref[:, g:g + 1]
            yc = uv_ref[ps, gs].astype(F32) * s
            y_s[ps, D_MLSTM + g * CMLP_GD:D_MLSTM + (g + 1) * CMLP_GD] = yc.astype(BF16)
    y = jnp.dot(y_s[...], wout_ref[...], preferred_element_type=F32)
    z = DEEPNORM_ALPHA * (x_ref[...] + pe_ref[...]) + m[2:3, :] * y
    x1 = _layer_norm_rows(z, l1w_ref[...], l1b_ref[...])
    x1_ref[...] = x1
    h2 = x1 * (1.0 + m[4:5, :]) + m[3:4, :]
    h2_ref[...] = h2
    lg_ref[...] = jnp.dot(h2.astype(BF16), wr_ref[...], preferred_element_type=F32) + br_ref[...]


def _mixer_out(ym, uv, x2d, pe, mods3, cnw, ws, bs_t, wout, l1w, l1b, wr, br, seq, bm):
    rows, dm = x2d.shape
    pe_blocks = seq // bm
    kern = functools.partial(_out_kernel, bm=bm)
    row_blk = lambda n: pl.BlockSpec((bm, n), lambda i: (i, 0))
    return pl.pallas_call(
        kern,
        grid=(rows // bm,),
        in_specs=[row_blk(D_MLSTM), row_blk(2 * D_CMLP), row_blk(dm),
                  pl.BlockSpec((bm, dm), lambda i: (i % pe_blocks, 0)),
                  pl.BlockSpec((1, N_MOD, dm), lambda i: (i // pe_blocks, 0, 0)),
                  _resident(cnw.shape), _resident(ws.shape), _resident(bs_t.shape),
                  _resident(wout.shape), _resident(l1w.shape), _resident(l1b.shape),
                  _resident(wr.shape), _resident(br.shape)],
        out_specs=[row_blk(dm), row_blk(dm), row_blk(LANE)],
        out_shape=[jax.ShapeDtypeStruct((rows, dm), F32),
                   jax.ShapeDtypeStruct((rows, dm), F32),
                   jax.ShapeDtypeStruct((rows, LANE), F32)],
        scratch_shapes=[pltpu.VMEM((bm, D_MLSTM + D_CMLP), BF16)],
        compiler_params=_cparams(("parallel",)),
        name="mixer_out",
    )(ym, uv, x2d, pe, mods3, cnw, ws, bs_t, wout, l1w, l1b, wr, br)


def _moe_kernel(slot_ref, comp_ref, cast_ref, cslot_ref, ne_ref, gtab_ref, stab_ref, h2_hbm,
                wgf_ref, wuf_ref, wdf_ref, ys_hbm, wg_s, wu_s, wd_s, acc_s, xbuf, xb_s, ybuf,
                gsem, ssem, pend, *, nt):
    del ne_ref
    s = pl.program_id(0)
    j = pl.program_id(1)
    n_steps = pl.num_programs(0)
    last = nt - 1
    rows_per_tile = MOE_ROWS // nt
    n_asg = ys_hbm.shape[0] * SUBLANES - 2 * MOE_ROWS

    groups = MOE_ROWS // SUBLANES

    def gather_wait(p):
        pltpu.make_async_copy(h2_hbm.at[pl.ds(0, groups)], xbuf.at[p], gsem.at[p]).wait()

    def scatter_wait(p):
        pltpu.make_async_copy(ybuf.at[p], ys_hbm.at[pl.ds(0, groups)], ssem.at[p]).wait()

    @pl.when(jnp.logical_and(s == 0, j == 0))
    def _():
        pend[0] = 0
        pend[1] = 0

    nxt = jnp.minimum(s + 1, n_steps - 1)

    @pl.when(jnp.logical_and(s + 1 < n_steps, comp_ref[nxt] == 1))
    def _():
        p = (s + 1) % 2

        def body(g, carry):
            grp = j * (rows_per_tile // SUBLANES) + g
            for i in range(SUBLANES):
                asg = gtab_ref[0, 0, grp * SUBLANES + i]
                tok = jnp.where(asg < 0, 0, asg >> 1)
                pltpu.make_async_copy(h2_hbm.at[tok >> 3, pl.ds(tok & 7, 1)], xbuf.at[p, grp, pl.ds(i, 1)],
                                      gsem.at[p]).start()
            return carry

        lax.fori_loop(0, rows_per_tile // SUBLANES, body, 0)

    @pl.when(cast_ref[s] == 1)
    def _():
        cs = cslot_ref[s]
        wg_s[cs, j] = wgf_ref[0].astype(BF16)
        wu_s[cs, j] = wuf_ref[0].astype(BF16)
        wd_s[cs, j] = wdf_ref[0].astype(BF16)

    @pl.when(comp_ref[s] == 1)
    def _():
        p = s % 2

        @pl.when(j == 0)
        def _():
            gather_wait(p)
            xb_s[...] = xbuf[p].reshape(MOE_ROWS, xb_s.shape[1]).astype(BF16)

        sl = slot_ref[s]
        x = xb_s[...]
        g = jnp.dot(x, wg_s[sl, j], preferred_element_type=F32)
        u = jnp.dot(x, wu_s[sl, j], preferred_element_type=F32)
        h = (_silu(g) * u).astype(BF16)
        part = jnp.dot(h, wd_s[sl, j], preferred_element_type=F32)

        @pl.when(j == 0)
        def _():
            acc_s[...] = part

        @pl.when(j > 0)
        def _():
            acc_s[...] += part

        @pl.when(j == last)
        def _():
            @pl.when(pend[p] == 1)
            def _():
                scatter_wait(p)

            ybuf[p] = acc_s[...].reshape(groups, SUBLANES, acc_s.shape[1])

            def body(g, carry):
                for i in range(SUBLANES):
                    asg = stab_ref[0, 0, g * SUBLANES + i]
                    dst = jnp.where(asg < 0, n_asg + p * MOE_ROWS + g * SUBLANES + i, asg)
                    pltpu.make_async_copy(ybuf.at[p, g, pl.ds(i, 1)], ys_hbm.at[dst >> 3, pl.ds(dst & 7, 1)],
                                          ssem.at[p]).start()
                return carry

            lax.fori_loop(0, MOE_ROWS // SUBLANES, body, 0)
            pend[p] = 1

    @pl.when(jnp.logical_and(s == n_steps - 1, j == last))
    def _():
        for p in range(2):
            @pl.when(pend[p] == 1)
            def _():
                scatter_wait(p)
                pend[p] = 0

        xbuf[...] = jnp.zeros_like(xbuf)
        fills = [pltpu.make_async_copy(xbuf.at[p], ys_hbm.at[pl.ds(n_asg // SUBLANES + p * groups, groups)],
                                       gsem.at[p]) for p in range(2)]
        for cp in fills:
            cp.start()
        for cp in fills:
            cp.wait()


def _experts(h2, row_asg, sched, wg, wu, wd):
    n_tok, dm = h2.shape
    n_blk = row_asg.shape[0]
    nt = D_EXPERT // MOE_FT
    n_asg = 2 * n_tok

    def tile(j, cast, s):
        return jnp.where(cast[s] == 1, j, nt - 1)

    smem_rows = lambda imap: pl.BlockSpec((1, 1, MOE_ROWS), imap, memory_space=pltpu.SMEM)
    grid_spec = pltpu.PrefetchScalarGridSpec(
        num_scalar_prefetch=5,
        grid=(n_blk + 1, nt),
        in_specs=[smem_rows(lambda s, j, *_: (jnp.minimum(s, n_blk - 1), 0, 0)),
                  smem_rows(lambda s, j, *_: (jnp.maximum(s - 1, 0), 0, 0)),
                  pl.BlockSpec(memory_space=pl.ANY),
                  pl.BlockSpec((1, dm, MOE_FT), lambda s, j, slot, comp, cast, cslot, ne: (ne[s], 0, tile(j, cast, s))),
                  pl.BlockSpec((1, dm, MOE_FT), lambda s, j, slot, comp, cast, cslot, ne: (ne[s], 0, tile(j, cast, s))),
                  pl.BlockSpec((1, MOE_FT, dm), lambda s, j, slot, comp, cast, cslot, ne: (ne[s], tile(j, cast, s), 0))],
        out_specs=pl.BlockSpec(memory_space=pl.ANY),
        scratch_shapes=[pltpu.VMEM((2, nt, dm, MOE_FT), BF16), pltpu.VMEM((2, nt, dm, MOE_FT), BF16),
                        pltpu.VMEM((2, nt, MOE_FT, dm), BF16), pltpu.VMEM((MOE_ROWS, dm), F32),
                        pltpu.VMEM((2, MOE_ROWS // SUBLANES, SUBLANES, dm), F32),
                        pltpu.VMEM((MOE_ROWS, dm), BF16),
                        pltpu.VMEM((2, MOE_ROWS // SUBLANES, SUBLANES, dm), F32),
                        pltpu.SemaphoreType.DMA((2,)), pltpu.SemaphoreType.DMA((2,)),
                        pltpu.SMEM((2,), jnp.int32)],
    )
    return pl.pallas_call(
        functools.partial(_moe_kernel, nt=nt),
        grid_spec=grid_spec,
        out_shape=jax.ShapeDtypeStruct(((n_asg + 2 * MOE_ROWS) // SUBLANES, SUBLANES, dm), F32),
        compiler_params=_cparams(("arbitrary", "arbitrary")),
        name="experts",
    )(*sched, row_asg, row_asg, h2.reshape(n_tok // SUBLANES, SUBLANES, dm), wg, wu, wd)


def _final_kernel(x1_ref, ys_ref, g_ref, m_ref, w_ref, b_ref, o_ref):
    m = m_ref[0]
    dm = x1_ref.shape[1]
    y = g_ref[:, 0:1] * ys_ref[:, :dm] + g_ref[:, 1:2] * ys_ref[:, dm:]
    z = DEEPNORM_ALPHA * x1_ref[...] + m[5:6, :] * y
    o_ref[...] = _layer_norm_rows(z, w_ref[...], b_ref[...])


def _final(x1, ys2, gates, mods3, w, b, seq, bm):
    rows, dm = x1.shape
    blocks_per_batch = seq // bm
    row_blk = lambda n: pl.BlockSpec((bm, n), lambda i: (i, 0))
    return pl.pallas_call(
        _final_kernel,
        grid=(rows // bm,),
        in_specs=[row_blk(dm), row_blk(2 * dm), row_blk(LANE),
                  pl.BlockSpec((1, N_MOD, dm), lambda i: (i // blocks_per_batch, 0, 0)),
                  _resident(w.shape), _resident(b.shape)],
        out_specs=row_blk(dm),
        out_shape=jax.ShapeDtypeStruct((rows, dm), F32),
        compiler_params=_cparams(("parallel",)),
        name="final_ln",
    )(x1, ys2, gates, mods3, w, b)


def _route(logits, n_tok):
    logits1 = logits[:, :N_GROUPS]
    grp = jnp.argmax(logits1, -1).astype(jnp.int32)
    gsel = grp[:, None] == jnp.arange(N_GROUPS, dtype=jnp.int32)[None, :]
    p_grp = jnp.sum(jnp.where(gsel, jax.nn.softmax(logits1, -1), 0.0), -1)
    logits2 = logits[:, N_GROUPS:N_GROUPS + N_EXPERTS].reshape(n_tok, N_GROUPS, EXPERTS_PER_GROUP)
    l2 = jnp.sum(jnp.where(gsel[:, :, None], logits2, 0.0), 1)
    eidx = jnp.arange(EXPERTS_PER_GROUP, dtype=jnp.int32)[None, :]
    i0 = jnp.argmax(l2, -1).astype(jnp.int32)
    v0 = jnp.max(l2, -1)
    l2m = jnp.where(eidx == i0[:, None], -jnp.inf, l2)
    i1 = jnp.argmax(l2m, -1).astype(jnp.int32)
    v1 = jnp.max(l2m, -1)
    top_v = jnp.stack([v0, v1], -1)
    gate = p_grp[:, None] * jax.nn.softmax(top_v, -1)
    e_flat = (grp[:, None] * EXPERTS_PER_GROUP + jnp.stack([i0, i1], -1)).reshape(-1)
    n_asg = e_flat.shape[0]
    blk = 256
    onehot = (e_flat[:, None] == jnp.arange(N_EXPERTS, dtype=jnp.int32)[None, :])
    oh = onehot.astype(BF16).reshape(n_asg // blk, blk, N_EXPERTS)
    tri = (jnp.arange(blk)[:, None] >= jnp.arange(blk)[None, :]).astype(BF16)
    within = jnp.einsum("ts,bse->bte", tri, oh, preferred_element_type=F32)
    totals = within[:, -1, :]
    offs = jnp.cumsum(totals, axis=0) - totals
    csum = (within + offs[:, None, :]).reshape(n_asg, N_EXPERTS)
    counts = (offs[-1] + totals[-1]).astype(jnp.int32)
    rank = jnp.sum(jnp.where(onehot, csum, 0.0), -1).astype(jnp.int32) - 1
    nblk_e = (counts + MOE_ROWS - 1) // MOE_ROWS
    padded = nblk_e * MOE_ROWS
    pad_end = jnp.cumsum(padded)
    pad_start = pad_end - padded
    pos = jnp.sum(jnp.where(onehot, pad_start[None, :], 0), -1) + rank
    n_blk = n_asg // MOE_ROWS + N_EXPERTS
    n_used = pad_end[-1] // MOE_ROWS
    bidx = jnp.arange(n_blk, dtype=jnp.int32)
    blk_e = jnp.minimum(jnp.sum(pad_end[None, :] <= (bidx * MOE_ROWS)[:, None], -1), N_EXPERTS - 1).astype(jnp.int32)
    used = bidx < n_used
    has = nblk_e > 0
    visit_of_e = jnp.cumsum(has.astype(jnp.int32)) - 1
    n_visits = visit_of_e[-1] + 1
    e_of_visit = jnp.sort(jnp.where(has, jnp.arange(N_EXPERTS, dtype=jnp.int32), N_EXPERTS))
    first_e = jnp.minimum(e_of_visit[0], N_EXPERTS - 1)
    visit_b = visit_of_e[blk_e]
    prev_e = jnp.concatenate([jnp.full((1,), -1, jnp.int32), blk_e[:-1]])
    is_first = used & (blk_e != prev_e)
    nxt_visit = visit_b + 1
    do_cast = is_first & (nxt_visit < n_visits)
    nxt_e = e_of_visit[jnp.minimum(nxt_visit, N_EXPERTS - 1)]
    ne_b = jnp.where(do_cast, nxt_e, -1)
    ne_b = lax.cummax(jnp.concatenate([first_e[None], ne_b]), axis=0)
    one = jnp.ones((1,), jnp.int32)
    zero = jnp.zeros((1,), jnp.int32)
    slot = jnp.concatenate([zero, (visit_b % 2).astype(jnp.int32)])
    comp = jnp.concatenate([zero, used.astype(jnp.int32)])
    cast = jnp.concatenate([one, do_cast.astype(jnp.int32)])
    cslot = jnp.concatenate([zero, ((visit_b + 1) % 2).astype(jnp.int32)])
    sched = (slot, comp, cast, cslot, ne_b.astype(jnp.int32))
    row_asg = jnp.full((n_blk * MOE_ROWS,), -1, jnp.int32).at[pos].set(jnp.arange(n_asg, dtype=jnp.int32))
    return gate, sched, row_asg.reshape(n_blk, 1, MOE_ROWS)


def _grid_pos_embed(rows):
    r = jnp.repeat(jnp.arange(rows, dtype=F32), GRID_W)
    col = jnp.tile(jnp.arange(GRID_W, dtype=F32), rows)
    quarter = D_MODEL // 4
    omega = 1.0 / (10000.0 ** (jnp.arange(quarter, dtype=F32) / quarter))
    ar = r[:, None] * omega
    ac = col[:, None] * omega
    return jnp.concatenate([jnp.sin(ar), jnp.cos(ar), jnp.sin(ac), jnp.cos(ac)], -1)


def kernel(x, c, ctx, c_ctx, w_mod, b_mod, w_in, conv_w, conv_b, gate_bias, mlstm_norm_w, cmlp_norm_w,
           w_s, b_s, w_out, ln1_w, ln1_b, router1_w, router1_b, router2_w, router2_b, w_gate, w_up,
           w_down, ln2_w, ln2_b):
    bsz, seq, dm = x.shape
    ctx_len = ctx.shape[1]
    n_tok = bsz * seq
    assert w_mod.shape[0] == 1 and dm == D_MODEL and seq % MCHUNK == 0 and ctx_len == MCHUNK
    pe = _grid_pos_embed(seq // GRID_W).astype(x.dtype)
    x2d = x.reshape(n_tok, dm)
    ctx2d = ctx.reshape(bsz * ctx_len, dm)

    mod_rows = 16
    cc = jnp.concatenate([c, c_ctx[None, :], jnp.zeros((mod_rows - bsz - 1, dm), c.dtype)], 0)
    mods3 = _modulation(cc, w_mod[0], b_mod[0]).reshape(mod_rows, N_MOD, dm)

    dq = D_MLSTM
    wi = w_in[0]
    w_qkvo = wi[:, :4 * dq].astype(BF16)
    w_g = jnp.pad(wi[:, 4 * dq:4 * dq + N_GATE_COLS], ((0, 0), (0, LANE - N_GATE_COLS))).astype(BF16)
    w_uv = wi[:, 4 * dq + N_GATE_COLS:].astype(BF16)
    bm_proj = 256
    blocks_per_seq = seq // bm_proj
    qkvo, g_x, uv = _projection(x2d, pe, mods3, lambda i: i // blocks_per_seq,
                                [w_qkvo, w_g, w_uv], [BF16, F32, BF16], [False, False, True], bm_proj)
    w_kv = wi[:, dq:3 * dq].astype(BF16)
    kv_c, g_c = _projection(ctx2d, None, mods3, lambda i: bsz, [w_kv, w_g], [BF16, F32],
                            [False, False], bm_proj)

    rowq, colq = _gate_stats(g_x, g_c, gate_bias[0], bsz, seq, ctx_len)
    ym = _mlstm(qkvo, kv_c, conv_w[0], conv_b[0].reshape(1, -1), rowq, colq,
                mlstm_norm_w[0].reshape(1, -1), bsz, seq, ctx_len)

    wr = jnp.pad(jnp.concatenate([router1_w[0], router2_w[0]], 1),
                 ((0, 0), (0, LANE - N_GROUPS - N_EXPERTS))).astype(BF16)
    br = jnp.pad(jnp.concatenate([router1_b[0], router2_b[0]], 0),
                 (0, LANE - N_GROUPS - N_EXPERTS)).reshape(1, LANE)
    x1, h2, logits = _mixer_out(ym, uv, x2d, pe, mods3, cmlp_norm_w[0].reshape(1, -1),
                                w_s[0].astype(BF16), b_s[0].T, w_out[0].astype(BF16),
                                ln1_w[0].reshape(1, -1), ln1_b[0].reshape(1, -1), wr, br, seq, 256)

    gate, sched, row_asg = _route(logits, n_tok)
    ys = _experts(h2, row_asg, sched, w_gate[0], w_up[0], w_down[0])
    gates = jnp.pad(gate.astype(F32), ((0, 0), (0, LANE - 2)))
    out = _final(x1, ys.reshape(-1, 2 * dm), gates, mods3, ln2_w[0].reshape(1, -1),
                 ln2_b[0].reshape(1, -1), seq, 256)
    return out.reshape(bsz, seq, dm)
```

```python
import functools

import jax
import jax.numpy as jnp
from jax import lax
from jax.experimental import pallas as pl
from jax.experimental.pallas import tpu as pltpu

F32 = jnp.float32
BF16 = jnp.bfloat16

D_MODEL = 2048
GRID_W = 64
D_MLSTM = 1024
D_CMLP = 1024
HEADS = 4
HD = 256
CMLP_GROUPS = 4
CMLP_GD = 256
CMLP_CHUNK = 128
N_GROUPS = 4
EXPERTS_PER_GROUP = 8
N_EXPERTS = 32
D_EXPERT = 1024
N_MOD = 6
N_GATE_COLS = 16
DEEPNORM_ALPHA = 2.0 ** 0.25
LN_EPS = 1e-6

LANE = 128
SUBLANES = 8
MCHUNK = 256
MOE_ROWS = 256
MOE_FT = 256
VMEM_LIMIT = 56 * 1024 * 1024


def _cparams(sem):
    return pltpu.CompilerParams(dimension_semantics=sem, vmem_limit_bytes=VMEM_LIMIT)


def _resident(shape):
    nd = len(shape)
    return pl.BlockSpec(shape, lambda *_: (0,) * nd, pipeline_mode=pl.Buffered(1))


def _sigmoid(x):
    return 1.0 / (1.0 + jnp.exp(-x))


def _silu(x):
    return x * _sigmoid(x)


def _log_sigmoid(x):
    return jnp.minimum(x, 0.0) - jnp.log1p(jnp.exp(-jnp.abs(x)))


def _gelu_tanh(x):
    c = 0.7978845608028654
    return 0.5 * x * (1.0 + jnp.tanh(c * (x + 0.044715 * (x * x * x))))


def _layer_norm_rows(z, w, b):
    mu = jnp.mean(z, axis=-1, keepdims=True)
    zc = z - mu
    var = jnp.mean(zc * zc, axis=-1, keepdims=True)
    return zc * lax.rsqrt(var + LN_EPS) * w + b


def _mod_kernel(c_ref, w_ref, b_ref, o_ref):
    s = _silu(c_ref[...]).astype(BF16)
    o_ref[...] = jnp.dot(s, w_ref[...].astype(BF16), preferred_element_type=F32) + b_ref[...]


def _modulation(cc, w_mod, b_mod):
    rows, dm = cc.shape
    n = w_mod.shape[1]
    tn = 1024
    return pl.pallas_call(
        _mod_kernel,
        grid=(n // tn,),
        in_specs=[pl.BlockSpec((rows, dm), lambda j: (0, 0)),
                  pl.BlockSpec((dm, tn), lambda j: (0, j)),
                  pl.BlockSpec((1, tn), lambda j: (0, j))],
        out_specs=pl.BlockSpec((rows, tn), lambda j: (0, j)),
        out_shape=jax.ShapeDtypeStruct((rows, n), F32),
        compiler_params=_cparams(("arbitrary",)),
        name="modulation",
    )(cc, w_mod, b_mod.reshape(1, n))


def _proj_kernel(*refs, n_w, has_pe, gelu_flags, tn):
    x_ref = refs[0]
    k = 1
    pe_ref = None
    if has_pe:
        pe_ref = refs[k]
        k += 1
    m_ref = refs[k]
    k += 1
    w_refs = refs[k:k + n_w]
    o_refs = refs[k + n_w:k + 2 * n_w]
    hx_ref = refs[k + 2 * n_w]
    x = x_ref[...]
    if has_pe:
        x = x + pe_ref[...]
    m = m_ref[0]
    hx_ref[...] = (x * (1.0 + m[1:2, :]) + m[0:1, :]).astype(BF16)
    for w_ref, o_ref, use_gelu in zip(w_refs, o_refs, gelu_flags):
        n = w_ref.shape[1]
        step = min(tn, n)
        for j in range(0, n, step):
            acc = jnp.dot(hx_ref[...], w_ref[:, j:j + step], preferred_element_type=F32)
            if use_gelu:
                acc = _gelu_tanh(acc)
            o_ref[:, j:j + step] = acc.astype(o_ref.dtype)


def _projection(x2d, pe, mods3, mod_row_of_block, weights, out_dtypes, gelu_flags, bm):
    rows, dm = x2d.shape
    has_pe = pe is not None
    n_w = len(weights)
    in_specs = [pl.BlockSpec((bm, dm), lambda i: (i, 0))]
    args = [x2d]
    if has_pe:
        pe_blocks = pe.shape[0] // bm
        in_specs.append(pl.BlockSpec((bm, dm), lambda i: (i % pe_blocks, 0)))
        args.append(pe)
    in_specs.append(pl.BlockSpec((1, N_MOD, dm), lambda i: (mod_row_of_block(i), 0, 0)))
    args.append(mods3)
    for w in weights:
        in_specs.append(_resident(w.shape))
        args.append(w)
    out_specs = [pl.BlockSpec((bm, w.shape[1]), lambda i: (i, 0)) for w in weights]
    out_shape = [jax.ShapeDtypeStruct((rows, w.shape[1]), dt) for w, dt in zip(weights, out_dtypes)]
    kern = functools.partial(_proj_kernel, n_w=n_w, has_pe=has_pe, gelu_flags=tuple(gelu_flags), tn=1024)
    return pl.pallas_call(
        kern,
        grid=(rows // bm,),
        in_specs=in_specs,
        out_specs=out_specs,
        out_shape=out_shape,
        scratch_shapes=[pltpu.VMEM((bm, dm), BF16)],
        compiler_params=_cparams(("parallel",)),
        name="projection",
    )(*args)


def _gate_kernel(lic_ref, lfc_ref, lir_ref, lfr_ref, row_ref, col_ref, *, nc):
    li = lic_ref[0, 0]
    lf = _log_sigmoid(lfc_ref[0, 0])
    length = li.shape[0]
    tid = lax.broadcasted_iota(jnp.int32, li.shape, 0)
    lane = lax.broadcasted_iota(jnp.int32, li.shape, 1)
    lane1 = lane[0:1, :]
    fwd = (lane < nc) | (lane == 2 * nc)

    def scan_sublanes(x, op, fill):
        p = x
        s = x
        k = 1
        while k < length:
            p = op(p, jnp.where(tid >= k, pltpu.roll(p, k, 0), fill))
            s = op(s, jnp.where(tid < length - k, pltpu.roll(s, length - k, 0), fill))
            k *= 2
        return jnp.where(fwd, p, s)

    b = scan_sublanes(lf, jnp.add, 0.0)
    btot = jnp.sum(lf, axis=0, keepdims=True)
    a = btot - b + li
    m_loc = jnp.max(a, axis=0, keepdims=True)
    r = li - b
    cm = scan_sublanes(r, jnp.maximum, -jnp.inf)

    m_ctx = jnp.maximum(btot, m_loc)
    m_in = jnp.where(lane1 == 0, pltpu.roll(m_ctx, LANE - 2 * nc, 1), pltpu.roll(m_ctx, LANE - 2, 1))
    for k in range(nc - 1):
        m_new = jnp.maximum(btot + m_in, m_loc)
        m_in = jnp.where(lane1 == k + 1, pltpu.roll(m_new, 1, 1),
                         jnp.where(lane1 == 2 * nc - 2 - k, pltpu.roll(m_new, LANE - 1, 1), m_in))
    is_ctx = lane1 >= 2 * nc
    m_in = jnp.where(is_ctx, 0.0, m_in)
    m_new = jnp.maximum(btot + m_in, m_loc)
    s_old = jnp.broadcast_to(jnp.exp(btot + m_in - m_new), li.shape)
    w = jnp.exp(a - m_new)
    big_m = jnp.maximum(m_in, cm)
    s_int = jnp.exp(m_in - big_m)
    e_neg = jnp.exp(-(b + big_m))
    g = 2 * nc
    col_ref[0, 0] = jnp.where(
        lane < g, w, jnp.where(
            lane < 2 * g, pltpu.roll(big_m, g, 1), jnp.where(
                lane < 3 * g, pltpu.roll(s_int, 2 * g, 1), jnp.where(
                    lane < 4 * g, pltpu.roll(e_neg, 3 * g, 1), jnp.where(
                        lane < 4 * g + 2, pltpu.roll(w, 3 * g, 1), pltpu.roll(s_old, 4 * g + 2, 1))))))

    lir = lir_ref[0, 0]
    lfr = _log_sigmoid(lfr_ref[0, 0])
    width = lir.shape[1]
    rid = lax.broadcasted_iota(jnp.int32, lir.shape, 0)
    pid = lax.broadcasted_iota(jnp.int32, lir.shape, 1)
    p = lfr
    s = lfr
    k = 1
    while k < width:
        p = p + jnp.where(pid >= k, pltpu.roll(p, k, 1), 0.0)
        s = s + jnp.where(pid < width - k, pltpu.roll(s, width - k, 1), 0.0)
        k *= 2
    row_ref[0, 0] = lir - jnp.where(rid < nc, p, s)


def _gate_stats(g_x, g_c, gate_bias, bsz, seq, ctx_len):
    nc = seq // MCHUNK
    gb = gate_bias.astype(F32)
    gx = g_x[:, :N_GATE_COLS].reshape(bsz, nc, MCHUNK, 2, 2, HEADS) + gb.reshape(2, 2, HEADS)
    gc = g_c[:, :N_GATE_COLS].reshape(bsz, ctx_len, 2, 2, HEADS) + gb.reshape(2, 2, HEADS)
    col_x = gx.transpose(4, 0, 5, 2, 3, 1).reshape(2, bsz, HEADS, MCHUNK, 2 * nc)
    col_c = gc.transpose(3, 0, 4, 1, 2)
    col = jnp.concatenate([col_x, col_c], -1)
    col = jnp.pad(col, ((0, 0),) * 4 + ((0, LANE - col.shape[-1]),))
    row = gx.transpose(4, 0, 5, 3, 1, 2).reshape(2, bsz, HEADS, 2 * nc, MCHUNK)
    blk_c = pl.BlockSpec((1, 1, MCHUNK, LANE), lambda b, h: (b, h, 0, 0))
    blk_r = pl.BlockSpec((1, 1, 2 * nc, MCHUNK), lambda b, h: (b, h, 0, 0))
    return pl.pallas_call(
        functools.partial(_gate_kernel, nc=nc),
        grid=(bsz, HEADS),
        in_specs=[blk_c, blk_c, blk_r, blk_r],
        out_specs=[blk_r, blk_c],
        out_shape=[jax.ShapeDtypeStruct((bsz, HEADS, 2 * nc, MCHUNK), F32),
                   jax.ShapeDtypeStruct((bsz, HEADS, MCHUNK, LANE), F32)],
        compiler_params=_cparams(("parallel", "parallel")),
        name="gate_stats",
    )(col[0], col[1], row[0], row[1])


def _mlstm_kernel(q_ref, k_ref, v_ref, o_ref, kc_ref, vc_ref, cwq_ref, cbq_ref, cwk_ref, cbk_ref,
                  row_ref, col_ref, nw_ref, y_ref, q_s, k_s, kc_s, ct_s, n_s, *, nc):
    lc = MCHUNK

    def conv_silu(x, w, b):
        n = x.shape[0]
        rid = lax.broadcasted_iota(jnp.int32, x.shape, 0)
        xm = jnp.where(rid == 0, 0.0, pltpu.roll(x, 1, 0))
        xp = jnp.where(rid == n - 1, 0.0, pltpu.roll(x, n - 1, 0))
        return _silu(xm * w[0:1, :] + x * w[1:2, :] + xp * w[2:3, :] + b)

    k_scale = HD ** -0.5
    q_s[...] = conv_silu(q_ref[...].astype(F32), cwq_ref[...], cbq_ref[...]).astype(BF16)
    k_s[...] = (conv_silu(k_ref[...].astype(F32), cwk_ref[...], cbk_ref[...]) * k_scale).astype(BF16)
    kc_s[...] = (conv_silu(kc_ref[...].astype(F32), cwk_ref[...], cbk_ref[...]) * k_scale).astype(BF16)

    def col(j):
        return col_ref[0, 0, :, j:j + 1]

    def local_state(kk, vv, wcol):
        vw = (vv.astype(F32) * wcol).astype(BF16)
        ct = jnp.dot(kk.T, vw, preferred_element_type=F32)
        nn = jnp.sum(kk.astype(F32) * wcol, axis=0, keepdims=True)
        return ct, nn

    for d in range(2):
        ct, nn = local_state(kc_s[...], vc_ref[...], col(8 * nc + d))
        order = list(range(nc)) if d == 0 else list(range(nc - 1, -1, -1))
        for pos, c in enumerate(order):
            idx = d * nc + c
            ct_s[idx] = ct.astype(BF16)
            n_s[idx] = nn
            if pos == nc - 1:
                break
            sl = pl.ds(c * lc, lc)
            ctl, nl = local_state(k_s[sl, :], v_ref[sl, :], col(idx))
            s_old = col_ref[0, 0, 0:1, 8 * nc + 2 + idx:8 * nc + 3 + idx]
            ct = s_old * ct + ctl
            nn = s_old * nn + nl

    tid = lax.broadcasted_iota(jnp.int32, (lc, lc), 0)
    sid = lax.broadcasted_iota(jnp.int32, (lc, lc), 1)
    masks = (sid <= tid, sid >= tid)
    for c in range(nc):
        sl = pl.ds(c * lc, lc)
        q = q_s[sl, :]
        kk = k_s[sl, :]
        v = v_ref[sl, :]
        qf = q.astype(F32)
        s = lax.dot_general(q, kk, (((1,), (1,)), ((), ())), preferred_element_type=F32)
        h = None
        for d in range(2):
            idx = d * nc + c
            r = row_ref[0, 0, idx:idx + 1, :]
            big_m = col(2 * nc + idx)
            s_int = col(4 * nc + idx)
            e_neg = col(6 * nc + idx)
            p = jnp.where(masks[d], jnp.exp(r - big_m), 0.0) * s
            den = (jnp.sum(p, axis=-1, keepdims=True)
                   + s_int * jnp.sum(qf * n_s[idx], axis=-1, keepdims=True))
            num = (jnp.dot(p.astype(BF16), v, preferred_element_type=F32)
                   + s_int * jnp.dot(q, ct_s[idx], preferred_element_type=F32))
            hd = num / jnp.maximum(jnp.abs(den), e_neg)
            h = hd if h is None else h + hd
        mu = jnp.mean(h, axis=-1, keepdims=True)
        hc = h - mu
        var = jnp.mean(hc * hc, axis=-1, keepdims=True)
        hn = hc * lax.rsqrt(var + LN_EPS) * nw_ref[...]
        y_ref[sl, :] = (hn * _sigmoid(o_ref[sl, :].astype(F32))).astype(BF16)


def _mlstm(qkvo, kv_ctx, conv_w, conv_b, rowq, colq, norm_w, bsz, seq, ctx_len):
    nc = seq // MCHUNK
    hq = D_MLSTM // HD
    kern = functools.partial(_mlstm_kernel, nc=nc)
    seq_blk = lambda off: pl.BlockSpec((seq, HD), lambda b, h: (b, off + h))
    ctx_blk = lambda off: pl.BlockSpec((ctx_len, HD), lambda b, h: (b, off + h))
    return pl.pallas_call(
        kern,
        grid=(bsz, HEADS),
        in_specs=[seq_blk(0), seq_blk(hq), seq_blk(2 * hq), seq_blk(3 * hq),
                  ctx_blk(0), ctx_blk(hq),
                  pl.BlockSpec((3, HD), lambda b, h: (0, h)),
                  pl.BlockSpec((1, HD), lambda b, h: (0, h)),
                  pl.BlockSpec((3, HD), lambda b, h: (0, hq + h)),
                  pl.BlockSpec((1, HD), lambda b, h: (0, hq + h)),
                  pl.BlockSpec((1, 1, 2 * nc, MCHUNK), lambda b, h: (b, h, 0, 0)),
                  pl.BlockSpec((1, 1, MCHUNK, LANE), lambda b, h: (b, h, 0, 0)),
                  pl.BlockSpec((1, HD), lambda b, h: (0, h))],
        out_specs=pl.BlockSpec((seq, HD), lambda b, h: (b, h)),
        out_shape=jax.ShapeDtypeStruct((bsz * seq, D_MLSTM), BF16),
        scratch_shapes=[pltpu.VMEM((seq, HD), BF16), pltpu.VMEM((seq, HD), BF16),
                        pltpu.VMEM((ctx_len, HD), BF16),
                        pltpu.VMEM((2 * nc, HD, HD), BF16), pltpu.VMEM((2 * nc, 1, HD), F32)],
        compiler_params=_cparams(("parallel", "parallel")),
        name="mlstm",
    )(qkvo, qkvo, qkvo, qkvo, kv_ctx, kv_ctx, conv_w, conv_b, conv_w, conv_b, rowq, colq, norm_w)


def _out_kernel(ym_ref, uv_ref, x_ref, pe_ref, m_ref, cnw_ref, ws_ref, bs_ref, wout_ref,
                l1w_ref, l1b_ref, wr_ref, br_ref, x1_ref, h2_ref, lg_ref, y_s, *, bm):
    m = m_ref[0]
    y_s[:, :D_MLSTM] = ym_ref[...]
    for g in range(CMLP_GROUPS):
        gs = slice(g * CMLP_GD, (g + 1) * CMLP_GD)
        vg = uv_ref[:, D_CMLP + g * CMLP_GD:D_CMLP + (g + 1) * CMLP_GD].astype(F32)
        mu = jnp.mean(vg, axis=-1, keepdims=True)
        vc = vg - mu
        var = jnp.mean(vc * vc, axis=-1, keepdims=True)
        vn = (vc * lax.rsqrt(var + LN_EPS) * cnw_ref[:, gs]).astype(BF16)
        for p in range(bm // CMLP_CHUNK):
            ps = slice(p * CMLP_CHUNK, (p + 1) * CMLP_CHUNK)
            s = jnp.dot(ws_ref[g], vn[ps, :], preferred_element_type=F32) + bs_ref[:, g:g + 1]
            yc = uv_ref[ps, gs].astype(F32) * s
            y_s[ps, D_MLSTM + g * CMLP_GD:D_MLSTM + (g + 1) * CMLP_GD] = yc.astype(BF16)
    y = jnp.dot(y_s[...], wout_ref[...], preferred_element_type=F32)
    z = DEEPNORM_ALPHA * (x_ref[...] + pe_ref[...]) + m[2:3, :] * y
    x1 = _layer_norm_rows(z, l1w_ref[...], l1b_ref[...])
    x1_ref[...] = x1
    h2 = x1 * (1.0 + m[4:5, :]) + m[3:4, :]
    h2_ref[...] = h2
    lg_ref[...] = jnp.dot(h2.astype(BF16), wr_ref[...], preferred_element_type=F32) + br_ref[...]


def _mixer_out(ym, uv, x2d, pe, mods3, cnw, ws, bs_t, wout, l1w, l1b, wr, br, seq, bm):
    rows, dm = x2d.shape
    pe_blocks = seq // bm
    kern = functools.partial(_out_kernel, bm=bm)
    row_blk = lambda n: pl.BlockSpec((bm, n), lambda i: (i, 0))
    return pl.pallas_call(
        kern,
        grid=(rows // bm,),
        in_specs=[row_blk(D_MLSTM), row_blk(2 * D_CMLP), row_blk(dm),
                  pl.BlockSpec((bm, dm), lambda i: (i % pe_blocks, 0)),
                  pl.BlockSpec((1, N_MOD, dm), lambda i: (i // pe_blocks, 0, 0)),
                  _resident(cnw.shape), _resident(ws.shape), _resident(bs_t.shape),
                  _resident(wout.shape), _resident(l1w.shape), _resident(l1b.shape),
                  _resident(wr.shape), _resident(br.shape)],
        out_specs=[row_blk(dm), row_blk(dm), row_blk(LANE)],
        out_shape=[jax.ShapeDtypeStruct((rows, dm), F32),
                   jax.ShapeDtypeStruct((rows, dm), F32),
                   jax.ShapeDtypeStruct((rows, LANE), F32)],
        scratch_shapes=[pltpu.VMEM((bm, D_MLSTM + D_CMLP), BF16)],
        compiler_params=_cparams(("parallel",)),
        name="mixer_out",
    )(ym, uv, x2d, pe, mods3, cnw, ws, bs_t, wout, l1w, l1b, wr, br)


def _moe_kernel(comp_ref, blk_ref, slot_ref, cast_ref, cslot_ref, cexp_ref, ctile_ref,
                gtab_ref, stab_ref, h2_hbm, wgf_ref, wuf_ref, wdf_ref, ys_hbm,
                wg_s, wu_s, wd_s, xbuf, ybuf, gsem, ssem, pend, *, nt):
    del blk_ref, cexp_ref
    s = pl.program_id(0)
    n_steps = pl.num_programs(0)
    groups = MOE_ROWS // SUBLANES
    dm = xbuf.shape[-1]
    n_asg = ys_hbm.shape[0] * SUBLANES - 2 * MOE_ROWS

    def gather_wait(p):
        pltpu.make_async_copy(h2_hbm.at[pl.ds(0, groups)], xbuf.at[p], gsem.at[p]).wait()

    def scatter_wait(p):
        pltpu.make_async_copy(ybuf.at[p], ys_hbm.at[pl.ds(0, groups)], ssem.at[p]).wait()

    @pl.when(s == 0)
    def _():
        pend[0] = 0
        pend[1] = 0

    nxt = jnp.minimum(s + 1, n_steps - 1)

    @pl.when(jnp.logical_and(s + 1 < n_steps, comp_ref[nxt] == 1))
    def _():
        p = (s + 1) % 2

        def body(g, carry):
            for i in range(SUBLANES):
                tok = gtab_ref[0, 0, g * SUBLANES + i]
                pltpu.make_async_copy(h2_hbm.at[tok >> 3, pl.ds(tok & 7, 1)], xbuf.at[p, g, pl.ds(i, 1)],
                                      gsem.at[p]).start()
            return carry

        lax.fori_loop(0, groups, body, 0)

    @pl.when(cast_ref[s] == 1)
    def _():
        cs = cslot_ref[s]
        t = ctile_ref[s]
        for k in range(nt):
            @pl.when(t == k)
            def _():
                wg_s[cs, :, k * MOE_FT:(k + 1) * MOE_FT] = wgf_ref[0].astype(BF16)
                wu_s[cs, :, k * MOE_FT:(k + 1) * MOE_FT] = wuf_ref[0].astype(BF16)
        wd_s[cs, t] = wdf_ref[0].astype(BF16)

    @pl.when(comp_ref[s] == 1)
    def _():
        p = s % 2
        sl = slot_ref[s]
        gather_wait(p)
        x = xbuf[p].reshape(MOE_ROWS, dm).astype(BF16)
        g = jnp.dot(x, wg_s[sl], preferred_element_type=F32)
        u = jnp.dot(x, wu_s[sl], preferred_element_type=F32)
        h = (_silu(g) * u).astype(BF16)
        y = jnp.dot(h, wd_s[sl].reshape(D_EXPERT, dm), preferred_element_type=F32)

        @pl.when(pend[p] == 1)
        def _():
            scatter_wait(p)

        ybuf[p] = y.reshape(groups, SUBLANES, dm)

        def body(g, carry):
            for i in range(SUBLANES):
                d = stab_ref[0, 0, g * SUBLANES + i]
                dst = jnp.where(d < 0, n_asg + p * MOE_ROWS + g * SUBLANES + i, d)
                pltpu.make_async_copy(ybuf.at[p, g, pl.ds(i, 1)], ys_hbm.at[dst >> 3, pl.ds(dst & 7, 1)],
                                      ssem.at[p]).start()
            return carry

        lax.fori_loop(0, groups, body, 0)
        pend[p] = 1

    @pl.when(s == n_steps - 1)
    def _():
        for p in range(2):
            @pl.when(pend[p] == 1)
            def _():
                scatter_wait(p)
                pend[p] = 0

        xbuf[...] = jnp.zeros_like(xbuf)
        fills = [pltpu.make_async_copy(xbuf.at[p], ys_hbm.at[pl.ds(n_asg // SUBLANES + p * groups, groups)],
                                       gsem.at[p]) for p in range(2)]
        for cp in fills:
            cp.start()
        for cp in fills:
            cp.wait()


def _experts(h2, gtab, stab, sched, wg, wu, wd):
    n_tok, dm = h2.shape
    nt = D_EXPERT // MOE_FT
    n_asg = 2 * n_tok
    n_steps = sched[0].shape[0]

    smem_rows = lambda imap: pl.BlockSpec((1, 1, MOE_ROWS), imap, memory_space=pltpu.SMEM)
    grid_spec = pltpu.PrefetchScalarGridSpec(
        num_scalar_prefetch=7,
        grid=(n_steps,),
        in_specs=[smem_rows(lambda s, comp, blk, *_: (blk[jnp.minimum(s + 1, n_steps - 1)], 0, 0)),
                  smem_rows(lambda s, comp, blk, *_: (blk[s], 0, 0)),
                  pl.BlockSpec(memory_space=pl.ANY),
                  pl.BlockSpec((1, dm, MOE_FT), lambda s, c, b, sl, ca, cs, ce, ct: (ce[s], 0, ct[s])),
                  pl.BlockSpec((1, dm, MOE_FT), lambda s, c, b, sl, ca, cs, ce, ct: (ce[s], 0, ct[s])),
                  pl.BlockSpec((1, MOE_FT, dm), lambda s, c, b, sl, ca, cs, ce, ct: (ce[s], ct[s], 0))],
        out_specs=pl.BlockSpec(memory_space=pl.ANY),
        scratch_shapes=[pltpu.VMEM((2, dm, D_EXPERT), BF16), pltpu.VMEM((2, dm, D_EXPERT), BF16),
                        pltpu.VMEM((2, nt, MOE_FT, dm), BF16),
                        pltpu.VMEM((2, MOE_ROWS // SUBLANES, SUBLANES, dm), F32),
                        pltpu.VMEM((2, MOE_ROWS // SUBLANES, SUBLANES, dm), F32),
                        pltpu.SemaphoreType.DMA((2,)), pltpu.SemaphoreType.DMA((2,)),
                        pltpu.SMEM((2,), jnp.int32)],
    )
    return pl.pallas_call(
        functools.partial(_moe_kernel, nt=nt),
        grid_spec=grid_spec,
        out_shape=jax.ShapeDtypeStruct(((n_asg + 2 * MOE_ROWS) // SUBLANES, SUBLANES, dm), F32),
        compiler_params=_cparams(("arbitrary",)),
        name="experts",
    )(*sched, gtab, stab, h2.reshape(n_tok // SUBLANES, SUBLANES, dm), wg, wu, wd)


def _final_kernel(x1_ref, y0_ref, y1_ref, g_ref, m_ref, w_ref, b_ref, o_ref):
    m = m_ref[0]
    y = g_ref[:, 0:1] * y0_ref[...] + g_ref[:, 1:2] * y1_ref[...]
    z = DEEPNORM_ALPHA * x1_ref[...] + m[5:6, :] * y
    o_ref[...] = _layer_norm_rows(z, w_ref[...], b_ref[...])


def _final(x1, ys, gates, mods3, w, b, seq, bm):
    rows, dm = x1.shape
    blocks_per_batch = seq // bm
    slot_blocks = rows // bm
    row_blk = lambda n: pl.BlockSpec((bm, n), lambda i: (i, 0))
    return pl.pallas_call(
        _final_kernel,
        grid=(rows // bm,),
        in_specs=[row_blk(dm), row_blk(dm), pl.BlockSpec((bm, dm), lambda i: (i + slot_blocks, 0)),
                  row_blk(LANE),
                  pl.BlockSpec((1, N_MOD, dm), lambda i: (i // blocks_per_batch, 0, 0)),
                  _resident(w.shape), _resident(b.shape)],
        out_specs=row_blk(dm),
        out_shape=jax.ShapeDtypeStruct((rows, dm), F32),
        compiler_params=_cparams(("parallel",)),
        name="final_ln",
    )(x1, ys, ys, gates, mods3, w, b)


def _route(logits, n_tok):
    logits1 = logits[:, :N_GROUPS]
    grp = jnp.argmax(logits1, -1).astype(jnp.int32)
    gsel = grp[:, None] == jnp.arange(N_GROUPS, dtype=jnp.int32)[None, :]
    p_grp = jnp.sum(jnp.where(gsel, jax.nn.softmax(logits1, -1), 0.0), -1)
    logits2 = logits[:, N_GROUPS:N_GROUPS + N_EXPERTS].reshape(n_tok, N_GROUPS, EXPERTS_PER_GROUP)
    l2 = jnp.sum(jnp.where(gsel[:, :, None], logits2, 0.0), 1)
    eidx = jnp.arange(EXPERTS_PER_GROUP, dtype=jnp.int32)[None, :]
    i0 = jnp.argmax(l2, -1).astype(jnp.int32)
    v0 = jnp.max(l2, -1)
    l2m = jnp.where(eidx == i0[:, None], -jnp.inf, l2)
    i1 = jnp.argmax(l2m, -1).astype(jnp.int32)
    v1 = jnp.max(l2m, -1)
    top_v = jnp.stack([v0, v1], -1)
    gate = p_grp[:, None] * jax.nn.softmax(top_v, -1)
    e_flat = (grp[:, None] * EXPERTS_PER_GROUP + jnp.stack([i0, i1], -1)).reshape(-1)
    n_asg = e_flat.shape[0]
    blk = 256
    earange = jnp.arange(N_EXPERTS, dtype=jnp.int32)
    onehot = (e_flat[:, None] == earange[None, :])
    oh = onehot.astype(BF16).reshape(n_asg // blk, blk, N_EXPERTS)
    tri = (jnp.arange(blk)[:, None] >= jnp.arange(blk)[None, :]).astype(BF16)
    within = jnp.einsum("ts,bse->bte", tri, oh, preferred_element_type=F32)
    totals = within[:, -1, :]
    offs = jnp.cumsum(totals, axis=0) - totals
    csum = (within + offs[:, None, :]).reshape(n_asg, N_EXPERTS)
    counts = (offs[-1] + totals[-1]).astype(jnp.int32)
    rank = jnp.sum(jnp.where(onehot, csum, 0.0), -1).astype(jnp.int32) - 1
    nblk_e = (counts + MOE_ROWS - 1) // MOE_ROWS
    pad_end = jnp.cumsum(nblk_e * MOE_ROWS)
    pad_start = pad_end - nblk_e * MOE_ROWS
    pos = jnp.sum(jnp.where(onehot, pad_start[None, :], 0), -1) + rank
    n_blk = n_asg // MOE_ROWS + N_EXPERTS

    row_asg = jnp.full((n_blk * MOE_ROWS,), -1, jnp.int32).at[pos].set(jnp.arange(n_asg, dtype=jnp.int32))
    gtab = (jnp.maximum(row_asg, 0) >> 1).reshape(n_blk, 1, MOE_ROWS)
    stab = jnp.where(row_asg < 0, -1, (row_asg & 1) * n_tok + (row_asg >> 1)).reshape(n_blk, 1, MOE_ROWS)

    nt = D_EXPERT // MOE_FT
    n_steps = n_blk + (nt - 1) * N_EXPERTS + nt
    has = nblk_e > 0
    n_visits = jnp.sum(has.astype(jnp.int32))
    e_of_visit = jnp.sort(jnp.where(has, earange, N_EXPERTS))
    vsel = e_of_visit[:, None] == earange[None, :]
    nb_v = jnp.sum(jnp.where(vsel, nblk_e[None, :], 0), -1)
    steps_v = jnp.where(nb_v > 0, jnp.maximum(nb_v, nt), 0)
    end_v = nt + jnp.cumsum(steps_v)
    start_v = end_v - steps_v
    first_blk_v = jnp.cumsum(nb_v) - nb_v
    sidx = jnp.arange(n_steps, dtype=jnp.int32)
    v = jnp.sum((end_v[None, :] <= sidx[:, None]).astype(jnp.int32), -1)
    pick = lambda arr, idx: jnp.sum(jnp.where(idx[:, None] == earange[None, :], arr[None, :], 0), -1)
    in_visit = (sidx >= nt) & (v < n_visits)
    k = sidx - pick(start_v, v)
    comp = in_visit & (k < pick(nb_v, v))
    blk_s = lax.cummax(jnp.where(comp, pick(first_blk_v, v) + k, 0), axis=0)
    prologue = sidx < nt
    cast = prologue | (in_visit & (k < nt) & (v + 1 < n_visits))
    cexp = jnp.where(prologue, e_of_visit[0], pick(e_of_visit, v + 1))
    ctile = jnp.where(prologue, sidx, k)
    code = lax.cummax(jnp.where(cast, cexp * nt + ctile, 0), axis=0)
    i32 = lambda a: a.astype(jnp.int32)
    sched = (i32(comp), i32(blk_s), i32(v % 2), i32(cast), i32(jnp.where(prologue, 0, (v + 1) % 2)),
             i32(jnp.minimum(code // nt, N_EXPERTS - 1)), i32(code % nt))
    return gate, sched, gtab, stab


def _grid_pos_embed(rows):
    r = jnp.repeat(jnp.arange(rows, dtype=F32), GRID_W)
    col = jnp.tile(jnp.arange(GRID_W, dtype=F32), rows)
    quarter = D_MODEL // 4
    omega = 1.0 / (10000.0 ** (jnp.arange(quarter, dtype=F32) / quarter))
    ar = r[:, None] * omega
    ac = col[:, None] * omega
    return jnp.concatenate([jnp.sin(ar), jnp.cos(ar), jnp.sin(ac), jnp.cos(ac)], -1)


def kernel(x, c, ctx, c_ctx, w_mod, b_mod, w_in, conv_w, conv_b, gate_bias, mlstm_norm_w, cmlp_norm_w,
           w_s, b_s, w_out, ln1_w, ln1_b, router1_w, router1_b, router2_w, router2_b, w_gate, w_up,
           w_down, ln2_w, ln2_b):
    bsz, seq, dm = x.shape
    ctx_len = ctx.shape[1]
    n_tok = bsz * seq
    assert w_mod.shape[0] == 1 and dm == D_MODEL and seq % MCHUNK == 0 and ctx_len == MCHUNK
    pe = _grid_pos_embed(seq // GRID_W).astype(x.dtype)
    x2d = x.reshape(n_tok, dm)
    ctx2d = ctx.reshape(bsz * ctx_len, dm)

    mod_rows = 16
    cc = jnp.concatenate([c, c_ctx[None, :], jnp.zeros((mod_rows - bsz - 1, dm), c.dtype)], 0)
    mods3 = _modulation(cc, w_mod[0], b_mod[0]).reshape(mod_rows, N_MOD, dm)

    dq = D_MLSTM
    wi = w_in[0]
    w_qkvo = wi[:, :4 * dq].astype(BF16)
    w_g = jnp.pad(wi[:, 4 * dq:4 * dq + N_GATE_COLS], ((0, 0), (0, LANE - N_GATE_COLS))).astype(BF16)
    w_uv = wi[:, 4 * dq + N_GATE_COLS:].astype(BF16)
    bm_proj = 256
    blocks_per_seq = seq // bm_proj
    qkvo, g_x, uv = _projection(x2d, pe, mods3, lambda i: i // blocks_per_seq,
                                [w_qkvo, w_g, w_uv], [BF16, F32, BF16], [False, False, True], bm_proj)
    w_kv = wi[:, dq:3 * dq].astype(BF16)
    kv_c, g_c = _projection(ctx2d, None, mods3, lambda i: bsz, [w_kv, w_g], [BF16, F32],
                            [False, False], bm_proj)

    rowq, colq = _gate_stats(g_x, g_c, gate_bias[0], bsz, seq, ctx_len)
    ym = _mlstm(qkvo, kv_c, conv_w[0], conv_b[0].reshape(1, -1), rowq, colq,
                mlstm_norm_w[0].reshape(1, -1), bsz, seq, ctx_len)

    wr = jnp.pad(jnp.concatenate([router1_w[0], router2_w[0]], 1),
                 ((0, 0), (0, LANE - N_GROUPS - N_EXPERTS))).astype(BF16)
    br = jnp.pad(jnp.concatenate([router1_b[0], router2_b[0]], 0),
                 (0, LANE - N_GROUPS - N_EXPERTS)).reshape(1, LANE)
    x1, h2, logits = _mixer_out(ym, uv, x2d, pe, mods3, cmlp_norm_w[0].reshape(1, -1),
                                w_s[0].astype(BF16), b_s[0].T, w_out[0].astype(BF16),
                                ln1_w[0].reshape(1, -1), ln1_b[0].reshape(1, -1), wr, br, seq, 256)

    gate, sched, gtab, stab = _route(logits, n_tok)
    ys = _experts(h2, gtab, stab, sched, w_gate[0], w_up[0], w_down[0])
    gates = jnp.pad(gate.astype(F32), ((0, 0), (0, LANE - 2)))
    out = _final(x1, ys.reshape(-1, dm), gates, mods3, ln2_w[0].reshape(1, -1),
                 ln2_b[0].reshape(1, -1), seq, 256)
    return out.reshape(bsz, seq, dm)
```

```python
import functools

import jax
import jax.numpy as jnp
from jax import lax
from jax.experimental import pallas as pl
from jax.experimental.pallas import tpu as pltpu

F32 = jnp.float32
BF16 = jnp.bfloat16

D_MODEL = 2048
GRID_W = 64
D_MLSTM = 1024
D_CMLP = 1024
HEADS = 4
HD = 256
CMLP_GROUPS = 4
CMLP_GD = 256
CMLP_CHUNK = 128
N_GROUPS = 4
EXPERTS_PER_GROUP = 8
N_EXPERTS = 32
D_EXPERT = 1024
N_MOD = 6
N_GATE_COLS = 16
DEEPNORM_ALPHA = 2.0 ** 0.25
LN_EPS = 1e-6

LANE = 128
SUBLANES = 8
MCHUNK = 256
MOE_ROWS = 256
MOE_FT = 256
ROW_LANES = D_MODEL // LANE
VMEM_LIMIT = 56 * 1024 * 1024


def _cparams(sem):
    return pltpu.CompilerParams(dimension_semantics=sem, vmem_limit_bytes=VMEM_LIMIT)


def _resident(shape):
    nd = len(shape)
    return pl.BlockSpec(shape, lambda *_: (0,) * nd, pipeline_mode=pl.Buffered(1))


def _sigmoid(x):
    return 1.0 / (1.0 + jnp.exp(-x))


def _silu(x):
    return x * _sigmoid(x)


def _log_sigmoid(x):
    return jnp.minimum(x, 0.0) - jnp.log1p(jnp.exp(-jnp.abs(x)))


def _gelu_tanh(x):
    c = 0.7978845608028654
    return 0.5 * x * (1.0 + jnp.tanh(c * (x + 0.044715 * (x * x * x))))


def _layer_norm_rows(z, w, b):
    mu = jnp.mean(z, axis=-1, keepdims=True)
    zc = z - mu
    var = jnp.mean(zc * zc, axis=-1, keepdims=True)
    return zc * lax.rsqrt(var + LN_EPS) * w + b


def _mod_kernel(c_ref, w_ref, b_ref, o_ref):
    s = _silu(c_ref[...]).astype(BF16)
    o_ref[...] = jnp.dot(s, w_ref[...].astype(BF16), preferred_element_type=F32) + b_ref[...]


def _modulation(cc, w_mod, b_mod):
    rows, dm = cc.shape
    n = w_mod.shape[1]
    tn = 1024
    return pl.pallas_call(
        _mod_kernel,
        grid=(n // tn,),
        in_specs=[pl.BlockSpec((rows, dm), lambda j: (0, 0)),
                  pl.BlockSpec((dm, tn), lambda j: (0, j)),
                  pl.BlockSpec((1, tn), lambda j: (0, j))],
        out_specs=pl.BlockSpec((rows, tn), lambda j: (0, j)),
        out_shape=jax.ShapeDtypeStruct((rows, n), F32),
        compiler_params=_cparams(("arbitrary",)),
        name="modulation",
    )(cc, w_mod, b_mod.reshape(1, n))


def _proj_kernel(*refs, n_w, has_pe, gelu_flags, tn):
    x_ref = refs[0]
    k = 1
    pe_ref = None
    if has_pe:
        pe_ref = refs[k]
        k += 1
    m_ref = refs[k]
    k += 1
    w_refs = refs[k:k + n_w]
    o_refs = refs[k + n_w:k + 2 * n_w]
    hx_ref = refs[k + 2 * n_w]
    x = x_ref[...]
    if has_pe:
        x = x + pe_ref[...]
    m = m_ref[0]
    hx_ref[...] = (x * (1.0 + m[1:2, :]) + m[0:1, :]).astype(BF16)
    for w_ref, o_ref, use_gelu in zip(w_refs, o_refs, gelu_flags):
        n = w_ref.shape[1]
        step = min(tn, n)
        for j in range(0, n, step):
            acc = jnp.dot(hx_ref[...], w_ref[:, j:j + step], preferred_element_type=F32)
            if use_gelu:
                acc = _gelu_tanh(acc)
            o_ref[:, j:j + step] = acc.astype(o_ref.dtype)


def _projection(x2d, pe, mods3, mod_row_of_block, weights, out_dtypes, gelu_flags, bm):
    rows, dm = x2d.shape
    has_pe = pe is not None
    n_w = len(weights)
    in_specs = [pl.BlockSpec((bm, dm), lambda i: (i, 0))]
    args = [x2d]
    if has_pe:
        pe_blocks = pe.shape[0] // bm
        in_specs.append(pl.BlockSpec((bm, dm), lambda i: (i % pe_blocks, 0)))
        args.append(pe)
    in_specs.append(pl.BlockSpec((1, N_MOD, dm), lambda i: (mod_row_of_block(i), 0, 0)))
    args.append(mods3)
    for w in weights:
        in_specs.append(_resident(w.shape))
        args.append(w)
    out_specs = [pl.BlockSpec((bm, w.shape[1]), lambda i: (i, 0)) for w in weights]
    out_shape = [jax.ShapeDtypeStruct((rows, w.shape[1]), dt) for w, dt in zip(weights, out_dtypes)]
    kern = functools.partial(_proj_kernel, n_w=n_w, has_pe=has_pe, gelu_flags=tuple(gelu_flags), tn=1024)
    return pl.pallas_call(
        kern,
        grid=(rows // bm,),
        in_specs=in_specs,
        out_specs=out_specs,
        out_shape=out_shape,
        scratch_shapes=[pltpu.VMEM((bm, dm), BF16)],
        compiler_params=_cparams(("parallel",)),
        name="projection",
    )(*args)


def _gate_kernel(lic_ref, lfc_ref, lir_ref, lfr_ref, row_ref, col_ref, *, nc):
    li = lic_ref[0, 0]
    lf = _log_sigmoid(lfc_ref[0, 0])
    length = li.shape[0]
    tid = lax.broadcasted_iota(jnp.int32, li.shape, 0)
    lane = lax.broadcasted_iota(jnp.int32, li.shape, 1)
    lane1 = lane[0:1, :]
    fwd = (lane < nc) | (lane == 2 * nc)

    def scan_sublanes(x, op, fill):
        p = x
        s = x
        k = 1
        while k < length:
            p = op(p, jnp.where(tid >= k, pltpu.roll(p, k, 0), fill))
            s = op(s, jnp.where(tid < length - k, pltpu.roll(s, length - k, 0), fill))
            k *= 2
        return jnp.where(fwd, p, s)

    b = scan_sublanes(lf, jnp.add, 0.0)
    btot = jnp.sum(lf, axis=0, keepdims=True)
    a = btot - b + li
    m_loc = jnp.max(a, axis=0, keepdims=True)
    r = li - b
    cm = scan_sublanes(r, jnp.maximum, -jnp.inf)

    m_ctx = jnp.maximum(btot, m_loc)
    m_in = jnp.where(lane1 == 0, pltpu.roll(m_ctx, LANE - 2 * nc, 1), pltpu.roll(m_ctx, LANE - 2, 1))
    for k in range(nc - 1):
        m_new = jnp.maximum(btot + m_in, m_loc)
        m_in = jnp.where(lane1 == k + 1, pltpu.roll(m_new, 1, 1),
                         jnp.where(lane1 == 2 * nc - 2 - k, pltpu.roll(m_new, LANE - 1, 1), m_in))
    is_ctx = lane1 >= 2 * nc
    m_in = jnp.where(is_ctx, 0.0, m_in)
    m_new = jnp.maximum(btot + m_in, m_loc)
    s_old = jnp.broadcast_to(jnp.exp(btot + m_in - m_new), li.shape)
    w = jnp.exp(a - m_new)
    big_m = jnp.maximum(m_in, cm)
    s_int = jnp.exp(m_in - big_m)
    e_neg = jnp.exp(-(b + big_m))
    g = 2 * nc
    col_ref[0, 0] = jnp.where(
        lane < g, w, jnp.where(
            lane < 2 * g, pltpu.roll(big_m, g, 1), jnp.where(
                lane < 3 * g, pltpu.roll(s_int, 2 * g, 1), jnp.where(
                    lane < 4 * g, pltpu.roll(e_neg, 3 * g, 1), jnp.where(
                        lane < 4 * g + 2, pltpu.roll(w, 3 * g, 1), pltpu.roll(s_old, 4 * g + 2, 1))))))

    lir = lir_ref[0, 0]
    lfr = _log_sigmoid(lfr_ref[0, 0])
    width = lir.shape[1]
    rid = lax.broadcasted_iota(jnp.int32, lir.shape, 0)
    pid = lax.broadcasted_iota(jnp.int32, lir.shape, 1)
    p = lfr
    s = lfr
    k = 1
    while k < width:
        p = p + jnp.where(pid >= k, pltpu.roll(p, k, 1), 0.0)
        s = s + jnp.where(pid < width - k, pltpu.roll(s, width - k, 1), 0.0)
        k *= 2
    row_ref[0, 0] = lir - jnp.where(rid < nc, p, s)


def _gate_stats(g_x, g_c, gate_bias, bsz, seq, ctx_len):
    nc = seq // MCHUNK
    gb = gate_bias.astype(F32)
    gx = g_x[:, :N_GATE_COLS].reshape(bsz, nc, MCHUNK, 2, 2, HEADS) + gb.reshape(2, 2, HEADS)
    gc = g_c[:, :N_GATE_COLS].reshape(bsz, ctx_len, 2, 2, HEADS) + gb.reshape(2, 2, HEADS)
    col_x = gx.transpose(4, 0, 5, 2, 3, 1).reshape(2, bsz, HEADS, MCHUNK, 2 * nc)
    col_c = gc.transpose(3, 0, 4, 1, 2)
    col = jnp.concatenate([col_x, col_c], -1)
    col = jnp.pad(col, ((0, 0),) * 4 + ((0, LANE - col.shape[-1]),))
    row = gx.transpose(4, 0, 5, 3, 1, 2).reshape(2, bsz, HEADS, 2 * nc, MCHUNK)
    blk_c = pl.BlockSpec((1, 1, MCHUNK, LANE), lambda b, h: (b, h, 0, 0))
    blk_r = pl.BlockSpec((1, 1, 2 * nc, MCHUNK), lambda b, h: (b, h, 0, 0))
    return pl.pallas_call(
        functools.partial(_gate_kernel, nc=nc),
        grid=(bsz, HEADS),
        in_specs=[blk_c, blk_c, blk_r, blk_r],
        out_specs=[blk_r, blk_c],
        out_shape=[jax.ShapeDtypeStruct((bsz, HEADS, 2 * nc, MCHUNK), F32),
                   jax.ShapeDtypeStruct((bsz, HEADS, MCHUNK, LANE), F32)],
        compiler_params=_cparams(("parallel", "parallel")),
        name="gate_stats",
    )(col[0], col[1], row[0], row[1])


def _mlstm_kernel(q_ref, k_ref, v_ref, o_ref, kc_ref, vc_ref, cwq_ref, cbq_ref, cwk_ref, cbk_ref,
                  row_ref, col_ref, nw_ref, y_ref, q_s, k_s, kc_s, ct_s, n_s, *, nc):
    lc = MCHUNK

    def conv_silu(x, w, b):
        n = x.shape[0]
        rid = lax.broadcasted_iota(jnp.int32, x.shape, 0)
        xm = jnp.where(rid == 0, 0.0, pltpu.roll(x, 1, 0))
        xp = jnp.where(rid == n - 1, 0.0, pltpu.roll(x, n - 1, 0))
        return _silu(xm * w[0:1, :] + x * w[1:2, :] + xp * w[2:3, :] + b)

    k_scale = HD ** -0.5
    q_s[...] = conv_silu(q_ref[...].astype(F32), cwq_ref[...], cbq_ref[...]).astype(BF16)
    k_s[...] = (conv_silu(k_ref[...].astype(F32), cwk_ref[...], cbk_ref[...]) * k_scale).astype(BF16)
    kc_s[...] = (conv_silu(kc_ref[...].astype(F32), cwk_ref[...], cbk_ref[...]) * k_scale).astype(BF16)

    def col(j):
        return col_ref[0, 0, :, j:j + 1]

    def local_state(kk, vv, wcol):
        vw = (vv.astype(F32) * wcol).astype(BF16)
        ct = jnp.dot(kk.T, vw, preferred_element_type=F32)
        nn = jnp.sum(kk.astype(F32) * wcol, axis=0, keepdims=True)
        return ct, nn

    for d in range(2):
        ct, nn = local_state(kc_s[...], vc_ref[...], col(8 * nc + d))
        order = list(range(nc)) if d == 0 else list(range(nc - 1, -1, -1))
        for pos, c in enumerate(order):
            idx = d * nc + c
            ct_s[idx] = ct.astype(BF16)
            n_s[idx] = nn
            if pos == nc - 1:
                break
            sl = pl.ds(c * lc, lc)
            ctl, nl = local_state(k_s[sl, :], v_ref[sl, :], col(idx))
            s_old = col_ref[0, 0, 0:1, 8 * nc + 2 + idx:8 * nc + 3 + idx]
            ct = s_old * ct + ctl
            nn = s_old * nn + nl

    tid = lax.broadcasted_iota(jnp.int32, (lc, lc), 0)
    sid = lax.broadcasted_iota(jnp.int32, (lc, lc), 1)
    masks = (sid <= tid, sid >= tid)
    for c in range(nc):
        sl = pl.ds(c * lc, lc)
        q = q_s[sl, :]
        kk = k_s[sl, :]
        v = v_ref[sl, :]
        qf = q.astype(F32)
        s = lax.dot_general(q, kk, (((1,), (1,)), ((), ())), preferred_element_type=F32)
        h = None
        for d in range(2):
            idx = d * nc + c
            r = row_ref[0, 0, idx:idx + 1, :]
            big_m = col(2 * nc + idx)
            s_int = col(4 * nc + idx)
            e_neg = col(6 * nc + idx)
            p = jnp.where(masks[d], jnp.exp(r - big_m), 0.0) * s
            den = (jnp.sum(p, axis=-1, keepdims=True)
                   + s_int * jnp.sum(qf * n_s[idx], axis=-1, keepdims=True))
            num = (jnp.dot(p.astype(BF16), v, preferred_element_type=F32)
                   + s_int * jnp.dot(q, ct_s[idx], preferred_element_type=F32))
            hd = num / jnp.maximum(jnp.abs(den), e_neg)
            h = hd if h is None else h + hd
        mu = jnp.mean(h, axis=-1, keepdims=True)
        hc = h - mu
        var = jnp.mean(hc * hc, axis=-1, keepdims=True)
        hn = hc * lax.rsqrt(var + LN_EPS) * nw_ref[...]
        y_ref[sl, :] = (hn * _sigmoid(o_ref[sl, :].astype(F32))).astype(BF16)


def _mlstm(qkvo, kv_ctx, conv_w, conv_b, rowq, colq, norm_w, bsz, seq, ctx_len):
    nc = seq // MCHUNK
    hq = D_MLSTM // HD
    kern = functools.partial(_mlstm_kernel, nc=nc)
    seq_blk = lambda off: pl.BlockSpec((seq, HD), lambda b, h: (b, off + h))
    ctx_blk = lambda off: pl.BlockSpec((ctx_len, HD), lambda b, h: (b, off + h))
    return pl.pallas_call(
        kern,
        grid=(bsz, HEADS),
        in_specs=[seq_blk(0), seq_blk(hq), seq_blk(2 * hq), seq_blk(3 * hq),
                  ctx_blk(0), ctx_blk(hq),
                  pl.BlockSpec((3, HD), lambda b, h: (0, h)),
                  pl.BlockSpec((1, HD), lambda b, h: (0, h)),
                  pl.BlockSpec((3, HD), lambda b, h: (0, hq + h)),
                  pl.BlockSpec((1, HD), lambda b, h: (0, hq + h)),
                  pl.BlockSpec((1, 1, 2 * nc, MCHUNK), lambda b, h: (b, h, 0, 0)),
                  pl.BlockSpec((1, 1, MCHUNK, LANE), lambda b, h: (b, h, 0, 0)),
                  pl.BlockSpec((1, HD), lambda b, h: (0, h))],
        out_specs=pl.BlockSpec((seq, HD), lambda b, h: (b, h)),
        out_shape=jax.ShapeDtypeStruct((bsz * seq, D_MLSTM), BF16),
        scratch_shapes=[pltpu.VMEM((seq, HD), BF16), pltpu.VMEM((seq, HD), BF16),
                        pltpu.VMEM((ctx_len, HD), BF16),
                        pltpu.VMEM((2 * nc, HD, HD), BF16), pltpu.VMEM((2 * nc, 1, HD), F32)],
        compiler_params=_cparams(("parallel", "parallel")),
        name="mlstm",
    )(qkvo, qkvo, qkvo, qkvo, kv_ctx, kv_ctx, conv_w, conv_b, conv_w, conv_b, rowq, colq, norm_w)


def _out_kernel(ym_ref, uv_ref, x_ref, pe_ref, m_ref, cnw_ref, ws_ref, bs_ref, wout_ref,
                l1w_ref, l1b_ref, wr_ref, br_ref, x1_ref, h2_ref, lg_ref, y_s, *, bm):
    m = m_ref[0]
    y_s[:, :D_MLSTM] = ym_ref[...]
    for g in range(CMLP_GROUPS):
        gs = slice(g * CMLP_GD, (g + 1) * CMLP_GD)
        vg = uv_ref[:, D_CMLP + g * CMLP_GD:D_CMLP + (g + 1) * CMLP_GD].astype(F32)
        mu = jnp.mean(vg, axis=-1, keepdims=True)
        vc = vg - mu
        var = jnp.mean(vc * vc, axis=-1, keepdims=True)
        vn = (vc * lax.rsqrt(var + LN_EPS) * cnw_ref[:, gs]).astype(BF16)
        for p in range(bm // CMLP_CHUNK):
            ps = slice(p * CMLP_CHUNK, (p + 1) * CMLP_CHUNK)
            s = jnp.dot(ws_ref[g], vn[ps, :], preferred_element_type=F32) + bs_ref[:, g:g + 1]
            yc = uv_ref[ps, gs].astype(F32) * s
            y_s[ps, D_MLSTM + g * CMLP_GD:D_MLSTM + (g + 1) * CMLP_GD] = yc.astype(BF16)
    y = jnp.dot(y_s[...], wout_ref[...], preferred_element_type=F32)
    z = DEEPNORM_ALPHA * (x_ref[...] + pe_ref[...]) + m[2:3, :] * y
    x1 = _layer_norm_rows(z, l1w_ref[...], l1b_ref[...])
    x1_ref[...] = x1
    h2 = x1 * (1.0 + m[4:5, :]) + m[3:4, :]
    for c in range(ROW_LANES):
        h2_ref[pl.ds(c, bm, stride=ROW_LANES), :] = h2[:, c * LANE:(c + 1) * LANE]
    lg_ref[...] = jnp.dot(h2.astype(BF16), wr_ref[...], preferred_element_type=F32) + br_ref[...]


def _mixer_out(ym, uv, x2d, pe, mods3, cnw, ws, bs_t, wout, l1w, l1b, wr, br, seq, bm):
    rows, dm = x2d.shape
    pe_blocks = seq // bm
    kern = functools.partial(_out_kernel, bm=bm)
    row_blk = lambda n: pl.BlockSpec((bm, n), lambda i: (i, 0))
    return pl.pallas_call(
        kern,
        grid=(rows // bm,),
        in_specs=[row_blk(D_MLSTM), row_blk(2 * D_CMLP), row_blk(dm),
                  pl.BlockSpec((bm, dm), lambda i: (i % pe_blocks, 0)),
                  pl.BlockSpec((1, N_MOD, dm), lambda i: (i // pe_blocks, 0, 0)),
                  _resident(cnw.shape), _resident(ws.shape), _resident(bs_t.shape),
                  _resident(wout.shape), _resident(l1w.shape), _resident(l1b.shape),
                  _resident(wr.shape), _resident(br.shape)],
        out_specs=[row_blk(dm), pl.BlockSpec((bm * ROW_LANES, LANE), lambda i: (i, 0)), row_blk(LANE)],
        out_shape=[jax.ShapeDtypeStruct((rows, dm), F32),
                   jax.ShapeDtypeStruct((rows * ROW_LANES, LANE), F32),
                   jax.ShapeDtypeStruct((rows, LANE), F32)],
        scratch_shapes=[pltpu.VMEM((bm, D_MLSTM + D_CMLP), BF16)],
        compiler_params=_cparams(("parallel",)),
        name="mixer_out",
    )(ym, uv, x2d, pe, mods3, cnw, ws, bs_t, wout, l1w, l1b, wr, br)


def _moe_kernel(comp_ref, blk_ref, slot_ref, cast_ref, cslot_ref, cexp_ref, ctile_ref,
                gtab_ref, stab_ref, h2_hbm, wgf_ref, wuf_ref, wdf_ref, ys_hbm,
                wg_s, wu_s, wd_s, xbuf, xb_s, ybuf, gsem, ssem, pend, *, nt):
    del blk_ref, cexp_ref
    s = pl.program_id(0)
    n_steps = pl.num_programs(0)
    rl = ROW_LANES
    buf_rows = MOE_ROWS * rl
    n_asg = ys_hbm.shape[0] // rl - 2 * MOE_ROWS

    def gather_wait(p):
        pltpu.make_async_copy(h2_hbm.at[pl.ds(0, buf_rows)], xbuf.at[p], gsem.at[p]).wait()

    def scatter_wait(p):
        pltpu.make_async_copy(ybuf.at[p], ys_hbm.at[pl.ds(0, buf_rows)], ssem.at[p]).wait()

    @pl.when(s == 0)
    def _():
        pend[0] = 0
        pend[1] = 0

    nxt = jnp.minimum(s + 1, n_steps - 1)

    @pl.when(jnp.logical_and(s + 1 < n_steps, comp_ref[nxt] == 1))
    def _():
        p = (s + 1) % 2

        def body(g, carry):
            for i in range(SUBLANES):
                r = g * SUBLANES + i
                src = pl.multiple_of(gtab_ref[0, 0, r] * rl, rl)
                dst = pl.multiple_of(r * rl, rl)
                pltpu.make_async_copy(h2_hbm.at[pl.ds(src, rl)], xbuf.at[p, pl.ds(dst, rl)],
                                      gsem.at[p]).start()
            return carry

        lax.fori_loop(0, MOE_ROWS // SUBLANES, body, 0)

    @pl.when(cast_ref[s] == 1)
    def _():
        cs = cslot_ref[s]
        t = ctile_ref[s]
        for k in range(nt):
            @pl.when(t == k)
            def _():
                wg_s[cs, :, k * MOE_FT:(k + 1) * MOE_FT] = wgf_ref[0].astype(BF16)
                wu_s[cs, :, k * MOE_FT:(k + 1) * MOE_FT] = wuf_ref[0].astype(BF16)
        wd_s[cs, t] = wdf_ref[0].astype(BF16)

    @pl.when(comp_ref[s] == 1)
    def _():
        p = s % 2
        sl = slot_ref[s]
        gather_wait(p)
        for c in range(rl):
            xb_s[:, c * LANE:(c + 1) * LANE] = xbuf[p, pl.ds(c, MOE_ROWS, stride=rl), :].astype(BF16)
        x = xb_s[...]
        dm = x.shape[1]
        g = jnp.dot(x, wg_s[sl], preferred_element_type=F32)
        u = jnp.dot(x, wu_s[sl], preferred_element_type=F32)
        h = (_silu(g) * u).astype(BF16)
        y = jnp.dot(h, wd_s[sl].reshape(D_EXPERT, dm), preferred_element_type=F32)

        @pl.when(pend[p] == 1)
        def _():
            scatter_wait(p)

        for c in range(rl):
            ybuf[p, pl.ds(c, MOE_ROWS, stride=rl), :] = y[:, c * LANE:(c + 1) * LANE]

        def body(g, carry):
            for i in range(SUBLANES):
                r = g * SUBLANES + i
                d = stab_ref[0, 0, r]
                dst = pl.multiple_of(jnp.where(d < 0, n_asg + p * MOE_ROWS + r, d) * rl, rl)
                src = pl.multiple_of(r * rl, rl)
                pltpu.make_async_copy(ybuf.at[p, pl.ds(src, rl)], ys_hbm.at[pl.ds(dst, rl)],
                                      ssem.at[p]).start()
            return carry

        lax.fori_loop(0, MOE_ROWS // SUBLANES, body, 0)
        pend[p] = 1

    @pl.when(s == n_steps - 1)
    def _():
        for p in range(2):
            @pl.when(pend[p] == 1)
            def _():
                scatter_wait(p)
                pend[p] = 0

        xbuf[...] = jnp.zeros_like(xbuf)
        fills = [pltpu.make_async_copy(xbuf.at[p], ys_hbm.at[pl.ds((n_asg + p * MOE_ROWS) * rl, buf_rows)],
                                       gsem.at[p]) for p in range(2)]
        for cp in fills:
            cp.start()
        for cp in fills:
            cp.wait()


def _experts(h2, gtab, stab, sched, wg, wu, wd):
    n_tok, dm = h2.shape[0] // ROW_LANES, D_MODEL
    nt = D_EXPERT // MOE_FT
    n_asg = 2 * n_tok
    n_steps = sched[0].shape[0]

    smem_rows = lambda imap: pl.BlockSpec((1, 1, MOE_ROWS), imap, memory_space=pltpu.SMEM)
    grid_spec = pltpu.PrefetchScalarGridSpec(
        num_scalar_prefetch=7,
        grid=(n_steps,),
        in_specs=[smem_rows(lambda s, comp, blk, *_: (blk[jnp.minimum(s + 1, n_steps - 1)], 0, 0)),
                  smem_rows(lambda s, comp, blk, *_: (blk[s], 0, 0)),
                  pl.BlockSpec(memory_space=pl.ANY),
                  pl.BlockSpec((1, dm, MOE_FT), lambda s, c, b, sl, ca, cs, ce, ct: (ce[s], 0, ct[s])),
                  pl.BlockSpec((1, dm, MOE_FT), lambda s, c, b, sl, ca, cs, ce, ct: (ce[s], 0, ct[s])),
                  pl.BlockSpec((1, MOE_FT, dm), lambda s, c, b, sl, ca, cs, ce, ct: (ce[s], ct[s], 0))],
        out_specs=pl.BlockSpec(memory_space=pl.ANY),
        scratch_shapes=[pltpu.VMEM((2, dm, D_EXPERT), BF16), pltpu.VMEM((2, dm, D_EXPERT), BF16),
                        pltpu.VMEM((2, nt, MOE_FT, dm), BF16),
                        pltpu.VMEM((2, MOE_ROWS * ROW_LANES, LANE), F32),
                        pltpu.VMEM((MOE_ROWS, dm), BF16),
                        pltpu.VMEM((2, MOE_ROWS * ROW_LANES, LANE), F32),
                        pltpu.SemaphoreType.DMA((2,)), pltpu.SemaphoreType.DMA((2,)),
                        pltpu.SMEM((2,), jnp.int32)],
    )
    return pl.pallas_call(
        functools.partial(_moe_kernel, nt=nt),
        grid_spec=grid_spec,
        out_shape=jax.ShapeDtypeStruct(((n_asg + 2 * MOE_ROWS) * ROW_LANES, LANE), F32),
        compiler_params=_cparams(("arbitrary",)),
        name="experts",
    )(*sched, gtab, stab, h2, wg, wu, wd)


def _final_kernel(x1_ref, y0_ref, y1_ref, g_ref, m_ref, w_ref, b_ref, o_ref, y_s):
    m = m_ref[0]
    bm = x1_ref.shape[0]
    g0 = g_ref[:, 0:1]
    g1 = g_ref[:, 1:2]
    for c in range(ROW_LANES):
        rows_c = pl.ds(c, bm, stride=ROW_LANES)
        y_s[:, c * LANE:(c + 1) * LANE] = g0 * y0_ref[rows_c, :] + g1 * y1_ref[rows_c, :]
    z = DEEPNORM_ALPHA * x1_ref[...] + m[5:6, :] * y_s[...]
    o_ref[...] = _layer_norm_rows(z, w_ref[...], b_ref[...])


def _final(x1, ys, gates, mods3, w, b, seq, bm):
    rows, dm = x1.shape
    blocks_per_batch = seq // bm
    slot_blocks = rows // bm
    row_blk = lambda n: pl.BlockSpec((bm, n), lambda i: (i, 0))
    return pl.pallas_call(
        _final_kernel,
        grid=(rows // bm,),
        in_specs=[row_blk(dm), pl.BlockSpec((bm * ROW_LANES, LANE), lambda i: (i, 0)),
                  pl.BlockSpec((bm * ROW_LANES, LANE), lambda i: (i + slot_blocks, 0)),
                  row_blk(LANE),
                  pl.BlockSpec((1, N_MOD, dm), lambda i: (i // blocks_per_batch, 0, 0)),
                  _resident(w.shape), _resident(b.shape)],
        out_specs=row_blk(dm),
        out_shape=jax.ShapeDtypeStruct((rows, dm), F32),
        scratch_shapes=[pltpu.VMEM((bm, dm), F32)],
        compiler_params=_cparams(("parallel",)),
        name="final_ln",
    )(x1, ys, ys, gates, mods3, w, b)


def _route(logits, n_tok):
    logits1 = logits[:, :N_GROUPS]
    grp = jnp.argmax(logits1, -1).astype(jnp.int32)
    gsel = grp[:, None] == jnp.arange(N_GROUPS, dtype=jnp.int32)[None, :]
    p_grp = jnp.sum(jnp.where(gsel, jax.nn.softmax(logits1, -1), 0.0), -1)
    logits2 = logits[:, N_GROUPS:N_GROUPS + N_EXPERTS].reshape(n_tok, N_GROUPS, EXPERTS_PER_GROUP)
    l2 = jnp.sum(jnp.where(gsel[:, :, None], logits2, 0.0), 1)
    eidx = jnp.arange(EXPERTS_PER_GROUP, dtype=jnp.int32)[None, :]
    i0 = jnp.argmax(l2, -1).astype(jnp.int32)
    v0 = jnp.max(l2, -1)
    l2m = jnp.where(eidx == i0[:, None], -jnp.inf, l2)
    i1 = jnp.argmax(l2m, -1).astype(jnp.int32)
    v1 = jnp.max(l2m, -1)
    top_v = jnp.stack([v0, v1], -1)
    gate = p_grp[:, None] * jax.nn.softmax(top_v, -1)
    e_flat = (grp[:, None] * EXPERTS_PER_GROUP + jnp.stack([i0, i1], -1)).reshape(-1)
    n_asg = e_flat.shape[0]
    blk = 256
    earange = jnp.arange(N_EXPERTS, dtype=jnp.int32)
    onehot = (e_flat[:, None] == earange[None, :])
    oh = onehot.astype(BF16).reshape(n_asg // blk, blk, N_EXPERTS)
    tri = (jnp.arange(blk)[:, None] >= jnp.arange(blk)[None, :]).astype(BF16)
    within = jnp.einsum("ts,bse->bte", tri, oh, preferred_element_type=F32)
    totals = within[:, -1, :]
    offs = jnp.cumsum(totals, axis=0) - totals
    csum = (within + offs[:, None, :]).reshape(n_asg, N_EXPERTS)
    counts = (offs[-1] + totals[-1]).astype(jnp.int32)
    rank = jnp.sum(jnp.where(onehot, csum, 0.0), -1).astype(jnp.int32) - 1
    nblk_e = (counts + MOE_ROWS - 1) // MOE_ROWS
    pad_end = jnp.cumsum(nblk_e * MOE_ROWS)
    pad_start = pad_end - nblk_e * MOE_ROWS
    pos = jnp.sum(jnp.where(onehot, pad_start[None, :], 0), -1) + rank
    n_blk = n_asg // MOE_ROWS + N_EXPERTS

    row_asg = jnp.full((n_blk * MOE_ROWS,), -1, jnp.int32).at[pos].set(
        jnp.arange(n_asg, dtype=jnp.int32), unique_indices=True)
    gtab = (jnp.maximum(row_asg, 0) >> 1).reshape(n_blk, 1, MOE_ROWS)
    stab = jnp.where(row_asg < 0, -1, (row_asg & 1) * n_tok + (row_asg >> 1)).reshape(n_blk, 1, MOE_ROWS)

    nt = D_EXPERT // MOE_FT
    n_steps = n_blk + (nt - 1) * N_EXPERTS + nt
    has = nblk_e > 0
    n_visits = jnp.sum(has.astype(jnp.int32))
    e_of_visit = jnp.sort(jnp.where(has, earange, N_EXPERTS))
    vsel = e_of_visit[:, None] == earange[None, :]
    nb_v = jnp.sum(jnp.where(vsel, nblk_e[None, :], 0), -1)
    steps_v = jnp.where(nb_v > 0, jnp.maximum(nb_v, nt), 0)
    end_v = nt + jnp.cumsum(steps_v)
    start_v = end_v - steps_v
    first_blk_v = jnp.cumsum(nb_v) - nb_v
    sidx = jnp.arange(n_steps, dtype=jnp.int32)
    v = jnp.sum((end_v[None, :] <= sidx[:, None]).astype(jnp.int32), -1)
    pick = lambda arr, idx: jnp.sum(jnp.where(idx[:, None] == earange[None, :], arr[None, :], 0), -1)
    in_visit = (sidx >= nt) & (v < n_visits)
    k = sidx - pick(start_v, v)
    comp = in_visit & (k < pick(nb_v, v))
    blk_s = lax.cummax(jnp.where(comp, pick(first_blk_v, v) + k, 0), axis=0)
    prologue = sidx < nt
    cast = prologue | (in_visit & (k < nt) & (v + 1 < n_visits))
    cexp = jnp.where(prologue, e_of_visit[0], pick(e_of_visit, v + 1))
    ctile = jnp.where(prologue, sidx, k)
    code = lax.cummax(jnp.where(cast, cexp * nt + ctile, 0), axis=0)
    i32 = lambda a: a.astype(jnp.int32)
    sched = (i32(comp), i32(blk_s), i32(v % 2), i32(cast), i32(jnp.where(prologue, 0, (v + 1) % 2)),
             i32(jnp.minimum(code // nt, N_EXPERTS - 1)), i32(code % nt))
    return gate, sched, gtab, stab


def _grid_pos_embed(rows):
    quarter = D_MODEL // 4
    omega = 1.0 / (10000.0 ** (jnp.arange(quarter, dtype=F32) / quarter))
    ar = jnp.arange(rows, dtype=F32)[:, None] * omega
    ac = jnp.arange(GRID_W, dtype=F32)[:, None] * omega
    shape = (rows, GRID_W, quarter)
    parts = [jnp.broadcast_to(jnp.sin(ar)[:, None, :], shape), jnp.broadcast_to(jnp.cos(ar)[:, None, :], shape),
             jnp.broadcast_to(jnp.sin(ac)[None, :, :], shape), jnp.broadcast_to(jnp.cos(ac)[None, :, :], shape)]
    return jnp.concatenate(parts, -1).reshape(rows * GRID_W, D_MODEL)


def kernel(x, c, ctx, c_ctx, w_mod, b_mod, w_in, conv_w, conv_b, gate_bias, mlstm_norm_w, cmlp_norm_w,
           w_s, b_s, w_out, ln1_w, ln1_b, router1_w, router1_b, router2_w, router2_b, w_gate, w_up,
           w_down, ln2_w, ln2_b):
    bsz, seq, dm = x.shape
    ctx_len = ctx.shape[1]
    n_tok = bsz * seq
    assert w_mod.shape[0] == 1 and dm == D_MODEL and seq % MCHUNK == 0 and ctx_len == MCHUNK
    pe = _grid_pos_embed(seq // GRID_W).astype(x.dtype)
    x2d = x.reshape(n_tok, dm)
    ctx2d = ctx.reshape(bsz * ctx_len, dm)

    mod_rows = 16
    cc = jnp.concatenate([c, c_ctx[None, :], jnp.zeros((mod_rows - bsz - 1, dm), c.dtype)], 0)
    mods3 = _modulation(cc, w_mod[0], b_mod[0]).reshape(mod_rows, N_MOD, dm)

    dq = D_MLSTM
    wi = w_in[0]
    w_qkvo = wi[:, :4 * dq].astype(BF16)
    w_g = jnp.pad(wi[:, 4 * dq:4 * dq + N_GATE_COLS], ((0, 0), (0, LANE - N_GATE_COLS))).astype(BF16)
    w_uv = wi[:, 4 * dq + N_GATE_COLS:].astype(BF16)
    bm_proj = 256
    blocks_per_seq = seq // bm_proj
    qkvo, g_x, uv = _projection(x2d, pe, mods3, lambda i: i // blocks_per_seq,
                                [w_qkvo, w_g, w_uv], [BF16, F32, BF16], [False, False, True], bm_proj)
    w_kv = wi[:, dq:3 * dq].astype(BF16)
    kv_c, g_c = _projection(ctx2d, None, mods3, lambda i: bsz, [w_kv, w_g], [BF16, F32],
                            [False, False], bm_proj)

    rowq, colq = _gate_stats(g_x, g_c, gate_bias[0], bsz, seq, ctx_len)
    ym = _mlstm(qkvo, kv_c, conv_w[0], conv_b[0].reshape(1, -1), rowq, colq,
                mlstm_norm_w[0].reshape(1, -1), bsz, seq, ctx_len)

    wr = jnp.pad(jnp.concatenate([router1_w[0], router2_w[0]], 1),
                 ((0, 0), (0, LANE - N_GROUPS - N_EXPERTS))).astype(BF16)
    br = jnp.pad(jnp.concatenate([router1_b[0], router2_b[0]], 0),
                 (0, LANE - N_GROUPS - N_EXPERTS)).reshape(1, LANE)
    x1, h2, logits = _mixer_out(ym, uv, x2d, pe, mods3, cmlp_norm_w[0].reshape(1, -1),
                                w_s[0].astype(BF16), b_s[0].T, w_out[0].astype(BF16),
                                ln1_w[0].reshape(1, -1), ln1_b[0].reshape(1, -1), wr, br, seq, 256)

    gate, sched, gtab, stab = _route(logits, n_tok)
    ys = _experts(h2, gtab, stab, sched, w_gate[0], w_up[0], w_down[0])
    gates = jnp.pad(gate.astype(F32), ((0, 0), (0, LANE - 2)))
    out = _final(x1, ys, gates, mods3, ln2_w[0].reshape(1, -1), ln2_b[0].reshape(1, -1), seq, 256)
    return out.reshape(bsz, seq, dm)
```

```python
import functools

import jax
import jax.numpy as jnp
from jax import lax
from jax.experimental import pallas as pl
from jax.experimental.pallas import tpu as pltpu

F32 = jnp.float32
BF16 = jnp.bfloat16

D_MODEL = 2048
GRID_W = 64
D_MLSTM = 1024
D_CMLP = 1024
HEADS = 4
HD = 256
CMLP_GROUPS = 4
CMLP_GD = 256
CMLP_CHUNK = 128
N_GROUPS = 4
EXPERTS_PER_GROUP = 8
N_EXPERTS = 32
D_EXPERT = 1024
N_MOD = 6
N_GATE_COLS = 16
DEEPNORM_ALPHA = 2.0 ** 0.25
LN_EPS = 1e-6

LANE = 128
SUBLANES = 8
MCHUNK = 256
MOE_ROWS = 256
MOE_FT = 256
VMEM_LIMIT = 56 * 1024 * 1024


def _cparams(sem):
    return pltpu.CompilerParams(dimension_semantics=sem, vmem_limit_bytes=VMEM_LIMIT)


def _resident(shape):
    nd = len(shape)
    return pl.BlockSpec(shape, lambda *_: (0,) * nd, pipeline_mode=pl.Buffered(1))


def _sigmoid(x):
    return 1.0 / (1.0 + jnp.exp(-x))


def _silu(x):
    return x * _sigmoid(x)


def _log_sigmoid(x):
    return jnp.minimum(x, 0.0) - jnp.log1p(jnp.exp(-jnp.abs(x)))


def _gelu_tanh(x):
    c = 0.7978845608028654
    return 0.5 * x * (1.0 + jnp.tanh(c * (x + 0.044715 * (x * x * x))))


def _pack_bf16_pairs(lo, hi):
    lo_b = lax.bitcast_convert_type(lo.astype(BF16).astype(F32), jnp.uint32)
    hi_b = lax.bitcast_convert_type(hi.astype(BF16).astype(F32), jnp.uint32)
    return (lo_b >> 16) | (hi_b & jnp.uint32(0xFFFF0000))


def _unpack_bf16_pairs(w):
    lo = lax.bitcast_convert_type(w << 16, F32)
    hi = lax.bitcast_convert_type(w & jnp.uint32(0xFFFF0000), F32)
    return lo, hi


def _layer_norm_rows(z, w, b):
    mu = jnp.mean(z, axis=-1, keepdims=True)
    zc = z - mu
    var = jnp.mean(zc * zc, axis=-1, keepdims=True)
    return zc * lax.rsqrt(var + LN_EPS) * w + b


def _mod_kernel(c_ref, w_ref, b_ref, o_ref):
    s = _silu(c_ref[...]).astype(BF16)
    o_ref[...] = jnp.dot(s, w_ref[...].astype(BF16), preferred_element_type=F32) + b_ref[...]


def _modulation(cc, w_mod, b_mod):
    rows, dm = cc.shape
    n = w_mod.shape[1]
    tn = 1024
    return pl.pallas_call(
        _mod_kernel,
        grid=(n // tn,),
        in_specs=[pl.BlockSpec((rows, dm), lambda j: (0, 0)),
                  pl.BlockSpec((dm, tn), lambda j: (0, j)),
                  pl.BlockSpec((1, tn), lambda j: (0, j))],
        out_specs=pl.BlockSpec((rows, tn), lambda j: (0, j)),
        out_shape=jax.ShapeDtypeStruct((rows, n), F32),
        compiler_params=_cparams(("arbitrary",)),
        name="modulation",
    )(cc, w_mod, b_mod.reshape(1, n))


def _proj_kernel(*refs, n_w, has_pe, gelu_flags, tn):
    x_ref = refs[0]
    k = 1
    pe_ref = None
    if has_pe:
        pe_ref = refs[k]
        k += 1
    m_ref = refs[k]
    k += 1
    w_refs = refs[k:k + n_w]
    o_refs = refs[k + n_w:k + 2 * n_w]
    hx_ref = refs[k + 2 * n_w]
    x = x_ref[...]
    if has_pe:
        x = x + pe_ref[...]
    m = m_ref[0]
    hx_ref[...] = (x * (1.0 + m[1:2, :]) + m[0:1, :]).astype(BF16)
    for w_ref, o_ref, use_gelu in zip(w_refs, o_refs, gelu_flags):
        n = w_ref.shape[1]
        step = min(tn, n)
        for j in range(0, n, step):
            acc = jnp.dot(hx_ref[...], w_ref[:, j:j + step], preferred_element_type=F32)
            if use_gelu:
                acc = _gelu_tanh(acc)
            o_ref[:, j:j + step] = acc.astype(o_ref.dtype)


def _projection(x2d, pe, mods3, mod_row_of_block, weights, out_dtypes, gelu_flags, bm):
    rows, dm = x2d.shape
    has_pe = pe is not None
    n_w = len(weights)
    in_specs = [pl.BlockSpec((bm, dm), lambda i: (i, 0))]
    args = [x2d]
    if has_pe:
        pe_blocks = pe.shape[0] // bm
        in_specs.append(pl.BlockSpec((bm, dm), lambda i: (i % pe_blocks, 0)))
        args.append(pe)
    in_specs.append(pl.BlockSpec((1, N_MOD, dm), lambda i: (mod_row_of_block(i), 0, 0)))
    args.append(mods3)
    for w in weights:
        in_specs.append(_resident(w.shape))
        args.append(w)
    out_specs = [pl.BlockSpec((bm, w.shape[1]), lambda i: (i, 0)) for w in weights]
    out_shape = [jax.ShapeDtypeStruct((rows, w.shape[1]), dt) for w, dt in zip(weights, out_dtypes)]
    kern = functools.partial(_proj_kernel, n_w=n_w, has_pe=has_pe, gelu_flags=tuple(gelu_flags), tn=1024)
    return pl.pallas_call(
        kern,
        grid=(rows // bm,),
        in_specs=in_specs,
        out_specs=out_specs,
        out_shape=out_shape,
        scratch_shapes=[pltpu.VMEM((bm, dm), BF16)],
        compiler_params=_cparams(("parallel",)),
        name="projection",
    )(*args)


def _gate_kernel(lic_ref, lfc_ref, lir_ref, lfr_ref, row_ref, col_ref, *, nc):
    li = lic_ref[0, 0]
    lf = _log_sigmoid(lfc_ref[0, 0])
    length = li.shape[0]
    tid = lax.broadcasted_iota(jnp.int32, li.shape, 0)
    lane = lax.broadcasted_iota(jnp.int32, li.shape, 1)
    lane1 = lane[0:1, :]
    fwd = (lane < nc) | (lane == 2 * nc)

    def scan_sublanes(x, op, fill):
        p = x
        s = x
        k = 1
        while k < length:
            p = op(p, jnp.where(tid >= k, pltpu.roll(p, k, 0), fill))
            s = op(s, jnp.where(tid < length - k, pltpu.roll(s, length - k, 0), fill))
            k *= 2
        return jnp.where(fwd, p, s)

    b = scan_sublanes(lf, jnp.add, 0.0)
    btot = jnp.sum(lf, axis=0, keepdims=True)
    a = btot - b + li
    m_loc = jnp.max(a, axis=0, keepdims=True)
    r = li - b
    cm = scan_sublanes(r, jnp.maximum, -jnp.inf)

    m_ctx = jnp.maximum(btot, m_loc)
    m_in = jnp.where(lane1 == 0, pltpu.roll(m_ctx, LANE - 2 * nc, 1), pltpu.roll(m_ctx, LANE - 2, 1))
    for k in range(nc - 1):
        m_new = jnp.maximum(btot + m_in, m_loc)
        m_in = jnp.where(lane1 == k + 1, pltpu.roll(m_new, 1, 1),
                         jnp.where(lane1 == 2 * nc - 2 - k, pltpu.roll(m_new, LANE - 1, 1), m_in))
    is_ctx = lane1 >= 2 * nc
    m_in = jnp.where(is_ctx, 0.0, m_in)
    m_new = jnp.maximum(btot + m_in, m_loc)
    s_old = jnp.broadcast_to(jnp.exp(btot + m_in - m_new), li.shape)
    w = jnp.exp(a - m_new)
    big_m = jnp.maximum(m_in, cm)
    s_int = jnp.exp(m_in - big_m)
    e_neg = jnp.exp(-(b + big_m))
    g = 2 * nc
    col_ref[0, 0] = jnp.where(
        lane < g, w, jnp.where(
            lane < 2 * g, pltpu.roll(big_m, g, 1), jnp.where(
                lane < 3 * g, pltpu.roll(s_int, 2 * g, 1), jnp.where(
                    lane < 4 * g, pltpu.roll(e_neg, 3 * g, 1), jnp.where(
                        lane < 4 * g + 2, pltpu.roll(w, 3 * g, 1), pltpu.roll(s_old, 4 * g + 2, 1))))))

    lir = lir_ref[0, 0]
    lfr = _log_sigmoid(lfr_ref[0, 0])
    width = lir.shape[1]
    rid = lax.broadcasted_iota(jnp.int32, lir.shape, 0)
    pid = lax.broadcasted_iota(jnp.int32, lir.shape, 1)
    p = lfr
    s = lfr
    k = 1
    while k < width:
        p = p + jnp.where(pid >= k, pltpu.roll(p, k, 1), 0.0)
        s = s + jnp.where(pid < width - k, pltpu.roll(s, width - k, 1), 0.0)
        k *= 2
    row_ref[0, 0] = lir - jnp.where(rid < nc, p, s)


def _gate_stats(g_x, g_c, gate_bias, bsz, seq, ctx_len):
    nc = seq // MCHUNK
    gb = gate_bias.astype(F32)
    gx = g_x[:, :N_GATE_COLS].reshape(bsz, nc, MCHUNK, 2, 2, HEADS) + gb.reshape(2, 2, HEADS)
    gc = g_c[:, :N_GATE_COLS].reshape(bsz, ctx_len, 2, 2, HEADS) + gb.reshape(2, 2, HEADS)
    col_x = gx.transpose(4, 0, 5, 2, 3, 1).reshape(2, bsz, HEADS, MCHUNK, 2 * nc)
    col_c = gc.transpose(3, 0, 4, 1, 2)
    col = jnp.concatenate([col_x, col_c], -1)
    col = jnp.pad(col, ((0, 0),) * 4 + ((0, LANE - col.shape[-1]),))
    row = gx.transpose(4, 0, 5, 3, 1, 2).reshape(2, bsz, HEADS, 2 * nc, MCHUNK)
    blk_c = pl.BlockSpec((1, 1, MCHUNK, LANE), lambda b, h: (b, h, 0, 0))
    blk_r = pl.BlockSpec((1, 1, 2 * nc, MCHUNK), lambda b, h: (b, h, 0, 0))
    return pl.pallas_call(
        functools.partial(_gate_kernel, nc=nc),
        grid=(bsz, HEADS),
        in_specs=[blk_c, blk_c, blk_r, blk_r],
        out_specs=[blk_r, blk_c],
        out_shape=[jax.ShapeDtypeStruct((bsz, HEADS, 2 * nc, MCHUNK), F32),
                   jax.ShapeDtypeStruct((bsz, HEADS, MCHUNK, LANE), F32)],
        compiler_params=_cparams(("parallel", "parallel")),
        name="gate_stats",
    )(col[0], col[1], row[0], row[1])


def _mlstm_kernel(q_ref, k_ref, v_ref, o_ref, kc_ref, vc_ref, cwq_ref, cbq_ref, cwk_ref, cbk_ref,
                  row_ref, col_ref, nw_ref, y_ref, q_s, k_s, kc_s, ct_s, n_s, *, nc):
    lc = MCHUNK

    def conv_silu(x, w, b):
        n = x.shape[0]
        rid = lax.broadcasted_iota(jnp.int32, x.shape, 0)
        xm = jnp.where(rid == 0, 0.0, pltpu.roll(x, 1, 0))
        xp = jnp.where(rid == n - 1, 0.0, pltpu.roll(x, n - 1, 0))
        return _silu(xm * w[0:1, :] + x * w[1:2, :] + xp * w[2:3, :] + b)

    k_scale = HD ** -0.5
    q_s[...] = conv_silu(q_ref[...].astype(F32), cwq_ref[...], cbq_ref[...]).astype(BF16)
    k_s[...] = (conv_silu(k_ref[...].astype(F32), cwk_ref[...], cbk_ref[...]) * k_scale).astype(BF16)
    kc_s[...] = (conv_silu(kc_ref[...].astype(F32), cwk_ref[...], cbk_ref[...]) * k_scale).astype(BF16)

    def col(j):
        return col_ref[0, 0, :, j:j + 1]

    def local_state(kk, vv, wcol):
        vw = (vv.astype(F32) * wcol).astype(BF16)
        ct = jnp.dot(kk.T, vw, preferred_element_type=F32)
        nn = jnp.sum(kk.astype(F32) * wcol, axis=0, keepdims=True)
        return ct, nn

    for d in range(2):
        ct, nn = local_state(kc_s[...], vc_ref[...], col(8 * nc + d))
        order = list(range(nc)) if d == 0 else list(range(nc - 1, -1, -1))
        for pos, c in enumerate(order):
            idx = d * nc + c
            ct_s[idx] = ct.astype(BF16)
            n_s[idx] = nn
            if pos == nc - 1:
                break
            sl = pl.ds(c * lc, lc)
            ctl, nl = local_state(k_s[sl, :], v_ref[sl, :], col(idx))
            s_old = col_ref[0, 0, 0:1, 8 * nc + 2 + idx:8 * nc + 3 + idx]
            ct = s_old * ct + ctl
            nn = s_old * nn + nl

    tid = lax.broadcasted_iota(jnp.int32, (lc, lc), 0)
    sid = lax.broadcasted_iota(jnp.int32, (lc, lc), 1)
    masks = (sid <= tid, sid >= tid)
    for c in range(nc):
        sl = pl.ds(c * lc, lc)
        q = q_s[sl, :]
        kk = k_s[sl, :]
        v = v_ref[sl, :]
        qf = q.astype(F32)
        s = lax.dot_general(q, kk, (((1,), (1,)), ((), ())), preferred_element_type=F32)
        h = None
        for d in range(2):
            idx = d * nc + c
            r = row_ref[0, 0, idx:idx + 1, :]
            big_m = col(2 * nc + idx)
            s_int = col(4 * nc + idx)
            e_neg = col(6 * nc + idx)
            p = jnp.where(masks[d], jnp.exp(r - big_m), 0.0) * s
            den = (jnp.sum(p, axis=-1, keepdims=True)
                   + s_int * jnp.sum(qf * n_s[idx], axis=-1, keepdims=True))
            num = (jnp.dot(p.astype(BF16), v, preferred_element_type=F32)
                   + s_int * jnp.dot(q, ct_s[idx], preferred_element_type=F32))
            hd = num / jnp.maximum(jnp.abs(den), e_neg)
            h = hd if h is None else h + hd
        mu = jnp.mean(h, axis=-1, keepdims=True)
        hc = h - mu
        var = jnp.mean(hc * hc, axis=-1, keepdims=True)
        hn = hc * lax.rsqrt(var + LN_EPS) * nw_ref[...]
        y_ref[sl, :] = (hn * _sigmoid(o_ref[sl, :].astype(F32))).astype(BF16)


def _mlstm(qkvo, kv_ctx, conv_w, conv_b, rowq, colq, norm_w, bsz, seq, ctx_len):
    nc = seq // MCHUNK
    hq = D_MLSTM // HD
    kern = functools.partial(_mlstm_kernel, nc=nc)
    seq_blk = lambda off: pl.BlockSpec((seq, HD), lambda b, h: (b, off + h))
    ctx_blk = lambda off: pl.BlockSpec((ctx_len, HD), lambda b, h: (b, off + h))
    return pl.pallas_call(
        kern,
        grid=(bsz, HEADS),
        in_specs=[seq_blk(0), seq_blk(hq), seq_blk(2 * hq), seq_blk(3 * hq),
                  ctx_blk(0), ctx_blk(hq),
                  pl.BlockSpec((3, HD), lambda b, h: (0, h)),
                  pl.BlockSpec((1, HD), lambda b, h: (0, h)),
                  pl.BlockSpec((3, HD), lambda b, h: (0, hq + h)),
                  pl.BlockSpec((1, HD), lambda b, h: (0, hq + h)),
                  pl.BlockSpec((1, 1, 2 * nc, MCHUNK), lambda b, h: (b, h, 0, 0)),
                  pl.BlockSpec((1, 1, MCHUNK, LANE), lambda b, h: (b, h, 0, 0)),
                  pl.BlockSpec((1, HD), lambda b, h: (0, h))],
        out_specs=pl.BlockSpec((seq, HD), lambda b, h: (b, h)),
        out_shape=jax.ShapeDtypeStruct((bsz * seq, D_MLSTM), BF16),
        scratch_shapes=[pltpu.VMEM((seq, HD), BF16), pltpu.VMEM((seq, HD), BF16),
                        pltpu.VMEM((ctx_len, HD), BF16),
                        pltpu.VMEM((2 * nc, HD, HD), BF16), pltpu.VMEM((2 * nc, 1, HD), F32)],
        compiler_params=_cparams(("parallel", "parallel")),
        name="mlstm",
    )(qkvo, qkvo, qkvo, qkvo, kv_ctx, kv_ctx, conv_w, conv_b, conv_w, conv_b, rowq, colq, norm_w)


def _out_kernel(ym_ref, uv_ref, x_ref, pe_ref, m_ref, cnw_ref, ws_ref, bs_ref, wout_ref,
                l1w_ref, l1b_ref, wr_ref, br_ref, x1_ref, h2_ref, lg_ref, y_s, *, bm):
    m = m_ref[0]
    y_s[:, :D_MLSTM] = ym_ref[...]
    for g in range(CMLP_GROUPS):
        gs = slice(g * CMLP_GD, (g + 1) * CMLP_GD)
        vg = uv_ref[:, D_CMLP + g * CMLP_GD:D_CMLP + (g + 1) * CMLP_GD].astype(F32)
        mu = jnp.mean(vg, axis=-1, keepdims=True)
        vc = vg - mu
        var = jnp.mean(vc * vc, axis=-1, keepdims=True)
        vn = (vc * lax.rsqrt(var + LN_EPS) * cnw_ref[:, gs]).astype(BF16)
        for p in range(bm // CMLP_CHUNK):
            ps = slice(p * CMLP_CHUNK, (p + 1) * CMLP_CHUNK)
            s = jnp.dot(ws_ref[g], vn[ps, :], preferred_element_type=F32) + bs_ref[:, g:g + 1]
            yc = uv_ref[ps, gs].astype(F32) * s
            y_s[ps, D_MLSTM + g * CMLP_GD:D_MLSTM + (g + 1) * CMLP_GD] = yc.astype(BF16)
    y = jnp.dot(y_s[...], wout_ref[...], preferred_element_type=F32)
    z = DEEPNORM_ALPHA * (x_ref[...] + pe_ref[...]) + m[2:3, :] * y
    x1 = _layer_norm_rows(z, l1w_ref[...], l1b_ref[...])
    x1_ref[...] = x1
    h2 = x1 * (1.0 + m[4:5, :]) + m[3:4, :]
    half = h2.shape[1] // 2
    h2_ref[...] = _pack_bf16_pairs(h2[:, :half], h2[:, half:])
    lg_ref[...] = jnp.dot(h2.astype(BF16), wr_ref[...], preferred_element_type=F32) + br_ref[...]


def _mixer_out(ym, uv, x2d, pe, mods3, cnw, ws, bs_t, wout, l1w, l1b, wr, br, seq, bm):
    rows, dm = x2d.shape
    pe_blocks = seq // bm
    kern = functools.partial(_out_kernel, bm=bm)
    row_blk = lambda n: pl.BlockSpec((bm, n), lambda i: (i, 0))
    return pl.pallas_call(
        kern,
        grid=(rows // bm,),
        in_specs=[row_blk(D_MLSTM), row_blk(2 * D_CMLP), row_blk(dm),
                  pl.BlockSpec((bm, dm), lambda i: (i % pe_blocks, 0)),
                  pl.BlockSpec((1, N_MOD, dm), lambda i: (i // pe_blocks, 0, 0)),
                  _resident(cnw.shape), _resident(ws.shape), _resident(bs_t.shape),
                  _resident(wout.shape), _resident(l1w.shape), _resident(l1b.shape),
                  _resident(wr.shape), _resident(br.shape)],
        out_specs=[row_blk(dm), row_blk(dm // 2), row_blk(LANE)],
        out_shape=[jax.ShapeDtypeStruct((rows, dm), F32),
                   jax.ShapeDtypeStruct((rows, dm // 2), jnp.uint32),
                   jax.ShapeDtypeStruct((rows, LANE), F32)],
        scratch_shapes=[pltpu.VMEM((bm, D_MLSTM + D_CMLP), BF16)],
        compiler_params=_cparams(("parallel",)),
        name="mixer_out",
    )(ym, uv, x2d, pe, mods3, cnw, ws, bs_t, wout, l1w, l1b, wr, br)


def _moe_kernel(comp_ref, blk_ref, slot_ref, cast_ref, cslot_ref, cexp_ref, ctile_ref,
                gtab_ref, stab_ref, h2_hbm, wgf_ref, wuf_ref, wdf_ref, ys_hbm,
                wg_s, wu_s, wd_s, xbuf, xb_s, ybuf, gsem, ssem, pend, *, nt):
    del blk_ref, cexp_ref
    s = pl.program_id(0)
    n_steps = pl.num_programs(0)
    groups = MOE_ROWS // SUBLANES
    half = xbuf.shape[-1]
    dm = 2 * half
    n_asg = ys_hbm.shape[0] * SUBLANES - 2 * MOE_ROWS

    def gather_wait(p):
        pltpu.make_async_copy(h2_hbm.at[pl.ds(0, groups)], xbuf.at[p], gsem.at[p]).wait()

    def scatter_wait(p):
        pltpu.make_async_copy(ybuf.at[p], ys_hbm.at[pl.ds(0, groups)], ssem.at[p]).wait()

    @pl.when(s == 0)
    def _():
        pend[0] = 0
        pend[1] = 0

    nxt = jnp.minimum(s + 1, n_steps - 1)

    @pl.when(jnp.logical_and(s + 1 < n_steps, comp_ref[nxt] == 1))
    def _():
        p = (s + 1) % 2

        def body(g, carry):
            for i in range(SUBLANES):
                tok = gtab_ref[0, 0, g * SUBLANES + i]
                pltpu.make_async_copy(h2_hbm.at[tok >> 3, pl.ds(tok & 7, 1)], xbuf.at[p, g, pl.ds(i, 1)],
                                      gsem.at[p]).start()
            return carry

        lax.fori_loop(0, groups, body, 0)

    @pl.when(cast_ref[s] == 1)
    def _():
        cs = cslot_ref[s]
        t = ctile_ref[s]
        for k in range(nt):
            @pl.when(t == k)
            def _():
                wg_s[cs, :, k * MOE_FT:(k + 1) * MOE_FT] = wgf_ref[0].astype(BF16)
                wu_s[cs, :, k * MOE_FT:(k + 1) * MOE_FT] = wuf_ref[0].astype(BF16)
        wd_s[cs, t] = wdf_ref[0].astype(BF16)

    @pl.when(comp_ref[s] == 1)
    def _():
        p = s % 2
        sl = slot_ref[s]
        gather_wait(p)
        x_lo, x_hi = _unpack_bf16_pairs(xbuf[p].reshape(MOE_ROWS, half))
        xb_s[:, :half] = x_lo.astype(BF16)
        xb_s[:, half:] = x_hi.astype(BF16)
        x = xb_s[...]
        g = jnp.dot(x, wg_s[sl], preferred_element_type=F32)
        u = jnp.dot(x, wu_s[sl], preferred_element_type=F32)
        h = (_silu(g) * u).astype(BF16)
        y = jnp.dot(h, wd_s[sl].reshape(D_EXPERT, dm), preferred_element_type=F32)

        @pl.when(pend[p] == 1)
        def _():
            scatter_wait(p)

        ybuf[p] = _pack_bf16_pairs(y[:, :half], y[:, half:]).reshape(groups, SUBLANES, half)

        def body(g, carry):
            for i in range(SUBLANES):
                d = stab_ref[0, 0, g * SUBLANES + i]
                dst = jnp.where(d < 0, n_asg + p * MOE_ROWS + g * SUBLANES + i, d)
                pltpu.make_async_copy(ybuf.at[p, g, pl.ds(i, 1)], ys_hbm.at[dst >> 3, pl.ds(dst & 7, 1)],
                                      ssem.at[p]).start()
            return carry

        lax.fori_loop(0, groups, body, 0)
        pend[p] = 1

    @pl.when(s == n_steps - 1)
    def _():
        for p in range(2):
            @pl.when(pend[p] == 1)
            def _():
                scatter_wait(p)
                pend[p] = 0

        xbuf[...] = jnp.zeros_like(xbuf)
        fills = [pltpu.make_async_copy(xbuf.at[p], ys_hbm.at[pl.ds(n_asg // SUBLANES + p * groups, groups)],
                                       gsem.at[p]) for p in range(2)]
        for cp in fills:
            cp.start()
        for cp in fills:
            cp.wait()


def _experts(h2, gtab, stab, sched, wg, wu, wd):
    n_tok, dm = h2.shape[0], 2 * h2.shape[1]
    nt = D_EXPERT // MOE_FT
    n_asg = 2 * n_tok
    n_steps = sched[0].shape[0]

    smem_rows = lambda imap: pl.BlockSpec((1, 1, MOE_ROWS), imap, memory_space=pltpu.SMEM)
    grid_spec = pltpu.PrefetchScalarGridSpec(
        num_scalar_prefetch=7,
        grid=(n_steps,),
        in_specs=[smem_rows(lambda s, comp, blk, *_: (blk[jnp.minimum(s + 1, n_steps - 1)], 0, 0)),
                  smem_rows(lambda s, comp, blk, *_: (blk[s], 0, 0)),
                  pl.BlockSpec(memory_space=pl.ANY),
                  pl.BlockSpec((1, dm, MOE_FT), lambda s, c, b, sl, ca, cs, ce, ct: (ce[s], 0, ct[s])),
                  pl.BlockSpec((1, dm, MOE_FT), lambda s, c, b, sl, ca, cs, ce, ct: (ce[s], 0, ct[s])),
                  pl.BlockSpec((1, MOE_FT, dm), lambda s, c, b, sl, ca, cs, ce, ct: (ce[s], ct[s], 0))],
        out_specs=pl.BlockSpec(memory_space=pl.ANY),
        scratch_shapes=[pltpu.VMEM((2, dm, D_EXPERT), BF16), pltpu.VMEM((2, dm, D_EXPERT), BF16),
                        pltpu.VMEM((2, nt, MOE_FT, dm), BF16),
                        pltpu.VMEM((2, MOE_ROWS // SUBLANES, SUBLANES, dm // 2), jnp.uint32),
                        pltpu.VMEM((MOE_ROWS, dm), BF16),
                        pltpu.VMEM((2, MOE_ROWS // SUBLANES, SUBLANES, dm // 2), jnp.uint32),
                        pltpu.SemaphoreType.DMA((2,)), pltpu.SemaphoreType.DMA((2,)),
                        pltpu.SMEM((2,), jnp.int32)],
    )
    return pl.pallas_call(
        functools.partial(_moe_kernel, nt=nt),
        grid_spec=grid_spec,
        out_shape=jax.ShapeDtypeStruct(((n_asg + 2 * MOE_ROWS) // SUBLANES, SUBLANES, dm // 2), jnp.uint32),
        compiler_params=_cparams(("arbitrary",)),
        name="experts",
    )(*sched, gtab, stab, h2.reshape(n_tok // SUBLANES, SUBLANES, dm // 2), wg, wu, wd)


def _final_kernel(x1_ref, y0_ref, y1_ref, g_ref, m_ref, w_ref, b_ref, o_ref):
    m = m_ref[0]
    lo0, hi0 = _unpack_bf16_pairs(y0_ref[...])
    lo1, hi1 = _unpack_bf16_pairs(y1_ref[...])
    g0 = g_ref[:, 0:1]
    g1 = g_ref[:, 1:2]
    y = jnp.concatenate([g0 * lo0 + g1 * lo1, g0 * hi0 + g1 * hi1], axis=-1)
    z = DEEPNORM_ALPHA * x1_ref[...] + m[5:6, :] * y
    o_ref[...] = _layer_norm_rows(z, w_ref[...], b_ref[...])


def _final(x1, ys, gates, mods3, w, b, seq, bm):
    rows, dm = x1.shape
    blocks_per_batch = seq // bm
    slot_blocks = rows // bm
    row_blk = lambda n: pl.BlockSpec((bm, n), lambda i: (i, 0))
    return pl.pallas_call(
        _final_kernel,
        grid=(rows // bm,),
        in_specs=[row_blk(dm), row_blk(dm // 2), pl.BlockSpec((bm, dm // 2), lambda i: (i + slot_blocks, 0)),
                  row_blk(LANE),
                  pl.BlockSpec((1, N_MOD, dm), lambda i: (i // blocks_per_batch, 0, 0)),
                  _resident(w.shape), _resident(b.shape)],
        out_specs=row_blk(dm),
        out_shape=jax.ShapeDtypeStruct((rows, dm), F32),
        compiler_params=_cparams(("parallel",)),
        name="final_ln",
    )(x1, ys, ys, gates, mods3, w, b)


def _route(logits, n_tok):
    logits1 = logits[:, :N_GROUPS]
    grp = jnp.argmax(logits1, -1).astype(jnp.int32)
    gsel = grp[:, None] == jnp.arange(N_GROUPS, dtype=jnp.int32)[None, :]
    p_grp = jnp.sum(jnp.where(gsel, jax.nn.softmax(logits1, -1), 0.0), -1)
    logits2 = logits[:, N_GROUPS:N_GROUPS + N_EXPERTS].reshape(n_tok, N_GROUPS, EXPERTS_PER_GROUP)
    l2 = jnp.sum(jnp.where(gsel[:, :, None], logits2, 0.0), 1)
    eidx = jnp.arange(EXPERTS_PER_GROUP, dtype=jnp.int32)[None, :]
    i0 = jnp.argmax(l2, -1).astype(jnp.int32)
    v0 = jnp.max(l2, -1)
    l2m = jnp.where(eidx == i0[:, None], -jnp.inf, l2)
    i1 = jnp.argmax(l2m, -1).astype(jnp.int32)
    v1 = jnp.max(l2m, -1)
    top_v = jnp.stack([v0, v1], -1)
    gate = p_grp[:, None] * jax.nn.softmax(top_v, -1)
    e_flat = (grp[:, None] * EXPERTS_PER_GROUP + jnp.stack([i0, i1], -1)).reshape(-1)
    n_asg = e_flat.shape[0]
    blk = 256
    earange = jnp.arange(N_EXPERTS, dtype=jnp.int32)
    onehot = (e_flat[:, None] == earange[None, :])
    oh = onehot.astype(BF16).reshape(n_asg // blk, blk, N_EXPERTS)
    tri = (jnp.arange(blk)[:, None] >= jnp.arange(blk)[None, :]).astype(BF16)
    within = jnp.einsum("ts,bse->bte", tri, oh, preferred_element_type=F32)
    totals = within[:, -1, :]
    offs = jnp.cumsum(totals, axis=0) - totals
    csum = (within + offs[:, None, :]).reshape(n_asg, N_EXPERTS)
    counts = (offs[-1] + totals[-1]).astype(jnp.int32)
    rank = jnp.sum(jnp.where(onehot, csum, 0.0), -1).astype(jnp.int32) - 1
    nblk_e = (counts + MOE_ROWS - 1) // MOE_ROWS
    pad_end = jnp.cumsum(nblk_e * MOE_ROWS)
    pad_start = pad_end - nblk_e * MOE_ROWS
    pos = jnp.sum(jnp.where(onehot, pad_start[None, :], 0), -1) + rank
    n_blk = n_asg // MOE_ROWS + N_EXPERTS

    row_asg = jnp.full((n_blk * MOE_ROWS,), -1, jnp.int32).at[pos].set(jnp.arange(n_asg, dtype=jnp.int32))
    gtab = (jnp.maximum(row_asg, 0) >> 1).reshape(n_blk, 1, MOE_ROWS)
    stab = jnp.where(row_asg < 0, -1, (row_asg & 1) * n_tok + (row_asg >> 1)).reshape(n_blk, 1, MOE_ROWS)

    nt = D_EXPERT // MOE_FT
    n_steps = n_blk + (nt - 1) * N_EXPERTS + nt
    has = nblk_e > 0
    n_visits = jnp.sum(has.astype(jnp.int32))
    e_of_visit = jnp.sort(jnp.where(has, earange, N_EXPERTS))
    vsel = e_of_visit[:, None] == earange[None, :]
    nb_v = jnp.sum(jnp.where(vsel, nblk_e[None, :], 0), -1)
    steps_v = jnp.where(nb_v > 0, jnp.maximum(nb_v, nt), 0)
    end_v = nt + jnp.cumsum(steps_v)
    start_v = end_v - steps_v
    first_blk_v = jnp.cumsum(nb_v) - nb_v
    sidx = jnp.arange(n_steps, dtype=jnp.int32)
    v = jnp.sum((end_v[None, :] <= sidx[:, None]).astype(jnp.int32), -1)
    pick = lambda arr, idx: jnp.sum(jnp.where(idx[:, None] == earange[None, :], arr[None, :], 0), -1)
    in_visit = (sidx >= nt) & (v < n_visits)
    k = sidx - pick(start_v, v)
    comp = in_visit & (k < pick(nb_v, v))
    blk_s = lax.cummax(jnp.where(comp, pick(first_blk_v, v) + k, 0), axis=0)
    prologue = sidx < nt
    cast = prologue | (in_visit & (k < nt) & (v + 1 < n_visits))
    cexp = jnp.where(prologue, e_of_visit[0], pick(e_of_visit, v + 1))
    ctile = jnp.where(prologue, sidx, k)
    code = lax.cummax(jnp.where(cast, cexp * nt + ctile, 0), axis=0)
    i32 = lambda a: a.astype(jnp.int32)
    sched = (i32(comp), i32(blk_s), i32(v % 2), i32(cast), i32(jnp.where(prologue, 0, (v + 1) % 2)),
             i32(jnp.minimum(code // nt, N_EXPERTS - 1)), i32(code % nt))
    return gate, sched, gtab, stab


def _grid_pos_embed(rows):
    quarter = D_MODEL // 4
    omega = 1.0 / (10000.0 ** (jnp.arange(quarter, dtype=F32) / quarter))
    ar = jnp.arange(rows, dtype=F32)[:, None] * omega
    ac = jnp.arange(GRID_W, dtype=F32)[:, None] * omega
    shape = (rows, GRID_W, quarter)
    parts = [jnp.broadcast_to(jnp.sin(ar)[:, None, :], shape), jnp.broadcast_to(jnp.cos(ar)[:, None, :], shape),
             jnp.broadcast_to(jnp.sin(ac)[None, :, :], shape), jnp.broadcast_to(jnp.cos(ac)[None, :, :], shape)]
    return jnp.concatenate(parts, -1).reshape(rows * GRID_W, D_MODEL)


def kernel(x, c, ctx, c_ctx, w_mod, b_mod, w_in, conv_w, conv_b, gate_bias, mlstm_norm_w, cmlp_norm_w,
           w_s, b_s, w_out, ln1_w, ln1_b, router1_w, router1_b, router2_w, router2_b, w_gate, w_up,
           w_down, ln2_w, ln2_b):
    bsz, seq, dm = x.shape
    ctx_len = ctx.shape[1]
    n_tok = bsz * seq
    assert w_mod.shape[0] == 1 and dm == D_MODEL and seq % MCHUNK == 0 and ctx_len == MCHUNK
    pe = _grid_pos_embed(seq // GRID_W).astype(x.dtype)
    x2d = x.reshape(n_tok, dm)
    ctx2d = ctx.reshape(bsz * ctx_len, dm)

    mod_rows = 16
    cc = jnp.concatenate([c, c_ctx[None, :], jnp.zeros((mod_rows - bsz - 1, dm), c.dtype)], 0)
    mods3 = _modulation(cc, w_mod[0], b_mod[0]).reshape(mod_rows, N_MOD, dm)

    dq = D_MLSTM
    wi = w_in[0]
    w_qkvo = wi[:, :4 * dq].astype(BF16)
    w_g = jnp.pad(wi[:, 4 * dq:4 * dq + N_GATE_COLS], ((0, 0), (0, LANE - N_GATE_COLS))).astype(BF16)
    w_uv = wi[:, 4 * dq + N_GATE_COLS:].astype(BF16)
    bm_proj = 256
    blocks_per_seq = seq // bm_proj
    qkvo, g_x, uv = _projection(x2d, pe, mods3, lambda i: i // blocks_per_seq,
                                [w_qkvo, w_g, w_uv], [BF16, F32, BF16], [False, False, True], bm_proj)
    w_kv = wi[:, dq:3 * dq].astype(BF16)
    kv_c, g_c = _projection(ctx2d, None, mods3, lambda i: bsz, [w_kv, w_g], [BF16, F32],
                            [False, False], bm_proj)

    rowq, colq = _gate_stats(g_x, g_c, gate_bias[0], bsz, seq, ctx_len)
    ym = _mlstm(qkvo, kv_c, conv_w[0], conv_b[0].reshape(1, -1), rowq, colq,
                mlstm_norm_w[0].reshape(1, -1), bsz, seq, ctx_len)

    wr = jnp.pad(jnp.concatenate([router1_w[0], router2_w[0]], 1),
                 ((0, 0), (0, LANE - N_GROUPS - N_EXPERTS))).astype(BF16)
    br = jnp.pad(jnp.concatenate([router1_b[0], router2_b[0]], 0),
                 (0, LANE - N_GROUPS - N_EXPERTS)).reshape(1, LANE)
    x1, h2, logits = _mixer_out(ym, uv, x2d, pe, mods3, cmlp_norm_w[0].reshape(1, -1),
                                w_s[0].astype(BF16), b_s[0].T, w_out[0].astype(BF16),
                                ln1_w[0].reshape(1, -1), ln1_b[0].reshape(1, -1), wr, br, seq, 256)

    gate, sched, gtab, stab = _route(logits, n_tok)
    ys = _experts(h2, gtab, stab, sched, w_gate[0], w_up[0], w_down[0])
    gates = jnp.pad(gate.astype(F32), ((0, 0), (0, LANE - 2)))
    out = _final(x1, ys.reshape(-1, dm // 2), gates, mods3, ln2_w[0].reshape(1, -1),
                 ln2_b[0].reshape(1, -1), seq, 256)
    return out.reshape(bsz, seq, dm)
```

```python
import functools

import jax
import jax.numpy as jnp
from jax import lax
from jax.experimental import pallas as pl
from jax.experimental.pallas import tpu as pltpu

F32 = jnp.float32
BF16 = jnp.bfloat16

D_MODEL = 2048
GRID_W = 64
D_MLSTM = 1024
D_CMLP = 1024
HEADS = 4
HD = 256
CMLP_GROUPS = 4
CMLP_GD = 256
CMLP_CHUNK = 128
N_GROUPS = 4
EXPERTS_PER_GROUP = 8
N_EXPERTS = 32
D_EXPERT = 1024
N_MOD = 6
N_GATE_COLS = 16
DEEPNORM_ALPHA = 2.0 ** 0.25
LN_EPS = 1e-6

LANE = 128
SUBLANES = 8
MCHUNK = 256
MOE_ROWS = 256
MOE_FT = 256
VMEM_LIMIT = 56 * 1024 * 1024


def _cparams(sem):
    return pltpu.CompilerParams(dimension_semantics=sem, vmem_limit_bytes=VMEM_LIMIT)


def _resident(shape):
    nd = len(shape)
    return pl.BlockSpec(shape, lambda *_: (0,) * nd, pipeline_mode=pl.Buffered(1))


def _sigmoid(x):
    return 0.5 * jnp.tanh(0.5 * x) + 0.5


def _silu(x):
    return x * _sigmoid(x)


def _log_sigmoid(x):
    return jnp.minimum(x, 0.0) - jnp.log1p(jnp.exp(-jnp.abs(x)))


def _gelu_tanh(x):
    c = 0.7978845608028654
    return 0.5 * x * (1.0 + jnp.tanh(c * (x + 0.044715 * (x * x * x))))


def _pack_bf16_pairs(lo, hi):
    lo_b = lax.bitcast_convert_type(lo.astype(BF16).astype(F32), jnp.uint32)
    hi_b = lax.bitcast_convert_type(hi.astype(BF16).astype(F32), jnp.uint32)
    return (lo_b >> 16) | (hi_b & jnp.uint32(0xFFFF0000))


def _unpack_bf16_pairs(w):
    lo = lax.bitcast_convert_type(w << 16, F32)
    hi = lax.bitcast_convert_type(w & jnp.uint32(0xFFFF0000), F32)
    return lo, hi


def _layer_norm_rows(z, w, b):
    mu = jnp.mean(z, axis=-1, keepdims=True)
    zc = z - mu
    var = jnp.mean(zc * zc, axis=-1, keepdims=True)
    return zc * lax.rsqrt(var + LN_EPS) * w + b


def _mod_kernel(c_ref, w_ref, b_ref, o_ref):
    s = _silu(c_ref[...]).astype(BF16)
    o_ref[...] = jnp.dot(s, w_ref[...].astype(BF16), preferred_element_type=F32) + b_ref[...]


def _modulation(cc, w_mod, b_mod):
    rows, dm = cc.shape
    n = w_mod.shape[1]
    tn = 1024
    return pl.pallas_call(
        _mod_kernel,
        grid=(n // tn,),
        in_specs=[pl.BlockSpec((rows, dm), lambda j: (0, 0)),
                  pl.BlockSpec((dm, tn), lambda j: (0, j)),
                  pl.BlockSpec((1, tn), lambda j: (0, j))],
        out_specs=pl.BlockSpec((rows, tn), lambda j: (0, j)),
        out_shape=jax.ShapeDtypeStruct((rows, n), F32),
        compiler_params=_cparams(("arbitrary",)),
        name="modulation",
    )(cc, w_mod, b_mod.reshape(1, n))


def _proj_kernel(*refs, n_w, has_pe, gelu_flags, tn):
    x_ref = refs[0]
    k = 1
    pe_ref = None
    if has_pe:
        pe_ref = refs[k]
        k += 1
    m_ref = refs[k]
    k += 1
    w_refs = refs[k:k + n_w]
    o_refs = refs[k + n_w:k + 2 * n_w]
    hx_ref = refs[k + 2 * n_w]
    x = x_ref[...]
    if has_pe:
        x = x + pe_ref[...]
    m = m_ref[0]
    hx_ref[...] = (x * (1.0 + m[1:2, :]) + m[0:1, :]).astype(BF16)
    for w_ref, o_ref, use_gelu in zip(w_refs, o_refs, gelu_flags):
        n = w_ref.shape[1]
        step = min(tn, n)
        for j in range(0, n, step):
            acc = jnp.dot(hx_ref[...], w_ref[:, j:j + step], preferred_element_type=F32)
            if use_gelu:
                acc = _gelu_tanh(acc)
            o_ref[:, j:j + step] = acc.astype(o_ref.dtype)


def _projection(x2d, pe, mods3, mod_row_of_block, weights, out_dtypes, gelu_flags, bm):
    rows, dm = x2d.shape
    has_pe = pe is not None
    n_w = len(weights)
    in_specs = [pl.BlockSpec((bm, dm), lambda i: (i, 0))]
    args = [x2d]
    if has_pe:
        pe_blocks = pe.shape[0] // bm
        in_specs.append(pl.BlockSpec((bm, dm), lambda i: (i % pe_blocks, 0)))
        args.append(pe)
    in_specs.append(pl.BlockSpec((1, N_MOD, dm), lambda i: (mod_row_of_block(i), 0, 0)))
    args.append(mods3)
    for w in weights:
        in_specs.append(_resident(w.shape))
        args.append(w)
    out_specs = [pl.BlockSpec((bm, w.shape[1]), lambda i: (i, 0)) for w in weights]
    out_shape = [jax.ShapeDtypeStruct((rows, w.shape[1]), dt) for w, dt in zip(weights, out_dtypes)]
    kern = functools.partial(_proj_kernel, n_w=n_w, has_pe=has_pe, gelu_flags=tuple(gelu_flags), tn=1024)
    return pl.pallas_call(
        kern,
        grid=(rows // bm,),
        in_specs=in_specs,
        out_specs=out_specs,
        out_shape=out_shape,
        scratch_shapes=[pltpu.VMEM((bm, dm), BF16)],
        compiler_params=_cparams(("parallel",)),
        name="projection",
    )(*args)


def _gate_kernel(lic_ref, lfc_ref, lir_ref, lfr_ref, row_ref, col_ref, *, nc):
    li = lic_ref[0, 0]
    lf = _log_sigmoid(lfc_ref[0, 0])
    length = li.shape[0]
    tid = lax.broadcasted_iota(jnp.int32, li.shape, 0)
    lane = lax.broadcasted_iota(jnp.int32, li.shape, 1)
    lane1 = lane[0:1, :]
    fwd = (lane < nc) | (lane == 2 * nc)

    def scan_sublanes(x, op, fill):
        p = x
        s = x
        k = 1
        while k < length:
            p = op(p, jnp.where(tid >= k, pltpu.roll(p, k, 0), fill))
            s = op(s, jnp.where(tid < length - k, pltpu.roll(s, length - k, 0), fill))
            k *= 2
        return jnp.where(fwd, p, s)

    b = scan_sublanes(lf, jnp.add, 0.0)
    btot = jnp.sum(lf, axis=0, keepdims=True)
    a = btot - b + li
    m_loc = jnp.max(a, axis=0, keepdims=True)
    r = li - b
    cm = scan_sublanes(r, jnp.maximum, -jnp.inf)

    m_ctx = jnp.maximum(btot, m_loc)
    m_in = jnp.where(lane1 == 0, pltpu.roll(m_ctx, LANE - 2 * nc, 1), pltpu.roll(m_ctx, LANE - 2, 1))
    for k in range(nc - 1):
        m_new = jnp.maximum(btot + m_in, m_loc)
        m_in = jnp.where(lane1 == k + 1, pltpu.roll(m_new, 1, 1),
                         jnp.where(lane1 == 2 * nc - 2 - k, pltpu.roll(m_new, LANE - 1, 1), m_in))
    is_ctx = lane1 >= 2 * nc
    m_in = jnp.where(is_ctx, 0.0, m_in)
    m_new = jnp.maximum(btot + m_in, m_loc)
    s_old = jnp.broadcast_to(jnp.exp(btot + m_in - m_new), li.shape)
    w = jnp.exp(a - m_new)
    big_m = jnp.maximum(m_in, cm)
    s_int = jnp.exp(m_in - big_m)
    e_neg = jnp.exp(-(b + big_m))
    g = 2 * nc
    col_ref[0, 0] = jnp.where(
        lane < g, w, jnp.where(
            lane < 2 * g, pltpu.roll(big_m, g, 1), jnp.where(
                lane < 3 * g, pltpu.roll(s_int, 2 * g, 1), jnp.where(
                    lane < 4 * g, pltpu.roll(e_neg, 3 * g, 1), jnp.where(
                        lane < 4 * g + 2, pltpu.roll(w, 3 * g, 1), pltpu.roll(s_old, 4 * g + 2, 1))))))

    lir = lir_ref[0, 0]
    lfr = _log_sigmoid(lfr_ref[0, 0])
    width = lir.shape[1]
    rid = lax.broadcasted_iota(jnp.int32, lir.shape, 0)
    pid = lax.broadcasted_iota(jnp.int32, lir.shape, 1)
    p = lfr
    s = lfr
    k = 1
    while k < width:
        p = p + jnp.where(pid >= k, pltpu.roll(p, k, 1), 0.0)
        s = s + jnp.where(pid < width - k, pltpu.roll(s, width - k, 1), 0.0)
        k *= 2
    row_ref[0, 0] = lir - jnp.where(rid < nc, p, s)


def _gate_stats(g_x, g_c, gate_bias, bsz, seq, ctx_len):
    nc = seq // MCHUNK
    gb = gate_bias.astype(F32)
    gx = g_x[:, :N_GATE_COLS].reshape(bsz, nc, MCHUNK, 2, 2, HEADS) + gb.reshape(2, 2, HEADS)
    gc = g_c[:, :N_GATE_COLS].reshape(bsz, ctx_len, 2, 2, HEADS) + gb.reshape(2, 2, HEADS)
    col_x = gx.transpose(4, 0, 5, 2, 3, 1).reshape(2, bsz, HEADS, MCHUNK, 2 * nc)
    col_c = gc.transpose(3, 0, 4, 1, 2)
    col = jnp.concatenate([col_x, col_c], -1)
    col = jnp.pad(col, ((0, 0),) * 4 + ((0, LANE - col.shape[-1]),))
    row = gx.transpose(4, 0, 5, 3, 1, 2).reshape(2, bsz, HEADS, 2 * nc, MCHUNK)
    blk_c = pl.BlockSpec((1, 1, MCHUNK, LANE), lambda b, h: (b, h, 0, 0))
    blk_r = pl.BlockSpec((1, 1, 2 * nc, MCHUNK), lambda b, h: (b, h, 0, 0))
    return pl.pallas_call(
        functools.partial(_gate_kernel, nc=nc),
        grid=(bsz, HEADS),
        in_specs=[blk_c, blk_c, blk_r, blk_r],
        out_specs=[blk_r, blk_c],
        out_shape=[jax.ShapeDtypeStruct((bsz, HEADS, 2 * nc, MCHUNK), F32),
                   jax.ShapeDtypeStruct((bsz, HEADS, MCHUNK, LANE), F32)],
        compiler_params=_cparams(("parallel", "parallel")),
        name="gate_stats",
    )(col[0], col[1], row[0], row[1])


def _mlstm_kernel(q_ref, k_ref, v_ref, o_ref, kc_ref, vc_ref, cwq_ref, cbq_ref, cwk_ref, cbk_ref,
                  row_ref, col_ref, nw_ref, y_ref, q_s, k_s, kc_s, ct_s, n_s, *, nc):
    lc = MCHUNK

    def conv_silu(x, w, b):
        n = x.shape[0]
        rid = lax.broadcasted_iota(jnp.int32, x.shape, 0)
        xm = jnp.where(rid == 0, 0.0, pltpu.roll(x, 1, 0))
        xp = jnp.where(rid == n - 1, 0.0, pltpu.roll(x, n - 1, 0))
        return _silu(xm * w[0:1, :] + x * w[1:2, :] + xp * w[2:3, :] + b)

    k_scale = HD ** -0.5
    q_s[...] = conv_silu(q_ref[...].astype(F32), cwq_ref[...], cbq_ref[...]).astype(BF16)
    k_s[...] = (conv_silu(k_ref[...].astype(F32), cwk_ref[...], cbk_ref[...]) * k_scale).astype(BF16)
    kc_s[...] = (conv_silu(kc_ref[...].astype(F32), cwk_ref[...], cbk_ref[...]) * k_scale).astype(BF16)

    def col(j):
        return col_ref[0, 0, :, j:j + 1]

    def local_state(kk, vv, wcol):
        vw = (vv.astype(F32) * wcol).astype(BF16)
        ct = lax.dot_general(kk, vw, (((0,), (0,)), ((), ())), preferred_element_type=F32)
        nn = jnp.sum(kk.astype(F32) * wcol, axis=0, keepdims=True)
        return ct, nn

    for d in range(2):
        ct, nn = local_state(kc_s[...], vc_ref[...], col(8 * nc + d))
        order = list(range(nc)) if d == 0 else list(range(nc - 1, -1, -1))
        for pos, c in enumerate(order):
            idx = d * nc + c
            ct_s[idx] = ct.astype(BF16)
            n_s[idx] = nn
            if pos == nc - 1:
                break
            sl = pl.ds(c * lc, lc)
            ctl, nl = local_state(k_s[sl, :], v_ref[sl, :], col(idx))
            s_old = col_ref[0, 0, 0:1, 8 * nc + 2 + idx:8 * nc + 3 + idx]
            ct = s_old * ct + ctl
            nn = s_old * nn + nl

    tid = lax.broadcasted_iota(jnp.int32, (lc, lc), 0)
    sid = lax.broadcasted_iota(jnp.int32, (lc, lc), 1)
    masks = (sid <= tid, sid >= tid)
    for c in range(nc):
        sl = pl.ds(c * lc, lc)
        q = q_s[sl, :]
        kk = k_s[sl, :]
        v = v_ref[sl, :]
        qf = q.astype(F32)
        s = lax.dot_general(q, kk, (((1,), (1,)), ((), ())), preferred_element_type=F32)
        h = None
        for d in range(2):
            idx = d * nc + c
            r = row_ref[0, 0, idx:idx + 1, :]
            big_m = col(2 * nc + idx)
            s_int = col(4 * nc + idx)
            e_neg = col(6 * nc + idx)
            p = jnp.where(masks[d], jnp.exp(r - big_m), 0.0) * s
            den = (jnp.sum(p, axis=-1, keepdims=True)
                   + s_int * jnp.sum(qf * n_s[idx], axis=-1, keepdims=True))
            num = (jnp.dot(p.astype(BF16), v, preferred_element_type=F32)
                   + s_int * jnp.dot(q, ct_s[idx], preferred_element_type=F32))
            hd = num * (1.0 / jnp.maximum(jnp.abs(den), e_neg))
            h = hd if h is None else h + hd
        mu = jnp.mean(h, axis=-1, keepdims=True)
        hc = h - mu
        var = jnp.mean(hc * hc, axis=-1, keepdims=True)
        hn = hc * lax.rsqrt(var + LN_EPS) * nw_ref[...]
        y_ref[sl, :] = (hn * _sigmoid(o_ref[sl, :].astype(F32))).astype(BF16)


def _mlstm(qkvo, kv_ctx, conv_w, conv_b, rowq, colq, norm_w, bsz, seq, ctx_len):
    nc = seq // MCHUNK
    hq = D_MLSTM // HD
    kern = functools.partial(_mlstm_kernel, nc=nc)
    seq_blk = lambda off: pl.BlockSpec((seq, HD), lambda b, h: (b, off + h))
    ctx_blk = lambda off: pl.BlockSpec((ctx_len, HD), lambda b, h: (b, off + h))
    return pl.pallas_call(
        kern,
        grid=(bsz, HEADS),
        in_specs=[seq_blk(0), seq_blk(hq), seq_blk(2 * hq), seq_blk(3 * hq),
                  ctx_blk(0), ctx_blk(hq),
                  pl.BlockSpec((3, HD), lambda b, h: (0, h)),
                  pl.BlockSpec((1, HD), lambda b, h: (0, h)),
                  pl.BlockSpec((3, HD), lambda b, h: (0, hq + h)),
                  pl.BlockSpec((1, HD), lambda b, h: (0, hq + h)),
                  pl.BlockSpec((1, 1, 2 * nc, MCHUNK), lambda b, h: (b, h, 0, 0)),
                  pl.BlockSpec((1, 1, MCHUNK, LANE), lambda b, h: (b, h, 0, 0)),
                  pl.BlockSpec((1, HD), lambda b, h: (0, h))],
        out_specs=pl.BlockSpec((seq, HD), lambda b, h: (b, h)),
        out_shape=jax.ShapeDtypeStruct((bsz * seq, D_MLSTM), BF16),
        scratch_shapes=[pltpu.VMEM((seq, HD), BF16), pltpu.VMEM((seq, HD), BF16),
                        pltpu.VMEM((ctx_len, HD), BF16),
                        pltpu.VMEM((2 * nc, HD, HD), BF16), pltpu.VMEM((2 * nc, 1, HD), F32)],
        compiler_params=_cparams(("parallel", "parallel")),
        name="mlstm",
    )(qkvo, qkvo, qkvo, qkvo, kv_ctx, kv_ctx, conv_w, conv_b, conv_w, conv_b, rowq, colq, norm_w)


def _out_kernel(ym_ref, uv_ref, x_ref, pe_ref, m_ref, cnw_ref, ws_ref, bs_ref, wout_ref,
                l1w_ref, l1b_ref, wr_ref, br_ref, x1_ref, h2_ref, lg_ref, y_s, *, bm):
    m = m_ref[0]
    y_s[:, :D_MLSTM] = ym_ref[...]
    for g in range(CMLP_GROUPS):
        gs = slice(g * CMLP_GD, (g + 1) * CMLP_GD)
        vg = uv_ref[:, D_CMLP + g * CMLP_GD:D_CMLP + (g + 1) * CMLP_GD].astype(F32)
        mu = jnp.mean(vg, axis=-1, keepdims=True)
        vc = vg - mu
        var = jnp.mean(vc * vc, axis=-1, keepdims=True)
        vn = (vc * lax.rsqrt(var + LN_EPS) * cnw_ref[:, gs]).astype(BF16)
        for p in range(bm // CMLP_CHUNK):
            ps = slice(p * CMLP_CHUNK, (p + 1) * CMLP_CHUNK)
            s = jnp.dot(ws_ref[g], vn[ps, :], preferred_element_type=F32) + bs_ref[:, g:g + 1]
            yc = uv_ref[ps, gs].astype(F32) * s
            y_s[ps, D_MLSTM + g * CMLP_GD:D_MLSTM + (g + 1) * CMLP_GD] = yc.astype(BF16)
    y = jnp.dot(y_s[...], wout_ref[...], preferred_element_type=F32)
    z = DEEPNORM_ALPHA * (x_ref[...] + pe_ref[...]) + m[2:3, :] * y
    x1 = _layer_norm_rows(z, l1w_ref[...], l1b_ref[...])
    x1_ref[...] = x1
    h2 = x1 * (1.0 + m[4:5, :]) + m[3:4, :]
    half = h2.shape[1] // 2
    h2_ref[...] = _pack_bf16_pairs(h2[:, :half], h2[:, half:])
    lg_ref[...] = jnp.dot(h2.astype(BF16), wr_ref[...], preferred_element_type=F32) + br_ref[...]


def _mixer_out(ym, uv, x2d, pe, mods3, cnw, ws, bs_t, wout, l1w, l1b, wr, br, seq, bm):
    rows, dm = x2d.shape
    pe_blocks = seq // bm
    kern = functools.partial(_out_kernel, bm=bm)
    row_blk = lambda n: pl.BlockSpec((bm, n), lambda i: (i, 0))
    return pl.pallas_call(
        kern,
        grid=(rows // bm,),
        in_specs=[row_blk(D_MLSTM), row_blk(2 * D_CMLP), row_blk(dm),
                  pl.BlockSpec((bm, dm), lambda i: (i % pe_blocks, 0)),
                  pl.BlockSpec((1, N_MOD, dm), lambda i: (i // pe_blocks, 0, 0)),
                  _resident(cnw.shape), _resident(ws.shape), _resident(bs_t.shape),
                  _resident(wout.shape), _resident(l1w.shape), _resident(l1b.shape),
                  _resident(wr.shape), _resident(br.shape)],
        out_specs=[row_blk(dm), row_blk(dm // 2), row_blk(LANE)],
        out_shape=[jax.ShapeDtypeStruct((rows, dm), F32),
                   jax.ShapeDtypeStruct((rows, dm // 2), jnp.uint32),
                   jax.ShapeDtypeStruct((rows, LANE), F32)],
        scratch_shapes=[pltpu.VMEM((bm, D_MLSTM + D_CMLP), BF16)],
        compiler_params=_cparams(("parallel",)),
        name="mixer_out",
    )(ym, uv, x2d, pe, mods3, cnw, ws, bs_t, wout, l1w, l1b, wr, br)


def _moe_kernel(comp_ref, blk_ref, slot_ref, cast_ref, cslot_ref, cexp_ref, ctile_ref,
                gtab_ref, stab_ref, h2_hbm, wgf_ref, wuf_ref, wdf_ref, ys_hbm,
                wg_s, wu_s, wd_s, xbuf, xb_s, ybuf, gsem, ssem, pend, *, nt):
    del blk_ref, cexp_ref
    s = pl.program_id(0)
    n_steps = pl.num_programs(0)
    groups = MOE_ROWS // SUBLANES
    half = xbuf.shape[-1]
    dm = 2 * half
    n_asg = ys_hbm.shape[0] * SUBLANES - 2 * MOE_ROWS

    def gather_wait(p, n):
        pltpu.make_async_copy(h2_hbm.at[pl.ds(0, n)], xbuf.at[p, pl.ds(0, n)], gsem.at[p]).wait()

    def scatter_wait(p, n):
        pltpu.make_async_copy(ybuf.at[p, pl.ds(0, n)], ys_hbm.at[pl.ds(0, n)], ssem.at[p]).wait()

    @pl.when(s == 0)
    def _():
        pend[0] = 0
        pend[1] = 0
        xbuf[...] = jnp.zeros_like(xbuf)

    nxt = jnp.minimum(s + 1, n_steps - 1)

    @pl.when(jnp.logical_and(s + 1 < n_steps, comp_ref[nxt] > 0))
    def _():
        p = (s + 1) % 2

        def body(g, carry):
            for i in range(SUBLANES):
                tok = gtab_ref[0, 0, g * SUBLANES + i]
                pltpu.make_async_copy(h2_hbm.at[tok >> 3, pl.ds(tok & 7, 1)], xbuf.at[p, g, pl.ds(i, 1)],
                                      gsem.at[p]).start()
            return carry

        lax.fori_loop(0, comp_ref[nxt], body, 0)

    @pl.when(cast_ref[s] == 1)
    def _():
        cs = cslot_ref[s]
        t = ctile_ref[s]
        for k in range(nt):
            @pl.when(t == k)
            def _():
                wg_s[cs, :, k * MOE_FT:(k + 1) * MOE_FT] = wgf_ref[0].astype(BF16)
                wu_s[cs, :, k * MOE_FT:(k + 1) * MOE_FT] = wuf_ref[0].astype(BF16)
        wd_s[cs, t] = wdf_ref[0].astype(BF16)

    @pl.when(comp_ref[s] > 0)
    def _():
        p = s % 2
        sl = slot_ref[s]
        ng = comp_ref[s]
        gather_wait(p, ng)
        x_lo, x_hi = _unpack_bf16_pairs(xbuf[p].reshape(MOE_ROWS, half))
        xb_s[:, :half] = x_lo.astype(BF16)
        xb_s[:, half:] = x_hi.astype(BF16)
        x = xb_s[...]
        g = jnp.dot(x, wg_s[sl], preferred_element_type=F32)
        u = jnp.dot(x, wu_s[sl], preferred_element_type=F32)
        h = (_silu(g) * u).astype(BF16)
        y = jnp.dot(h, wd_s[sl].reshape(D_EXPERT, dm), preferred_element_type=F32)

        @pl.when(pend[p] > 0)
        def _():
            scatter_wait(p, pend[p])

        ybuf[p] = _pack_bf16_pairs(y[:, :half], y[:, half:]).reshape(groups, SUBLANES, half)

        def body(g, carry):
            for i in range(SUBLANES):
                d = stab_ref[0, 0, g * SUBLANES + i]
                dst = jnp.where(d < 0, n_asg + p * MOE_ROWS + g * SUBLANES + i, d)
                pltpu.make_async_copy(ybuf.at[p, g, pl.ds(i, 1)], ys_hbm.at[dst >> 3, pl.ds(dst & 7, 1)],
                                      ssem.at[p]).start()
            return carry

        lax.fori_loop(0, ng, body, 0)
        pend[p] = ng

    @pl.when(s == n_steps - 1)
    def _():
        for p in range(2):
            @pl.when(pend[p] > 0)
            def _():
                scatter_wait(p, pend[p])
                pend[p] = 0

        xbuf[...] = jnp.zeros_like(xbuf)
        fills = [pltpu.make_async_copy(xbuf.at[p], ys_hbm.at[pl.ds(n_asg // SUBLANES + p * groups, groups)],
                                       gsem.at[p]) for p in range(2)]
        for cp in fills:
            cp.start()
        for cp in fills:
            cp.wait()


def _experts(h2, gtab, stab, sched, wg, wu, wd):
    n_tok, dm = h2.shape[0], 2 * h2.shape[1]
    nt = D_EXPERT // MOE_FT
    n_asg = 2 * n_tok
    n_steps = sched[0].shape[0]

    smem_rows = lambda imap: pl.BlockSpec((1, 1, MOE_ROWS), imap, memory_space=pltpu.SMEM)
    grid_spec = pltpu.PrefetchScalarGridSpec(
        num_scalar_prefetch=7,
        grid=(n_steps,),
        in_specs=[smem_rows(lambda s, comp, blk, *_: (blk[jnp.minimum(s + 1, n_steps - 1)], 0, 0)),
                  smem_rows(lambda s, comp, blk, *_: (blk[s], 0, 0)),
                  pl.BlockSpec(memory_space=pl.ANY),
                  pl.BlockSpec((1, dm, MOE_FT), lambda s, c, b, sl, ca, cs, ce, ct: (ce[s], 0, ct[s])),
                  pl.BlockSpec((1, dm, MOE_FT), lambda s, c, b, sl, ca, cs, ce, ct: (ce[s], 0, ct[s])),
                  pl.BlockSpec((1, MOE_FT, dm), lambda s, c, b, sl, ca, cs, ce, ct: (ce[s], ct[s], 0))],
        out_specs=pl.BlockSpec(memory_space=pl.ANY),
        scratch_shapes=[pltpu.VMEM((2, dm, D_EXPERT), BF16), pltpu.VMEM((2, dm, D_EXPERT), BF16),
                        pltpu.VMEM((2, nt, MOE_FT, dm), BF16),
                        pltpu.VMEM((2, MOE_ROWS // SUBLANES, SUBLANES, dm // 2), jnp.uint32),
                        pltpu.VMEM((MOE_ROWS, dm), BF16),
                        pltpu.VMEM((2, MOE_ROWS // SUBLANES, SUBLANES, dm // 2), jnp.uint32),
                        pltpu.SemaphoreType.DMA((2,)), pltpu.SemaphoreType.DMA((2,)),
                        pltpu.SMEM((2,), jnp.int32)],
    )
    return pl.pallas_call(
        functools.partial(_moe_kernel, nt=nt),
        grid_spec=grid_spec,
        out_shape=jax.ShapeDtypeStruct(((n_asg + 2 * MOE_ROWS) // SUBLANES, SUBLANES, dm // 2), jnp.uint32),
        compiler_params=_cparams(("arbitrary",)),
        name="experts",
    )(*sched, gtab, stab, h2.reshape(n_tok // SUBLANES, SUBLANES, dm // 2), wg, wu, wd)


def _final_kernel(x1_ref, y0_ref, y1_ref, g_ref, m_ref, w_ref, b_ref, o_ref):
    m = m_ref[0]
    lo0, hi0 = _unpack_bf16_pairs(y0_ref[...])
    lo1, hi1 = _unpack_bf16_pairs(y1_ref[...])
    g0 = g_ref[:, 0:1]
    g1 = g_ref[:, 1:2]
    y = jnp.concatenate([g0 * lo0 + g1 * lo1, g0 * hi0 + g1 * hi1], axis=-1)
    z = DEEPNORM_ALPHA * x1_ref[...] + m[5:6, :] * y
    o_ref[...] = _layer_norm_rows(z, w_ref[...], b_ref[...])


def _final(x1, ys, gates, mods3, w, b, seq, bm):
    rows, dm = x1.shape
    blocks_per_batch = seq // bm
    slot_blocks = rows // bm
    row_blk = lambda n: pl.BlockSpec((bm, n), lambda i: (i, 0))
    return pl.pallas_call(
        _final_kernel,
        grid=(rows // bm,),
        in_specs=[row_blk(dm), row_blk(dm // 2), pl.BlockSpec((bm, dm // 2), lambda i: (i + slot_blocks, 0)),
                  row_blk(LANE),
                  pl.BlockSpec((1, N_MOD, dm), lambda i: (i // blocks_per_batch, 0, 0)),
                  _resident(w.shape), _resident(b.shape)],
        out_specs=row_blk(dm),
        out_shape=jax.ShapeDtypeStruct((rows, dm), F32),
        compiler_params=_cparams(("parallel",)),
        name="final_ln",
    )(x1, ys, ys, gates, mods3, w, b)


def _route(logits, n_tok):
    logits1 = logits[:, :N_GROUPS]
    grp = jnp.argmax(logits1, -1).astype(jnp.int32)
    gsel = grp[:, None] == jnp.arange(N_GROUPS, dtype=jnp.int32)[None, :]
    p_grp = jnp.sum(jnp.where(gsel, jax.nn.softmax(logits1, -1), 0.0), -1)
    logits2 = logits[:, N_GROUPS:N_GROUPS + N_EXPERTS].reshape(n_tok, N_GROUPS, EXPERTS_PER_GROUP)
    l2 = jnp.sum(jnp.where(gsel[:, :, None], logits2, 0.0), 1)
    eidx = jnp.arange(EXPERTS_PER_GROUP, dtype=jnp.int32)[None, :]
    i0 = jnp.argmax(l2, -1).astype(jnp.int32)
    v0 = jnp.max(l2, -1)
    l2m = jnp.where(eidx == i0[:, None], -jnp.inf, l2)
    i1 = jnp.argmax(l2m, -1).astype(jnp.int32)
    v1 = jnp.max(l2m, -1)
    top_v = jnp.stack([v0, v1], -1)
    gate = p_grp[:, None] * jax.nn.softmax(top_v, -1)
    e_flat = (grp[:, None] * EXPERTS_PER_GROUP + jnp.stack([i0, i1], -1)).reshape(-1)
    n_asg = e_flat.shape[0]
    blk = 256
    earange = jnp.arange(N_EXPERTS, dtype=jnp.int32)
    onehot = (e_flat[:, None] == earange[None, :])
    oh = onehot.astype(BF16).reshape(n_asg // blk, blk, N_EXPERTS)
    tri = (jnp.arange(blk)[:, None] >= jnp.arange(blk)[None, :]).astype(BF16)
    within = jnp.einsum("ts,bse->bte", tri, oh, preferred_element_type=F32)
    totals = within[:, -1, :]
    offs = jnp.cumsum(totals, axis=0) - totals
    csum = (within + offs[:, None, :]).reshape(n_asg, N_EXPERTS)
    counts = (offs[-1] + totals[-1]).astype(jnp.int32)
    rank = jnp.sum(jnp.where(onehot, csum, 0.0), -1).astype(jnp.int32) - 1
    nblk_e = (counts + MOE_ROWS - 1) // MOE_ROWS
    pad_end = jnp.cumsum(nblk_e * MOE_ROWS)
    pad_start = pad_end - nblk_e * MOE_ROWS
    pos = jnp.sum(jnp.where(onehot, pad_start[None, :], 0), -1) + rank
    n_blk = n_asg // MOE_ROWS + N_EXPERTS

    row_asg = jnp.full((n_blk * MOE_ROWS,), -1, jnp.int32).at[pos].set(jnp.arange(n_asg, dtype=jnp.int32))
    gtab = (jnp.maximum(row_asg, 0) >> 1).reshape(n_blk, 1, MOE_ROWS)
    stab = jnp.where(row_asg < 0, -1, (row_asg & 1) * n_tok + (row_asg >> 1)).reshape(n_blk, 1, MOE_ROWS)

    nt = D_EXPERT // MOE_FT
    n_steps = n_blk + (nt - 1) * N_EXPERTS + nt
    has = nblk_e > 0
    n_visits = jnp.sum(has.astype(jnp.int32))
    e_of_visit = jnp.sort(jnp.where(has, earange, N_EXPERTS))
    vsel = e_of_visit[:, None] == earange[None, :]
    nb_v = jnp.sum(jnp.where(vsel, nblk_e[None, :], 0), -1)
    steps_v = jnp.where(nb_v > 0, jnp.maximum(nb_v, nt), 0)
    end_v = nt + jnp.cumsum(steps_v)
    start_v = end_v - steps_v
    first_blk_v = jnp.cumsum(nb_v) - nb_v
    sidx = jnp.arange(n_steps, dtype=jnp.int32)
    v = jnp.sum((end_v[None, :] <= sidx[:, None]).astype(jnp.int32), -1)
    pick = lambda arr, idx: jnp.sum(jnp.where(idx[:, None] == earange[None, :], arr[None, :], 0), -1)
    in_visit = (sidx >= nt) & (v < n_visits)
    k = sidx - pick(start_v, v)
    comp = in_visit & (k < pick(nb_v, v))
    cnt_v = jnp.sum(jnp.where(vsel, counts[None, :], 0), -1)
    rows_s = jnp.clip(pick(cnt_v, v) - k * MOE_ROWS, 0, MOE_ROWS)
    groups_s = jnp.where(comp, (rows_s + SUBLANES - 1) // SUBLANES, 0)
    blk_s = lax.cummax(jnp.where(comp, pick(first_blk_v, v) + k, 0), axis=0)
    prologue = sidx < nt
    cast = prologue | (in_visit & (k < nt) & (v + 1 < n_visits))
    cexp = jnp.where(prologue, e_of_visit[0], pick(e_of_visit, v + 1))
    ctile = jnp.where(prologue, sidx, k)
    code = lax.cummax(jnp.where(cast, cexp * nt + ctile, 0), axis=0)
    i32 = lambda a: a.astype(jnp.int32)
    sched = (i32(groups_s), i32(blk_s), i32(v % 2), i32(cast), i32(jnp.where(prologue, 0, (v + 1) % 2)),
             i32(jnp.minimum(code // nt, N_EXPERTS - 1)), i32(code % nt))
    return gate, sched, gtab, stab


def _grid_pos_embed(rows):
    quarter = D_MODEL // 4
    omega = 1.0 / (10000.0 ** (jnp.arange(quarter, dtype=F32) / quarter))
    ar = jnp.arange(rows, dtype=F32)[:, None] * omega
    ac = jnp.arange(GRID_W, dtype=F32)[:, None] * omega
    shape = (rows, GRID_W, quarter)
    parts = [jnp.broadcast_to(jnp.sin(ar)[:, None, :], shape), jnp.broadcast_to(jnp.cos(ar)[:, None, :], shape),
             jnp.broadcast_to(jnp.sin(ac)[None, :, :], shape), jnp.broadcast_to(jnp.cos(ac)[None, :, :], shape)]
    return jnp.concatenate(parts, -1).reshape(rows * GRID_W, D_MODEL)


def kernel(x, c, ctx, c_ctx, w_mod, b_mod, w_in, conv_w, conv_b, gate_bias, mlstm_norm_w, cmlp_norm_w,
           w_s, b_s, w_out, ln1_w, ln1_b, router1_w, router1_b, router2_w, router2_b, w_gate, w_up,
           w_down, ln2_w, ln2_b):
    bsz, seq, dm = x.shape
    ctx_len = ctx.shape[1]
    n_tok = bsz * seq
    assert w_mod.shape[0] == 1 and dm == D_MODEL and seq % MCHUNK == 0 and ctx_len == MCHUNK
    pe = _grid_pos_embed(seq // GRID_W).astype(x.dtype)
    x2d = x.reshape(n_tok, dm)
    ctx2d = ctx.reshape(bsz * ctx_len, dm)

    mod_rows = 16
    cc = jnp.concatenate([c, c_ctx[None, :], jnp.zeros((mod_rows - bsz - 1, dm), c.dtype)], 0)
    mods3 = _modulation(cc, w_mod[0], b_mod[0]).reshape(mod_rows, N_MOD, dm)

    dq = D_MLSTM
    wi = w_in[0]
    w_qkvo = wi[:, :4 * dq].astype(BF16)
    w_g = jnp.pad(wi[:, 4 * dq:4 * dq + N_GATE_COLS], ((0, 0), (0, LANE - N_GATE_COLS))).astype(BF16)
    w_uv = wi[:, 4 * dq + N_GATE_COLS:].astype(BF16)
    bm_proj = 256
    blocks_per_seq = seq // bm_proj
    qkvo, g_x, uv = _projection(x2d, pe, mods3, lambda i: i // blocks_per_seq,
                                [w_qkvo, w_g, w_uv], [BF16, F32, BF16], [False, False, True], bm_proj)
    w_kv = wi[:, dq:3 * dq].astype(BF16)
    kv_c, g_c = _projection(ctx2d, None, mods3, lambda i: bsz, [w_kv, w_g], [BF16, F32],
                            [False, False], bm_proj)

    rowq, colq = _gate_stats(g_x, g_c, gate_bias[0], bsz, seq, ctx_len)
    ym = _mlstm(qkvo, kv_c, conv_w[0], conv_b[0].reshape(1, -1), rowq, colq,
                mlstm_norm_w[0].reshape(1, -1), bsz, seq, ctx_len)

    wr = jnp.pad(jnp.concatenate([router1_w[0], router2_w[0]], 1),
                 ((0, 0), (0, LANE - N_GROUPS - N_EXPERTS))).astype(BF16)
    br = jnp.pad(jnp.concatenate([router1_b[0], router2_b[0]], 0),
                 (0, LANE - N_GROUPS - N_EXPERTS)).reshape(1, LANE)
    x1, h2, logits = _mixer_out(ym, uv, x2d, pe, mods3, cmlp_norm_w[0].reshape(1, -1),
                                w_s[0].astype(BF16), b_s[0].T, w_out[0].astype(BF16),
                                ln1_w[0].reshape(1, -1), ln1_b[0].reshape(1, -1), wr, br, seq, 256)

    gate, sched, gtab, stab = _route(logits, n_tok)
    ys = _experts(h2, gtab, stab, sched, w_gate[0], w_up[0], w_down[0])
    gates = jnp.pad(gate.astype(F32), ((0, 0), (0, LANE - 2)))
    out = _final(x1, ys.reshape(-1, dm // 2), gates, mods3, ln2_w[0].reshape(1, -1),
                 ln2_b[0].reshape(1, -1), seq, 256)
    return out.reshape(bsz, seq, dm)
```

```python
import functools

import jax
import jax.numpy as jnp
from jax import lax
from jax.experimental import pallas as pl
from jax.experimental.pallas import tpu as pltpu

F32 = jnp.float32
BF16 = jnp.bfloat16

D_MODEL = 2048
GRID_W = 64
D_MLSTM = 1024
D_CMLP = 1024
HEADS = 4
HD = 256
CMLP_GROUPS = 4
CMLP_GD = 256
CMLP_CHUNK = 128
N_GROUPS = 4
EXPERTS_PER_GROUP = 8
N_EXPERTS = 32
D_EXPERT = 1024
N_MOD = 6
N_GATE_COLS = 16
DEEPNORM_ALPHA = 2.0 ** 0.25
LN_EPS = 1e-6

LANE = 128
SUBLANES = 8
MCHUNK = 256
MOE_ROWS = 256
MOE_FT = 256
VMEM_LIMIT = 56 * 1024 * 1024


def _cparams(sem):
    return pltpu.CompilerParams(dimension_semantics=sem, vmem_limit_bytes=VMEM_LIMIT)


def _resident(shape):
    nd = len(shape)
    return pl.BlockSpec(shape, lambda *_: (0,) * nd, pipeline_mode=pl.Buffered(1))


def _sigmoid(x):
    return 0.5 * jnp.tanh(0.5 * x) + 0.5


def _silu(x):
    return x * _sigmoid(x)


def _log_sigmoid(x):
    return jnp.minimum(x, 0.0) - jnp.log1p(jnp.exp(-jnp.abs(x)))


def _gelu_tanh(x):
    c = 0.7978845608028654
    return 0.5 * x * (1.0 + jnp.tanh(c * (x + 0.044715 * (x * x * x))))


def _pack_bf16_pairs(lo, hi):
    lo_b = lax.bitcast_convert_type(lo.astype(BF16).astype(F32), jnp.uint32)
    hi_b = lax.bitcast_convert_type(hi.astype(BF16).astype(F32), jnp.uint32)
    return (lo_b >> 16) | (hi_b & jnp.uint32(0xFFFF0000))


def _unpack_bf16_pairs(w):
    lo = lax.bitcast_convert_type(w << 16, F32)
    hi = lax.bitcast_convert_type(w & jnp.uint32(0xFFFF0000), F32)
    return lo, hi


def _layer_norm_rows(z, w, b):
    mu = jnp.mean(z, axis=-1, keepdims=True)
    zc = z - mu
    var = jnp.mean(zc * zc, axis=-1, keepdims=True)
    return zc * lax.rsqrt(var + LN_EPS) * w + b


def _mod_kernel(c_ref, w_ref, b_ref, o_ref):
    s = _silu(c_ref[...]).astype(BF16)
    o_ref[...] = jnp.dot(s, w_ref[...].astype(BF16), preferred_element_type=F32) + b_ref[...]


def _modulation(cc, w_mod, b_mod):
    rows, dm = cc.shape
    n = w_mod.shape[1]
    tn = 1024
    return pl.pallas_call(
        _mod_kernel,
        grid=(n // tn,),
        in_specs=[pl.BlockSpec((rows, dm), lambda j: (0, 0)),
                  pl.BlockSpec((dm, tn), lambda j: (0, j)),
                  pl.BlockSpec((1, tn), lambda j: (0, j))],
        out_specs=pl.BlockSpec((rows, tn), lambda j: (0, j)),
        out_shape=jax.ShapeDtypeStruct((rows, n), F32),
        compiler_params=_cparams(("arbitrary",)),
        name="modulation",
    )(cc, w_mod, b_mod.reshape(1, n))


def _proj_kernel(*refs, n_w, has_pe, gelu_flags, tn):
    x_ref = refs[0]
    k = 1
    pe_ref = None
    if has_pe:
        pe_ref = refs[k]
        k += 1
    m_ref = refs[k]
    k += 1
    w_refs = refs[k:k + n_w]
    o_refs = refs[k + n_w:k + 2 * n_w]
    hx_ref = refs[k + 2 * n_w]
    x = x_ref[...]
    if has_pe:
        x = x + pe_ref[...]
    m = m_ref[0]
    hx_ref[...] = (x * (1.0 + m[1:2, :]) + m[0:1, :]).astype(BF16)
    for w_ref, o_ref, use_gelu in zip(w_refs, o_refs, gelu_flags):
        n = w_ref.shape[1]
        step = min(tn, n)
        for j in range(0, n, step):
            acc = jnp.dot(hx_ref[...], w_ref[:, j:j + step], preferred_element_type=F32)
            if use_gelu:
                acc = _gelu_tanh(acc)
            o_ref[:, j:j + step] = acc.astype(o_ref.dtype)


def _projection(x2d, pe, mods3, mod_row_of_block, weights, out_dtypes, gelu_flags, bm):
    rows, dm = x2d.shape
    has_pe = pe is not None
    n_w = len(weights)
    in_specs = [pl.BlockSpec((bm, dm), lambda i: (i, 0))]
    args = [x2d]
    if has_pe:
        pe_blocks = pe.shape[0] // bm
        in_specs.append(pl.BlockSpec((bm, dm), lambda i: (i % pe_blocks, 0)))
        args.append(pe)
    in_specs.append(pl.BlockSpec((1, N_MOD, dm), lambda i: (mod_row_of_block(i), 0, 0)))
    args.append(mods3)
    for w in weights:
        in_specs.append(_resident(w.shape))
        args.append(w)
    out_specs = [pl.BlockSpec((bm, w.shape[1]), lambda i: (i, 0)) for w in weights]
    out_shape = [jax.ShapeDtypeStruct((rows, w.shape[1]), dt) for w, dt in zip(weights, out_dtypes)]
    kern = functools.partial(_proj_kernel, n_w=n_w, has_pe=has_pe, gelu_flags=tuple(gelu_flags), tn=1024)
    return pl.pallas_call(
        kern,
        grid=(rows // bm,),
        in_specs=in_specs,
        out_specs=out_specs,
        out_shape=out_shape,
        scratch_shapes=[pltpu.VMEM((bm, dm), BF16)],
        compiler_params=_cparams(("parallel",)),
        name="projection",
    )(*args)


def _gate_kernel(lic_ref, lfc_ref, lir_ref, lfr_ref, row_ref, col_ref, *, nc):
    li = lic_ref[0, 0]
    lf = _log_sigmoid(lfc_ref[0, 0])
    length = li.shape[0]
    tid = lax.broadcasted_iota(jnp.int32, li.shape, 0)
    lane = lax.broadcasted_iota(jnp.int32, li.shape, 1)
    lane1 = lane[0:1, :]
    fwd = (lane < nc) | (lane == 2 * nc)

    def scan_sublanes(x, op, fill):
        p = x
        s = x
        k = 1
        while k < length:
            p = op(p, jnp.where(tid >= k, pltpu.roll(p, k, 0), fill))
            s = op(s, jnp.where(tid < length - k, pltpu.roll(s, length - k, 0), fill))
            k *= 2
        return jnp.where(fwd, p, s)

    b = scan_sublanes(lf, jnp.add, 0.0)
    btot = jnp.sum(lf, axis=0, keepdims=True)
    a = btot - b + li
    m_loc = jnp.max(a, axis=0, keepdims=True)
    r = li - b
    cm = scan_sublanes(r, jnp.maximum, -jnp.inf)

    m_ctx = jnp.maximum(btot, m_loc)
    m_in = jnp.where(lane1 == 0, pltpu.roll(m_ctx, LANE - 2 * nc, 1), pltpu.roll(m_ctx, LANE - 2, 1))
    for k in range(nc - 1):
        m_new = jnp.maximum(btot + m_in, m_loc)
        m_in = jnp.where(lane1 == k + 1, pltpu.roll(m_new, 1, 1),
                         jnp.where(lane1 == 2 * nc - 2 - k, pltpu.roll(m_new, LANE - 1, 1), m_in))
    is_ctx = lane1 >= 2 * nc
    m_in = jnp.where(is_ctx, 0.0, m_in)
    m_new = jnp.maximum(btot + m_in, m_loc)
    s_old = jnp.broadcast_to(jnp.exp(btot + m_in - m_new), li.shape)
    w = jnp.exp(a - m_new)
    big_m = jnp.maximum(m_in, cm)
    s_int = jnp.exp(m_in - big_m)
    e_neg = jnp.exp(-(b + big_m))
    g = 2 * nc
    col_ref[0, 0] = jnp.where(
        lane < g, w, jnp.where(
            lane < 2 * g, pltpu.roll(big_m, g, 1), jnp.where(
                lane < 3 * g, pltpu.roll(s_int, 2 * g, 1), jnp.where(
                    lane < 4 * g, pltpu.roll(e_neg, 3 * g, 1), jnp.where(
                        lane < 4 * g + 2, pltpu.roll(w, 3 * g, 1), pltpu.roll(s_old, 4 * g + 2, 1))))))

    lir = lir_ref[0, 0]
    lfr = _log_sigmoid(lfr_ref[0, 0])
    width = lir.shape[1]
    rid = lax.broadcasted_iota(jnp.int32, lir.shape, 0)
    pid = lax.broadcasted_iota(jnp.int32, lir.shape, 1)
    p = lfr
    s = lfr
    k = 1
    while k < width:
        p = p + jnp.where(pid >= k, pltpu.roll(p, k, 1), 0.0)
        s = s + jnp.where(pid < width - k, pltpu.roll(s, width - k, 1), 0.0)
        k *= 2
    row_ref[0, 0] = lir - jnp.where(rid < nc, p, s)


def _gate_stats(g_x, g_c, gate_bias, bsz, seq, ctx_len):
    nc = seq // MCHUNK
    gb = gate_bias.astype(F32)
    gx = g_x[:, :N_GATE_COLS].reshape(bsz, nc, MCHUNK, 2, 2, HEADS) + gb.reshape(2, 2, HEADS)
    gc = g_c[:, :N_GATE_COLS].reshape(bsz, ctx_len, 2, 2, HEADS) + gb.reshape(2, 2, HEADS)
    col_x = gx.transpose(4, 0, 5, 2, 3, 1).reshape(2, bsz, HEADS, MCHUNK, 2 * nc)
    col_c = gc.transpose(3, 0, 4, 1, 2)
    col = jnp.concatenate([col_x, col_c], -1)
    col = jnp.pad(col, ((0, 0),) * 4 + ((0, LANE - col.shape[-1]),))
    row = gx.transpose(4, 0, 5, 3, 1, 2).reshape(2, bsz, HEADS, 2 * nc, MCHUNK)
    blk_c = pl.BlockSpec((1, 1, MCHUNK, LANE), lambda b, h: (b, h, 0, 0))
    blk_r = pl.BlockSpec((1, 1, 2 * nc, MCHUNK), lambda b, h: (b, h, 0, 0))
    return pl.pallas_call(
        functools.partial(_gate_kernel, nc=nc),
        grid=(bsz, HEADS),
        in_specs=[blk_c, blk_c, blk_r, blk_r],
        out_specs=[blk_r, blk_c],
        out_shape=[jax.ShapeDtypeStruct((bsz, HEADS, 2 * nc, MCHUNK), F32),
                   jax.ShapeDtypeStruct((bsz, HEADS, MCHUNK, LANE), F32)],
        compiler_params=_cparams(("parallel", "parallel")),
        name="gate_stats",
    )(col[0], col[1], row[0], row[1])


def _mlstm_kernel(q_ref, k_ref, v_ref, o_ref, kc_ref, vc_ref, cwq_ref, cbq_ref, cwk_ref, cbk_ref,
                  row_ref, col_ref, nw_ref, y_ref, q_s, k_s, kc_s, ct_s, n_s, *, nc):
    lc = MCHUNK

    def conv_silu(x, w, b):
        n = x.shape[0]
        rid = lax.broadcasted_iota(jnp.int32, x.shape, 0)
        xm = jnp.where(rid == 0, 0.0, pltpu.roll(x, 1, 0))
        xp = jnp.where(rid == n - 1, 0.0, pltpu.roll(x, n - 1, 0))
        return _silu(xm * w[0:1, :] + x * w[1:2, :] + xp * w[2:3, :] + b)

    k_scale = HD ** -0.5
    q_s[...] = conv_silu(q_ref[...].astype(F32), cwq_ref[...], cbq_ref[...]).astype(BF16)
    k_s[...] = (conv_silu(k_ref[...].astype(F32), cwk_ref[...], cbk_ref[...]) * k_scale).astype(BF16)
    kc_s[...] = (conv_silu(kc_ref[...].astype(F32), cwk_ref[...], cbk_ref[...]) * k_scale).astype(BF16)

    def col(j):
        return col_ref[0, 0, :, j:j + 1]

    def local_state(kk, vv, wcol):
        vw = (vv.astype(F32) * wcol).astype(BF16)
        ct = lax.dot_general(kk, vw, (((0,), (0,)), ((), ())), preferred_element_type=F32)
        nn = jnp.sum(kk.astype(F32) * wcol, axis=0, keepdims=True)
        return ct, nn

    for d in range(2):
        ct, nn = local_state(kc_s[...], vc_ref[...], col(8 * nc + d))
        order = list(range(nc)) if d == 0 else list(range(nc - 1, -1, -1))
        for pos, c in enumerate(order):
            idx = d * nc + c
            ct_s[idx] = ct.astype(BF16)
            n_s[idx] = nn
            if pos == nc - 1:
                break
            sl = pl.ds(c * lc, lc)
            ctl, nl = local_state(k_s[sl, :], v_ref[sl, :], col(idx))
            s_old = col_ref[0, 0, 0:1, 8 * nc + 2 + idx:8 * nc + 3 + idx]
            ct = s_old * ct + ctl
            nn = s_old * nn + nl

    tid = lax.broadcasted_iota(jnp.int32, (lc, lc), 0)
    sid = lax.broadcasted_iota(jnp.int32, (lc, lc), 1)
    masks = (sid <= tid, sid >= tid)
    for c in range(nc):
        sl = pl.ds(c * lc, lc)
        q = q_s[sl, :]
        kk = k_s[sl, :]
        v = v_ref[sl, :]
        qf = q.astype(F32)
        s = lax.dot_general(q, kk, (((1,), (1,)), ((), ())), preferred_element_type=F32)
        h = None
        for d in range(2):
            idx = d * nc + c
            r = row_ref[0, 0, idx:idx + 1, :]
            big_m = col(2 * nc + idx)
            s_int = col(4 * nc + idx)
            e_neg = col(6 * nc + idx)
            p = jnp.where(masks[d], jnp.exp(r - big_m), 0.0) * s
            den = (jnp.sum(p, axis=-1, keepdims=True)
                   + s_int * jnp.sum(qf * n_s[idx], axis=-1, keepdims=True))
            num = (jnp.dot(p.astype(BF16), v, preferred_element_type=F32)
                   + s_int * jnp.dot(q, ct_s[idx], preferred_element_type=F32))
            hd = num * (1.0 / jnp.maximum(jnp.abs(den), e_neg))
            h = hd if h is None else h + hd
        mu = jnp.mean(h, axis=-1, keepdims=True)
        hc = h - mu
        var = jnp.mean(hc * hc, axis=-1, keepdims=True)
        hn = hc * lax.rsqrt(var + LN_EPS) * nw_ref[...]
        y_ref[sl, :] = (hn * _sigmoid(o_ref[sl, :].astype(F32))).astype(BF16)


def _mlstm(qkvo, kv_ctx, conv_w, conv_b, rowq, colq, norm_w, bsz, seq, ctx_len):
    nc = seq // MCHUNK
    hq = D_MLSTM // HD
    kern = functools.partial(_mlstm_kernel, nc=nc)
    seq_blk = lambda off: pl.BlockSpec((seq, HD), lambda b, h: (b, off + h))
    ctx_blk = lambda off: pl.BlockSpec((ctx_len, HD), lambda b, h: (b, off + h))
    return pl.pallas_call(
        kern,
        grid=(bsz, HEADS),
        in_specs=[seq_blk(0), seq_blk(hq), seq_blk(2 * hq), seq_blk(3 * hq),
                  ctx_blk(0), ctx_blk(hq),
                  pl.BlockSpec((3, HD), lambda b, h: (0, h)),
                  pl.BlockSpec((1, HD), lambda b, h: (0, h)),
                  pl.BlockSpec((3, HD), lambda b, h: (0, hq + h)),
                  pl.BlockSpec((1, HD), lambda b, h: (0, hq + h)),
                  pl.BlockSpec((1, 1, 2 * nc, MCHUNK), lambda b, h: (b, h, 0, 0)),
                  pl.BlockSpec((1, 1, MCHUNK, LANE), lambda b, h: (b, h, 0, 0)),
                  pl.BlockSpec((1, HD), lambda b, h: (0, h))],
        out_specs=pl.BlockSpec((seq, HD), lambda b, h: (b, h)),
        out_shape=jax.ShapeDtypeStruct((bsz * seq, D_MLSTM), BF16),
        scratch_shapes=[pltpu.VMEM((seq, HD), BF16), pltpu.VMEM((seq, HD), BF16),
                        pltpu.VMEM((ctx_len, HD), BF16),
                        pltpu.VMEM((2 * nc, HD, HD), BF16), pltpu.VMEM((2 * nc, 1, HD), F32)],
        compiler_params=_cparams(("parallel", "parallel")),
        name="mlstm",
    )(qkvo, qkvo, qkvo, qkvo, kv_ctx, kv_ctx, conv_w, conv_b, conv_w, conv_b, rowq, colq, norm_w)


def _out_kernel(ym_ref, uv_ref, x_ref, pe_ref, m_ref, cnw_ref, ws_ref, bs_ref, wout_ref,
                l1w_ref, l1b_ref, wr_ref, br_ref, x1_ref, h2_ref, lg_ref, y_s, *, bm):
    m = m_ref[0]
    y_s[:, :D_MLSTM] = ym_ref[...]
    for g in range(CMLP_GROUPS):
        gs = slice(g * CMLP_GD, (g + 1) * CMLP_GD)
        vg = uv_ref[:, D_CMLP + g * CMLP_GD:D_CMLP + (g + 1) * CMLP_GD].astype(F32)
        mu = jnp.mean(vg, axis=-1, keepdims=True)
        vc = vg - mu
        var = jnp.mean(vc * vc, axis=-1, keepdims=True)
        vn = (vc * lax.rsqrt(var + LN_EPS) * cnw_ref[:, gs]).astype(BF16)
        for p in range(bm // CMLP_CHUNK):
            ps = slice(p * CMLP_CHUNK, (p + 1) * CMLP_CHUNK)
            s = jnp.dot(ws_ref[g], vn[ps, :], preferred_element_type=F32) + bs_ref[:, g:g + 1]
            yc = uv_ref[ps, gs].astype(F32) * s
            y_s[ps, D_MLSTM + g * CMLP_GD:D_MLSTM + (g + 1) * CMLP_GD] = yc.astype(BF16)
    y = jnp.dot(y_s[...], wout_ref[...], preferred_element_type=F32)
    z = DEEPNORM_ALPHA * (x_ref[...] + pe_ref[...]) + m[2:3, :] * y
    x1 = _layer_norm_rows(z, l1w_ref[...], l1b_ref[...])
    x1_ref[...] = x1
    h2 = x1 * (1.0 + m[4:5, :]) + m[3:4, :]
    half = h2.shape[1] // 2
    h2_ref[...] = _pack_bf16_pairs(h2[:, :half], h2[:, half:])
    lg = jnp.dot(h2.astype(BF16), wr_ref[...], preferred_element_type=F32) + br_ref[...]

    lane = lax.broadcasted_iota(jnp.int32, lg.shape, 1)
    lane_f = lane.astype(F32)
    neg = -jnp.inf
    far = float(LANE)
    is_grp = lane < N_GROUPS
    m1 = jnp.max(jnp.where(is_grp, lg, neg), axis=-1, keepdims=True)
    grp = jnp.min(jnp.where(is_grp & (lg == m1), lane_f, far), axis=-1, keepdims=True)
    p_grp = 1.0 / jnp.sum(jnp.where(is_grp, jnp.exp(lg - m1), 0.0), axis=-1, keepdims=True)
    first = N_GROUPS + EXPERTS_PER_GROUP * grp
    in_grp = (lane_f >= first) & (lane_f < first + EXPERTS_PER_GROUP)
    v0 = jnp.max(jnp.where(in_grp, lg, neg), axis=-1, keepdims=True)
    i0 = jnp.min(jnp.where(in_grp & (lg == v0), lane_f, far), axis=-1, keepdims=True)
    rest = in_grp & (lane_f != i0)
    v1 = jnp.max(jnp.where(rest, lg, neg), axis=-1, keepdims=True)
    i1 = jnp.min(jnp.where(rest & (lg == v1), lane_f, far), axis=-1, keepdims=True)
    s1 = jnp.exp(v1 - v0)
    g0 = p_grp / (1.0 + s1)
    lg_ref[...] = jnp.where(lane == 0, g0, jnp.where(lane == 1, g0 * s1, jnp.where(
        lane == 2, i0 - N_GROUPS, jnp.where(lane == 3, i1 - N_GROUPS, 0.0))))


def _mixer_out(ym, uv, x2d, pe, mods3, cnw, ws, bs_t, wout, l1w, l1b, wr, br, seq, bm):
    rows, dm = x2d.shape
    pe_blocks = seq // bm
    kern = functools.partial(_out_kernel, bm=bm)
    row_blk = lambda n: pl.BlockSpec((bm, n), lambda i: (i, 0))
    return pl.pallas_call(
        kern,
        grid=(rows // bm,),
        in_specs=[row_blk(D_MLSTM), row_blk(2 * D_CMLP), row_blk(dm),
                  pl.BlockSpec((bm, dm), lambda i: (i % pe_blocks, 0)),
                  pl.BlockSpec((1, N_MOD, dm), lambda i: (i // pe_blocks, 0, 0)),
                  _resident(cnw.shape), _resident(ws.shape), _resident(bs_t.shape),
                  _resident(wout.shape), _resident(l1w.shape), _resident(l1b.shape),
                  _resident(wr.shape), _resident(br.shape)],
        out_specs=[row_blk(dm), row_blk(dm // 2), row_blk(LANE)],
        out_shape=[jax.ShapeDtypeStruct((rows, dm), F32),
                   jax.ShapeDtypeStruct((rows, dm // 2), jnp.uint32),
                   jax.ShapeDtypeStruct((rows, LANE), F32)],
        scratch_shapes=[pltpu.VMEM((bm, D_MLSTM + D_CMLP), BF16)],
        compiler_params=_cparams(("parallel",)),
        name="mixer_out",
    )(ym, uv, x2d, pe, mods3, cnw, ws, bs_t, wout, l1w, l1b, wr, br)


def _moe_kernel(comp_ref, blk_ref, slot_ref, cast_ref, cslot_ref, cexp_ref, ctile_ref,
                gtab_ref, stab_ref, h2_hbm, wgf_ref, wuf_ref, wdf_ref, ys_hbm,
                wg_s, wu_s, wd_s, xbuf, xb_s, ybuf, gsem, ssem, pend, *, nt):
    del blk_ref, cexp_ref
    s = pl.program_id(0)
    n_steps = pl.num_programs(0)
    groups = MOE_ROWS // SUBLANES
    half = xbuf.shape[-1]
    dm = 2 * half
    n_asg = ys_hbm.shape[0] * SUBLANES - 2 * MOE_ROWS

    def gather_wait(p, n):
        pltpu.make_async_copy(h2_hbm.at[pl.ds(0, n)], xbuf.at[p, pl.ds(0, n)], gsem.at[p]).wait()

    def scatter_wait(p, n):
        pltpu.make_async_copy(ybuf.at[p, pl.ds(0, n)], ys_hbm.at[pl.ds(0, n)], ssem.at[p]).wait()

    @pl.when(s == 0)
    def _():
        pend[0] = 0
        pend[1] = 0
        xbuf[...] = jnp.zeros_like(xbuf)

    nxt = jnp.minimum(s + 1, n_steps - 1)

    @pl.when(jnp.logical_and(s + 1 < n_steps, comp_ref[nxt] > 0))
    def _():
        p = (s + 1) % 2

        def body(g, carry):
            for i in range(SUBLANES):
                tok = gtab_ref[0, 0, g * SUBLANES + i]
                pltpu.make_async_copy(h2_hbm.at[tok >> 3, pl.ds(tok & 7, 1)], xbuf.at[p, g, pl.ds(i, 1)],
                                      gsem.at[p]).start()
            return carry

        lax.fori_loop(0, comp_ref[nxt], body, 0)

    @pl.when(cast_ref[s] == 1)
    def _():
        cs = cslot_ref[s]
        t = ctile_ref[s]
        for k in range(nt):
            @pl.when(t == k)
            def _():
                wg_s[cs, :, k * MOE_FT:(k + 1) * MOE_FT] = wgf_ref[0].astype(BF16)
                wu_s[cs, :, k * MOE_FT:(k + 1) * MOE_FT] = wuf_ref[0].astype(BF16)
        wd_s[cs, t] = wdf_ref[0].astype(BF16)

    @pl.when(comp_ref[s] > 0)
    def _():
        p = s % 2
        sl = slot_ref[s]
        ng = comp_ref[s]
        gather_wait(p, ng)
        x_lo, x_hi = _unpack_bf16_pairs(xbuf[p].reshape(MOE_ROWS, half))
        xb_s[:, :half] = x_lo.astype(BF16)
        xb_s[:, half:] = x_hi.astype(BF16)
        x = xb_s[...]
        g = jnp.dot(x, wg_s[sl], preferred_element_type=F32)
        u = jnp.dot(x, wu_s[sl], preferred_element_type=F32)
        h = (_silu(g) * u).astype(BF16)
        y = jnp.dot(h, wd_s[sl].reshape(D_EXPERT, dm), preferred_element_type=F32)

        @pl.when(pend[p] > 0)
        def _():
            scatter_wait(p, pend[p])

        ybuf[p] = _pack_bf16_pairs(y[:, :half], y[:, half:]).reshape(groups, SUBLANES, half)

        def body(g, carry):
            for i in range(SUBLANES):
                d = stab_ref[0, 0, g * SUBLANES + i]
                dst = jnp.where(d < 0, n_asg + p * MOE_ROWS + g * SUBLANES + i, d)
                pltpu.make_async_copy(ybuf.at[p, g, pl.ds(i, 1)], ys_hbm.at[dst >> 3, pl.ds(dst & 7, 1)],
                                      ssem.at[p]).start()
            return carry

        lax.fori_loop(0, ng, body, 0)
        pend[p] = ng

    @pl.when(s == n_steps - 1)
    def _():
        for p in range(2):
            @pl.when(pend[p] > 0)
            def _():
                scatter_wait(p, pend[p])
                pend[p] = 0

        xbuf[...] = jnp.zeros_like(xbuf)
        fills = [pltpu.make_async_copy(xbuf.at[p], ys_hbm.at[pl.ds(n_asg // SUBLANES + p * groups, groups)],
                                       gsem.at[p]) for p in range(2)]
        for cp in fills:
            cp.start()
        for cp in fills:
            cp.wait()


def _experts(h2, gtab, stab, sched, wg, wu, wd):
    n_tok, dm = h2.shape[0], 2 * h2.shape[1]
    nt = D_EXPERT // MOE_FT
    n_asg = 2 * n_tok
    n_steps = sched[0].shape[0]

    smem_rows = lambda imap: pl.BlockSpec((1, 1, MOE_ROWS), imap, memory_space=pltpu.SMEM)
    grid_spec = pltpu.PrefetchScalarGridSpec(
        num_scalar_prefetch=7,
        grid=(n_steps,),
        in_specs=[smem_rows(lambda s, comp, blk, *_: (blk[jnp.minimum(s + 1, n_steps - 1)], 0, 0)),
                  smem_rows(lambda s, comp, blk, *_: (blk[s], 0, 0)),
                  pl.BlockSpec(memory_space=pl.ANY),
                  pl.BlockSpec((1, dm, MOE_FT), lambda s, c, b, sl, ca, cs, ce, ct: (ce[s], 0, ct[s])),
                  pl.BlockSpec((1, dm, MOE_FT), lambda s, c, b, sl, ca, cs, ce, ct: (ce[s], 0, ct[s])),
                  pl.BlockSpec((1, MOE_FT, dm), lambda s, c, b, sl, ca, cs, ce, ct: (ce[s], ct[s], 0))],
        out_specs=pl.BlockSpec(memory_space=pl.ANY),
        scratch_shapes=[pltpu.VMEM((2, dm, D_EXPERT), BF16), pltpu.VMEM((2, dm, D_EXPERT), BF16),
                        pltpu.VMEM((2, nt, MOE_FT, dm), BF16),
                        pltpu.VMEM((2, MOE_ROWS // SUBLANES, SUBLANES, dm // 2), jnp.uint32),
                        pltpu.VMEM((MOE_ROWS, dm), BF16),
                        pltpu.VMEM((2, MOE_ROWS // SUBLANES, SUBLANES, dm // 2), jnp.uint32),
                        pltpu.SemaphoreType.DMA((2,)), pltpu.SemaphoreType.DMA((2,)),
                        pltpu.SMEM((2,), jnp.int32)],
    )
    return pl.pallas_call(
        functools.partial(_moe_kernel, nt=nt),
        grid_spec=grid_spec,
        out_shape=jax.ShapeDtypeStruct(((n_asg + 2 * MOE_ROWS) // SUBLANES, SUBLANES, dm // 2), jnp.uint32),
        compiler_params=_cparams(("arbitrary",)),
        name="experts",
    )(*sched, gtab, stab, h2.reshape(n_tok // SUBLANES, SUBLANES, dm // 2), wg, wu, wd)


def _final_kernel(x1_ref, y0_ref, y1_ref, g_ref, m_ref, w_ref, b_ref, o_ref):
    m = m_ref[0]
    lo0, hi0 = _unpack_bf16_pairs(y0_ref[...])
    lo1, hi1 = _unpack_bf16_pairs(y1_ref[...])
    g0 = g_ref[:, 0:1]
    g1 = g_ref[:, 1:2]
    y = jnp.concatenate([g0 * lo0 + g1 * lo1, g0 * hi0 + g1 * hi1], axis=-1)
    z = DEEPNORM_ALPHA * x1_ref[...] + m[5:6, :] * y
    o_ref[...] = _layer_norm_rows(z, w_ref[...], b_ref[...])


def _final(x1, ys, gates, mods3, w, b, seq, bm):
    rows, dm = x1.shape
    blocks_per_batch = seq // bm
    slot_blocks = rows // bm
    row_blk = lambda n: pl.BlockSpec((bm, n), lambda i: (i, 0))
    return pl.pallas_call(
        _final_kernel,
        grid=(rows // bm,),
        in_specs=[row_blk(dm), row_blk(dm // 2), pl.BlockSpec((bm, dm // 2), lambda i: (i + slot_blocks, 0)),
                  row_blk(LANE),
                  pl.BlockSpec((1, N_MOD, dm), lambda i: (i // blocks_per_batch, 0, 0)),
                  _resident(w.shape), _resident(b.shape)],
        out_specs=row_blk(dm),
        out_shape=jax.ShapeDtypeStruct((rows, dm), F32),
        compiler_params=_cparams(("parallel",)),
        name="final_ln",
    )(x1, ys, ys, gates, mods3, w, b)


def _route(route, n_tok):
    e_flat = route[:, 2:4].astype(jnp.int32).reshape(-1)
    n_asg = e_flat.shape[0]
    earange = jnp.arange(N_EXPERTS, dtype=jnp.int32)
    counts = jnp.sum((e_flat[:, None] == earange[None, :]).astype(jnp.int32), 0)
    nblk_e = (counts + MOE_ROWS - 1) // MOE_ROWS
    pad_end = jnp.cumsum(nblk_e * MOE_ROWS)
    pad_start = pad_end - nblk_e * MOE_ROWS
    n_blk = n_asg // MOE_ROWS + N_EXPERTS

    order = jnp.argsort(e_flat, stable=True).astype(jnp.int32)
    starts = jnp.cumsum(counts) - counts
    ridx = jnp.arange(n_blk * MOE_ROWS, dtype=jnp.int32)
    e_row = jnp.minimum(jnp.sum((pad_end[None, :] <= ridx[:, None]).astype(jnp.int32), -1), N_EXPERTS - 1)
    rsel = e_row[:, None] == earange[None, :]
    k_row = ridx - jnp.sum(jnp.where(rsel, pad_start[None, :], 0), -1)
    valid = k_row < jnp.sum(jnp.where(rsel, counts[None, :], 0), -1)
    src = jnp.clip(jnp.sum(jnp.where(rsel, starts[None, :], 0), -1) + k_row, 0, n_asg - 1)
    row_asg = jnp.where(valid, order[src], -1)
    gtab = (jnp.maximum(row_asg, 0) >> 1).reshape(n_blk, 1, MOE_ROWS)
    stab = jnp.where(row_asg < 0, -1, (row_asg & 1) * n_tok + (row_asg >> 1)).reshape(n_blk, 1, MOE_ROWS)

    nt = D_EXPERT // MOE_FT
    n_steps = n_blk + (nt - 1) * N_EXPERTS + nt
    has = nblk_e > 0
    n_visits = jnp.sum(has.astype(jnp.int32))
    e_of_visit = jnp.sort(jnp.where(has, earange, N_EXPERTS))
    vsel = e_of_visit[:, None] == earange[None, :]
    nb_v = jnp.sum(jnp.where(vsel, nblk_e[None, :], 0), -1)
    steps_v = jnp.where(nb_v > 0, jnp.maximum(nb_v, nt), 0)
    end_v = nt + jnp.cumsum(steps_v)
    start_v = end_v - steps_v
    first_blk_v = jnp.cumsum(nb_v) - nb_v
    sidx = jnp.arange(n_steps, dtype=jnp.int32)
    v = jnp.sum((end_v[None, :] <= sidx[:, None]).astype(jnp.int32), -1)
    pick = lambda arr, idx: jnp.sum(jnp.where(idx[:, None] == earange[None, :], arr[None, :], 0), -1)
    in_visit = (sidx >= nt) & (v < n_visits)
    k = sidx - pick(start_v, v)
    comp = in_visit & (k < pick(nb_v, v))
    cnt_v = jnp.sum(jnp.where(vsel, counts[None, :], 0), -1)
    rows_s = jnp.clip(pick(cnt_v, v) - k * MOE_ROWS, 0, MOE_ROWS)
    groups_s = jnp.where(comp, (rows_s + SUBLANES - 1) // SUBLANES, 0)
    blk_s = lax.cummax(jnp.where(comp, pick(first_blk_v, v) + k, 0), axis=0)
    prologue = sidx < nt
    cast = prologue | (in_visit & (k < nt) & (v + 1 < n_visits))
    cexp = jnp.where(prologue, e_of_visit[0], pick(e_of_visit, v + 1))
    ctile = jnp.where(prologue, sidx, k)
    code = lax.cummax(jnp.where(cast, cexp * nt + ctile, 0), axis=0)
    i32 = lambda a: a.astype(jnp.int32)
    sched = (i32(groups_s), i32(blk_s), i32(v % 2), i32(cast), i32(jnp.where(prologue, 0, (v + 1) % 2)),
             i32(jnp.minimum(code // nt, N_EXPERTS - 1)), i32(code % nt))
    return sched, gtab, stab


def _grid_pos_embed(rows):
    quarter = D_MODEL // 4
    omega = 1.0 / (10000.0 ** (jnp.arange(quarter, dtype=F32) / quarter))
    ar = jnp.arange(rows, dtype=F32)[:, None] * omega
    ac = jnp.arange(GRID_W, dtype=F32)[:, None] * omega
    shape = (rows, GRID_W, quarter)
    parts = [jnp.broadcast_to(jnp.sin(ar)[:, None, :], shape), jnp.broadcast_to(jnp.cos(ar)[:, None, :], shape),
             jnp.broadcast_to(jnp.sin(ac)[None, :, :], shape), jnp.broadcast_to(jnp.cos(ac)[None, :, :], shape)]
    return jnp.concatenate(parts, -1).reshape(rows * GRID_W, D_MODEL)


def kernel(x, c, ctx, c_ctx, w_mod, b_mod, w_in, conv_w, conv_b, gate_bias, mlstm_norm_w, cmlp_norm_w,
           w_s, b_s, w_out, ln1_w, ln1_b, router1_w, router1_b, router2_w, router2_b, w_gate, w_up,
           w_down, ln2_w, ln2_b):
    bsz, seq, dm = x.shape
    ctx_len = ctx.shape[1]
    n_tok = bsz * seq
    assert w_mod.shape[0] == 1 and dm == D_MODEL and seq % MCHUNK == 0 and ctx_len == MCHUNK
    pe = _grid_pos_embed(seq // GRID_W).astype(x.dtype)
    x2d = x.reshape(n_tok, dm)
    ctx2d = ctx.reshape(bsz * ctx_len, dm)

    mod_rows = 16
    cc = jnp.concatenate([c, c_ctx[None, :], jnp.zeros((mod_rows - bsz - 1, dm), c.dtype)], 0)
    mods3 = _modulation(cc, w_mod[0], b_mod[0]).reshape(mod_rows, N_MOD, dm)

    dq = D_MLSTM
    wi = w_in[0]
    w_qkvo = wi[:, :4 * dq].astype(BF16)
    w_g = jnp.pad(wi[:, 4 * dq:4 * dq + N_GATE_COLS], ((0, 0), (0, LANE - N_GATE_COLS))).astype(BF16)
    w_uv = wi[:, 4 * dq + N_GATE_COLS:].astype(BF16)
    bm_proj = 256
    blocks_per_seq = seq // bm_proj
    qkvo, g_x, uv = _projection(x2d, pe, mods3, lambda i: i // blocks_per_seq,
                                [w_qkvo, w_g, w_uv], [BF16, F32, BF16], [False, False, True], bm_proj)
    w_kv = wi[:, dq:3 * dq].astype(BF16)
    kv_c, g_c = _projection(ctx2d, None, mods3, lambda i: bsz, [w_kv, w_g], [BF16, F32],
                            [False, False], bm_proj)

    rowq, colq = _gate_stats(g_x, g_c, gate_bias[0], bsz, seq, ctx_len)
    ym = _mlstm(qkvo, kv_c, conv_w[0], conv_b[0].reshape(1, -1), rowq, colq,
                mlstm_norm_w[0].reshape(1, -1), bsz, seq, ctx_len)

    wr = jnp.pad(jnp.concatenate([router1_w[0], router2_w[0]], 1),
                 ((0, 0), (0, LANE - N_GROUPS - N_EXPERTS))).astype(BF16)
    br = jnp.pad(jnp.concatenate([router1_b[0], router2_b[0]], 0),
                 (0, LANE - N_GROUPS - N_EXPERTS)).reshape(1, LANE)
    x1, h2, route = _mixer_out(ym, uv, x2d, pe, mods3, cmlp_norm_w[0].reshape(1, -1),
                                w_s[0].astype(BF16), b_s[0].T, w_out[0].astype(BF16),
                                ln1_w[0].reshape(1, -1), ln1_b[0].reshape(1, -1), wr, br, seq, 256)

    sched, gtab, stab = _route(route, n_tok)
    ys = _experts(h2, gtab, stab, sched, w_gate[0], w_up[0], w_down[0])
    out = _final(x1, ys.reshape(-1, dm // 2), route, mods3, ln2_w[0].reshape(1, -1),
                 ln2_b[0].reshape(1, -1), seq, 256)
    return out.reshape(bsz, seq, dm)
```

```python
import functools

import jax
import jax.numpy as jnp
from jax import lax
from jax.experimental import pallas as pl
from jax.experimental.pallas import tpu as pltpu

F32 = jnp.float32
BF16 = jnp.bfloat16

D_MODEL = 2048
GRID_W = 64
D_MLSTM = 1024
D_CMLP = 1024
HEADS = 4
HD = 256
CMLP_GROUPS = 4
CMLP_GD = 256
CMLP_CHUNK = 128
N_GROUPS = 4
EXPERTS_PER_GROUP = 8
N_EXPERTS = 32
D_EXPERT = 1024
N_MOD = 6
N_GATE_COLS = 16
DEEPNORM_ALPHA = 2.0 ** 0.25
LN_EPS = 1e-6

LANE = 128
SUBLANES = 8
MCHUNK = 256
MOE_ROWS = 256
MOE_FT = 256
GROUP_STEP = 4
VMEM_LIMIT = 56 * 1024 * 1024


def _cparams(sem):
    return pltpu.CompilerParams(dimension_semantics=sem, vmem_limit_bytes=VMEM_LIMIT)


def _resident(shape):
    nd = len(shape)
    return pl.BlockSpec(shape, lambda *_: (0,) * nd, pipeline_mode=pl.Buffered(1))


def _sigmoid(x):
    return 0.5 * jnp.tanh(0.5 * x) + 0.5


def _silu(x):
    return x * _sigmoid(x)


def _log_sigmoid(x):
    return jnp.minimum(x, 0.0) - jnp.log1p(jnp.exp(-jnp.abs(x)))


def _gelu_tanh(x):
    c = 0.7978845608028654
    return 0.5 * x * (1.0 + jnp.tanh(c * (x + 0.044715 * (x * x * x))))


def _pack_bf16_pairs(lo, hi):
    lo_b = lax.bitcast_convert_type(lo.astype(BF16).astype(F32), jnp.uint32)
    hi_b = lax.bitcast_convert_type(hi.astype(BF16).astype(F32), jnp.uint32)
    return (lo_b >> 16) | (hi_b & jnp.uint32(0xFFFF0000))


def _unpack_bf16_pairs(w):
    lo = lax.bitcast_convert_type(w << 16, F32)
    hi = lax.bitcast_convert_type(w & jnp.uint32(0xFFFF0000), F32)
    return lo, hi


def _layer_norm_rows(z, w, b):
    mu = jnp.mean(z, axis=-1, keepdims=True)
    zc = z - mu
    var = jnp.mean(zc * zc, axis=-1, keepdims=True)
    return zc * lax.rsqrt(var + LN_EPS) * w + b


def _mod_kernel(c_ref, w_ref, b_ref, o_ref):
    s = _silu(c_ref[...]).astype(BF16)
    o_ref[...] = jnp.dot(s, w_ref[...].astype(BF16), preferred_element_type=F32) + b_ref[...]


def _modulation(cc, w_mod, b_mod):
    rows, dm = cc.shape
    n = w_mod.shape[1]
    tn = 1024
    return pl.pallas_call(
        _mod_kernel,
        grid=(n // tn,),
        in_specs=[pl.BlockSpec((rows, dm), lambda j: (0, 0)),
                  pl.BlockSpec((dm, tn), lambda j: (0, j)),
                  pl.BlockSpec((1, tn), lambda j: (0, j))],
        out_specs=pl.BlockSpec((rows, tn), lambda j: (0, j)),
        out_shape=jax.ShapeDtypeStruct((rows, n), F32),
        compiler_params=_cparams(("arbitrary",)),
        name="modulation",
    )(cc, w_mod, b_mod.reshape(1, n))


def _proj_kernel(*refs, n_w, has_pe, gelu_flags, tn):
    x_ref = refs[0]
    k = 1
    pe_ref = None
    if has_pe:
        pe_ref = refs[k]
        k += 1
    m_ref = refs[k]
    k += 1
    w_refs = refs[k:k + n_w]
    o_refs = refs[k + n_w:k + 2 * n_w]
    hx_ref = refs[k + 2 * n_w]
    x = x_ref[...]
    if has_pe:
        x = x + pe_ref[...]
    m = m_ref[0]
    hx_ref[...] = (x * (1.0 + m[1:2, :]) + m[0:1, :]).astype(BF16)
    for w_ref, o_ref, use_gelu in zip(w_refs, o_refs, gelu_flags):
        n = w_ref.shape[1]
        step = min(tn, n)
        for j in range(0, n, step):
            acc = jnp.dot(hx_ref[...], w_ref[:, j:j + step], preferred_element_type=F32)
            if use_gelu:
                acc = _gelu_tanh(acc)
            o_ref[:, j:j + step] = acc.astype(o_ref.dtype)


def _projection(x2d, pe, mods3, mod_row_of_block, weights, out_dtypes, gelu_flags, bm):
    rows, dm = x2d.shape
    has_pe = pe is not None
    n_w = len(weights)
    in_specs = [pl.BlockSpec((bm, dm), lambda i: (i, 0))]
    args = [x2d]
    if has_pe:
        pe_blocks = pe.shape[0] // bm
        in_specs.append(pl.BlockSpec((bm, dm), lambda i: (i % pe_blocks, 0)))
        args.append(pe)
    in_specs.append(pl.BlockSpec((1, N_MOD, dm), lambda i: (mod_row_of_block(i), 0, 0)))
    args.append(mods3)
    for w in weights:
        in_specs.append(_resident(w.shape))
        args.append(w)
    out_specs = [pl.BlockSpec((bm, w.shape[1]), lambda i: (i, 0)) for w in weights]
    out_shape = [jax.ShapeDtypeStruct((rows, w.shape[1]), dt) for w, dt in zip(weights, out_dtypes)]
    kern = functools.partial(_proj_kernel, n_w=n_w, has_pe=has_pe, gelu_flags=tuple(gelu_flags), tn=1024)
    return pl.pallas_call(
        kern,
        grid=(rows // bm,),
        in_specs=in_specs,
        out_specs=out_specs,
        out_shape=out_shape,
        scratch_shapes=[pltpu.VMEM((bm, dm), BF16)],
        compiler_params=_cparams(("parallel",)),
        name="projection",
    )(*args)


def _gate_kernel(lic_ref, lfc_ref, lir_ref, lfr_ref, row_ref, col_ref, *, nc):
    li = lic_ref[0, 0]
    lf = _log_sigmoid(lfc_ref[0, 0])
    length = li.shape[0]
    tid = lax.broadcasted_iota(jnp.int32, li.shape, 0)
    lane = lax.broadcasted_iota(jnp.int32, li.shape, 1)
    lane1 = lane[0:1, :]
    fwd = (lane < nc) | (lane == 2 * nc)

    def scan_sublanes(x, op, fill):
        p = x
        s = x
        k = 1
        while k < length:
            p = op(p, jnp.where(tid >= k, pltpu.roll(p, k, 0), fill))
            s = op(s, jnp.where(tid < length - k, pltpu.roll(s, length - k, 0), fill))
            k *= 2
        return jnp.where(fwd, p, s)

    b = scan_sublanes(lf, jnp.add, 0.0)
    btot = jnp.sum(lf, axis=0, keepdims=True)
    a = btot - b + li
    m_loc = jnp.max(a, axis=0, keepdims=True)
    r = li - b
    cm = scan_sublanes(r, jnp.maximum, -jnp.inf)

    m_ctx = jnp.maximum(btot, m_loc)
    m_in = jnp.where(lane1 == 0, pltpu.roll(m_ctx, LANE - 2 * nc, 1), pltpu.roll(m_ctx, LANE - 2, 1))
    for k in range(nc - 1):
        m_new = jnp.maximum(btot + m_in, m_loc)
        m_in = jnp.where(lane1 == k + 1, pltpu.roll(m_new, 1, 1),
                         jnp.where(lane1 == 2 * nc - 2 - k, pltpu.roll(m_new, LANE - 1, 1), m_in))
    is_ctx = lane1 >= 2 * nc
    m_in = jnp.where(is_ctx, 0.0, m_in)
    m_new = jnp.maximum(btot + m_in, m_loc)
    s_old = jnp.broadcast_to(jnp.exp(btot + m_in - m_new), li.shape)
    w = jnp.exp(a - m_new)
    big_m = jnp.maximum(m_in, cm)
    s_int = jnp.exp(m_in - big_m)
    e_neg = jnp.exp(-(b + big_m))
    g = 2 * nc
    col_ref[0, 0] = jnp.where(
        lane < g, w, jnp.where(
            lane < 2 * g, pltpu.roll(big_m, g, 1), jnp.where(
                lane < 3 * g, pltpu.roll(s_int, 2 * g, 1), jnp.where(
                    lane < 4 * g, pltpu.roll(e_neg, 3 * g, 1), jnp.where(
                        lane < 4 * g + 2, pltpu.roll(w, 3 * g, 1), pltpu.roll(s_old, 4 * g + 2, 1))))))

    lir = lir_ref[0, 0]
    lfr = _log_sigmoid(lfr_ref[0, 0])
    width = lir.shape[1]
    rid = lax.broadcasted_iota(jnp.int32, lir.shape, 0)
    pid = lax.broadcasted_iota(jnp.int32, lir.shape, 1)
    p = lfr
    s = lfr
    k = 1
    while k < width:
        p = p + jnp.where(pid >= k, pltpu.roll(p, k, 1), 0.0)
        s = s + jnp.where(pid < width - k, pltpu.roll(s, width - k, 1), 0.0)
        k *= 2
    row_ref[0, 0] = lir - jnp.where(rid < nc, p, s)


def _gate_stats(g_x, g_c, gate_bias, bsz, seq, ctx_len):
    nc = seq // MCHUNK
    gb = gate_bias.astype(F32)
    gx = g_x[:, :N_GATE_COLS].reshape(bsz, nc, MCHUNK, 2, 2, HEADS) + gb.reshape(2, 2, HEADS)
    gc = g_c[:, :N_GATE_COLS].reshape(bsz, ctx_len, 2, 2, HEADS) + gb.reshape(2, 2, HEADS)
    col_x = gx.transpose(4, 0, 5, 2, 3, 1).reshape(2, bsz, HEADS, MCHUNK, 2 * nc)
    col_c = gc.transpose(3, 0, 4, 1, 2)
    col = jnp.concatenate([col_x, col_c], -1)
    col = jnp.pad(col, ((0, 0),) * 4 + ((0, LANE - col.shape[-1]),))
    row = gx.transpose(4, 0, 5, 3, 1, 2).reshape(2, bsz, HEADS, 2 * nc, MCHUNK)
    blk_c = pl.BlockSpec((1, 1, MCHUNK, LANE), lambda b, h: (b, h, 0, 0))
    blk_r = pl.BlockSpec((1, 1, 2 * nc, MCHUNK), lambda b, h: (b, h, 0, 0))
    return pl.pallas_call(
        functools.partial(_gate_kernel, nc=nc),
        grid=(bsz, HEADS),
        in_specs=[blk_c, blk_c, blk_r, blk_r],
        out_specs=[blk_r, blk_c],
        out_shape=[jax.ShapeDtypeStruct((bsz, HEADS, 2 * nc, MCHUNK), F32),
                   jax.ShapeDtypeStruct((bsz, HEADS, MCHUNK, LANE), F32)],
        compiler_params=_cparams(("parallel", "parallel")),
        name="gate_stats",
    )(col[0], col[1], row[0], row[1])


def _mlstm_kernel(q_ref, k_ref, v_ref, o_ref, kc_ref, vc_ref, cwq_ref, cbq_ref, cwk_ref, cbk_ref,
                  row_ref, col_ref, nw_ref, y_ref, q_s, k_s, kc_s, ct_s, n_s, *, nc):
    lc = MCHUNK

    def conv_silu(x, w, b):
        n = x.shape[0]
        rid = lax.broadcasted_iota(jnp.int32, x.shape, 0)
        xm = jnp.where(rid == 0, 0.0, pltpu.roll(x, 1, 0))
        xp = jnp.where(rid == n - 1, 0.0, pltpu.roll(x, n - 1, 0))
        return _silu(xm * w[0:1, :] + x * w[1:2, :] + xp * w[2:3, :] + b)

    k_scale = HD ** -0.5
    q_s[...] = conv_silu(q_ref[...].astype(F32), cwq_ref[...], cbq_ref[...]).astype(BF16)
    k_s[...] = (conv_silu(k_ref[...].astype(F32), cwk_ref[...], cbk_ref[...]) * k_scale).astype(BF16)
    kc_s[...] = (conv_silu(kc_ref[...].astype(F32), cwk_ref[...], cbk_ref[...]) * k_scale).astype(BF16)

    def col(j):
        return col_ref[0, 0, :, j:j + 1]

    def local_state(kk, vv, wcol):
        vw = (vv.astype(F32) * wcol).astype(BF16)
        ct = lax.dot_general(kk, vw, (((0,), (0,)), ((), ())), preferred_element_type=F32)
        nn = jnp.sum(kk.astype(F32) * wcol, axis=0, keepdims=True)
        return ct, nn

    for d in range(2):
        ct, nn = local_state(kc_s[...], vc_ref[...], col(8 * nc + d))
        order = list(range(nc)) if d == 0 else list(range(nc - 1, -1, -1))
        for pos, c in enumerate(order):
            idx = d * nc + c
            ct_s[idx] = ct.astype(BF16)
            n_s[idx] = nn
            if pos == nc - 1:
                break
            sl = pl.ds(c * lc, lc)
            ctl, nl = local_state(k_s[sl, :], v_ref[sl, :], col(idx))
            s_old = col_ref[0, 0, 0:1, 8 * nc + 2 + idx:8 * nc + 3 + idx]
            ct = s_old * ct + ctl
            nn = s_old * nn + nl

    tid = lax.broadcasted_iota(jnp.int32, (lc, lc), 0)
    sid = lax.broadcasted_iota(jnp.int32, (lc, lc), 1)
    masks = (sid <= tid, sid >= tid)
    for c in range(nc):
        sl = pl.ds(c * lc, lc)
        q = q_s[sl, :]
        kk = k_s[sl, :]
        v = v_ref[sl, :]
        qf = q.astype(F32)
        s = lax.dot_general(q, kk, (((1,), (1,)), ((), ())), preferred_element_type=F32)
        h = None
        for d in range(2):
            idx = d * nc + c
            r = row_ref[0, 0, idx:idx + 1, :]
            big_m = col(2 * nc + idx)
            s_int = col(4 * nc + idx)
            e_neg = col(6 * nc + idx)
            p = jnp.where(masks[d], jnp.exp(r - big_m), 0.0) * s
            den = (jnp.sum(p, axis=-1, keepdims=True)
                   + s_int * jnp.sum(qf * n_s[idx], axis=-1, keepdims=True))
            num = (jnp.dot(p.astype(BF16), v, preferred_element_type=F32)
                   + s_int * jnp.dot(q, ct_s[idx], preferred_element_type=F32))
            hd = num * (1.0 / jnp.maximum(jnp.abs(den), e_neg))
            h = hd if h is None else h + hd
        mu = jnp.mean(h, axis=-1, keepdims=True)
        hc = h - mu
        var = jnp.mean(hc * hc, axis=-1, keepdims=True)
        hn = hc * lax.rsqrt(var + LN_EPS) * nw_ref[...]
        y_ref[sl, :] = (hn * _sigmoid(o_ref[sl, :].astype(F32))).astype(BF16)


def _mlstm(qkvo, kv_ctx, conv_w, conv_b, rowq, colq, norm_w, bsz, seq, ctx_len):
    nc = seq // MCHUNK
    hq = D_MLSTM // HD
    kern = functools.partial(_mlstm_kernel, nc=nc)
    seq_blk = lambda off: pl.BlockSpec((seq, HD), lambda b, h: (b, off + h))
    ctx_blk = lambda off: pl.BlockSpec((ctx_len, HD), lambda b, h: (b, off + h))
    return pl.pallas_call(
        kern,
        grid=(bsz, HEADS),
        in_specs=[seq_blk(0), seq_blk(hq), seq_blk(2 * hq), seq_blk(3 * hq),
                  ctx_blk(0), ctx_blk(hq),
                  pl.BlockSpec((3, HD), lambda b, h: (0, h)),
                  pl.BlockSpec((1, HD), lambda b, h: (0, h)),
                  pl.BlockSpec((3, HD), lambda b, h: (0, hq + h)),
                  pl.BlockSpec((1, HD), lambda b, h: (0, hq + h)),
                  pl.BlockSpec((1, 1, 2 * nc, MCHUNK), lambda b, h: (b, h, 0, 0)),
                  pl.BlockSpec((1, 1, MCHUNK, LANE), lambda b, h: (b, h, 0, 0)),
                  pl.BlockSpec((1, HD), lambda b, h: (0, h))],
        out_specs=pl.BlockSpec((seq, HD), lambda b, h: (b, h)),
        out_shape=jax.ShapeDtypeStruct((bsz * seq, D_MLSTM), BF16),
        scratch_shapes=[pltpu.VMEM((seq, HD), BF16), pltpu.VMEM((seq, HD), BF16),
                        pltpu.VMEM((ctx_len, HD), BF16),
                        pltpu.VMEM((2 * nc, HD, HD), BF16), pltpu.VMEM((2 * nc, 1, HD), F32)],
        compiler_params=_cparams(("parallel", "parallel")),
        name="mlstm",
    )(qkvo, qkvo, qkvo, qkvo, kv_ctx, kv_ctx, conv_w, conv_b, conv_w, conv_b, rowq, colq, norm_w)


def _out_kernel(ym_ref, uv_ref, x_ref, pe_ref, m_ref, cnw_ref, ws_ref, bs_ref, wout_ref,
                l1w_ref, l1b_ref, wr_ref, br_ref, x1_ref, h2_ref, lg_ref, y_s, *, bm):
    m = m_ref[0]
    y_s[:, :D_MLSTM] = ym_ref[...]
    for g in range(CMLP_GROUPS):
        gs = slice(g * CMLP_GD, (g + 1) * CMLP_GD)
        vg = uv_ref[:, D_CMLP + g * CMLP_GD:D_CMLP + (g + 1) * CMLP_GD].astype(F32)
        mu = jnp.mean(vg, axis=-1, keepdims=True)
        vc = vg - mu
        var = jnp.mean(vc * vc, axis=-1, keepdims=True)
        vn = (vc * lax.rsqrt(var + LN_EPS) * cnw_ref[:, gs]).astype(BF16)
        for p in range(bm // CMLP_CHUNK):
            ps = slice(p * CMLP_CHUNK, (p + 1) * CMLP_CHUNK)
            s = jnp.dot(ws_ref[g], vn[ps, :], preferred_element_type=F32) + bs_ref[:, g:g + 1]
            yc = uv_ref[ps, gs].astype(F32) * s
            y_s[ps, D_MLSTM + g * CMLP_GD:D_MLSTM + (g + 1) * CMLP_GD] = yc.astype(BF16)
    y = jnp.dot(y_s[...], wout_ref[...], preferred_element_type=F32)
    z = DEEPNORM_ALPHA * (x_ref[...] + pe_ref[...]) + m[2:3, :] * y
    x1 = _layer_norm_rows(z, l1w_ref[...], l1b_ref[...])
    x1_ref[...] = x1
    h2 = x1 * (1.0 + m[4:5, :]) + m[3:4, :]
    half = h2.shape[1] // 2
    h2_ref[...] = _pack_bf16_pairs(h2[:, :half], h2[:, half:])
    lg = jnp.dot(h2.astype(BF16), wr_ref[...], preferred_element_type=F32) + br_ref[...]

    lane = lax.broadcasted_iota(jnp.int32, lg.shape, 1)
    lane_f = lane.astype(F32)
    neg = -jnp.inf
    far = float(LANE)
    is_grp = lane < N_GROUPS
    m1 = jnp.max(jnp.where(is_grp, lg, neg), axis=-1, keepdims=True)
    grp = jnp.min(jnp.where(is_grp & (lg == m1), lane_f, far), axis=-1, keepdims=True)
    p_grp = 1.0 / jnp.sum(jnp.where(is_grp, jnp.exp(lg - m1), 0.0), axis=-1, keepdims=True)
    first = N_GROUPS + EXPERTS_PER_GROUP * grp
    in_grp = (lane_f >= first) & (lane_f < first + EXPERTS_PER_GROUP)
    v0 = jnp.max(jnp.where(in_grp, lg, neg), axis=-1, keepdims=True)
    i0 = jnp.min(jnp.where(in_grp & (lg == v0), lane_f, far), axis=-1, keepdims=True)
    rest = in_grp & (lane_f != i0)
    v1 = jnp.max(jnp.where(rest, lg, neg), axis=-1, keepdims=True)
    i1 = jnp.min(jnp.where(rest & (lg == v1), lane_f, far), axis=-1, keepdims=True)
    s1 = jnp.exp(v1 - v0)
    g0 = p_grp / (1.0 + s1)
    lg_ref[...] = jnp.where(lane == 0, g0, jnp.where(lane == 1, g0 * s1, jnp.where(
        lane == 2, i0 - N_GROUPS, jnp.where(lane == 3, i1 - N_GROUPS, 0.0))))


def _mixer_out(ym, uv, x2d, pe, mods3, cnw, ws, bs_t, wout, l1w, l1b, wr, br, seq, bm):
    rows, dm = x2d.shape
    pe_blocks = seq // bm
    kern = functools.partial(_out_kernel, bm=bm)
    row_blk = lambda n: pl.BlockSpec((bm, n), lambda i: (i, 0))
    return pl.pallas_call(
        kern,
        grid=(rows // bm,),
        in_specs=[row_blk(D_MLSTM), row_blk(2 * D_CMLP), row_blk(dm),
                  pl.BlockSpec((bm, dm), lambda i: (i % pe_blocks, 0)),
                  pl.BlockSpec((1, N_MOD, dm), lambda i: (i // pe_blocks, 0, 0)),
                  _resident(cnw.shape), _resident(ws.shape), _resident(bs_t.shape),
                  _resident(wout.shape), _resident(l1w.shape), _resident(l1b.shape),
                  _resident(wr.shape), _resident(br.shape)],
        out_specs=[row_blk(dm), row_blk(dm // 2), row_blk(LANE)],
        out_shape=[jax.ShapeDtypeStruct((rows, dm), F32),
                   jax.ShapeDtypeStruct((rows, dm // 2), jnp.uint32),
                   jax.ShapeDtypeStruct((rows, LANE), F32)],
        scratch_shapes=[pltpu.VMEM((bm, D_MLSTM + D_CMLP), BF16)],
        compiler_params=_cparams(("parallel",)),
        name="mixer_out",
    )(ym, uv, x2d, pe, mods3, cnw, ws, bs_t, wout, l1w, l1b, wr, br)


def _moe_kernel(comp_ref, blk_ref, slot_ref, cast_ref, cslot_ref, cexp_ref, ctile_ref,
                gtab_ref, stab_ref, h2_hbm, wgf_ref, wuf_ref, wdf_ref, ys_hbm,
                wg_s, wu_s, wd_s, xbuf, xb_s, ybuf, gsem, ssem, pend, *, nt):
    del blk_ref, cexp_ref
    s = pl.program_id(0)
    n_steps = pl.num_programs(0)
    groups = MOE_ROWS // SUBLANES
    half = xbuf.shape[-1]
    dm = 2 * half
    n_asg = ys_hbm.shape[0] * SUBLANES - 2 * MOE_ROWS

    def gather_wait(p, n):
        pltpu.make_async_copy(h2_hbm.at[pl.ds(0, n)], xbuf.at[p, pl.ds(0, n)], gsem.at[p]).wait()

    def scatter_wait(p, n):
        pltpu.make_async_copy(ybuf.at[p, pl.ds(0, n)], ys_hbm.at[pl.ds(0, n)], ssem.at[p]).wait()

    @pl.when(s == 0)
    def _():
        pend[0] = 0
        pend[1] = 0
        xbuf[...] = jnp.zeros_like(xbuf)

    nxt = jnp.minimum(s + 1, n_steps - 1)

    @pl.when(jnp.logical_and(s + 1 < n_steps, comp_ref[nxt] > 0))
    def _():
        p = (s + 1) % 2

        for g0 in range(0, groups, GROUP_STEP):
            @pl.when(g0 < comp_ref[nxt])
            def _():
                for r in range(g0 * SUBLANES, (g0 + GROUP_STEP) * SUBLANES):
                    tok = gtab_ref[0, 0, r]
                    pltpu.make_async_copy(h2_hbm.at[tok >> 3, pl.ds(tok & 7, 1)],
                                          xbuf.at[p, r // SUBLANES, pl.ds(r % SUBLANES, 1)],
                                          gsem.at[p]).start()

    @pl.when(cast_ref[s] == 1)
    def _():
        cs = cslot_ref[s]
        t = ctile_ref[s]
        for k in range(nt):
            @pl.when(t == k)
            def _():
                wg_s[cs, :, k * MOE_FT:(k + 1) * MOE_FT] = wgf_ref[0].astype(BF16)
                wu_s[cs, :, k * MOE_FT:(k + 1) * MOE_FT] = wuf_ref[0].astype(BF16)
        wd_s[cs, t] = wdf_ref[0].astype(BF16)

    @pl.when(comp_ref[s] > 0)
    def _():
        p = s % 2
        sl = slot_ref[s]
        ng = comp_ref[s]
        gather_wait(p, ng)
        x_lo, x_hi = _unpack_bf16_pairs(xbuf[p].reshape(MOE_ROWS, half))
        xb_s[:, :half] = x_lo.astype(BF16)
        xb_s[:, half:] = x_hi.astype(BF16)
        x = xb_s[...]
        g = jnp.dot(x, wg_s[sl], preferred_element_type=F32)
        u = jnp.dot(x, wu_s[sl], preferred_element_type=F32)
        h = (_silu(g) * u).astype(BF16)
        y = jnp.dot(h, wd_s[sl].reshape(D_EXPERT, dm), preferred_element_type=F32)

        @pl.when(pend[p] > 0)
        def _():
            scatter_wait(p, pend[p])

        ybuf[p] = _pack_bf16_pairs(y[:, :half], y[:, half:]).reshape(groups, SUBLANES, half)

        for g0 in range(0, groups, GROUP_STEP):
            @pl.when(g0 < ng)
            def _():
                for r in range(g0 * SUBLANES, (g0 + GROUP_STEP) * SUBLANES):
                    d = stab_ref[0, 0, r]
                    dst = jnp.where(d < 0, n_asg + p * MOE_ROWS + r, d)
                    pltpu.make_async_copy(ybuf.at[p, r // SUBLANES, pl.ds(r % SUBLANES, 1)],
                                          ys_hbm.at[dst >> 3, pl.ds(dst & 7, 1)], ssem.at[p]).start()

        pend[p] = ng

    @pl.when(s == n_steps - 1)
    def _():
        for p in range(2):
            @pl.when(pend[p] > 0)
            def _():
                scatter_wait(p, pend[p])
                pend[p] = 0

        xbuf[...] = jnp.zeros_like(xbuf)
        fills = [pltpu.make_async_copy(xbuf.at[p], ys_hbm.at[pl.ds(n_asg // SUBLANES + p * groups, groups)],
                                       gsem.at[p]) for p in range(2)]
        for cp in fills:
            cp.start()
        for cp in fills:
            cp.wait()


def _experts(h2, gtab, stab, sched, wg, wu, wd):
    n_tok, dm = h2.shape[0], 2 * h2.shape[1]
    nt = D_EXPERT // MOE_FT
    n_asg = 2 * n_tok
    n_steps = sched[0].shape[0]

    smem_rows = lambda imap: pl.BlockSpec((1, 1, MOE_ROWS), imap, memory_space=pltpu.SMEM)
    grid_spec = pltpu.PrefetchScalarGridSpec(
        num_scalar_prefetch=7,
        grid=(n_steps,),
        in_specs=[smem_rows(lambda s, comp, blk, *_: (blk[jnp.minimum(s + 1, n_steps - 1)], 0, 0)),
                  smem_rows(lambda s, comp, blk, *_: (blk[s], 0, 0)),
                  pl.BlockSpec(memory_space=pl.ANY),
                  pl.BlockSpec((1, dm, MOE_FT), lambda s, c, b, sl, ca, cs, ce, ct: (ce[s], 0, ct[s])),
                  pl.BlockSpec((1, dm, MOE_FT), lambda s, c, b, sl, ca, cs, ce, ct: (ce[s], 0, ct[s])),
                  pl.BlockSpec((1, MOE_FT, dm), lambda s, c, b, sl, ca, cs, ce, ct: (ce[s], ct[s], 0))],
        out_specs=pl.BlockSpec(memory_space=pl.ANY),
        scratch_shapes=[pltpu.VMEM((2, dm, D_EXPERT), BF16), pltpu.VMEM((2, dm, D_EXPERT), BF16),
                        pltpu.VMEM((2, nt, MOE_FT, dm), BF16),
                        pltpu.VMEM((2, MOE_ROWS // SUBLANES, SUBLANES, dm // 2), jnp.uint32),
                        pltpu.VMEM((MOE_ROWS, dm), BF16),
                        pltpu.VMEM((2, MOE_ROWS // SUBLANES, SUBLANES, dm // 2), jnp.uint32),
                        pltpu.SemaphoreType.DMA((2,)), pltpu.SemaphoreType.DMA((2,)),
                        pltpu.SMEM((2,), jnp.int32)],
    )
    return pl.pallas_call(
        functools.partial(_moe_kernel, nt=nt),
        grid_spec=grid_spec,
        out_shape=jax.ShapeDtypeStruct(((n_asg + 2 * MOE_ROWS) // SUBLANES, SUBLANES, dm // 2), jnp.uint32),
        compiler_params=_cparams(("arbitrary",)),
        name="experts",
    )(*sched, gtab, stab, h2.reshape(n_tok // SUBLANES, SUBLANES, dm // 2), wg, wu, wd)


def _final_kernel(x1_ref, y0_ref, y1_ref, g_ref, m_ref, w_ref, b_ref, o_ref):
    m = m_ref[0]
    lo0, hi0 = _unpack_bf16_pairs(y0_ref[...])
    lo1, hi1 = _unpack_bf16_pairs(y1_ref[...])
    g0 = g_ref[:, 0:1]
    g1 = g_ref[:, 1:2]
    y = jnp.concatenate([g0 * lo0 + g1 * lo1, g0 * hi0 + g1 * hi1], axis=-1)
    z = DEEPNORM_ALPHA * x1_ref[...] + m[5:6, :] * y
    o_ref[...] = _layer_norm_rows(z, w_ref[...], b_ref[...])


def _final(x1, ys, gates, mods3, w, b, seq, bm):
    rows, dm = x1.shape
    blocks_per_batch = seq // bm
    slot_blocks = rows // bm
    row_blk = lambda n: pl.BlockSpec((bm, n), lambda i: (i, 0))
    return pl.pallas_call(
        _final_kernel,
        grid=(rows // bm,),
        in_specs=[row_blk(dm), row_blk(dm // 2), pl.BlockSpec((bm, dm // 2), lambda i: (i + slot_blocks, 0)),
                  row_blk(LANE),
                  pl.BlockSpec((1, N_MOD, dm), lambda i: (i // blocks_per_batch, 0, 0)),
                  _resident(w.shape), _resident(b.shape)],
        out_specs=row_blk(dm),
        out_shape=jax.ShapeDtypeStruct((rows, dm), F32),
        compiler_params=_cparams(("parallel",)),
        name="final_ln",
    )(x1, ys, ys, gates, mods3, w, b)


def _route(route, n_tok):
    e_flat = route[:, 2:4].astype(jnp.int32).reshape(-1)
    n_asg = e_flat.shape[0]
    earange = jnp.arange(N_EXPERTS, dtype=jnp.int32)
    counts = jnp.sum((e_flat[:, None] == earange[None, :]).astype(jnp.int32), 0)
    nblk_e = (counts + MOE_ROWS - 1) // MOE_ROWS
    pad_end = jnp.cumsum(nblk_e * MOE_ROWS)
    pad_start = pad_end - nblk_e * MOE_ROWS
    n_blk = n_asg // MOE_ROWS + N_EXPERTS

    order = jnp.argsort(e_flat, stable=True).astype(jnp.int32)
    starts = jnp.cumsum(counts) - counts
    ridx = jnp.arange(n_blk * MOE_ROWS, dtype=jnp.int32)
    e_row = jnp.minimum(jnp.sum((pad_end[None, :] <= ridx[:, None]).astype(jnp.int32), -1), N_EXPERTS - 1)
    rsel = e_row[:, None] == earange[None, :]
    k_row = ridx - jnp.sum(jnp.where(rsel, pad_start[None, :], 0), -1)
    valid = k_row < jnp.sum(jnp.where(rsel, counts[None, :], 0), -1)
    src = jnp.clip(jnp.sum(jnp.where(rsel, starts[None, :], 0), -1) + k_row, 0, n_asg - 1)
    row_asg = jnp.where(valid, order[src], -1)
    gtab = (jnp.maximum(row_asg, 0) >> 1).reshape(n_blk, 1, MOE_ROWS)
    stab = jnp.where(row_asg < 0, -1, (row_asg & 1) * n_tok + (row_asg >> 1)).reshape(n_blk, 1, MOE_ROWS)

    nt = D_EXPERT // MOE_FT
    n_steps = n_blk + (nt - 1) * N_EXPERTS + nt
    has = nblk_e > 0
    n_visits = jnp.sum(has.astype(jnp.int32))
    e_of_visit = jnp.sort(jnp.where(has, earange, N_EXPERTS))
    vsel = e_of_visit[:, None] == earange[None, :]
    nb_v = jnp.sum(jnp.where(vsel, nblk_e[None, :], 0), -1)
    steps_v = jnp.where(nb_v > 0, jnp.maximum(nb_v, nt), 0)
    end_v = nt + jnp.cumsum(steps_v)
    start_v = end_v - steps_v
    first_blk_v = jnp.cumsum(nb_v) - nb_v
    sidx = jnp.arange(n_steps, dtype=jnp.int32)
    v = jnp.sum((end_v[None, :] <= sidx[:, None]).astype(jnp.int32), -1)
    pick = lambda arr, idx: jnp.sum(jnp.where(idx[:, None] == earange[None, :], arr[None, :], 0), -1)
    in_visit = (sidx >= nt) & (v < n_visits)
    k = sidx - pick(start_v, v)
    comp = in_visit & (k < pick(nb_v, v))
    cnt_v = jnp.sum(jnp.where(vsel, counts[None, :], 0), -1)
    rows_s = jnp.clip(pick(cnt_v, v) - k * MOE_ROWS, 0, MOE_ROWS)
    chunk_rows = GROUP_STEP * SUBLANES
    groups_s = jnp.where(comp, (rows_s + chunk_rows - 1) // chunk_rows * GROUP_STEP, 0)
    blk_s = lax.cummax(jnp.where(comp, pick(first_blk_v, v) + k, 0), axis=0)
    prologue = sidx < nt
    cast = prologue | (in_visit & (k < nt) & (v + 1 < n_visits))
    cexp = jnp.where(prologue, e_of_visit[0], pick(e_of_visit, v + 1))
    ctile = jnp.where(prologue, sidx, k)
    code = lax.cummax(jnp.where(cast, cexp * nt + ctile, 0), axis=0)
    i32 = lambda a: a.astype(jnp.int32)
    sched = (i32(groups_s), i32(blk_s), i32(v % 2), i32(cast), i32(jnp.where(prologue, 0, (v + 1) % 2)),
             i32(jnp.minimum(code // nt, N_EXPERTS - 1)), i32(code % nt))
    return sched, gtab, stab


def _grid_pos_embed(rows):
    quarter = D_MODEL // 4
    omega = 1.0 / (10000.0 ** (jnp.arange(quarter, dtype=F32) / quarter))
    ar = jnp.arange(rows, dtype=F32)[:, None] * omega
    ac = jnp.arange(GRID_W, dtype=F32)[:, None] * omega
    shape = (rows, GRID_W, quarter)
    parts = [jnp.broadcast_to(jnp.sin(ar)[:, None, :], shape), jnp.broadcast_to(jnp.cos(ar)[:, None, :], shape),
             jnp.broadcast_to(jnp.sin(ac)[None, :, :], shape), jnp.broadcast_to(jnp.cos(ac)[None, :, :], shape)]
    return jnp.concatenate(parts, -1).reshape(rows * GRID_W, D_MODEL)


def kernel(x, c, ctx, c_ctx, w_mod, b_mod, w_in, conv_w, conv_b, gate_bias, mlstm_norm_w, cmlp_norm_w,
           w_s, b_s, w_out, ln1_w, ln1_b, router1_w, router1_b, router2_w, router2_b, w_gate, w_up,
           w_down, ln2_w, ln2_b):
    bsz, seq, dm = x.shape
    ctx_len = ctx.shape[1]
    n_tok = bsz * seq
    assert w_mod.shape[0] == 1 and dm == D_MODEL and seq % MCHUNK == 0 and ctx_len == MCHUNK
    pe = _grid_pos_embed(seq // GRID_W).astype(x.dtype)
    x2d = x.reshape(n_tok, dm)
    ctx2d = ctx.reshape(bsz * ctx_len, dm)

    mod_rows = 16
    cc = jnp.concatenate([c, c_ctx[None, :], jnp.zeros((mod_rows - bsz - 1, dm), c.dtype)], 0)
    mods3 = _modulation(cc, w_mod[0], b_mod[0]).reshape(mod_rows, N_MOD, dm)

    dq = D_MLSTM
    wi = w_in[0]
    w_qkvo = wi[:, :4 * dq].astype(BF16)
    w_g = jnp.pad(wi[:, 4 * dq:4 * dq + N_GATE_COLS], ((0, 0), (0, LANE - N_GATE_COLS))).astype(BF16)
    w_uv = wi[:, 4 * dq + N_GATE_COLS:].astype(BF16)
    bm_proj = 256
    blocks_per_seq = seq // bm_proj
    qkvo, g_x, uv = _projection(x2d, pe, mods3, lambda i: i // blocks_per_seq,
                                [w_qkvo, w_g, w_uv], [BF16, F32, BF16], [False, False, True], bm_proj)
    w_kv = wi[:, dq:3 * dq].astype(BF16)
    kv_c, g_c = _projection(ctx2d, None, mods3, lambda i: bsz, [w_kv, w_g], [BF16, F32],
                            [False, False], bm_proj)

    rowq, colq = _gate_stats(g_x, g_c, gate_bias[0], bsz, seq, ctx_len)
    ym = _mlstm(qkvo, kv_c, conv_w[0], conv_b[0].reshape(1, -1), rowq, colq,
                mlstm_norm_w[0].reshape(1, -1), bsz, seq, ctx_len)

    wr = jnp.pad(jnp.concatenate([router1_w[0], router2_w[0]], 1),
                 ((0, 0), (0, LANE - N_GROUPS - N_EXPERTS))).astype(BF16)
    br = jnp.pad(jnp.concatenate([router1_b[0], router2_b[0]], 0),
                 (0, LANE - N_GROUPS - N_EXPERTS)).reshape(1, LANE)
    x1, h2, route = _mixer_out(ym, uv, x2d, pe, mods3, cmlp_norm_w[0].reshape(1, -1),
                                w_s[0].astype(BF16), b_s[0].T, w_out[0].astype(BF16),
                                ln1_w[0].reshape(1, -1), ln1_b[0].reshape(1, -1), wr, br, seq, 256)

    sched, gtab, stab = _route(route, n_tok)
    ys = _experts(h2, gtab, stab, sched, w_gate[0], w_up[0], w_down[0])
    out = _final(x1, ys.reshape(-1, dm // 2), route, mods3, ln2_w[0].reshape(1, -1),
                 ln2_b[0].reshape(1, -1), seq, 256)
    return out.reshape(bsz, seq, dm)
```

```python
import functools

import jax
import jax.numpy as jnp
from jax import lax
from jax.experimental import pallas as pl
from jax.experimental.pallas import tpu as pltpu

F32 = jnp.float32
BF16 = jnp.bfloat16

D_MODEL = 2048
GRID_W = 64
D_MLSTM = 1024
D_CMLP = 1024
HEADS = 4
HD = 256
CMLP_GROUPS = 4
CMLP_GD = 256
CMLP_CHUNK = 128
N_GROUPS = 4
EXPERTS_PER_GROUP = 8
N_EXPERTS = 32
D_EXPERT = 1024
N_MOD = 6
N_GATE_COLS = 16
DEEPNORM_ALPHA = 2.0 ** 0.25
LN_EPS = 1e-6

LANE = 128
SUBLANES = 8
MCHUNK = 256
MOE_ROWS = 256
MOE_FT = 256
GROUP_STEP = 4
VMEM_LIMIT = 56 * 1024 * 1024


def _cparams(sem):
    return pltpu.CompilerParams(dimension_semantics=sem, vmem_limit_bytes=VMEM_LIMIT)


def _resident(shape):
    nd = len(shape)
    return pl.BlockSpec(shape, lambda *_: (0,) * nd, pipeline_mode=pl.Buffered(1))


def _sigmoid(x):
    return 0.5 * jnp.tanh(0.5 * x) + 0.5


def _silu(x):
    return x * _sigmoid(x)


def _log_sigmoid(x):
    return jnp.minimum(x, 0.0) - jnp.log1p(jnp.exp(-jnp.abs(x)))


def _gelu_tanh(x):
    c = 0.7978845608028654
    return 0.5 * x * (1.0 + jnp.tanh(c * (x + 0.044715 * (x * x * x))))


def _pack_bf16_pairs(lo, hi):
    lo_b = lax.bitcast_convert_type(lo.astype(BF16).astype(F32), jnp.uint32)
    hi_b = lax.bitcast_convert_type(hi.astype(BF16).astype(F32), jnp.uint32)
    return (lo_b >> 16) | (hi_b & jnp.uint32(0xFFFF0000))


def _unpack_bf16_pairs(w):
    lo = lax.bitcast_convert_type(w << 16, F32)
    hi = lax.bitcast_convert_type(w & jnp.uint32(0xFFFF0000), F32)
    return lo, hi


def _layer_norm_rows(z, w, b):
    mu = jnp.mean(z, axis=-1, keepdims=True)
    zc = z - mu
    var = jnp.mean(zc * zc, axis=-1, keepdims=True)
    return zc * lax.rsqrt(var + LN_EPS) * w + b


def _mod_kernel(c_ref, w_ref, b_ref, o_ref):
    s = _silu(c_ref[...]).astype(BF16)
    o_ref[...] = jnp.dot(s, w_ref[...].astype(BF16), preferred_element_type=F32) + b_ref[...]


def _modulation(cc, w_mod, b_mod):
    rows, dm = cc.shape
    n = w_mod.shape[1]
    tn = 1024
    return pl.pallas_call(
        _mod_kernel,
        grid=(n // tn,),
        in_specs=[pl.BlockSpec((rows, dm), lambda j: (0, 0)),
                  pl.BlockSpec((dm, tn), lambda j: (0, j)),
                  pl.BlockSpec((1, tn), lambda j: (0, j))],
        out_specs=pl.BlockSpec((rows, tn), lambda j: (0, j)),
        out_shape=jax.ShapeDtypeStruct((rows, n), F32),
        compiler_params=_cparams(("arbitrary",)),
        name="modulation",
    )(cc, w_mod, b_mod.reshape(1, n))


def _proj_kernel(*refs, n_w, has_pe, gelu_flags, has_cmlp, tn):
    x_ref = refs[0]
    k = 1
    pe_ref = None
    if has_pe:
        pe_ref = refs[k]
        k += 1
    m_ref = refs[k]
    k += 1
    w_refs = refs[k:k + n_w]
    k += n_w
    if has_cmlp:
        cnw_ref, ws_ref, bs_ref = refs[k:k + 3]
        k += 3
    o_refs = refs[k:k + n_w]
    hx_ref = refs[k + n_w]
    x = x_ref[...]
    if has_pe:
        x = x + pe_ref[...]
    m = m_ref[0]
    hx_ref[...] = (x * (1.0 + m[1:2, :]) + m[0:1, :]).astype(BF16)
    n_plain = n_w - 1 if has_cmlp else n_w
    for w_ref, o_ref, use_gelu in list(zip(w_refs, o_refs, gelu_flags))[:n_plain]:
        n = w_ref.shape[1]
        step = min(tn, n)
        for j in range(0, n, step):
            acc = jnp.dot(hx_ref[...], w_ref[:, j:j + step], preferred_element_type=F32)
            if use_gelu:
                acc = _gelu_tanh(acc)
            o_ref[:, j:j + step] = acc.astype(o_ref.dtype)
    if has_cmlp:
        w_ref, o_ref, s_ref = w_refs[-1], o_refs[-1], refs[k + n_w + 1]
        bm = x_ref.shape[0]
        vgate = _gelu_tanh(jnp.dot(hx_ref[...], w_ref[:, D_CMLP:], preferred_element_type=F32))
        for g in range(CMLP_GROUPS):
            gs = slice(g * CMLP_GD, (g + 1) * CMLP_GD)
            vg = vgate[:, gs]
            mu = jnp.mean(vg, axis=-1, keepdims=True)
            vc = vg - mu
            var = jnp.mean(vc * vc, axis=-1, keepdims=True)
            vn = (vc * lax.rsqrt(var + LN_EPS) * cnw_ref[:, gs]).astype(BF16)
            for p in range(bm // CMLP_CHUNK):
                ps = slice(p * CMLP_CHUNK, (p + 1) * CMLP_CHUNK)
                s_ref[ps, gs] = (jnp.dot(ws_ref[g], vn[ps, :], preferred_element_type=F32)
                                 + bs_ref[:, g:g + 1])
        u = _gelu_tanh(jnp.dot(hx_ref[...], w_ref[:, :D_CMLP], preferred_element_type=F32))
        o_ref[...] = (u * s_ref[...]).astype(o_ref.dtype)


def _projection(x2d, pe, mods3, mod_row_of_block, weights, out_dtypes, gelu_flags, bm, cmlp=None):
    rows, dm = x2d.shape
    has_pe = pe is not None
    n_w = len(weights)
    in_specs = [pl.BlockSpec((bm, dm), lambda i: (i, 0))]
    args = [x2d]
    if has_pe:
        pe_blocks = pe.shape[0] // bm
        in_specs.append(pl.BlockSpec((bm, dm), lambda i: (i % pe_blocks, 0)))
        args.append(pe)
    in_specs.append(pl.BlockSpec((1, N_MOD, dm), lambda i: (mod_row_of_block(i), 0, 0)))
    args.append(mods3)
    for w in weights:
        in_specs.append(_resident(w.shape))
        args.append(w)
    out_widths = [w.shape[1] for w in weights]
    scratch = [pltpu.VMEM((bm, dm), BF16)]
    if cmlp is not None:
        for a in cmlp:
            in_specs.append(_resident(a.shape))
            args.append(a)
        out_widths[-1] = D_CMLP
        scratch.append(pltpu.VMEM((bm, D_CMLP), F32))
    out_specs = [pl.BlockSpec((bm, n), lambda i: (i, 0)) for n in out_widths]
    out_shape = [jax.ShapeDtypeStruct((rows, n), dt) for n, dt in zip(out_widths, out_dtypes)]
    kern = functools.partial(_proj_kernel, n_w=n_w, has_pe=has_pe, gelu_flags=tuple(gelu_flags),
                             has_cmlp=cmlp is not None, tn=1024)
    return pl.pallas_call(
        kern,
        grid=(rows // bm,),
        in_specs=in_specs,
        out_specs=out_specs,
        out_shape=out_shape,
        scratch_shapes=scratch,
        compiler_params=_cparams(("parallel",)),
        name="projection",
    )(*args)


def _gate_kernel(lic_ref, lfc_ref, lir_ref, lfr_ref, row_ref, col_ref, *, nc):
    li = lic_ref[0, 0]
    lf = _log_sigmoid(lfc_ref[0, 0])
    length = li.shape[0]
    tid = lax.broadcasted_iota(jnp.int32, li.shape, 0)
    lane = lax.broadcasted_iota(jnp.int32, li.shape, 1)
    lane1 = lane[0:1, :]
    fwd = (lane < nc) | (lane == 2 * nc)

    def scan_sublanes(x, op, fill):
        p = x
        s = x
        k = 1
        while k < length:
            p = op(p, jnp.where(tid >= k, pltpu.roll(p, k, 0), fill))
            s = op(s, jnp.where(tid < length - k, pltpu.roll(s, length - k, 0), fill))
            k *= 2
        return jnp.where(fwd, p, s)

    b = scan_sublanes(lf, jnp.add, 0.0)
    btot = jnp.sum(lf, axis=0, keepdims=True)
    a = btot - b + li
    m_loc = jnp.max(a, axis=0, keepdims=True)
    r = li - b
    cm = scan_sublanes(r, jnp.maximum, -jnp.inf)

    m_ctx = jnp.maximum(btot, m_loc)
    m_in = jnp.where(lane1 == 0, pltpu.roll(m_ctx, LANE - 2 * nc, 1), pltpu.roll(m_ctx, LANE - 2, 1))
    for k in range(nc - 1):
        m_new = jnp.maximum(btot + m_in, m_loc)
        m_in = jnp.where(lane1 == k + 1, pltpu.roll(m_new, 1, 1),
                         jnp.where(lane1 == 2 * nc - 2 - k, pltpu.roll(m_new, LANE - 1, 1), m_in))
    is_ctx = lane1 >= 2 * nc
    m_in = jnp.where(is_ctx, 0.0, m_in)
    m_new = jnp.maximum(btot + m_in, m_loc)
    s_old = jnp.broadcast_to(jnp.exp(btot + m_in - m_new), li.shape)
    w = jnp.exp(a - m_new)
    big_m = jnp.maximum(m_in, cm)
    s_int = jnp.exp(m_in - big_m)
    e_neg = jnp.exp(-(b + big_m))
    g = 2 * nc
    col_ref[0, 0] = jnp.where(
        lane < g, w, jnp.where(
            lane < 2 * g, pltpu.roll(big_m, g, 1), jnp.where(
                lane < 3 * g, pltpu.roll(s_int, 2 * g, 1), jnp.where(
                    lane < 4 * g, pltpu.roll(e_neg, 3 * g, 1), jnp.where(
                        lane < 4 * g + 2, pltpu.roll(w, 3 * g, 1), pltpu.roll(s_old, 4 * g + 2, 1))))))

    lir = lir_ref[0, 0]
    lfr = _log_sigmoid(lfr_ref[0, 0])
    width = lir.shape[1]
    rid = lax.broadcasted_iota(jnp.int32, lir.shape, 0)
    pid = lax.broadcasted_iota(jnp.int32, lir.shape, 1)
    p = lfr
    s = lfr
    k = 1
    while k < width:
        p = p + jnp.where(pid >= k, pltpu.roll(p, k, 1), 0.0)
        s = s + jnp.where(pid < width - k, pltpu.roll(s, width - k, 1), 0.0)
        k *= 2
    row_ref[0, 0] = lir - jnp.where(rid < nc, p, s)


def _gate_stats(g_x, g_c, gate_bias, bsz, seq, ctx_len):
    nc = seq // MCHUNK
    gb = gate_bias.astype(F32)
    gx = g_x[:, :N_GATE_COLS].reshape(bsz, nc, MCHUNK, 2, 2, HEADS) + gb.reshape(2, 2, HEADS)
    gc = g_c[:, :N_GATE_COLS].reshape(bsz, ctx_len, 2, 2, HEADS) + gb.reshape(2, 2, HEADS)
    col_x = gx.transpose(4, 0, 5, 2, 3, 1).reshape(2, bsz, HEADS, MCHUNK, 2 * nc)
    col_c = gc.transpose(3, 0, 4, 1, 2)
    col = jnp.concatenate([col_x, col_c], -1)
    col = jnp.pad(col, ((0, 0),) * 4 + ((0, LANE - col.shape[-1]),))
    row = gx.transpose(4, 0, 5, 3, 1, 2).reshape(2, bsz, HEADS, 2 * nc, MCHUNK)
    blk_c = pl.BlockSpec((1, 1, MCHUNK, LANE), lambda b, h: (b, h, 0, 0))
    blk_r = pl.BlockSpec((1, 1, 2 * nc, MCHUNK), lambda b, h: (b, h, 0, 0))
    return pl.pallas_call(
        functools.partial(_gate_kernel, nc=nc),
        grid=(bsz, HEADS),
        in_specs=[blk_c, blk_c, blk_r, blk_r],
        out_specs=[blk_r, blk_c],
        out_shape=[jax.ShapeDtypeStruct((bsz, HEADS, 2 * nc, MCHUNK), F32),
                   jax.ShapeDtypeStruct((bsz, HEADS, MCHUNK, LANE), F32)],
        compiler_params=_cparams(("parallel", "parallel")),
        name="gate_stats",
    )(col[0], col[1], row[0], row[1])


def _mlstm_kernel(q_ref, k_ref, v_ref, o_ref, kc_ref, vc_ref, cwq_ref, cbq_ref, cwk_ref, cbk_ref,
                  row_ref, col_ref, nw_ref, y_ref, q_s, k_s, kc_s, ct_s, n_s, *, nc):
    lc = MCHUNK

    def conv_silu(x, w, b):
        n = x.shape[0]
        rid = lax.broadcasted_iota(jnp.int32, x.shape, 0)
        xm = jnp.where(rid == 0, 0.0, pltpu.roll(x, 1, 0))
        xp = jnp.where(rid == n - 1, 0.0, pltpu.roll(x, n - 1, 0))
        return _silu(xm * w[0:1, :] + x * w[1:2, :] + xp * w[2:3, :] + b)

    k_scale = HD ** -0.5
    q_s[...] = conv_silu(q_ref[...].astype(F32), cwq_ref[...], cbq_ref[...]).astype(BF16)
    k_s[...] = (conv_silu(k_ref[...].astype(F32), cwk_ref[...], cbk_ref[...]) * k_scale).astype(BF16)
    kc_s[...] = (conv_silu(kc_ref[...].astype(F32), cwk_ref[...], cbk_ref[...]) * k_scale).astype(BF16)

    def col(j):
        return col_ref[0, 0, :, j:j + 1]

    def local_state(kk, vv, wcol):
        vw = (vv.astype(F32) * wcol).astype(BF16)
        ct = lax.dot_general(kk, vw, (((0,), (0,)), ((), ())), preferred_element_type=F32)
        nn = jnp.sum(kk.astype(F32) * wcol, axis=0, keepdims=True)
        return ct, nn

    for d in range(2):
        ct, nn = local_state(kc_s[...], vc_ref[...], col(8 * nc + d))
        order = list(range(nc)) if d == 0 else list(range(nc - 1, -1, -1))
        for pos, c in enumerate(order):
            idx = d * nc + c
            ct_s[idx] = ct.astype(BF16)
            n_s[idx] = nn
            if pos == nc - 1:
                break
            sl = pl.ds(c * lc, lc)
            ctl, nl = local_state(k_s[sl, :], v_ref[sl, :], col(idx))
            s_old = col_ref[0, 0, 0:1, 8 * nc + 2 + idx:8 * nc + 3 + idx]
            ct = s_old * ct + ctl
            nn = s_old * nn + nl

    tid = lax.broadcasted_iota(jnp.int32, (lc, lc), 0)
    sid = lax.broadcasted_iota(jnp.int32, (lc, lc), 1)
    masks = (sid <= tid, sid >= tid)
    for c in range(nc):
        sl = pl.ds(c * lc, lc)
        q = q_s[sl, :]
        kk = k_s[sl, :]
        v = v_ref[sl, :]
        qf = q.astype(F32)
        s = lax.dot_general(q, kk, (((1,), (1,)), ((), ())), preferred_element_type=F32)
        h = None
        for d in range(2):
            idx = d * nc + c
            r = row_ref[0, 0, idx:idx + 1, :]
            big_m = col(2 * nc + idx)
            s_int = col(4 * nc + idx)
            e_neg = col(6 * nc + idx)
            p = jnp.where(masks[d], jnp.exp(r - big_m), 0.0) * s
            den = (jnp.sum(p, axis=-1, keepdims=True)
                   + s_int * jnp.sum(qf * n_s[idx], axis=-1, keepdims=True))
            num = (jnp.dot(p.astype(BF16), v, preferred_element_type=F32)
                   + s_int * jnp.dot(q, ct_s[idx], preferred_element_type=F32))
            hd = num * (1.0 / jnp.maximum(jnp.abs(den), e_neg))
            h = hd if h is None else h + hd
        mu = jnp.mean(h, axis=-1, keepdims=True)
        hc = h - mu
        var = jnp.mean(hc * hc, axis=-1, keepdims=True)
        hn = hc * lax.rsqrt(var + LN_EPS) * nw_ref[...]
        y_ref[sl, :] = (hn * _sigmoid(o_ref[sl, :].astype(F32))).astype(BF16)


def _mlstm(qkvo, kv_ctx, conv_w, conv_b, rowq, colq, norm_w, bsz, seq, ctx_len):
    nc = seq // MCHUNK
    hq = D_MLSTM // HD
    kern = functools.partial(_mlstm_kernel, nc=nc)
    seq_blk = lambda off: pl.BlockSpec((seq, HD), lambda b, h: (b, off + h))
    ctx_blk = lambda off: pl.BlockSpec((ctx_len, HD), lambda b, h: (b, off + h))
    return pl.pallas_call(
        kern,
        grid=(bsz, HEADS),
        in_specs=[seq_blk(0), seq_blk(hq), seq_blk(2 * hq), seq_blk(3 * hq),
                  ctx_blk(0), ctx_blk(hq),
                  pl.BlockSpec((3, HD), lambda b, h: (0, h)),
                  pl.BlockSpec((1, HD), lambda b, h: (0, h)),
                  pl.BlockSpec((3, HD), lambda b, h: (0, hq + h)),
                  pl.BlockSpec((1, HD), lambda b, h: (0, hq + h)),
                  pl.BlockSpec((1, 1, 2 * nc, MCHUNK), lambda b, h: (b, h, 0, 0)),
                  pl.BlockSpec((1, 1, MCHUNK, LANE), lambda b, h: (b, h, 0, 0)),
                  pl.BlockSpec((1, HD), lambda b, h: (0, h))],
        out_specs=pl.BlockSpec((seq, HD), lambda b, h: (b, h)),
        out_shape=jax.ShapeDtypeStruct((bsz * seq, D_MLSTM), BF16),
        scratch_shapes=[pltpu.VMEM((seq, HD), BF16), pltpu.VMEM((seq, HD), BF16),
                        pltpu.VMEM((ctx_len, HD), BF16),
                        pltpu.VMEM((2 * nc, HD, HD), BF16), pltpu.VMEM((2 * nc, 1, HD), F32)],
        compiler_params=_cparams(("parallel", "parallel")),
        name="mlstm",
    )(qkvo, qkvo, qkvo, qkvo, kv_ctx, kv_ctx, conv_w, conv_b, conv_w, conv_b, rowq, colq, norm_w)


def _out_kernel(ym_ref, yc_ref, x_ref, pe_ref, m_ref, wout_ref,
                l1w_ref, l1b_ref, wr_ref, br_ref, x1_ref, h2_ref, lg_ref):
    m = m_ref[0]
    y = (jnp.dot(ym_ref[...], wout_ref[:D_MLSTM, :], preferred_element_type=F32)
         + jnp.dot(yc_ref[...], wout_ref[D_MLSTM:, :], preferred_element_type=F32))
    z = DEEPNORM_ALPHA * (x_ref[...] + pe_ref[...]) + m[2:3, :] * y
    x1 = _layer_norm_rows(z, l1w_ref[...], l1b_ref[...])
    x1_ref[...] = x1
    h2 = x1 * (1.0 + m[4:5, :]) + m[3:4, :]
    half = h2.shape[1] // 2
    h2_ref[...] = _pack_bf16_pairs(h2[:, :half], h2[:, half:])
    lg = jnp.dot(h2.astype(BF16), wr_ref[...], preferred_element_type=F32) + br_ref[...]

    lane = lax.broadcasted_iota(jnp.int32, lg.shape, 1)
    lane_f = lane.astype(F32)
    neg = -jnp.inf
    far = float(LANE)
    is_grp = lane < N_GROUPS
    m1 = jnp.max(jnp.where(is_grp, lg, neg), axis=-1, keepdims=True)
    grp = jnp.min(jnp.where(is_grp & (lg == m1), lane_f, far), axis=-1, keepdims=True)
    p_grp = 1.0 / jnp.sum(jnp.where(is_grp, jnp.exp(lg - m1), 0.0), axis=-1, keepdims=True)
    first = N_GROUPS + EXPERTS_PER_GROUP * grp
    in_grp = (lane_f >= first) & (lane_f < first + EXPERTS_PER_GROUP)
    v0 = jnp.max(jnp.where(in_grp, lg, neg), axis=-1, keepdims=True)
    i0 = jnp.min(jnp.where(in_grp & (lg == v0), lane_f, far), axis=-1, keepdims=True)
    rest = in_grp & (lane_f != i0)
    v1 = jnp.max(jnp.where(rest, lg, neg), axis=-1, keepdims=True)
    i1 = jnp.min(jnp.where(rest & (lg == v1), lane_f, far), axis=-1, keepdims=True)
    s1 = jnp.exp(v1 - v0)
    g0 = p_grp / (1.0 + s1)
    lg_ref[...] = jnp.where(lane == 0, g0, jnp.where(lane == 1, g0 * s1, jnp.where(
        lane == 2, i0 - N_GROUPS, jnp.where(lane == 3, i1 - N_GROUPS, 0.0))))


def _mixer_out(ym, yc, x2d, pe, mods3, wout, l1w, l1b, wr, br, seq, bm):
    rows, dm = x2d.shape
    pe_blocks = seq // bm
    row_blk = lambda n: pl.BlockSpec((bm, n), lambda i: (i, 0))
    return pl.pallas_call(
        _out_kernel,
        grid=(rows // bm,),
        in_specs=[row_blk(D_MLSTM), row_blk(D_CMLP), row_blk(dm),
                  pl.BlockSpec((bm, dm), lambda i: (i % pe_blocks, 0)),
                  pl.BlockSpec((1, N_MOD, dm), lambda i: (i // pe_blocks, 0, 0)),
                  _resident(wout.shape), _resident(l1w.shape), _resident(l1b.shape),
                  _resident(wr.shape), _resident(br.shape)],
        out_specs=[row_blk(dm), row_blk(dm // 2), row_blk(LANE)],
        out_shape=[jax.ShapeDtypeStruct((rows, dm), F32),
                   jax.ShapeDtypeStruct((rows, dm // 2), jnp.uint32),
                   jax.ShapeDtypeStruct((rows, LANE), F32)],
        compiler_params=_cparams(("parallel",)),
        name="mixer_out",
    )(ym, yc, x2d, pe, mods3, wout, l1w, l1b, wr, br)


def _moe_kernel(comp_ref, blk_ref, slot_ref, cast_ref, cslot_ref, cexp_ref, ctile_ref,
                gtab_ref, stab_ref, h2_hbm, wgf_ref, wuf_ref, wdf_ref, ys_hbm,
                wg_s, wu_s, wd_s, xbuf, xb_s, ybuf, gsem, ssem, pend, *, nt):
    del blk_ref, cexp_ref
    s = pl.program_id(0)
    n_steps = pl.num_programs(0)
    groups = MOE_ROWS // SUBLANES
    half = xbuf.shape[-1]
    dm = 2 * half
    n_asg = ys_hbm.shape[0] * SUBLANES - 2 * MOE_ROWS

    def gather_wait(p, n):
        pltpu.make_async_copy(h2_hbm.at[pl.ds(0, n)], xbuf.at[p, pl.ds(0, n)], gsem.at[p]).wait()

    def scatter_wait(p, n):
        pltpu.make_async_copy(ybuf.at[p, pl.ds(0, n)], ys_hbm.at[pl.ds(0, n)], ssem.at[p]).wait()

    @pl.when(s == 0)
    def _():
        pend[0] = 0
        pend[1] = 0
        xbuf[...] = jnp.zeros_like(xbuf)

    nxt = jnp.minimum(s + 1, n_steps - 1)

    @pl.when(jnp.logical_and(s + 1 < n_steps, comp_ref[nxt] > 0))
    def _():
        p = (s + 1) % 2

        for g0 in range(0, groups, GROUP_STEP):
            @pl.when(g0 < comp_ref[nxt])
            def _():
                for r in range(g0 * SUBLANES, (g0 + GROUP_STEP) * SUBLANES):
                    tok = gtab_ref[0, 0, r]
                    pltpu.make_async_copy(h2_hbm.at[tok >> 3, pl.ds(tok & 7, 1)],
                                          xbuf.at[p, r // SUBLANES, pl.ds(r % SUBLANES, 1)],
                                          gsem.at[p]).start()

    @pl.when(cast_ref[s] == 1)
    def _():
        cs = cslot_ref[s]
        t = ctile_ref[s]
        for k in range(nt):
            @pl.when(t == k)
            def _():
                wg_s[cs, :, k * MOE_FT:(k + 1) * MOE_FT] = wgf_ref[0].astype(BF16)
                wu_s[cs, :, k * MOE_FT:(k + 1) * MOE_FT] = wuf_ref[0].astype(BF16)
        wd_s[cs, t] = wdf_ref[0].astype(BF16)

    @pl.when(comp_ref[s] > 0)
    def _():
        p = s % 2
        sl = slot_ref[s]
        ng = comp_ref[s]
        gather_wait(p, ng)
        x_lo, x_hi = _unpack_bf16_pairs(xbuf[p].reshape(MOE_ROWS, half))
        xb_s[:, :half] = x_lo.astype(BF16)
        xb_s[:, half:] = x_hi.astype(BF16)
        x = xb_s[...]
        g = jnp.dot(x, wg_s[sl], preferred_element_type=F32)
        u = jnp.dot(x, wu_s[sl], preferred_element_type=F32)
        h = (_silu(g) * u).astype(BF16)
        y = jnp.dot(h, wd_s[sl].reshape(D_EXPERT, dm), preferred_element_type=F32)

        @pl.when(pend[p] > 0)
        def _():
            scatter_wait(p, pend[p])

        ybuf[p] = _pack_bf16_pairs(y[:, :half], y[:, half:]).reshape(groups, SUBLANES, half)

        for g0 in range(0, groups, GROUP_STEP):
            @pl.when(g0 < ng)
            def _():
                for r in range(g0 * SUBLANES, (g0 + GROUP_STEP) * SUBLANES):
                    d = stab_ref[0, 0, r]
                    dst = jnp.where(d < 0, n_asg + p * MOE_ROWS + r, d)
                    pltpu.make_async_copy(ybuf.at[p, r // SUBLANES, pl.ds(r % SUBLANES, 1)],
                                          ys_hbm.at[dst >> 3, pl.ds(dst & 7, 1)], ssem.at[p]).start()

        pend[p] = ng

    @pl.when(s == n_steps - 1)
    def _():
        for p in range(2):
            @pl.when(pend[p] > 0)
            def _():
                scatter_wait(p, pend[p])
                pend[p] = 0

        xbuf[...] = jnp.zeros_like(xbuf)
        fills = [pltpu.make_async_copy(xbuf.at[p], ys_hbm.at[pl.ds(n_asg // SUBLANES + p * groups, groups)],
                                       gsem.at[p]) for p in range(2)]
        for cp in fills:
            cp.start()
        for cp in fills:
            cp.wait()


def _experts(h2, gtab, stab, sched, wg, wu, wd):
    n_tok, dm = h2.shape[0], 2 * h2.shape[1]
    nt = D_EXPERT // MOE_FT
    n_asg = 2 * n_tok
    n_steps = sched[0].shape[0]

    smem_rows = lambda imap: pl.BlockSpec((1, 1, MOE_ROWS), imap, memory_space=pltpu.SMEM)
    grid_spec = pltpu.PrefetchScalarGridSpec(
        num_scalar_prefetch=7,
        grid=(n_steps,),
        in_specs=[smem_rows(lambda s, comp, blk, *_: (blk[jnp.minimum(s + 1, n_steps - 1)], 0, 0)),
                  smem_rows(lambda s, comp, blk, *_: (blk[s], 0, 0)),
                  pl.BlockSpec(memory_space=pl.ANY),
                  pl.BlockSpec((1, dm, MOE_FT), lambda s, c, b, sl, ca, cs, ce, ct: (ce[s], 0, ct[s])),
                  pl.BlockSpec((1, dm, MOE_FT), lambda s, c, b, sl, ca, cs, ce, ct: (ce[s], 0, ct[s])),
                  pl.BlockSpec((1, MOE_FT, dm), lambda s, c, b, sl, ca, cs, ce, ct: (ce[s], ct[s], 0))],
        out_specs=pl.BlockSpec(memory_space=pl.ANY),
        scratch_shapes=[pltpu.VMEM((2, dm, D_EXPERT), BF16), pltpu.VMEM((2, dm, D_EXPERT), BF16),
                        pltpu.VMEM((2, nt, MOE_FT, dm), BF16),
                        pltpu.VMEM((2, MOE_ROWS // SUBLANES, SUBLANES, dm // 2), jnp.uint32),
                        pltpu.VMEM((MOE_ROWS, dm), BF16),
                        pltpu.VMEM((2, MOE_ROWS // SUBLANES, SUBLANES, dm // 2), jnp.uint32),
                        pltpu.SemaphoreType.DMA((2,)), pltpu.SemaphoreType.DMA((2,)),
                        pltpu.SMEM((2,), jnp.int32)],
    )
    return pl.pallas_call(
        functools.partial(_moe_kernel, nt=nt),
        grid_spec=grid_spec,
        out_shape=jax.ShapeDtypeStruct(((n_asg + 2 * MOE_ROWS) // SUBLANES, SUBLANES, dm // 2), jnp.uint32),
        compiler_params=_cparams(("arbitrary",)),
        name="experts",
    )(*sched, gtab, stab, h2.reshape(n_tok // SUBLANES, SUBLANES, dm // 2), wg, wu, wd)


def _final_kernel(x1_ref, y0_ref, y1_ref, g_ref, m_ref, w_ref, b_ref, o_ref):
    m = m_ref[0]
    lo0, hi0 = _unpack_bf16_pairs(y0_ref[...])
    lo1, hi1 = _unpack_bf16_pairs(y1_ref[...])
    g0 = g_ref[:, 0:1]
    g1 = g_ref[:, 1:2]
    y = jnp.concatenate([g0 * lo0 + g1 * lo1, g0 * hi0 + g1 * hi1], axis=-1)
    z = DEEPNORM_ALPHA * x1_ref[...] + m[5:6, :] * y
    o_ref[...] = _layer_norm_rows(z, w_ref[...], b_ref[...])


def _final(x1, ys, gates, mods3, w, b, seq, bm):
    rows, dm = x1.shape
    blocks_per_batch = seq // bm
    slot_blocks = rows // bm
    row_blk = lambda n: pl.BlockSpec((bm, n), lambda i: (i, 0))
    return pl.pallas_call(
        _final_kernel,
        grid=(rows // bm,),
        in_specs=[row_blk(dm), row_blk(dm // 2), pl.BlockSpec((bm, dm // 2), lambda i: (i + slot_blocks, 0)),
                  row_blk(LANE),
                  pl.BlockSpec((1, N_MOD, dm), lambda i: (i // blocks_per_batch, 0, 0)),
                  _resident(w.shape), _resident(b.shape)],
        out_specs=row_blk(dm),
        out_shape=jax.ShapeDtypeStruct((rows, dm), F32),
        compiler_params=_cparams(("parallel",)),
        name="final_ln",
    )(x1, ys, ys, gates, mods3, w, b)


def _route(route, n_tok):
    e_flat = route[:, 2:4].astype(jnp.int32).reshape(-1)
    n_asg = e_flat.shape[0]
    earange = jnp.arange(N_EXPERTS, dtype=jnp.int32)
    counts = jnp.sum((e_flat[:, None] == earange[None, :]).astype(jnp.int32), 0)
    nblk_e = (counts + MOE_ROWS - 1) // MOE_ROWS
    pad_end = jnp.cumsum(nblk_e * MOE_ROWS)
    pad_start = pad_end - nblk_e * MOE_ROWS
    n_blk = n_asg // MOE_ROWS + N_EXPERTS

    order = jnp.argsort(e_flat, stable=True).astype(jnp.int32)
    starts = jnp.cumsum(counts) - counts
    ridx = jnp.arange(n_blk * MOE_ROWS, dtype=jnp.int32)
    e_row = jnp.minimum(jnp.sum((pad_end[None, :] <= ridx[:, None]).astype(jnp.int32), -1), N_EXPERTS - 1)
    rsel = e_row[:, None] == earange[None, :]
    k_row = ridx - jnp.sum(jnp.where(rsel, pad_start[None, :], 0), -1)
    valid = k_row < jnp.sum(jnp.where(rsel, counts[None, :], 0), -1)
    src = jnp.clip(jnp.sum(jnp.where(rsel, starts[None, :], 0), -1) + k_row, 0, n_asg - 1)
    row_asg = jnp.where(valid, order[src], -1)
    gtab = (jnp.maximum(row_asg, 0) >> 1).reshape(n_blk, 1, MOE_ROWS)
    stab = jnp.where(row_asg < 0, -1, (row_asg & 1) * n_tok + (row_asg >> 1)).reshape(n_blk, 1, MOE_ROWS)

    nt = D_EXPERT // MOE_FT
    n_steps = n_blk + (nt - 1) * N_EXPERTS + nt
    has = nblk_e > 0
    n_visits = jnp.sum(has.astype(jnp.int32))
    e_of_visit = jnp.sort(jnp.where(has, earange, N_EXPERTS))
    vsel = e_of_visit[:, None] == earange[None, :]
    nb_v = jnp.sum(jnp.where(vsel, nblk_e[None, :], 0), -1)
    steps_v = jnp.where(nb_v > 0, jnp.maximum(nb_v, nt), 0)
    end_v = nt + jnp.cumsum(steps_v)
    start_v = end_v - steps_v
    first_blk_v = jnp.cumsum(nb_v) - nb_v
    sidx = jnp.arange(n_steps, dtype=jnp.int32)
    v = jnp.sum((end_v[None, :] <= sidx[:, None]).astype(jnp.int32), -1)
    pick = lambda arr, idx: jnp.sum(jnp.where(idx[:, None] == earange[None, :], arr[None, :], 0), -1)
    in_visit = (sidx >= nt) & (v < n_visits)
    k = sidx - pick(start_v, v)
    comp = in_visit & (k < pick(nb_v, v))
    cnt_v = jnp.sum(jnp.where(vsel, counts[None, :], 0), -1)
    rows_s = jnp.clip(pick(cnt_v, v) - k * MOE_ROWS, 0, MOE_ROWS)
    chunk_rows = GROUP_STEP * SUBLANES
    groups_s = jnp.where(comp, (rows_s + chunk_rows - 1) // chunk_rows * GROUP_STEP, 0)
    blk_s = lax.cummax(jnp.where(comp, pick(first_blk_v, v) + k, 0), axis=0)
    prologue = sidx < nt
    cast = prologue | (in_visit & (k < nt) & (v + 1 < n_visits))
    cexp = jnp.where(prologue, e_of_visit[0], pick(e_of_visit, v + 1))
    ctile = jnp.where(prologue, sidx, k)
    code = lax.cummax(jnp.where(cast, cexp * nt + ctile, 0), axis=0)
    i32 = lambda a: a.astype(jnp.int32)
    sched = (i32(groups_s), i32(blk_s), i32(v % 2), i32(cast), i32(jnp.where(prologue, 0, (v + 1) % 2)),
             i32(jnp.minimum(code // nt, N_EXPERTS - 1)), i32(code % nt))
    return sched, gtab, stab


def _grid_pos_embed(rows):
    quarter = D_MODEL // 4
    omega = 1.0 / (10000.0 ** (jnp.arange(quarter, dtype=F32) / quarter))
    ar = jnp.arange(rows, dtype=F32)[:, None] * omega
    ac = jnp.arange(GRID_W, dtype=F32)[:, None] * omega
    shape = (rows, GRID_W, quarter)
    parts = [jnp.broadcast_to(jnp.sin(ar)[:, None, :], shape), jnp.broadcast_to(jnp.cos(ar)[:, None, :], shape),
             jnp.broadcast_to(jnp.sin(ac)[None, :, :], shape), jnp.broadcast_to(jnp.cos(ac)[None, :, :], shape)]
    return jnp.concatenate(parts, -1).reshape(rows * GRID_W, D_MODEL)


def kernel(x, c, ctx, c_ctx, w_mod, b_mod, w_in, conv_w, conv_b, gate_bias, mlstm_norm_w, cmlp_norm_w,
           w_s, b_s, w_out, ln1_w, ln1_b, router1_w, router1_b, router2_w, router2_b, w_gate, w_up,
           w_down, ln2_w, ln2_b):
    bsz, seq, dm = x.shape
    ctx_len = ctx.shape[1]
    n_tok = bsz * seq
    assert w_mod.shape[0] == 1 and dm == D_MODEL and seq % MCHUNK == 0 and ctx_len == MCHUNK
    pe = _grid_pos_embed(seq // GRID_W).astype(x.dtype)
    x2d = x.reshape(n_tok, dm)
    ctx2d = ctx.reshape(bsz * ctx_len, dm)

    mod_rows = 16
    cc = jnp.concatenate([c, c_ctx[None, :], jnp.zeros((mod_rows - bsz - 1, dm), c.dtype)], 0)
    mods3 = _modulation(cc, w_mod[0], b_mod[0]).reshape(mod_rows, N_MOD, dm)

    dq = D_MLSTM
    wi = w_in[0]
    w_qkvo = wi[:, :4 * dq].astype(BF16)
    w_g = jnp.pad(wi[:, 4 * dq:4 * dq + N_GATE_COLS], ((0, 0), (0, LANE - N_GATE_COLS))).astype(BF16)
    w_uv = wi[:, 4 * dq + N_GATE_COLS:].astype(BF16)
    bm_proj = 256
    blocks_per_seq = seq // bm_proj
    cmlp = (cmlp_norm_w[0].reshape(1, -1), w_s[0].astype(BF16), b_s[0].T)
    qkvo, g_x, yc = _projection(x2d, pe, mods3, lambda i: i // blocks_per_seq,
                                [w_qkvo, w_g, w_uv], [BF16, F32, BF16], [False, False, True], bm_proj,
                                cmlp=cmlp)
    w_kv = wi[:, dq:3 * dq].astype(BF16)
    kv_c, g_c = _projection(ctx2d, None, mods3, lambda i: bsz, [w_kv, w_g], [BF16, F32],
                            [False, False], bm_proj)

    rowq, colq = _gate_stats(g_x, g_c, gate_bias[0], bsz, seq, ctx_len)
    ym = _mlstm(qkvo, kv_c, conv_w[0], conv_b[0].reshape(1, -1), rowq, colq,
                mlstm_norm_w[0].reshape(1, -1), bsz, seq, ctx_len)

    wr = jnp.pad(jnp.concatenate([router1_w[0], router2_w[0]], 1),
                 ((0, 0), (0, LANE - N_GROUPS - N_EXPERTS))).astype(BF16)
    br = jnp.pad(jnp.concatenate([router1_b[0], router2_b[0]], 0),
                 (0, LANE - N_GROUPS - N_EXPERTS)).reshape(1, LANE)
    x1, h2, route = _mixer_out(ym, yc, x2d, pe, mods3, w_out[0].astype(BF16),
                               ln1_w[0].reshape(1, -1), ln1_b[0].reshape(1, -1), wr, br, seq, 256)

    sched, gtab, stab = _route(route, n_tok)
    ys = _experts(h2, gtab, stab, sched, w_gate[0], w_up[0], w_down[0])
    out = _final(x1, ys.reshape(-1, dm // 2), route, mods3, ln2_w[0].reshape(1, -1),
                 ln2_b[0].reshape(1, -1), seq, 256)
    return out.reshape(bsz, seq, dm)
```

```python
import functools

import jax
import jax.numpy as jnp
from jax import lax
from jax.experimental import pallas as pl
from jax.experimental.pallas import tpu as pltpu

F32 = jnp.float32
BF16 = jnp.bfloat16

D_MODEL = 2048
GRID_W = 64
D_MLSTM = 1024
D_CMLP = 1024
HEADS = 4
HD = 256
CMLP_GROUPS = 4
CMLP_GD = 256
CMLP_CHUNK = 128
N_GROUPS = 4
EXPERTS_PER_GROUP = 8
N_EXPERTS = 32
D_EXPERT = 1024
N_MOD = 6
N_GATE_COLS = 16
DEEPNORM_ALPHA = 2.0 ** 0.25
LN_EPS = 1e-6

LANE = 128
SUBLANES = 8
MCHUNK = 256
MOE_ROWS = 256
MOE_FT = 256
GROUP_STEP = 4
VMEM_LIMIT = 56 * 1024 * 1024


def _cparams(sem):
    return pltpu.CompilerParams(dimension_semantics=sem, vmem_limit_bytes=VMEM_LIMIT)


def _resident(shape):
    nd = len(shape)
    return pl.BlockSpec(shape, lambda *_: (0,) * nd, pipeline_mode=pl.Buffered(1))


def _sigmoid(x):
    return 0.5 * jnp.tanh(0.5 * x) + 0.5


def _silu(x):
    return x * _sigmoid(x)


def _log_sigmoid(x):
    return jnp.minimum(x, 0.0) - jnp.log1p(jnp.exp(-jnp.abs(x)))


def _gelu_tanh(x):
    c = 0.7978845608028654
    return 0.5 * x * (1.0 + jnp.tanh(c * (x + 0.044715 * (x * x * x))))


def _pack_bf16_pairs(lo, hi):
    lo_b = lax.bitcast_convert_type(lo.astype(BF16).astype(F32), jnp.uint32)
    hi_b = lax.bitcast_convert_type(hi.astype(BF16).astype(F32), jnp.uint32)
    return (lo_b >> 16) | (hi_b & jnp.uint32(0xFFFF0000))


def _unpack_bf16_pairs(w):
    lo = lax.bitcast_convert_type(w << 16, F32)
    hi = lax.bitcast_convert_type(w & jnp.uint32(0xFFFF0000), F32)
    return lo, hi


def _layer_norm_rows(z, w, b):
    mu = jnp.mean(z, axis=-1, keepdims=True)
    zc = z - mu
    var = jnp.mean(zc * zc, axis=-1, keepdims=True)
    return zc * lax.rsqrt(var + LN_EPS) * w + b


def _mod_kernel(c_ref, w_ref, b_ref, o_ref):
    s = _silu(c_ref[...]).astype(BF16)
    o_ref[...] = jnp.dot(s, w_ref[...].astype(BF16), preferred_element_type=F32) + b_ref[...]


def _modulation(cc, w_mod, b_mod):
    rows, dm = cc.shape
    n = w_mod.shape[1]
    tn = 1024
    return pl.pallas_call(
        _mod_kernel,
        grid=(n // tn,),
        in_specs=[pl.BlockSpec((rows, dm), lambda j: (0, 0)),
                  pl.BlockSpec((dm, tn), lambda j: (0, j)),
                  pl.BlockSpec((1, tn), lambda j: (0, j))],
        out_specs=pl.BlockSpec((rows, tn), lambda j: (0, j)),
        out_shape=jax.ShapeDtypeStruct((rows, n), F32),
        compiler_params=_cparams(("arbitrary",)),
        name="modulation",
    )(cc, w_mod, b_mod.reshape(1, n))


def _proj_kernel(*refs, n_w, has_pe, gelu_flags, has_cmlp, tn):
    x_ref = refs[0]
    k = 1
    pe_ref = None
    if has_pe:
        pe_ref = refs[k]
        k += 1
    m_ref = refs[k]
    k += 1
    w_refs = refs[k:k + n_w]
    k += n_w
    if has_cmlp:
        cnw_ref, ws_ref, bs_ref = refs[k:k + 3]
        k += 3
    o_refs = refs[k:k + n_w]
    hx_ref = refs[k + n_w]
    x = x_ref[...]
    if has_pe:
        x = x + pe_ref[...]
    m = m_ref[0]
    hx_ref[...] = (x * (1.0 + m[1:2, :]) + m[0:1, :]).astype(BF16)
    n_plain = n_w - 1 if has_cmlp else n_w
    for w_ref, o_ref, use_gelu in list(zip(w_refs, o_refs, gelu_flags))[:n_plain]:
        n = w_ref.shape[1]
        step = min(tn, n)
        for j in range(0, n, step):
            acc = jnp.dot(hx_ref[...], w_ref[:, j:j + step], preferred_element_type=F32)
            if use_gelu:
                acc = _gelu_tanh(acc)
            o_ref[:, j:j + step] = acc.astype(o_ref.dtype)
    if has_cmlp:
        w_ref, o_ref, s_ref = w_refs[-1], o_refs[-1], refs[k + n_w + 1]
        bm = x_ref.shape[0]
        vgate = _gelu_tanh(jnp.dot(hx_ref[...], w_ref[:, D_CMLP:], preferred_element_type=F32))
        for g in range(CMLP_GROUPS):
            gs = slice(g * CMLP_GD, (g + 1) * CMLP_GD)
            vg = vgate[:, gs]
            mu = jnp.mean(vg, axis=-1, keepdims=True)
            vc = vg - mu
            var = jnp.mean(vc * vc, axis=-1, keepdims=True)
            vn = (vc * lax.rsqrt(var + LN_EPS) * cnw_ref[:, gs]).astype(BF16)
            for p in range(bm // CMLP_CHUNK):
                ps = slice(p * CMLP_CHUNK, (p + 1) * CMLP_CHUNK)
                s_ref[ps, gs] = (jnp.dot(ws_ref[g], vn[ps, :], preferred_element_type=F32)
                                 + bs_ref[:, g:g + 1])
        u = _gelu_tanh(jnp.dot(hx_ref[...], w_ref[:, :D_CMLP], preferred_element_type=F32))
        o_ref[...] = (u * s_ref[...]).astype(o_ref.dtype)


def _projection(x2d, pe, mods3, mod_row_of_block, weights, out_dtypes, gelu_flags, bm, cmlp=None):
    rows, dm = x2d.shape
    has_pe = pe is not None
    n_w = len(weights)
    in_specs = [pl.BlockSpec((bm, dm), lambda i: (i, 0))]
    args = [x2d]
    if has_pe:
        pe_blocks = pe.shape[0] // bm
        in_specs.append(pl.BlockSpec((bm, dm), lambda i: (i % pe_blocks, 0)))
        args.append(pe)
    in_specs.append(pl.BlockSpec((1, N_MOD, dm), lambda i: (mod_row_of_block(i), 0, 0)))
    args.append(mods3)
    for w in weights:
        in_specs.append(_resident(w.shape))
        args.append(w)
    out_widths = [w.shape[1] for w in weights]
    scratch = [pltpu.VMEM((bm, dm), BF16)]
    if cmlp is not None:
        for a in cmlp:
            in_specs.append(_resident(a.shape))
            args.append(a)
        out_widths[-1] = D_CMLP
        scratch.append(pltpu.VMEM((bm, D_CMLP), F32))
    out_specs = [pl.BlockSpec((bm, n), lambda i: (i, 0)) for n in out_widths]
    out_shape = [jax.ShapeDtypeStruct((rows, n), dt) for n, dt in zip(out_widths, out_dtypes)]
    kern = functools.partial(_proj_kernel, n_w=n_w, has_pe=has_pe, gelu_flags=tuple(gelu_flags),
                             has_cmlp=cmlp is not None, tn=1024)
    return pl.pallas_call(
        kern,
        grid=(rows // bm,),
        in_specs=in_specs,
        out_specs=out_specs,
        out_shape=out_shape,
        scratch_shapes=scratch,
        compiler_params=_cparams(("parallel",)),
        name="projection",
    )(*args)


def _gate_kernel(lic_ref, lfc_ref, lir_ref, lfr_ref, row_ref, col_ref, *, nc):
    li = lic_ref[0, 0]
    lf = _log_sigmoid(lfc_ref[0, 0])
    length = li.shape[0]
    tid = lax.broadcasted_iota(jnp.int32, li.shape, 0)
    lane = lax.broadcasted_iota(jnp.int32, li.shape, 1)
    lane1 = lane[0:1, :]
    fwd = (lane < nc) | (lane == 2 * nc)

    def scan_sublanes(x, op, fill):
        p = x
        s = x
        k = 1
        while k < length:
            p = op(p, jnp.where(tid >= k, pltpu.roll(p, k, 0), fill))
            s = op(s, jnp.where(tid < length - k, pltpu.roll(s, length - k, 0), fill))
            k *= 2
        return jnp.where(fwd, p, s)

    b = scan_sublanes(lf, jnp.add, 0.0)
    btot = jnp.sum(lf, axis=0, keepdims=True)
    a = btot - b + li
    m_loc = jnp.max(a, axis=0, keepdims=True)
    r = li - b
    cm = scan_sublanes(r, jnp.maximum, -jnp.inf)

    m_ctx = jnp.maximum(btot, m_loc)
    m_in = jnp.where(lane1 == 0, pltpu.roll(m_ctx, LANE - 2 * nc, 1), pltpu.roll(m_ctx, LANE - 2, 1))
    for k in range(nc - 1):
        m_new = jnp.maximum(btot + m_in, m_loc)
        m_in = jnp.where(lane1 == k + 1, pltpu.roll(m_new, 1, 1),
                         jnp.where(lane1 == 2 * nc - 2 - k, pltpu.roll(m_new, LANE - 1, 1), m_in))
    is_ctx = lane1 >= 2 * nc
    m_in = jnp.where(is_ctx, 0.0, m_in)
    m_new = jnp.maximum(btot + m_in, m_loc)
    s_old = jnp.broadcast_to(jnp.exp(btot + m_in - m_new), li.shape)
    w = jnp.exp(a - m_new)
    big_m = jnp.maximum(m_in, cm)
    s_int = jnp.exp(m_in - big_m)
    e_neg = jnp.exp(-(b + big_m))
    g = 2 * nc
    col_ref[0, 0] = jnp.where(
        lane < g, w, jnp.where(
            lane < 2 * g, pltpu.roll(big_m, g, 1), jnp.where(
                lane < 3 * g, pltpu.roll(s_int, 2 * g, 1), jnp.where(
                    lane < 4 * g, pltpu.roll(e_neg, 3 * g, 1), jnp.where(
                        lane < 4 * g + 2, pltpu.roll(w, 3 * g, 1), pltpu.roll(s_old, 4 * g + 2, 1))))))

    lir = lir_ref[0, 0]
    lfr = _log_sigmoid(lfr_ref[0, 0])
    width = lir.shape[1]
    rid = lax.broadcasted_iota(jnp.int32, lir.shape, 0)
    pid = lax.broadcasted_iota(jnp.int32, lir.shape, 1)
    p = lfr
    s = lfr
    k = 1
    while k < width:
        p = p + jnp.where(pid >= k, pltpu.roll(p, k, 1), 0.0)
        s = s + jnp.where(pid < width - k, pltpu.roll(s, width - k, 1), 0.0)
        k *= 2
    row_ref[0, 0] = lir - jnp.where(rid < nc, p, s)


def _gate_stats(g_x, g_c, gate_bias, bsz, seq, ctx_len):
    nc = seq // MCHUNK
    gb = gate_bias.astype(F32)
    gx = g_x[:, :N_GATE_COLS].reshape(bsz, nc, MCHUNK, 2, 2, HEADS) + gb.reshape(2, 2, HEADS)
    gc = g_c[:, :N_GATE_COLS].reshape(bsz, ctx_len, 2, 2, HEADS) + gb.reshape(2, 2, HEADS)
    col_x = gx.transpose(4, 0, 5, 2, 3, 1).reshape(2, bsz, HEADS, MCHUNK, 2 * nc)
    col_c = gc.transpose(3, 0, 4, 1, 2)
    col = jnp.concatenate([col_x, col_c], -1)
    col = jnp.pad(col, ((0, 0),) * 4 + ((0, LANE - col.shape[-1]),))
    row = gx.transpose(4, 0, 5, 3, 1, 2).reshape(2, bsz, HEADS, 2 * nc, MCHUNK)
    blk_c = pl.BlockSpec((1, 1, MCHUNK, LANE), lambda b, h: (b, h, 0, 0))
    blk_r = pl.BlockSpec((1, 1, 2 * nc, MCHUNK), lambda b, h: (b, h, 0, 0))
    return pl.pallas_call(
        functools.partial(_gate_kernel, nc=nc),
        grid=(bsz, HEADS),
        in_specs=[blk_c, blk_c, blk_r, blk_r],
        out_specs=[blk_r, blk_c],
        out_shape=[jax.ShapeDtypeStruct((bsz, HEADS, 2 * nc, MCHUNK), F32),
                   jax.ShapeDtypeStruct((bsz, HEADS, MCHUNK, LANE), F32)],
        compiler_params=_cparams(("parallel", "parallel")),
        name="gate_stats",
    )(col[0], col[1], row[0], row[1])


def _mlstm_kernel(q_ref, k_ref, v_ref, o_ref, kc_ref, vc_ref, cwq_ref, cbq_ref, cwk_ref, cbk_ref,
                  row_ref, col_ref, nw_ref, band_ref, edge_ref, y_ref, q_s, k_s, kc_s, ct_s, n_s, *, nc):
    lc = MCHUNK
    halo_rows = edge_ref.shape[1]

    def conv_silu_chunk(x_ref, c, n_chunks, w, b, scale):
        x = x_ref[pl.ds(c * lc, lc), :]
        wb = w.astype(BF16)
        taps = jnp.concatenate([x * wb[0:1, :], x * wb[1:2, :], x * wb[2:3, :]], axis=0)
        y = jnp.dot(band_ref[...], taps, preferred_element_type=F32)
        if n_chunks > 1:
            rid = lax.broadcasted_iota(jnp.int32, (halo_rows, x.shape[1]), 0)
            wf = wb.astype(F32)
            halo = jnp.zeros((halo_rows, x.shape[1]), F32)
            if c > 0:
                prev = x_ref[pl.ds(c * lc - halo_rows, halo_rows), :].astype(F32)[halo_rows - 1:, :]
                halo = jnp.where(rid == 0, prev * wf[0:1, :], halo)
            if c < n_chunks - 1:
                nxt = x_ref[pl.ds((c + 1) * lc, halo_rows), :].astype(F32)[0:1, :]
                halo = jnp.where(rid == 1, nxt * wf[2:3, :], halo)
            y = y + jnp.dot(edge_ref[...], halo.astype(BF16), preferred_element_type=F32)
        y = _silu(y + b)
        if scale != 1.0:
            y = y * scale
        return y.astype(BF16)

    k_scale = HD ** -0.5
    for c in range(nc):
        sl = pl.ds(c * lc, lc)
        q_s[sl, :] = conv_silu_chunk(q_ref, c, nc, cwq_ref[...], cbq_ref[...], 1.0)
        k_s[sl, :] = conv_silu_chunk(k_ref, c, nc, cwk_ref[...], cbk_ref[...], k_scale)
    kc_s[...] = conv_silu_chunk(kc_ref, 0, 1, cwk_ref[...], cbk_ref[...], k_scale)

    def col(j):
        return col_ref[0, 0, :, j:j + 1]

    def local_state(kk, vv, wcol):
        vw = (vv.astype(F32) * wcol).astype(BF16)
        ct = lax.dot_general(kk, vw, (((0,), (0,)), ((), ())), preferred_element_type=F32)
        nn = jnp.sum(kk.astype(F32) * wcol, axis=0, keepdims=True)
        return ct, nn

    for d in range(2):
        ct, nn = local_state(kc_s[...], vc_ref[...], col(8 * nc + d))
        order = list(range(nc)) if d == 0 else list(range(nc - 1, -1, -1))
        for pos, c in enumerate(order):
            idx = d * nc + c
            ct_s[idx] = ct.astype(BF16)
            n_s[idx] = nn
            if pos == nc - 1:
                break
            sl = pl.ds(c * lc, lc)
            ctl, nl = local_state(k_s[sl, :], v_ref[sl, :], col(idx))
            s_old = col_ref[0, 0, 0:1, 8 * nc + 2 + idx:8 * nc + 3 + idx]
            ct = s_old * ct + ctl
            nn = s_old * nn + nl

    tid = lax.broadcasted_iota(jnp.int32, (lc, lc), 0)
    sid = lax.broadcasted_iota(jnp.int32, (lc, lc), 1)
    masks = (sid <= tid, sid >= tid)
    for c in range(nc):
        sl = pl.ds(c * lc, lc)
        q = q_s[sl, :]
        kk = k_s[sl, :]
        v = v_ref[sl, :]
        qf = q.astype(F32)
        s = lax.dot_general(q, kk, (((1,), (1,)), ((), ())), preferred_element_type=F32)
        h = None
        for d in range(2):
            idx = d * nc + c
            r = row_ref[0, 0, idx:idx + 1, :]
            big_m = col(2 * nc + idx)
            s_int = col(4 * nc + idx)
            e_neg = col(6 * nc + idx)
            p = jnp.where(masks[d], jnp.exp(r - big_m), 0.0) * s
            den = (jnp.sum(p, axis=-1, keepdims=True)
                   + s_int * jnp.sum(qf * n_s[idx], axis=-1, keepdims=True))
            num = (jnp.dot(p.astype(BF16), v, preferred_element_type=F32)
                   + s_int * jnp.dot(q, ct_s[idx], preferred_element_type=F32))
            hd = num * (1.0 / jnp.maximum(jnp.abs(den), e_neg))
            h = hd if h is None else h + hd
        mu = jnp.mean(h, axis=-1, keepdims=True)
        hc = h - mu
        var = jnp.mean(hc * hc, axis=-1, keepdims=True)
        hn = hc * lax.rsqrt(var + LN_EPS) * nw_ref[...]
        y_ref[sl, :] = (hn * _sigmoid(o_ref[sl, :].astype(F32))).astype(BF16)


def _mlstm(qkvo, kv_ctx, conv_w, conv_b, rowq, colq, norm_w, bsz, seq, ctx_len):
    nc = seq // MCHUNK
    hq = D_MLSTM // HD
    kern = functools.partial(_mlstm_kernel, nc=nc)
    ii = jnp.arange(MCHUNK)[:, None]
    jj = jnp.arange(MCHUNK)[None, :]
    band = jnp.concatenate([(jj == ii + t - 1) for t in range(3)], axis=1).astype(BF16)
    hh = jnp.arange(2 * SUBLANES)[None, :]
    edge = (((ii == 0) & (hh == 0)) | ((ii == MCHUNK - 1) & (hh == 1))).astype(BF16)
    seq_blk = lambda off: pl.BlockSpec((seq, HD), lambda b, h: (b, off + h))
    ctx_blk = lambda off: pl.BlockSpec((ctx_len, HD), lambda b, h: (b, off + h))
    return pl.pallas_call(
        kern,
        grid=(bsz, HEADS),
        in_specs=[seq_blk(0), seq_blk(hq), seq_blk(2 * hq), seq_blk(3 * hq),
                  ctx_blk(0), ctx_blk(hq),
                  pl.BlockSpec((3, HD), lambda b, h: (0, h)),
                  pl.BlockSpec((1, HD), lambda b, h: (0, h)),
                  pl.BlockSpec((3, HD), lambda b, h: (0, hq + h)),
                  pl.BlockSpec((1, HD), lambda b, h: (0, hq + h)),
                  pl.BlockSpec((1, 1, 2 * nc, MCHUNK), lambda b, h: (b, h, 0, 0)),
                  pl.BlockSpec((1, 1, MCHUNK, LANE), lambda b, h: (b, h, 0, 0)),
                  pl.BlockSpec((1, HD), lambda b, h: (0, h)),
                  _resident(band.shape), _resident(edge.shape)],
        out_specs=pl.BlockSpec((seq, HD), lambda b, h: (b, h)),
        out_shape=jax.ShapeDtypeStruct((bsz * seq, D_MLSTM), BF16),
        scratch_shapes=[pltpu.VMEM((seq, HD), BF16), pltpu.VMEM((seq, HD), BF16),
                        pltpu.VMEM((ctx_len, HD), BF16),
                        pltpu.VMEM((2 * nc, HD, HD), BF16), pltpu.VMEM((2 * nc, 1, HD), F32)],
        compiler_params=_cparams(("parallel", "parallel")),
        name="mlstm",
    )(qkvo, qkvo, qkvo, qkvo, kv_ctx, kv_ctx, conv_w, conv_b, conv_w, conv_b, rowq, colq, norm_w,
      band, edge)


def _out_kernel(ym_ref, yc_ref, x_ref, pe_ref, m_ref, wout_ref,
                l1w_ref, l1b_ref, wr_ref, br_ref, x1_ref, h2_ref, lg_ref):
    m = m_ref[0]
    y = (jnp.dot(ym_ref[...], wout_ref[:D_MLSTM, :], preferred_element_type=F32)
         + jnp.dot(yc_ref[...], wout_ref[D_MLSTM:, :], preferred_element_type=F32))
    z = DEEPNORM_ALPHA * (x_ref[...] + pe_ref[...]) + m[2:3, :] * y
    x1 = _layer_norm_rows(z, l1w_ref[...], l1b_ref[...])
    x1_ref[...] = x1
    h2 = x1 * (1.0 + m[4:5, :]) + m[3:4, :]
    half = h2.shape[1] // 2
    h2_ref[...] = _pack_bf16_pairs(h2[:, :half], h2[:, half:])
    lg = jnp.dot(h2.astype(BF16), wr_ref[...], preferred_element_type=F32) + br_ref[...]

    lane = lax.broadcasted_iota(jnp.int32, lg.shape, 1)
    lane_f = lane.astype(F32)
    neg = -jnp.inf
    far = float(LANE)
    is_grp = lane < N_GROUPS
    m1 = jnp.max(jnp.where(is_grp, lg, neg), axis=-1, keepdims=True)
    grp = jnp.min(jnp.where(is_grp & (lg == m1), lane_f, far), axis=-1, keepdims=True)
    p_grp = 1.0 / jnp.sum(jnp.where(is_grp, jnp.exp(lg - m1), 0.0), axis=-1, keepdims=True)
    first = N_GROUPS + EXPERTS_PER_GROUP * grp
    in_grp = (lane_f >= first) & (lane_f < first + EXPERTS_PER_GROUP)
    v0 = jnp.max(jnp.where(in_grp, lg, neg), axis=-1, keepdims=True)
    i0 = jnp.min(jnp.where(in_grp & (lg == v0), lane_f, far), axis=-1, keepdims=True)
    rest = in_grp & (lane_f != i0)
    v1 = jnp.max(jnp.where(rest, lg, neg), axis=-1, keepdims=True)
    i1 = jnp.min(jnp.where(rest & (lg == v1), lane_f, far), axis=-1, keepdims=True)
    s1 = jnp.exp(v1 - v0)
    g0 = p_grp / (1.0 + s1)
    lg_ref[...] = jnp.where(lane == 0, g0, jnp.where(lane == 1, g0 * s1, jnp.where(
        lane == 2, i0 - N_GROUPS, jnp.where(lane == 3, i1 - N_GROUPS, 0.0))))


def _mixer_out(ym, yc, x2d, pe, mods3, wout, l1w, l1b, wr, br, seq, bm):
    rows, dm = x2d.shape
    pe_blocks = seq // bm
    row_blk = lambda n: pl.BlockSpec((bm, n), lambda i: (i, 0))
    return pl.pallas_call(
        _out_kernel,
        grid=(rows // bm,),
        in_specs=[row_blk(D_MLSTM), row_blk(D_CMLP), row_blk(dm),
                  pl.BlockSpec((bm, dm), lambda i: (i % pe_blocks, 0)),
                  pl.BlockSpec((1, N_MOD, dm), lambda i: (i // pe_blocks, 0, 0)),
                  _resident(wout.shape), _resident(l1w.shape), _resident(l1b.shape),
                  _resident(wr.shape), _resident(br.shape)],
        out_specs=[row_blk(dm), row_blk(dm // 2), row_blk(LANE)],
        out_shape=[jax.ShapeDtypeStruct((rows, dm), F32),
                   jax.ShapeDtypeStruct((rows, dm // 2), jnp.uint32),
                   jax.ShapeDtypeStruct((rows, LANE), F32)],
        compiler_params=_cparams(("parallel",)),
        name="mixer_out",
    )(ym, yc, x2d, pe, mods3, wout, l1w, l1b, wr, br)


def _moe_kernel(comp_ref, blk_ref, slot_ref, cast_ref, cslot_ref, cexp_ref, ctile_ref,
                gtab_ref, stab_ref, h2_hbm, wgf_ref, wuf_ref, wdf_ref, ys_hbm,
                wg_s, wu_s, wd_s, xbuf, xb_s, ybuf, gsem, ssem, pend, *, nt):
    del blk_ref, cexp_ref
    s = pl.program_id(0)
    n_steps = pl.num_programs(0)
    groups = MOE_ROWS // SUBLANES
    half = xbuf.shape[-1]
    dm = 2 * half
    n_asg = ys_hbm.shape[0] * SUBLANES - 2 * MOE_ROWS

    def gather_wait(p, n):
        pltpu.make_async_copy(h2_hbm.at[pl.ds(0, n)], xbuf.at[p, pl.ds(0, n)], gsem.at[p]).wait()

    def scatter_wait(p, n):
        pltpu.make_async_copy(ybuf.at[p, pl.ds(0, n)], ys_hbm.at[pl.ds(0, n)], ssem.at[p]).wait()

    @pl.when(s == 0)
    def _():
        pend[0] = 0
        pend[1] = 0
        xbuf[...] = jnp.zeros_like(xbuf)

    nxt = jnp.minimum(s + 1, n_steps - 1)

    @pl.when(jnp.logical_and(s + 1 < n_steps, comp_ref[nxt] > 0))
    def _():
        p = (s + 1) % 2

        for g0 in range(0, groups, GROUP_STEP):
            @pl.when(g0 < comp_ref[nxt])
            def _():
                for r in range(g0 * SUBLANES, (g0 + GROUP_STEP) * SUBLANES):
                    tok = gtab_ref[0, 0, r]
                    pltpu.make_async_copy(h2_hbm.at[tok >> 3, pl.ds(tok & 7, 1)],
                                          xbuf.at[p, r // SUBLANES, pl.ds(r % SUBLANES, 1)],
                                          gsem.at[p]).start()

    @pl.when(cast_ref[s] == 1)
    def _():
        cs = cslot_ref[s]
        t = ctile_ref[s]
        for k in range(nt):
            @pl.when(t == k)
            def _():
                wg_s[cs, :, k * MOE_FT:(k + 1) * MOE_FT] = wgf_ref[0].astype(BF16)
                wu_s[cs, :, k * MOE_FT:(k + 1) * MOE_FT] = wuf_ref[0].astype(BF16)
        wd_s[cs, t] = wdf_ref[0].astype(BF16)

    @pl.when(comp_ref[s] > 0)
    def _():
        p = s % 2
        sl = slot_ref[s]
        ng = comp_ref[s]
        gather_wait(p, ng)
        x_lo, x_hi = _unpack_bf16_pairs(xbuf[p].reshape(MOE_ROWS, half))
        xb_s[:, :half] = x_lo.astype(BF16)
        xb_s[:, half:] = x_hi.astype(BF16)
        x = xb_s[...]
        g = jnp.dot(x, wg_s[sl], preferred_element_type=F32)
        u = jnp.dot(x, wu_s[sl], preferred_element_type=F32)
        h = (_silu(g) * u).astype(BF16)
        y = jnp.dot(h, wd_s[sl].reshape(D_EXPERT, dm), preferred_element_type=F32)

        @pl.when(pend[p] > 0)
        def _():
            scatter_wait(p, pend[p])

        ybuf[p] = _pack_bf16_pairs(y[:, :half], y[:, half:]).reshape(groups, SUBLANES, half)

        for g0 in range(0, groups, GROUP_STEP):
            @pl.when(g0 < ng)
            def _():
                for r in range(g0 * SUBLANES, (g0 + GROUP_STEP) * SUBLANES):
                    d = stab_ref[0, 0, r]
                    dst = jnp.where(d < 0, n_asg + p * MOE_ROWS + r, d)
                    pltpu.make_async_copy(ybuf.at[p, r // SUBLANES, pl.ds(r % SUBLANES, 1)],
                                          ys_hbm.at[dst >> 3, pl.ds(dst & 7, 1)], ssem.at[p]).start()

        pend[p] = ng

    @pl.when(s == n_steps - 1)
    def _():
        for p in range(2):
            @pl.when(pend[p] > 0)
            def _():
                scatter_wait(p, pend[p])
                pend[p] = 0

        xbuf[...] = jnp.zeros_like(xbuf)
        fills = [pltpu.make_async_copy(xbuf.at[p], ys_hbm.at[pl.ds(n_asg // SUBLANES + p * groups, groups)],
                                       gsem.at[p]) for p in range(2)]
        for cp in fills:
            cp.start()
        for cp in fills:
            cp.wait()


def _experts(h2, gtab, stab, sched, wg, wu, wd):
    n_tok, dm = h2.shape[0], 2 * h2.shape[1]
    nt = D_EXPERT // MOE_FT
    n_asg = 2 * n_tok
    n_steps = sched[0].shape[0]

    smem_rows = lambda imap: pl.BlockSpec((1, 1, MOE_ROWS), imap, memory_space=pltpu.SMEM)
    grid_spec = pltpu.PrefetchScalarGridSpec(
        num_scalar_prefetch=7,
        grid=(n_steps,),
        in_specs=[smem_rows(lambda s, comp, blk, *_: (blk[jnp.minimum(s + 1, n_steps - 1)], 0, 0)),
                  smem_rows(lambda s, comp, blk, *_: (blk[s], 0, 0)),
                  pl.BlockSpec(memory_space=pl.ANY),
                  pl.BlockSpec((1, dm, MOE_FT), lambda s, c, b, sl, ca, cs, ce, ct: (ce[s], 0, ct[s])),
                  pl.BlockSpec((1, dm, MOE_FT), lambda s, c, b, sl, ca, cs, ce, ct: (ce[s], 0, ct[s])),
                  pl.BlockSpec((1, MOE_FT, dm), lambda s, c, b, sl, ca, cs, ce, ct: (ce[s], ct[s], 0))],
        out_specs=pl.BlockSpec(memory_space=pl.ANY),
        scratch_shapes=[pltpu.VMEM((2, dm, D_EXPERT), BF16), pltpu.VMEM((2, dm, D_EXPERT), BF16),
                        pltpu.VMEM((2, nt, MOE_FT, dm), BF16),
                        pltpu.VMEM((2, MOE_ROWS // SUBLANES, SUBLANES, dm // 2), jnp.uint32),
                        pltpu.VMEM((MOE_ROWS, dm), BF16),
                        pltpu.VMEM((2, MOE_ROWS // SUBLANES, SUBLANES, dm // 2), jnp.uint32),
                        pltpu.SemaphoreType.DMA((2,)), pltpu.SemaphoreType.DMA((2,)),
                        pltpu.SMEM((2,), jnp.int32)],
    )
    return pl.pallas_call(
        functools.partial(_moe_kernel, nt=nt),
        grid_spec=grid_spec,
        out_shape=jax.ShapeDtypeStruct(((n_asg + 2 * MOE_ROWS) // SUBLANES, SUBLANES, dm // 2), jnp.uint32),
        compiler_params=_cparams(("arbitrary",)),
        name="experts",
    )(*sched, gtab, stab, h2.reshape(n_tok // SUBLANES, SUBLANES, dm // 2), wg, wu, wd)


def _final_kernel(x1_ref, y0_ref, y1_ref, g_ref, m_ref, w_ref, b_ref, o_ref):
    m = m_ref[0]
    lo0, hi0 = _unpack_bf16_pairs(y0_ref[...])
    lo1, hi1 = _unpack_bf16_pairs(y1_ref[...])
    g0 = g_ref[:, 0:1]
    g1 = g_ref[:, 1:2]
    y = jnp.concatenate([g0 * lo0 + g1 * lo1, g0 * hi0 + g1 * hi1], axis=-1)
    z = DEEPNORM_ALPHA * x1_ref[...] + m[5:6, :] * y
    o_ref[...] = _layer_norm_rows(z, w_ref[...], b_ref[...])


def _final(x1, ys, gates, mods3, w, b, seq, bm):
    rows, dm = x1.shape
    blocks_per_batch = seq // bm
    slot_blocks = rows // bm
    row_blk = lambda n: pl.BlockSpec((bm, n), lambda i: (i, 0))
    return pl.pallas_call(
        _final_kernel,
        grid=(rows // bm,),
        in_specs=[row_blk(dm), row_blk(dm // 2), pl.BlockSpec((bm, dm // 2), lambda i: (i + slot_blocks, 0)),
                  row_blk(LANE),
                  pl.BlockSpec((1, N_MOD, dm), lambda i: (i // blocks_per_batch, 0, 0)),
                  _resident(w.shape), _resident(b.shape)],
        out_specs=row_blk(dm),
        out_shape=jax.ShapeDtypeStruct((rows, dm), F32),
        compiler_params=_cparams(("parallel",)),
        name="final_ln",
    )(x1, ys, ys, gates, mods3, w, b)


def _route(route, n_tok):
    e_flat = route[:, 2:4].astype(jnp.int32).reshape(-1)
    n_asg = e_flat.shape[0]
    earange = jnp.arange(N_EXPERTS, dtype=jnp.int32)
    counts = jnp.sum((e_flat[:, None] == earange[None, :]).astype(jnp.int32), 0)
    nblk_e = (counts + MOE_ROWS - 1) // MOE_ROWS
    pad_end = jnp.cumsum(nblk_e * MOE_ROWS)
    pad_start = pad_end - nblk_e * MOE_ROWS
    n_blk = n_asg // MOE_ROWS + N_EXPERTS

    order = jnp.argsort(e_flat, stable=True).astype(jnp.int32)
    starts = jnp.cumsum(counts) - counts
    ridx = jnp.arange(n_blk * MOE_ROWS, dtype=jnp.int32)
    e_row = jnp.minimum(jnp.sum((pad_end[None, :] <= ridx[:, None]).astype(jnp.int32), -1), N_EXPERTS - 1)
    rsel = e_row[:, None] == earange[None, :]
    k_row = ridx - jnp.sum(jnp.where(rsel, pad_start[None, :], 0), -1)
    valid = k_row < jnp.sum(jnp.where(rsel, counts[None, :], 0), -1)
    src = jnp.clip(jnp.sum(jnp.where(rsel, starts[None, :], 0), -1) + k_row, 0, n_asg - 1)
    row_asg = jnp.where(valid, order[src], -1)
    gtab = (jnp.maximum(row_asg, 0) >> 1).reshape(n_blk, 1, MOE_ROWS)
    stab = jnp.where(row_asg < 0, -1, (row_asg & 1) * n_tok + (row_asg >> 1)).reshape(n_blk, 1, MOE_ROWS)

    nt = D_EXPERT // MOE_FT
    n_steps = n_blk + (nt - 1) * N_EXPERTS + nt
    has = nblk_e > 0
    n_visits = jnp.sum(has.astype(jnp.int32))
    e_of_visit = jnp.sort(jnp.where(has, earange, N_EXPERTS))
    vsel = e_of_visit[:, None] == earange[None, :]
    nb_v = jnp.sum(jnp.where(vsel, nblk_e[None, :], 0), -1)
    steps_v = jnp.where(nb_v > 0, jnp.maximum(nb_v, nt), 0)
    end_v = nt + jnp.cumsum(steps_v)
    start_v = end_v - steps_v
    first_blk_v = jnp.cumsum(nb_v) - nb_v
    sidx = jnp.arange(n_steps, dtype=jnp.int32)
    v = jnp.sum((end_v[None, :] <= sidx[:, None]).astype(jnp.int32), -1)
    pick = lambda arr, idx: jnp.sum(jnp.where(idx[:, None] == earange[None, :], arr[None, :], 0), -1)
    in_visit = (sidx >= nt) & (v < n_visits)
    k = sidx - pick(start_v, v)
    comp = in_visit & (k < pick(nb_v, v))
    cnt_v = jnp.sum(jnp.where(vsel, counts[None, :], 0), -1)
    rows_s = jnp.clip(pick(cnt_v, v) - k * MOE_ROWS, 0, MOE_ROWS)
    chunk_rows = GROUP_STEP * SUBLANES
    groups_s = jnp.where(comp, (rows_s + chunk_rows - 1) // chunk_rows * GROUP_STEP, 0)
    blk_s = lax.cummax(jnp.where(comp, pick(first_blk_v, v) + k, 0), axis=0)
    prologue = sidx < nt
    cast = prologue | (in_visit & (k < nt) & (v + 1 < n_visits))
    cexp = jnp.where(prologue, e_of_visit[0], pick(e_of_visit, v + 1))
    ctile = jnp.where(prologue, sidx, k)
    code = lax.cummax(jnp.where(cast, cexp * nt + ctile, 0), axis=0)
    i32 = lambda a: a.astype(jnp.int32)
    sched = (i32(groups_s), i32(blk_s), i32(v % 2), i32(cast), i32(jnp.where(prologue, 0, (v + 1) % 2)),
             i32(jnp.minimum(code // nt, N_EXPERTS - 1)), i32(code % nt))
    return sched, gtab, stab


def _grid_pos_embed(rows):
    quarter = D_MODEL // 4
    omega = 1.0 / (10000.0 ** (jnp.arange(quarter, dtype=F32) / quarter))
    ar = jnp.arange(rows, dtype=F32)[:, None] * omega
    ac = jnp.arange(GRID_W, dtype=F32)[:, None] * omega
    shape = (rows, GRID_W, quarter)
    parts = [jnp.broadcast_to(jnp.sin(ar)[:, None, :], shape), jnp.broadcast_to(jnp.cos(ar)[:, None, :], shape),
             jnp.broadcast_to(jnp.sin(ac)[None, :, :], shape), jnp.broadcast_to(jnp.cos(ac)[None, :, :], shape)]
    return jnp.concatenate(parts, -1).reshape(rows * GRID_W, D_MODEL)


def kernel(x, c, ctx, c_ctx, w_mod, b_mod, w_in, conv_w, conv_b, gate_bias, mlstm_norm_w, cmlp_norm_w,
           w_s, b_s, w_out, ln1_w, ln1_b, router1_w, router1_b, router2_w, router2_b, w_gate, w_up,
           w_down, ln2_w, ln2_b):
    bsz, seq, dm = x.shape
    ctx_len = ctx.shape[1]
    n_tok = bsz * seq
    assert w_mod.shape[0] == 1 and dm == D_MODEL and seq % MCHUNK == 0 and ctx_len == MCHUNK
    pe = _grid_pos_embed(seq // GRID_W).astype(x.dtype)
    x2d = x.reshape(n_tok, dm)
    ctx2d = ctx.reshape(bsz * ctx_len, dm)

    mod_rows = 16
    cc = jnp.concatenate([c, c_ctx[None, :], jnp.zeros((mod_rows - bsz - 1, dm), c.dtype)], 0)
    mods3 = _modulation(cc, w_mod[0], b_mod[0]).reshape(mod_rows, N_MOD, dm)

    dq = D_MLSTM
    wi = w_in[0]
    w_qkvo = wi[:, :4 * dq].astype(BF16)
    w_g = jnp.pad(wi[:, 4 * dq:4 * dq + N_GATE_COLS], ((0, 0), (0, LANE - N_GATE_COLS))).astype(BF16)
    w_uv = wi[:, 4 * dq + N_GATE_COLS:].astype(BF16)
    bm_proj = 256
    blocks_per_seq = seq // bm_proj
    cmlp = (cmlp_norm_w[0].reshape(1, -1), w_s[0].astype(BF16), b_s[0].T)
    qkvo, g_x, yc = _projection(x2d, pe, mods3, lambda i: i // blocks_per_seq,
                                [w_qkvo, w_g, w_uv], [BF16, F32, BF16], [False, False, True], bm_proj,
                                cmlp=cmlp)
    w_kv = wi[:, dq:3 * dq].astype(BF16)
    kv_c, g_c = _projection(ctx2d, None, mods3, lambda i: bsz, [w_kv, w_g], [BF16, F32],
                            [False, False], bm_proj)

    rowq, colq = _gate_stats(g_x, g_c, gate_bias[0], bsz, seq, ctx_len)
    ym = _mlstm(qkvo, kv_c, conv_w[0], conv_b[0].reshape(1, -1), rowq, colq,
                mlstm_norm_w[0].reshape(1, -1), bsz, seq, ctx_len)

    wr = jnp.pad(jnp.concatenate([router1_w[0], router2_w[0]], 1),
                 ((0, 0), (0, LANE - N_GROUPS - N_EXPERTS))).astype(BF16)
    br = jnp.pad(jnp.concatenate([router1_b[0], router2_b[0]], 0),
                 (0, LANE - N_GROUPS - N_EXPERTS)).reshape(1, LANE)
    x1, h2, route = _mixer_out(ym, yc, x2d, pe, mods3, w_out[0].astype(BF16),
                               ln1_w[0].reshape(1, -1), ln1_b[0].reshape(1, -1), wr, br, seq, 256)

    sched, gtab, stab = _route(route, n_tok)
    ys = _experts(h2, gtab, stab, sched, w_gate[0], w_up[0], w_down[0])
    out = _final(x1, ys.reshape(-1, dm // 2), route, mods3, ln2_w[0].reshape(1, -1),
                 ln2_b[0].reshape(1, -1), seq, 256)
    return out.reshape(bsz, seq, dm)
```

```python
import functools

import jax
import jax.numpy as jnp
from jax import lax
from jax.experimental import pallas as pl
from jax.experimental.pallas import tpu as pltpu

F32 = jnp.float32
BF16 = jnp.bfloat16

D_MODEL = 2048
GRID_W = 64
D_MLSTM = 1024
D_CMLP = 1024
HEADS = 4
HD = 256
CMLP_GROUPS = 4
CMLP_GD = 256
CMLP_CHUNK = 128
N_GROUPS = 4
EXPERTS_PER_GROUP = 8
N_EXPERTS = 32
D_EXPERT = 1024
N_MOD = 6
N_GATE_COLS = 16
DEEPNORM_ALPHA = 2.0 ** 0.25
LN_EPS = 1e-6

LANE = 128
SUBLANES = 8
MCHUNK = 256
MOE_ROWS = 256
MOE_FT = 256
GROUP_STEP = 4
VMEM_LIMIT = 56 * 1024 * 1024


def _cparams(sem):
    return pltpu.CompilerParams(dimension_semantics=sem, vmem_limit_bytes=VMEM_LIMIT)


def _resident(shape):
    nd = len(shape)
    return pl.BlockSpec(shape, lambda *_: (0,) * nd, pipeline_mode=pl.Buffered(1))


def _sigmoid(x):
    return 0.5 * jnp.tanh(0.5 * x) + 0.5


def _silu(x):
    return x * _sigmoid(x)


def _log_sigmoid(x):
    return jnp.minimum(x, 0.0) - jnp.log1p(jnp.exp(-jnp.abs(x)))


def _gelu_tanh(x):
    c = 0.7978845608028654
    return 0.5 * x * (1.0 + jnp.tanh(c * (x + 0.044715 * (x * x * x))))


def _pack_bf16_pairs(lo, hi):
    lo_b = lax.bitcast_convert_type(lo.astype(BF16).astype(F32), jnp.uint32)
    hi_b = lax.bitcast_convert_type(hi.astype(BF16).astype(F32), jnp.uint32)
    return (lo_b >> 16) | (hi_b & jnp.uint32(0xFFFF0000))


def _unpack_bf16_pairs(w):
    lo = lax.bitcast_convert_type(w << 16, F32)
    hi = lax.bitcast_convert_type(w & jnp.uint32(0xFFFF0000), F32)
    return lo, hi


def _layer_norm_rows(z, w, b):
    mu = jnp.mean(z, axis=-1, keepdims=True)
    zc = z - mu
    var = jnp.mean(zc * zc, axis=-1, keepdims=True)
    return zc * lax.rsqrt(var + LN_EPS) * w + b


def _mod_kernel(c_ref, w_ref, b_ref, o_ref):
    s = _silu(c_ref[...]).astype(BF16)
    o_ref[...] = jnp.dot(s, w_ref[...].astype(BF16), preferred_element_type=F32) + b_ref[...]


def _modulation(cc, w_mod, b_mod):
    rows, dm = cc.shape
    n = w_mod.shape[1]
    tn = 1024
    return pl.pallas_call(
        _mod_kernel,
        grid=(n // tn,),
        in_specs=[pl.BlockSpec((rows, dm), lambda j: (0, 0)),
                  pl.BlockSpec((dm, tn), lambda j: (0, j)),
                  pl.BlockSpec((1, tn), lambda j: (0, j))],
        out_specs=pl.BlockSpec((rows, tn), lambda j: (0, j)),
        out_shape=jax.ShapeDtypeStruct((rows, n), F32),
        compiler_params=_cparams(("arbitrary",)),
        name="modulation",
    )(cc, w_mod, b_mod.reshape(1, n))


def _proj_kernel(*refs, n_w, has_pe, gelu_flags, has_cmlp, tn):
    x_ref = refs[0]
    k = 1
    pe_ref = None
    if has_pe:
        pe_ref = refs[k]
        k += 1
    m_ref = refs[k]
    k += 1
    w_refs = refs[k:k + n_w]
    k += n_w
    if has_cmlp:
        cnw_ref, ws_ref, bs_ref = refs[k:k + 3]
        k += 3
    o_refs = refs[k:k + n_w]
    hx_ref = refs[k + n_w]
    x = x_ref[...]
    if has_pe:
        x = x + pe_ref[...]
    m = m_ref[0]
    hx_ref[...] = (x * (1.0 + m[1:2, :]) + m[0:1, :]).astype(BF16)
    n_plain = n_w - 1 if has_cmlp else n_w
    for w_ref, o_ref, use_gelu in list(zip(w_refs, o_refs, gelu_flags))[:n_plain]:
        n = w_ref.shape[1]
        step = min(tn, n)
        for j in range(0, n, step):
            acc = jnp.dot(hx_ref[...], w_ref[:, j:j + step], preferred_element_type=F32)
            if use_gelu:
                acc = _gelu_tanh(acc)
            o_ref[:, j:j + step] = acc.astype(o_ref.dtype)
    if has_cmlp:
        w_ref, o_ref, s_ref = w_refs[-1], o_refs[-1], refs[k + n_w + 1]
        bm = x_ref.shape[0]
        vgate = _gelu_tanh(jnp.dot(hx_ref[...], w_ref[:, D_CMLP:], preferred_element_type=F32))
        for g in range(CMLP_GROUPS):
            gs = slice(g * CMLP_GD, (g + 1) * CMLP_GD)
            vg = vgate[:, gs]
            mu = jnp.mean(vg, axis=-1, keepdims=True)
            vc = vg - mu
            var = jnp.mean(vc * vc, axis=-1, keepdims=True)
            vn = (vc * lax.rsqrt(var + LN_EPS) * cnw_ref[:, gs]).astype(BF16)
            for p in range(bm // CMLP_CHUNK):
                ps = slice(p * CMLP_CHUNK, (p + 1) * CMLP_CHUNK)
                s_ref[ps, gs] = (jnp.dot(ws_ref[g], vn[ps, :], preferred_element_type=F32)
                                 + bs_ref[:, g:g + 1])
        u = _gelu_tanh(jnp.dot(hx_ref[...], w_ref[:, :D_CMLP], preferred_element_type=F32))
        o_ref[...] = (u * s_ref[...]).astype(o_ref.dtype)


def _projection(x2d, pe, mods3, mod_row_of_block, weights, out_dtypes, gelu_flags, bm, cmlp=None):
    rows, dm = x2d.shape
    has_pe = pe is not None
    n_w = len(weights)
    in_specs = [pl.BlockSpec((bm, dm), lambda i: (i, 0))]
    args = [x2d]
    if has_pe:
        pe_blocks = pe.shape[0] // bm
        in_specs.append(pl.BlockSpec((bm, dm), lambda i: (i % pe_blocks, 0)))
        args.append(pe)
    in_specs.append(pl.BlockSpec((1, N_MOD, dm), lambda i: (mod_row_of_block(i), 0, 0)))
    args.append(mods3)
    for w in weights:
        in_specs.append(_resident(w.shape))
        args.append(w)
    out_widths = [w.shape[1] for w in weights]
    scratch = [pltpu.VMEM((bm, dm), BF16)]
    if cmlp is not None:
        for a in cmlp:
            in_specs.append(_resident(a.shape))
            args.append(a)
        out_widths[-1] = D_CMLP
        scratch.append(pltpu.VMEM((bm, D_CMLP), F32))
    out_specs = [pl.BlockSpec((bm, n), lambda i: (i, 0)) for n in out_widths]
    out_shape = [jax.ShapeDtypeStruct((rows, n), dt) for n, dt in zip(out_widths, out_dtypes)]
    kern = functools.partial(_proj_kernel, n_w=n_w, has_pe=has_pe, gelu_flags=tuple(gelu_flags),
                             has_cmlp=cmlp is not None, tn=1024)
    return pl.pallas_call(
        kern,
        grid=(rows // bm,),
        in_specs=in_specs,
        out_specs=out_specs,
        out_shape=out_shape,
        scratch_shapes=scratch,
        compiler_params=_cparams(("parallel",)),
        name="projection",
    )(*args)


def _gate_kernel(lic_ref, lfc_ref, lir_ref, lfr_ref, row_ref, col_ref, *, nc):
    li = lic_ref[0, 0]
    lf = _log_sigmoid(lfc_ref[0, 0])
    length = li.shape[0]
    tid = lax.broadcasted_iota(jnp.int32, li.shape, 0)
    lane = lax.broadcasted_iota(jnp.int32, li.shape, 1)
    lane1 = lane[0:1, :]
    fwd = (lane < nc) | (lane == 2 * nc)

    def scan_sublanes(x, op, fill):
        p = x
        s = x
        k = 1
        while k < length:
            p = op(p, jnp.where(tid >= k, pltpu.roll(p, k, 0), fill))
            s = op(s, jnp.where(tid < length - k, pltpu.roll(s, length - k, 0), fill))
            k *= 2
        return jnp.where(fwd, p, s)

    b = scan_sublanes(lf, jnp.add, 0.0)
    btot = jnp.sum(lf, axis=0, keepdims=True)
    a = btot - b + li
    m_loc = jnp.max(a, axis=0, keepdims=True)
    r = li - b
    cm = scan_sublanes(r, jnp.maximum, -jnp.inf)

    m_ctx = jnp.maximum(btot, m_loc)
    m_in = jnp.where(lane1 == 0, pltpu.roll(m_ctx, LANE - 2 * nc, 1), pltpu.roll(m_ctx, LANE - 2, 1))
    for k in range(nc - 1):
        m_new = jnp.maximum(btot + m_in, m_loc)
        m_in = jnp.where(lane1 == k + 1, pltpu.roll(m_new, 1, 1),
                         jnp.where(lane1 == 2 * nc - 2 - k, pltpu.roll(m_new, LANE - 1, 1), m_in))
    is_ctx = lane1 >= 2 * nc
    m_in = jnp.where(is_ctx, 0.0, m_in)
    m_new = jnp.maximum(btot + m_in, m_loc)
    s_old = jnp.broadcast_to(jnp.exp(btot + m_in - m_new), li.shape)
    w = jnp.exp(a - m_new)
    big_m = jnp.maximum(m_in, cm)
    s_int = jnp.exp(m_in - big_m)
    e_neg = jnp.exp(-(b + big_m))
    g = 2 * nc
    col_ref[0, 0] = jnp.where(
        lane < g, w, jnp.where(
            lane < 2 * g, pltpu.roll(big_m, g, 1), jnp.where(
                lane < 3 * g, pltpu.roll(s_int, 2 * g, 1), jnp.where(
                    lane < 4 * g, pltpu.roll(e_neg, 3 * g, 1), jnp.where(
                        lane < 4 * g + 2, pltpu.roll(w, 3 * g, 1), pltpu.roll(s_old, 4 * g + 2, 1))))))

    lir = lir_ref[0, 0]
    lfr = _log_sigmoid(lfr_ref[0, 0])
    width = lir.shape[1]
    rid = lax.broadcasted_iota(jnp.int32, lir.shape, 0)
    pid = lax.broadcasted_iota(jnp.int32, lir.shape, 1)
    p = lfr
    s = lfr
    k = 1
    while k < width:
        p = p + jnp.where(pid >= k, pltpu.roll(p, k, 1), 0.0)
        s = s + jnp.where(pid < width - k, pltpu.roll(s, width - k, 1), 0.0)
        k *= 2
    row_ref[0, 0] = lir - jnp.where(rid < nc, p, s)


def _gate_stats(g_x, g_c, gate_bias, bsz, seq, ctx_len):
    nc = seq // MCHUNK
    gb = gate_bias.astype(F32)
    gx = g_x[:, :N_GATE_COLS].reshape(bsz, nc, MCHUNK, 2, 2, HEADS) + gb.reshape(2, 2, HEADS)
    gc = g_c[:, :N_GATE_COLS].reshape(bsz, ctx_len, 2, 2, HEADS) + gb.reshape(2, 2, HEADS)
    col_x = gx.transpose(4, 0, 5, 2, 3, 1).reshape(2, bsz, HEADS, MCHUNK, 2 * nc)
    col_c = gc.transpose(3, 0, 4, 1, 2)
    col = jnp.concatenate([col_x, col_c], -1)
    col = jnp.pad(col, ((0, 0),) * 4 + ((0, LANE - col.shape[-1]),))
    row = gx.transpose(4, 0, 5, 3, 1, 2).reshape(2, bsz, HEADS, 2 * nc, MCHUNK)
    blk_c = pl.BlockSpec((1, 1, MCHUNK, LANE), lambda b, h: (b, h, 0, 0))
    blk_r = pl.BlockSpec((1, 1, 2 * nc, MCHUNK), lambda b, h: (b, h, 0, 0))
    return pl.pallas_call(
        functools.partial(_gate_kernel, nc=nc),
        grid=(bsz, HEADS),
        in_specs=[blk_c, blk_c, blk_r, blk_r],
        out_specs=[blk_r, blk_c],
        out_shape=[jax.ShapeDtypeStruct((bsz, HEADS, 2 * nc, MCHUNK), F32),
                   jax.ShapeDtypeStruct((bsz, HEADS, MCHUNK, LANE), F32)],
        compiler_params=_cparams(("parallel", "parallel")),
        name="gate_stats",
    )(col[0], col[1], row[0], row[1])


def _mlstm_kernel(q_ref, k_ref, v_ref, o_ref, kc_ref, vc_ref, cwq_ref, cbq_ref, cwk_ref, cbk_ref,
                  row_ref, col_ref, nw_ref, band_ref, edge_ref, y_ref, q_s, k_s, kc_s, ct_s, n_s, *, nc):
    lc = MCHUNK
    halo_rows = edge_ref.shape[1]

    def conv_silu_chunk(x_ref, c, n_chunks, w, b, scale):
        x = x_ref[pl.ds(c * lc, lc), :]
        wb = w.astype(BF16)
        taps = jnp.concatenate([x * wb[0:1, :], x * wb[1:2, :], x * wb[2:3, :]], axis=0)
        y = jnp.dot(band_ref[...], taps, preferred_element_type=F32)
        if n_chunks > 1:
            rid = lax.broadcasted_iota(jnp.int32, (halo_rows, x.shape[1]), 0)
            wf = wb.astype(F32)
            halo = jnp.zeros((halo_rows, x.shape[1]), F32)
            if c > 0:
                prev = x_ref[pl.ds(c * lc - halo_rows, halo_rows), :].astype(F32)[halo_rows - 1:, :]
                halo = jnp.where(rid == 0, prev * wf[0:1, :], halo)
            if c < n_chunks - 1:
                nxt = x_ref[pl.ds((c + 1) * lc, halo_rows), :].astype(F32)[0:1, :]
                halo = jnp.where(rid == 1, nxt * wf[2:3, :], halo)
            y = y + jnp.dot(edge_ref[...], halo.astype(BF16), preferred_element_type=F32)
        y = _silu(y + b)
        if scale != 1.0:
            y = y * scale
        return y.astype(BF16)

    k_scale = HD ** -0.5
    for c in range(nc):
        sl = pl.ds(c * lc, lc)
        q_s[sl, :] = conv_silu_chunk(q_ref, c, nc, cwq_ref[...], cbq_ref[...], 1.0)
        k_s[sl, :] = conv_silu_chunk(k_ref, c, nc, cwk_ref[...], cbk_ref[...], k_scale)
    kc_s[...] = conv_silu_chunk(kc_ref, 0, 1, cwk_ref[...], cbk_ref[...], k_scale)

    def col(j):
        return col_ref[0, 0, :, j:j + 1]

    def local_state(kk, vv, wcol):
        vw = (vv.astype(F32) * wcol).astype(BF16)
        ct = lax.dot_general(kk, vw, (((0,), (0,)), ((), ())), preferred_element_type=F32)
        nn = jnp.sum(kk.astype(F32) * wcol, axis=0, keepdims=True)
        return ct, nn

    for d in range(2):
        ct, nn = local_state(kc_s[...], vc_ref[...], col(8 * nc + d))
        order = list(range(nc)) if d == 0 else list(range(nc - 1, -1, -1))
        for pos, c in enumerate(order):
            idx = d * nc + c
            ct_s[idx] = ct.astype(BF16)
            n_s[idx] = nn
            if pos == nc - 1:
                break
            sl = pl.ds(c * lc, lc)
            ctl, nl = local_state(k_s[sl, :], v_ref[sl, :], col(idx))
            s_old = col_ref[0, 0, 0:1, 8 * nc + 2 + idx:8 * nc + 3 + idx]
            ct = s_old * ct + ctl
            nn = s_old * nn + nl

    tid = lax.broadcasted_iota(jnp.int32, (lc, lc), 0)
    sid = lax.broadcasted_iota(jnp.int32, (lc, lc), 1)
    masks = (sid <= tid, sid >= tid)
    for c in range(nc):
        sl = pl.ds(c * lc, lc)
        q = q_s[sl, :]
        kk = k_s[sl, :]
        v = v_ref[sl, :]
        qf = q.astype(F32)
        s = lax.dot_general(q, kk, (((1,), (1,)), ((), ())), preferred_element_type=F32)
        h = None
        for d in range(2):
            idx = d * nc + c
            r = row_ref[0, 0, idx:idx + 1, :]
            big_m = col(2 * nc + idx)
            s_int = col(4 * nc + idx)
            e_neg = col(6 * nc + idx)
            p = jnp.where(masks[d], jnp.exp(r - big_m), 0.0) * s
            den = (jnp.sum(p, axis=-1, keepdims=True)
                   + s_int * jnp.sum(qf * n_s[idx], axis=-1, keepdims=True))
            num = (jnp.dot(p.astype(BF16), v, preferred_element_type=F32)
                   + s_int * jnp.dot(q, ct_s[idx], preferred_element_type=F32))
            hd = num * (1.0 / jnp.maximum(jnp.abs(den), e_neg))
            h = hd if h is None else h + hd
        mu = jnp.mean(h, axis=-1, keepdims=True)
        hc = h - mu
        var = jnp.mean(hc * hc, axis=-1, keepdims=True)
        hn = hc * lax.rsqrt(var + LN_EPS) * nw_ref[...]
        y_ref[sl, :] = (hn * _sigmoid(o_ref[sl, :].astype(F32))).astype(BF16)


def _mlstm(qkvo, kv_ctx, conv_w, conv_b, rowq, colq, norm_w, bsz, seq, ctx_len):
    nc = seq // MCHUNK
    hq = D_MLSTM // HD
    kern = functools.partial(_mlstm_kernel, nc=nc)
    ii = jnp.arange(MCHUNK)[:, None]
    jj = jnp.arange(MCHUNK)[None, :]
    band = jnp.concatenate([(jj == ii + t - 1) for t in range(3)], axis=1).astype(BF16)
    hh = jnp.arange(2 * SUBLANES)[None, :]
    edge = (((ii == 0) & (hh == 0)) | ((ii == MCHUNK - 1) & (hh == 1))).astype(BF16)
    seq_blk = lambda off: pl.BlockSpec((seq, HD), lambda b, h: (b, off + h))
    ctx_blk = lambda off: pl.BlockSpec((ctx_len, HD), lambda b, h: (b, off + h))
    return pl.pallas_call(
        kern,
        grid=(bsz, HEADS),
        in_specs=[seq_blk(0), seq_blk(hq), seq_blk(2 * hq), seq_blk(3 * hq),
                  ctx_blk(0), ctx_blk(hq),
                  pl.BlockSpec((3, HD), lambda b, h: (0, h)),
                  pl.BlockSpec((1, HD), lambda b, h: (0, h)),
                  pl.BlockSpec((3, HD), lambda b, h: (0, hq + h)),
                  pl.BlockSpec((1, HD), lambda b, h: (0, hq + h)),
                  pl.BlockSpec((1, 1, 2 * nc, MCHUNK), lambda b, h: (b, h, 0, 0)),
                  pl.BlockSpec((1, 1, MCHUNK, LANE), lambda b, h: (b, h, 0, 0)),
                  pl.BlockSpec((1, HD), lambda b, h: (0, h)),
                  _resident(band.shape), _resident(edge.shape)],
        out_specs=pl.BlockSpec((seq, HD), lambda b, h: (b, h)),
        out_shape=jax.ShapeDtypeStruct((bsz * seq, D_MLSTM), BF16),
        scratch_shapes=[pltpu.VMEM((seq, HD), BF16), pltpu.VMEM((seq, HD), BF16),
                        pltpu.VMEM((ctx_len, HD), BF16),
                        pltpu.VMEM((2 * nc, HD, HD), BF16), pltpu.VMEM((2 * nc, 1, HD), F32)],
        compiler_params=_cparams(("parallel", "parallel")),
        name="mlstm",
    )(qkvo, qkvo, qkvo, qkvo, kv_ctx, kv_ctx, conv_w, conv_b, conv_w, conv_b, rowq, colq, norm_w,
      band, edge)


def _out_kernel(ym_ref, yc_ref, x_ref, pe_ref, m_ref, wout_ref,
                l1w_ref, l1b_ref, wr_ref, br_ref, x1_ref, h2_ref, lg_ref):
    m = m_ref[0]
    y = (jnp.dot(ym_ref[...], wout_ref[:D_MLSTM, :], preferred_element_type=F32)
         + jnp.dot(yc_ref[...], wout_ref[D_MLSTM:, :], preferred_element_type=F32))
    z = DEEPNORM_ALPHA * (x_ref[...] + pe_ref[...]) + m[2:3, :] * y
    x1 = _layer_norm_rows(z, l1w_ref[...], l1b_ref[...])
    x1_ref[...] = x1
    h2 = x1 * (1.0 + m[4:5, :]) + m[3:4, :]
    half = h2.shape[1] // 2
    h2_ref[...] = _pack_bf16_pairs(h2[:, :half], h2[:, half:])
    lg = jnp.dot(h2.astype(BF16), wr_ref[...], preferred_element_type=F32) + br_ref[...]

    lgt = lg.T
    bm = lg.shape[0]
    epg = EXPERTS_PER_GROUP
    rid = lax.broadcasted_iota(jnp.int32, (SUBLANES, bm), 0).astype(F32)
    neg = -jnp.inf
    far = float(SUBLANES)
    gl = lgt[N_EXPERTS:N_EXPERTS + SUBLANES, :]
    is_grp = rid < N_GROUPS
    m1 = jnp.max(jnp.where(is_grp, gl, neg), axis=0, keepdims=True)
    grp = jnp.min(jnp.where(is_grp & (gl == m1), rid, far), axis=0, keepdims=True)
    p_grp = 1.0 / jnp.sum(jnp.where(is_grp, jnp.exp(gl - m1), 0.0), axis=0, keepdims=True)
    l2 = lgt[0:epg, :]
    for g in range(1, N_GROUPS):
        l2 = jnp.where(grp == g, lgt[g * epg:(g + 1) * epg, :], l2)
    v0 = jnp.max(l2, axis=0, keepdims=True)
    i0 = jnp.min(jnp.where(l2 == v0, rid, far), axis=0, keepdims=True)
    l2m = jnp.where(rid == i0, neg, l2)
    v1 = jnp.max(l2m, axis=0, keepdims=True)
    i1 = jnp.min(jnp.where(l2m == v1, rid, far), axis=0, keepdims=True)
    s1 = jnp.exp(v1 - v0)
    g0 = p_grp / (1.0 + s1)
    res = jnp.where(rid == 0, g0, jnp.where(rid == 1, g0 * s1, jnp.where(
        rid == 2, grp * epg + i0, jnp.where(rid == 3, grp * epg + i1, 0.0))))
    lg_ref[...] = jnp.concatenate([res, jnp.zeros((LANE - SUBLANES, bm), F32)], axis=0).T


def _mixer_out(ym, yc, x2d, pe, mods3, wout, l1w, l1b, wr, br, seq, bm):
    rows, dm = x2d.shape
    pe_blocks = seq // bm
    row_blk = lambda n: pl.BlockSpec((bm, n), lambda i: (i, 0))
    return pl.pallas_call(
        _out_kernel,
        grid=(rows // bm,),
        in_specs=[row_blk(D_MLSTM), row_blk(D_CMLP), row_blk(dm),
                  pl.BlockSpec((bm, dm), lambda i: (i % pe_blocks, 0)),
                  pl.BlockSpec((1, N_MOD, dm), lambda i: (i // pe_blocks, 0, 0)),
                  _resident(wout.shape), _resident(l1w.shape), _resident(l1b.shape),
                  _resident(wr.shape), _resident(br.shape)],
        out_specs=[row_blk(dm), row_blk(dm // 2), row_blk(LANE)],
        out_shape=[jax.ShapeDtypeStruct((rows, dm), F32),
                   jax.ShapeDtypeStruct((rows, dm // 2), jnp.uint32),
                   jax.ShapeDtypeStruct((rows, LANE), F32)],
        compiler_params=_cparams(("parallel",)),
        name="mixer_out",
    )(ym, yc, x2d, pe, mods3, wout, l1w, l1b, wr, br)


def _moe_kernel(comp_ref, blk_ref, slot_ref, cast_ref, cslot_ref, cexp_ref, ctile_ref,
                gtab_ref, stab_ref, h2_hbm, wgf_ref, wuf_ref, wdf_ref, ys_hbm,
                wg_s, wu_s, wd_s, xbuf, xb_s, ybuf, gsem, ssem, pend, *, nt):
    del blk_ref, cexp_ref
    s = pl.program_id(0)
    n_steps = pl.num_programs(0)
    groups = MOE_ROWS // SUBLANES
    half = xbuf.shape[-1]
    dm = 2 * half
    n_asg = ys_hbm.shape[0] * SUBLANES - 2 * MOE_ROWS

    def gather_wait(p, n):
        pltpu.make_async_copy(h2_hbm.at[pl.ds(0, n)], xbuf.at[p, pl.ds(0, n)], gsem.at[p]).wait()

    def scatter_wait(p, n):
        pltpu.make_async_copy(ybuf.at[p, pl.ds(0, n)], ys_hbm.at[pl.ds(0, n)], ssem.at[p]).wait()

    @pl.when(s == 0)
    def _():
        pend[0] = 0
        pend[1] = 0
        xbuf[...] = jnp.zeros_like(xbuf)

    nxt = jnp.minimum(s + 1, n_steps - 1)

    @pl.when(jnp.logical_and(s + 1 < n_steps, comp_ref[nxt] > 0))
    def _():
        p = (s + 1) % 2

        for g0 in range(0, groups, GROUP_STEP):
            @pl.when(g0 < comp_ref[nxt])
            def _():
                for r in range(g0 * SUBLANES, (g0 + GROUP_STEP) * SUBLANES):
                    tok = gtab_ref[0, 0, r]
                    pltpu.make_async_copy(h2_hbm.at[tok >> 3, pl.ds(tok & 7, 1)],
                                          xbuf.at[p, r // SUBLANES, pl.ds(r % SUBLANES, 1)],
                                          gsem.at[p]).start()

    @pl.when(cast_ref[s] == 1)
    def _():
        cs = cslot_ref[s]
        t = ctile_ref[s]
        for k in range(nt):
            @pl.when(t == k)
            def _():
                wg_s[cs, :, k * MOE_FT:(k + 1) * MOE_FT] = wgf_ref[0].astype(BF16)
                wu_s[cs, :, k * MOE_FT:(k + 1) * MOE_FT] = wuf_ref[0].astype(BF16)
        wd_s[cs, t] = wdf_ref[0].astype(BF16)

    @pl.when(comp_ref[s] > 0)
    def _():
        p = s % 2
        sl = slot_ref[s]
        ng = comp_ref[s]
        gather_wait(p, ng)
        x_lo, x_hi = _unpack_bf16_pairs(xbuf[p].reshape(MOE_ROWS, half))
        xb_s[:, :half] = x_lo.astype(BF16)
        xb_s[:, half:] = x_hi.astype(BF16)
        x = xb_s[...]
        g = jnp.dot(x, wg_s[sl], preferred_element_type=F32)
        u = jnp.dot(x, wu_s[sl], preferred_element_type=F32)
        h = (_silu(g) * u).astype(BF16)
        y = jnp.dot(h, wd_s[sl].reshape(D_EXPERT, dm), preferred_element_type=F32)

        @pl.when(pend[p] > 0)
        def _():
            scatter_wait(p, pend[p])

        ybuf[p] = _pack_bf16_pairs(y[:, :half], y[:, half:]).reshape(groups, SUBLANES, half)

        for g0 in range(0, groups, GROUP_STEP):
            @pl.when(g0 < ng)
            def _():
                for r in range(g0 * SUBLANES, (g0 + GROUP_STEP) * SUBLANES):
                    d = stab_ref[0, 0, r]
                    dst = jnp.where(d < 0, n_asg + p * MOE_ROWS + r, d)
                    pltpu.make_async_copy(ybuf.at[p, r // SUBLANES, pl.ds(r % SUBLANES, 1)],
                                          ys_hbm.at[dst >> 3, pl.ds(dst & 7, 1)], ssem.at[p]).start()

        pend[p] = ng

    @pl.when(s == n_steps - 1)
    def _():
        for p in range(2):
            @pl.when(pend[p] > 0)
            def _():
                scatter_wait(p, pend[p])
                pend[p] = 0

        xbuf[...] = jnp.zeros_like(xbuf)
        fills = [pltpu.make_async_copy(xbuf.at[p], ys_hbm.at[pl.ds(n_asg // SUBLANES + p * groups, groups)],
                                       gsem.at[p]) for p in range(2)]
        for cp in fills:
            cp.start()
        for cp in fills:
            cp.wait()


def _experts(h2, gtab, stab, sched, wg, wu, wd):
    n_tok, dm = h2.shape[0], 2 * h2.shape[1]
    nt = D_EXPERT // MOE_FT
    n_asg = 2 * n_tok
    n_steps = sched[0].shape[0]

    smem_rows = lambda imap: pl.BlockSpec((1, 1, MOE_ROWS), imap, memory_space=pltpu.SMEM)
    grid_spec = pltpu.PrefetchScalarGridSpec(
        num_scalar_prefetch=7,
        grid=(n_steps,),
        in_specs=[smem_rows(lambda s, comp, blk, *_: (blk[jnp.minimum(s + 1, n_steps - 1)], 0, 0)),
                  smem_rows(lambda s, comp, blk, *_: (blk[s], 0, 0)),
                  pl.BlockSpec(memory_space=pl.ANY),
                  pl.BlockSpec((1, dm, MOE_FT), lambda s, c, b, sl, ca, cs, ce, ct: (ce[s], 0, ct[s])),
                  pl.BlockSpec((1, dm, MOE_FT), lambda s, c, b, sl, ca, cs, ce, ct: (ce[s], 0, ct[s])),
                  pl.BlockSpec((1, MOE_FT, dm), lambda s, c, b, sl, ca, cs, ce, ct: (ce[s], ct[s], 0))],
        out_specs=pl.BlockSpec(memory_space=pl.ANY),
        scratch_shapes=[pltpu.VMEM((2, dm, D_EXPERT), BF16), pltpu.VMEM((2, dm, D_EXPERT), BF16),
                        pltpu.VMEM((2, nt, MOE_FT, dm), BF16),
                        pltpu.VMEM((2, MOE_ROWS // SUBLANES, SUBLANES, dm // 2), jnp.uint32),
                        pltpu.VMEM((MOE_ROWS, dm), BF16),
                        pltpu.VMEM((2, MOE_ROWS // SUBLANES, SUBLANES, dm // 2), jnp.uint32),
                        pltpu.SemaphoreType.DMA((2,)), pltpu.SemaphoreType.DMA((2,)),
                        pltpu.SMEM((2,), jnp.int32)],
    )
    return pl.pallas_call(
        functools.partial(_moe_kernel, nt=nt),
        grid_spec=grid_spec,
        out_shape=jax.ShapeDtypeStruct(((n_asg + 2 * MOE_ROWS) // SUBLANES, SUBLANES, dm // 2), jnp.uint32),
        compiler_params=_cparams(("arbitrary",)),
        name="experts",
    )(*sched, gtab, stab, h2.reshape(n_tok // SUBLANES, SUBLANES, dm // 2), wg, wu, wd)


def _final_kernel(x1_ref, y0_ref, y1_ref, g_ref, m_ref, w_ref, b_ref, o_ref):
    m = m_ref[0]
    lo0, hi0 = _unpack_bf16_pairs(y0_ref[...])
    lo1, hi1 = _unpack_bf16_pairs(y1_ref[...])
    g0 = g_ref[:, 0:1]
    g1 = g_ref[:, 1:2]
    y = jnp.concatenate([g0 * lo0 + g1 * lo1, g0 * hi0 + g1 * hi1], axis=-1)
    z = DEEPNORM_ALPHA * x1_ref[...] + m[5:6, :] * y
    o_ref[...] = _layer_norm_rows(z, w_ref[...], b_ref[...])


def _final(x1, ys, gates, mods3, w, b, seq, bm):
    rows, dm = x1.shape
    blocks_per_batch = seq // bm
    slot_blocks = rows // bm
    row_blk = lambda n: pl.BlockSpec((bm, n), lambda i: (i, 0))
    return pl.pallas_call(
        _final_kernel,
        grid=(rows // bm,),
        in_specs=[row_blk(dm), row_blk(dm // 2), pl.BlockSpec((bm, dm // 2), lambda i: (i + slot_blocks, 0)),
                  row_blk(LANE),
                  pl.BlockSpec((1, N_MOD, dm), lambda i: (i // blocks_per_batch, 0, 0)),
                  _resident(w.shape), _resident(b.shape)],
        out_specs=row_blk(dm),
        out_shape=jax.ShapeDtypeStruct((rows, dm), F32),
        compiler_params=_cparams(("parallel",)),
        name="final_ln",
    )(x1, ys, ys, gates, mods3, w, b)


def _route(route, n_tok):
    e_flat = route[:, 2:4].astype(jnp.int32).reshape(-1)
    n_asg = e_flat.shape[0]
    earange = jnp.arange(N_EXPERTS, dtype=jnp.int32)
    counts = jnp.sum((e_flat[:, None] == earange[None, :]).astype(jnp.int32), 0)
    nblk_e = (counts + MOE_ROWS - 1) // MOE_ROWS
    pad_end = jnp.cumsum(nblk_e * MOE_ROWS)
    pad_start = pad_end - nblk_e * MOE_ROWS
    n_blk = n_asg // MOE_ROWS + N_EXPERTS

    order = jnp.argsort(e_flat, stable=True).astype(jnp.int32)
    starts = jnp.cumsum(counts) - counts
    ridx = jnp.arange(n_blk * MOE_ROWS, dtype=jnp.int32)
    e_row = jnp.minimum(jnp.sum((pad_end[None, :] <= ridx[:, None]).astype(jnp.int32), -1), N_EXPERTS - 1)
    rsel = e_row[:, None] == earange[None, :]
    k_row = ridx - jnp.sum(jnp.where(rsel, pad_start[None, :], 0), -1)
    valid = k_row < jnp.sum(jnp.where(rsel, counts[None, :], 0), -1)
    src = jnp.clip(jnp.sum(jnp.where(rsel, starts[None, :], 0), -1) + k_row, 0, n_asg - 1)
    row_asg = jnp.where(valid, order[src], -1)
    gtab = (jnp.maximum(row_asg, 0) >> 1).reshape(n_blk, 1, MOE_ROWS)
    stab = jnp.where(row_asg < 0, -1, (row_asg & 1) * n_tok + (row_asg >> 1)).reshape(n_blk, 1, MOE_ROWS)

    nt = D_EXPERT // MOE_FT
    n_steps = n_blk + (nt - 1) * N_EXPERTS + nt
    has = nblk_e > 0
    n_visits = jnp.sum(has.astype(jnp.int32))
    e_of_visit = jnp.sort(jnp.where(has, earange, N_EXPERTS))
    vsel = e_of_visit[:, None] == earange[None, :]
    nb_v = jnp.sum(jnp.where(vsel, nblk_e[None, :], 0), -1)
    steps_v = jnp.where(nb_v > 0, jnp.maximum(nb_v, nt), 0)
    end_v = nt + jnp.cumsum(steps_v)
    start_v = end_v - steps_v
    first_blk_v = jnp.cumsum(nb_v) - nb_v
    sidx = jnp.arange(n_steps, dtype=jnp.int32)
    v = jnp.sum((end_v[None, :] <= sidx[:, None]).astype(jnp.int32), -1)
    pick = lambda arr, idx: jnp.sum(jnp.where(idx[:, None] == earange[None, :], arr[None, :], 0), -1)
    in_visit = (sidx >= nt) & (v < n_visits)
    k = sidx - pick(start_v, v)
    comp = in_visit & (k < pick(nb_v, v))
    cnt_v = jnp.sum(jnp.where(vsel, counts[None, :], 0), -1)
    rows_s = jnp.clip(pick(cnt_v, v) - k * MOE_ROWS, 0, MOE_ROWS)
    chunk_rows = GROUP_STEP * SUBLANES
    groups_s = jnp.where(comp, (rows_s + chunk_rows - 1) // chunk_rows * GROUP_STEP, 0)
    blk_s = lax.cummax(jnp.where(comp, pick(first_blk_v, v) + k, 0), axis=0)
    prologue = sidx < nt
    cast = prologue | (in_visit & (k < nt) & (v + 1 < n_visits))
    cexp = jnp.where(prologue, e_of_visit[0], pick(e_of_visit, v + 1))
    ctile = jnp.where(prologue, sidx, k)
    code = lax.cummax(jnp.where(cast, cexp * nt + ctile, 0), axis=0)
    i32 = lambda a: a.astype(jnp.int32)
    sched = (i32(groups_s), i32(blk_s), i32(v % 2), i32(cast), i32(jnp.where(prologue, 0, (v + 1) % 2)),
             i32(jnp.minimum(code // nt, N_EXPERTS - 1)), i32(code % nt))
    return sched, gtab, stab


def _grid_pos_embed(rows):
    quarter = D_MODEL // 4
    omega = 1.0 / (10000.0 ** (jnp.arange(quarter, dtype=F32) / quarter))
    ar = jnp.arange(rows, dtype=F32)[:, None] * omega
    ac = jnp.arange(GRID_W, dtype=F32)[:, None] * omega
    shape = (rows, GRID_W, quarter)
    parts = [jnp.broadcast_to(jnp.sin(ar)[:, None, :], shape), jnp.broadcast_to(jnp.cos(ar)[:, None, :], shape),
             jnp.broadcast_to(jnp.sin(ac)[None, :, :], shape), jnp.broadcast_to(jnp.cos(ac)[None, :, :], shape)]
    return jnp.concatenate(parts, -1).reshape(rows * GRID_W, D_MODEL)


def kernel(x, c, ctx, c_ctx, w_mod, b_mod, w_in, conv_w, conv_b, gate_bias, mlstm_norm_w, cmlp_norm_w,
           w_s, b_s, w_out, ln1_w, ln1_b, router1_w, router1_b, router2_w, router2_b, w_gate, w_up,
           w_down, ln2_w, ln2_b):
    bsz, seq, dm = x.shape
    ctx_len = ctx.shape[1]
    n_tok = bsz * seq
    assert w_mod.shape[0] == 1 and dm == D_MODEL and seq % MCHUNK == 0 and ctx_len == MCHUNK
    pe = _grid_pos_embed(seq // GRID_W).astype(x.dtype)
    x2d = x.reshape(n_tok, dm)
    ctx2d = ctx.reshape(bsz * ctx_len, dm)

    mod_rows = 16
    cc = jnp.concatenate([c, c_ctx[None, :], jnp.zeros((mod_rows - bsz - 1, dm), c.dtype)], 0)
    mods3 = _modulation(cc, w_mod[0], b_mod[0]).reshape(mod_rows, N_MOD, dm)

    dq = D_MLSTM
    wi = w_in[0]
    w_qkvo = wi[:, :4 * dq].astype(BF16)
    w_g = jnp.pad(wi[:, 4 * dq:4 * dq + N_GATE_COLS], ((0, 0), (0, LANE - N_GATE_COLS))).astype(BF16)
    w_uv = wi[:, 4 * dq + N_GATE_COLS:].astype(BF16)
    bm_proj = 256
    blocks_per_seq = seq // bm_proj
    cmlp = (cmlp_norm_w[0].reshape(1, -1), w_s[0].astype(BF16), b_s[0].T)
    qkvo, g_x, yc = _projection(x2d, pe, mods3, lambda i: i // blocks_per_seq,
                                [w_qkvo, w_g, w_uv], [BF16, F32, BF16], [False, False, True], bm_proj,
                                cmlp=cmlp)
    w_kv = wi[:, dq:3 * dq].astype(BF16)
    kv_c, g_c = _projection(ctx2d, None, mods3, lambda i: bsz, [w_kv, w_g], [BF16, F32],
                            [False, False], bm_proj)

    rowq, colq = _gate_stats(g_x, g_c, gate_bias[0], bsz, seq, ctx_len)
    ym = _mlstm(qkvo, kv_c, conv_w[0], conv_b[0].reshape(1, -1), rowq, colq,
                mlstm_norm_w[0].reshape(1, -1), bsz, seq, ctx_len)

    wr = jnp.pad(jnp.concatenate([router2_w[0], router1_w[0]], 1),
                 ((0, 0), (0, LANE - N_GROUPS - N_EXPERTS))).astype(BF16)
    br = jnp.pad(jnp.concatenate([router2_b[0], router1_b[0]], 0),
                 (0, LANE - N_GROUPS - N_EXPERTS)).reshape(1, LANE)
    x1, h2, route = _mixer_out(ym, yc, x2d, pe, mods3, w_out[0].astype(BF16),
                               ln1_w[0].reshape(1, -1), ln1_b[0].reshape(1, -1), wr, br, seq, 256)

    sched, gtab, stab = _route(route, n_tok)
    ys = _experts(h2, gtab, stab, sched, w_gate[0], w_up[0], w_down[0])
    out = _final(x1, ys.reshape(-1, dm // 2), route, mods3, ln2_w[0].reshape(1, -1),
                 ln2_b[0].reshape(1, -1), seq, 256)
    return out.reshape(bsz, seq, dm)
```

```python
import functools

import jax
import jax.numpy as jnp
from jax import lax
from jax.experimental import pallas as pl
from jax.experimental.pallas import tpu as pltpu

F32 = jnp.float32
BF16 = jnp.bfloat16

D_MODEL = 2048
GRID_W = 64
D_MLSTM = 1024
D_CMLP = 1024
HEADS = 4
HD = 256
CMLP_GROUPS = 4
CMLP_GD = 256
CMLP_CHUNK = 128
N_GROUPS = 4
EXPERTS_PER_GROUP = 8
N_EXPERTS = 32
D_EXPERT = 1024
N_MOD = 6
N_GATE_COLS = 16
DEEPNORM_ALPHA = 2.0 ** 0.25
LN_EPS = 1e-6

LANE = 128
SUBLANES = 8
MCHUNK = 256
MOE_ROWS = 256
MOE_FT = 256
GROUP_STEP = 4
VMEM_LIMIT = 56 * 1024 * 1024


def _cparams(sem):
    return pltpu.CompilerParams(dimension_semantics=sem, vmem_limit_bytes=VMEM_LIMIT)


def _resident(shape):
    nd = len(shape)
    return pl.BlockSpec(shape, lambda *_: (0,) * nd, pipeline_mode=pl.Buffered(1))


def _sigmoid(x):
    return 0.5 * jnp.tanh(0.5 * x) + 0.5


def _silu(x):
    return x * _sigmoid(x)


def _log_sigmoid(x):
    return jnp.minimum(x, 0.0) - jnp.log1p(jnp.exp(-jnp.abs(x)))


def _gelu_tanh(x):
    c = 0.7978845608028654
    return 0.5 * x * (1.0 + jnp.tanh(c * (x + 0.044715 * (x * x * x))))


def _pack_bf16_pairs(lo, hi):
    lo_b = lax.bitcast_convert_type(lo.astype(BF16).astype(F32), jnp.uint32)
    hi_b = lax.bitcast_convert_type(hi.astype(BF16).astype(F32), jnp.uint32)
    return (lo_b >> 16) | (hi_b & jnp.uint32(0xFFFF0000))


def _unpack_bf16_pairs(w):
    lo = lax.bitcast_convert_type(w << 16, F32)
    hi = lax.bitcast_convert_type(w & jnp.uint32(0xFFFF0000), F32)
    return lo, hi


def _layer_norm_rows(z, w, b):
    mu = jnp.mean(z, axis=-1, keepdims=True)
    zc = z - mu
    var = jnp.mean(zc * zc, axis=-1, keepdims=True)
    return zc * lax.rsqrt(var + LN_EPS) * w + b


def _mod_kernel(c_ref, w_ref, b_ref, o_ref):
    s = _silu(c_ref[...]).astype(BF16)
    o_ref[...] = jnp.dot(s, w_ref[...].astype(BF16), preferred_element_type=F32) + b_ref[...]


def _modulation(cc, w_mod, b_mod):
    rows, dm = cc.shape
    n = w_mod.shape[1]
    tn = 1024
    return pl.pallas_call(
        _mod_kernel,
        grid=(n // tn,),
        in_specs=[pl.BlockSpec((rows, dm), lambda j: (0, 0)),
                  pl.BlockSpec((dm, tn), lambda j: (0, j)),
                  pl.BlockSpec((1, tn), lambda j: (0, j))],
        out_specs=pl.BlockSpec((rows, tn), lambda j: (0, j)),
        out_shape=jax.ShapeDtypeStruct((rows, n), F32),
        compiler_params=_cparams(("arbitrary",)),
        name="modulation",
    )(cc, w_mod, b_mod.reshape(1, n))


def _proj_kernel(*refs, n_w, has_pe, gelu_flags, has_cmlp, tn):
    x_ref = refs[0]
    k = 1
    pe_ref = None
    if has_pe:
        pe_ref = refs[k]
        k += 1
    m_ref = refs[k]
    k += 1
    w_refs = refs[k:k + n_w]
    k += n_w
    if has_cmlp:
        cnw_ref, ws_ref, bs_ref = refs[k:k + 3]
        k += 3
    o_refs = refs[k:k + n_w]
    hx_ref = refs[k + n_w]
    x = x_ref[...]
    if has_pe:
        x = x + pe_ref[...]
    m = m_ref[0]
    hx_ref[...] = (x * (1.0 + m[1:2, :]) + m[0:1, :]).astype(BF16)
    n_plain = n_w - 1 if has_cmlp else n_w
    for w_ref, o_ref, use_gelu in list(zip(w_refs, o_refs, gelu_flags))[:n_plain]:
        n = w_ref.shape[1]
        step = min(tn, n)
        for j in range(0, n, step):
            acc = jnp.dot(hx_ref[...], w_ref[:, j:j + step], preferred_element_type=F32)
            if use_gelu:
                acc = _gelu_tanh(acc)
            o_ref[:, j:j + step] = acc.astype(o_ref.dtype)
    if has_cmlp:
        w_ref, o_ref, s_ref = w_refs[-1], o_refs[-1], refs[k + n_w + 1]
        bm = x_ref.shape[0]
        vgate = _gelu_tanh(jnp.dot(hx_ref[...], w_ref[:, D_CMLP:], preferred_element_type=F32))
        for g in range(CMLP_GROUPS):
            gs = slice(g * CMLP_GD, (g + 1) * CMLP_GD)
            vg = vgate[:, gs]
            mu = jnp.mean(vg, axis=-1, keepdims=True)
            vc = vg - mu
            var = jnp.mean(vc * vc, axis=-1, keepdims=True)
            vn = (vc * lax.rsqrt(var + LN_EPS) * cnw_ref[:, gs]).astype(BF16)
            for p in range(bm // CMLP_CHUNK):
                ps = slice(p * CMLP_CHUNK, (p + 1) * CMLP_CHUNK)
                s_ref[ps, gs] = (jnp.dot(ws_ref[g], vn[ps, :], preferred_element_type=F32)
                                 + bs_ref[:, g:g + 1])
        u = _gelu_tanh(jnp.dot(hx_ref[...], w_ref[:, :D_CMLP], preferred_element_type=F32))
        o_ref[...] = (u * s_ref[...]).astype(o_ref.dtype)


def _projection(x2d, pe, mods3, mod_row_of_block, weights, out_dtypes, gelu_flags, bm, cmlp=None):
    rows, dm = x2d.shape
    has_pe = pe is not None
    n_w = len(weights)
    in_specs = [pl.BlockSpec((bm, dm), lambda i: (i, 0))]
    args = [x2d]
    if has_pe:
        pe_blocks = pe.shape[0] // bm
        in_specs.append(pl.BlockSpec((bm, dm), lambda i: (i % pe_blocks, 0)))
        args.append(pe)
    in_specs.append(pl.BlockSpec((1, N_MOD, dm), lambda i: (mod_row_of_block(i), 0, 0)))
    args.append(mods3)
    for w in weights:
        in_specs.append(_resident(w.shape))
        args.append(w)
    out_widths = [w.shape[1] for w in weights]
    scratch = [pltpu.VMEM((bm, dm), BF16)]
    if cmlp is not None:
        for a in cmlp:
            in_specs.append(_resident(a.shape))
            args.append(a)
        out_widths[-1] = D_CMLP
        scratch.append(pltpu.VMEM((bm, D_CMLP), F32))
    out_specs = [pl.BlockSpec((bm, n), lambda i: (i, 0)) for n in out_widths]
    out_shape = [jax.ShapeDtypeStruct((rows, n), dt) for n, dt in zip(out_widths, out_dtypes)]
    kern = functools.partial(_proj_kernel, n_w=n_w, has_pe=has_pe, gelu_flags=tuple(gelu_flags),
                             has_cmlp=cmlp is not None, tn=1024)
    return pl.pallas_call(
        kern,
        grid=(rows // bm,),
        in_specs=in_specs,
        out_specs=out_specs,
        out_shape=out_shape,
        scratch_shapes=scratch,
        compiler_params=_cparams(("parallel",)),
        name="projection",
    )(*args)


def _gate_kernel(lic_ref, lfc_ref, lir_ref, lfr_ref, row_ref, col_ref, *, nc):
    li = lic_ref[0, 0]
    lf = _log_sigmoid(lfc_ref[0, 0])
    length = li.shape[0]
    tid = lax.broadcasted_iota(jnp.int32, li.shape, 0)
    lane = lax.broadcasted_iota(jnp.int32, li.shape, 1)
    lane1 = lane[0:1, :]
    fwd = (lane < nc) | (lane == 2 * nc)

    def scan_sublanes(x, op, fill):
        p = x
        s = x
        k = 1
        while k < length:
            p = op(p, jnp.where(tid >= k, pltpu.roll(p, k, 0), fill))
            s = op(s, jnp.where(tid < length - k, pltpu.roll(s, length - k, 0), fill))
            k *= 2
        return jnp.where(fwd, p, s)

    b = scan_sublanes(lf, jnp.add, 0.0)
    btot = jnp.sum(lf, axis=0, keepdims=True)
    a = btot - b + li
    m_loc = jnp.max(a, axis=0, keepdims=True)
    r = li - b
    cm = scan_sublanes(r, jnp.maximum, -jnp.inf)

    m_ctx = jnp.maximum(btot, m_loc)
    m_in = jnp.where(lane1 == 0, pltpu.roll(m_ctx, LANE - 2 * nc, 1), pltpu.roll(m_ctx, LANE - 2, 1))
    for k in range(nc - 1):
        m_new = jnp.maximum(btot + m_in, m_loc)
        m_in = jnp.where(lane1 == k + 1, pltpu.roll(m_new, 1, 1),
                         jnp.where(lane1 == 2 * nc - 2 - k, pltpu.roll(m_new, LANE - 1, 1), m_in))
    is_ctx = lane1 >= 2 * nc
    m_in = jnp.where(is_ctx, 0.0, m_in)
    m_new = jnp.maximum(btot + m_in, m_loc)
    s_old = jnp.broadcast_to(jnp.exp(btot + m_in - m_new), li.shape)
    w = jnp.exp(a - m_new)
    big_m = jnp.maximum(m_in, cm)
    s_int = jnp.exp(m_in - big_m)
    e_neg = jnp.exp(-(b + big_m))
    g = 2 * nc
    col_ref[0, 0] = jnp.where(
        lane < g, w, jnp.where(
            lane < 2 * g, pltpu.roll(big_m, g, 1), jnp.where(
                lane < 3 * g, pltpu.roll(s_int, 2 * g, 1), jnp.where(
                    lane < 4 * g, pltpu.roll(e_neg, 3 * g, 1), jnp.where(
                        lane < 4 * g + 2, pltpu.roll(w, 3 * g, 1), pltpu.roll(s_old, 4 * g + 2, 1))))))

    lir = lir_ref[0, 0]
    lfr = _log_sigmoid(lfr_ref[0, 0])
    width = lir.shape[1]
    rid = lax.broadcasted_iota(jnp.int32, lir.shape, 0)
    pid = lax.broadcasted_iota(jnp.int32, lir.shape, 1)
    p = lfr
    s = lfr
    k = 1
    while k < width:
        p = p + jnp.where(pid >= k, pltpu.roll(p, k, 1), 0.0)
        s = s + jnp.where(pid < width - k, pltpu.roll(s, width - k, 1), 0.0)
        k *= 2
    row_ref[0, 0] = lir - jnp.where(rid < nc, p, s)


def _gate_stats(g_x, g_c, gate_bias, bsz, seq, ctx_len):
    nc = seq // MCHUNK
    gb = gate_bias.astype(F32)
    gx = g_x[:, :N_GATE_COLS].reshape(bsz, nc, MCHUNK, 2, 2, HEADS) + gb.reshape(2, 2, HEADS)
    gc = g_c[:, :N_GATE_COLS].reshape(bsz, ctx_len, 2, 2, HEADS) + gb.reshape(2, 2, HEADS)
    col_x = gx.transpose(4, 0, 5, 2, 3, 1).reshape(2, bsz, HEADS, MCHUNK, 2 * nc)
    col_c = gc.transpose(3, 0, 4, 1, 2)
    col = jnp.concatenate([col_x, col_c], -1)
    col = jnp.pad(col, ((0, 0),) * 4 + ((0, LANE - col.shape[-1]),))
    row = gx.transpose(4, 0, 5, 3, 1, 2).reshape(2, bsz, HEADS, 2 * nc, MCHUNK)
    blk_c = pl.BlockSpec((1, 1, MCHUNK, LANE), lambda b, h: (b, h, 0, 0))
    blk_r = pl.BlockSpec((1, 1, 2 * nc, MCHUNK), lambda b, h: (b, h, 0, 0))
    return pl.pallas_call(
        functools.partial(_gate_kernel, nc=nc),
        grid=(bsz, HEADS),
        in_specs=[blk_c, blk_c, blk_r, blk_r],
        out_specs=[blk_r, blk_c],
        out_shape=[jax.ShapeDtypeStruct((bsz, HEADS, 2 * nc, MCHUNK), F32),
                   jax.ShapeDtypeStruct((bsz, HEADS, MCHUNK, LANE), F32)],
        compiler_params=_cparams(("parallel", "parallel")),
        name="gate_stats",
    )(col[0], col[1], row[0], row[1])


def _mlstm_kernel(q_ref, k_ref, v_ref, o_ref, kc_ref, vc_ref, cwq_ref, cbq_ref, cwk_ref, cbk_ref,
                  row_ref, col_ref, nw_ref, band_ref, edge_ref, y_ref, q_s, k_s, kc_s, ct_s, n_s, *, nc):
    lc = MCHUNK
    halo_rows = edge_ref.shape[1]

    def conv_silu_chunk(x_ref, c, n_chunks, w, b, scale):
        x = x_ref[pl.ds(c * lc, lc), :]
        wb = w.astype(BF16)
        taps = jnp.concatenate([x * wb[0:1, :], x * wb[1:2, :], x * wb[2:3, :]], axis=0)
        y = jnp.dot(band_ref[...], taps, preferred_element_type=F32)
        if n_chunks > 1:
            rid = lax.broadcasted_iota(jnp.int32, (halo_rows, x.shape[1]), 0)
            wf = wb.astype(F32)
            halo = jnp.zeros((halo_rows, x.shape[1]), F32)
            if c > 0:
                prev = x_ref[pl.ds(c * lc - halo_rows, halo_rows), :].astype(F32)[halo_rows - 1:, :]
                halo = jnp.where(rid == 0, prev * wf[0:1, :], halo)
            if c < n_chunks - 1:
                nxt = x_ref[pl.ds((c + 1) * lc, halo_rows), :].astype(F32)[0:1, :]
                halo = jnp.where(rid == 1, nxt * wf[2:3, :], halo)
            y = y + jnp.dot(edge_ref[...], halo.astype(BF16), preferred_element_type=F32)
        y = _silu(y + b)
        if scale != 1.0:
            y = y * scale
        return y.astype(BF16)

    k_scale = HD ** -0.5
    for c in range(nc):
        sl = pl.ds(c * lc, lc)
        q_s[sl, :] = conv_silu_chunk(q_ref, c, nc, cwq_ref[...], cbq_ref[...], 1.0)
        k_s[sl, :] = conv_silu_chunk(k_ref, c, nc, cwk_ref[...], cbk_ref[...], k_scale)
    kc_s[...] = conv_silu_chunk(kc_ref, 0, 1, cwk_ref[...], cbk_ref[...], k_scale)

    def col(j):
        return col_ref[0, 0, :, j:j + 1]

    def local_state(kk, vv, wcol):
        vw = (vv.astype(F32) * wcol).astype(BF16)
        ct = lax.dot_general(kk, vw, (((0,), (0,)), ((), ())), preferred_element_type=F32)
        nn = jnp.sum(kk.astype(F32) * wcol, axis=0, keepdims=True)
        return ct, nn

    for d in range(2):
        ct, nn = local_state(kc_s[...], vc_ref[...], col(8 * nc + d))
        order = list(range(nc)) if d == 0 else list(range(nc - 1, -1, -1))
        for pos, c in enumerate(order):
            idx = d * nc + c
            ct_s[idx] = ct.astype(BF16)
            n_s[idx] = nn
            if pos == nc - 1:
                break
            sl = pl.ds(c * lc, lc)
            ctl, nl = local_state(k_s[sl, :], v_ref[sl, :], col(idx))
            s_old = col_ref[0, 0, 0:1, 8 * nc + 2 + idx:8 * nc + 3 + idx]
            ct = s_old * ct + ctl
            nn = s_old * nn + nl

    tid = lax.broadcasted_iota(jnp.int32, (lc, lc), 0)
    sid = lax.broadcasted_iota(jnp.int32, (lc, lc), 1)
    masks = (sid <= tid, sid >= tid)
    for c in range(nc):
        sl = pl.ds(c * lc, lc)
        q = q_s[sl, :]
        kk = k_s[sl, :]
        v = v_ref[sl, :]
        qf = q.astype(F32)
        s = lax.dot_general(q, kk, (((1,), (1,)), ((), ())), preferred_element_type=F32)
        h = None
        for d in range(2):
            idx = d * nc + c
            r = row_ref[0, 0, idx:idx + 1, :]
            big_m = col(2 * nc + idx)
            s_int = col(4 * nc + idx)
            e_neg = col(6 * nc + idx)
            p = jnp.where(masks[d], jnp.exp(r - big_m), 0.0) * s
            den = (jnp.sum(p, axis=-1, keepdims=True)
                   + s_int * jnp.sum(qf * n_s[idx], axis=-1, keepdims=True))
            num = (jnp.dot(p.astype(BF16), v, preferred_element_type=F32)
                   + s_int * jnp.dot(q, ct_s[idx], preferred_element_type=F32))
            hd = num * (1.0 / jnp.maximum(jnp.abs(den), e_neg))
            h = hd if h is None else h + hd
        mu = jnp.mean(h, axis=-1, keepdims=True)
        hc = h - mu
        var = jnp.mean(hc * hc, axis=-1, keepdims=True)
        hn = hc * lax.rsqrt(var + LN_EPS) * nw_ref[...]
        y_ref[sl, :] = (hn * _sigmoid(o_ref[sl, :].astype(F32))).astype(BF16)


def _mlstm(qkvo, kv_ctx, conv_w, conv_b, rowq, colq, norm_w, bsz, seq, ctx_len):
    nc = seq // MCHUNK
    hq = D_MLSTM // HD
    kern = functools.partial(_mlstm_kernel, nc=nc)
    ii = jnp.arange(MCHUNK)[:, None]
    jj = jnp.arange(MCHUNK)[None, :]
    band = jnp.concatenate([(jj == ii + t - 1) for t in range(3)], axis=1).astype(BF16)
    hh = jnp.arange(2 * SUBLANES)[None, :]
    edge = (((ii == 0) & (hh == 0)) | ((ii == MCHUNK - 1) & (hh == 1))).astype(BF16)
    seq_blk = lambda off: pl.BlockSpec((seq, HD), lambda b, h: (b, off + h))
    ctx_blk = lambda off: pl.BlockSpec((ctx_len, HD), lambda b, h: (b, off + h))
    return pl.pallas_call(
        kern,
        grid=(bsz, HEADS),
        in_specs=[seq_blk(0), seq_blk(hq), seq_blk(2 * hq), seq_blk(3 * hq),
                  ctx_blk(0), ctx_blk(hq),
                  pl.BlockSpec((3, HD), lambda b, h: (0, h)),
                  pl.BlockSpec((1, HD), lambda b, h: (0, h)),
                  pl.BlockSpec((3, HD), lambda b, h: (0, hq + h)),
                  pl.BlockSpec((1, HD), lambda b, h: (0, hq + h)),
                  pl.BlockSpec((1, 1, 2 * nc, MCHUNK), lambda b, h: (b, h, 0, 0)),
                  pl.BlockSpec((1, 1, MCHUNK, LANE), lambda b, h: (b, h, 0, 0)),
                  pl.BlockSpec((1, HD), lambda b, h: (0, h)),
                  _resident(band.shape), _resident(edge.shape)],
        out_specs=pl.BlockSpec((seq, HD), lambda b, h: (b, h)),
        out_shape=jax.ShapeDtypeStruct((bsz * seq, D_MLSTM), BF16),
        scratch_shapes=[pltpu.VMEM((seq, HD), BF16), pltpu.VMEM((seq, HD), BF16),
                        pltpu.VMEM((ctx_len, HD), BF16),
                        pltpu.VMEM((2 * nc, HD, HD), BF16), pltpu.VMEM((2 * nc, 1, HD), F32)],
        compiler_params=_cparams(("parallel", "parallel")),
        name="mlstm",
    )(qkvo, qkvo, qkvo, qkvo, kv_ctx, kv_ctx, conv_w, conv_b, conv_w, conv_b, rowq, colq, norm_w,
      band, edge)


def _out_kernel(ym_ref, yc_ref, x_ref, pe_ref, m_ref, wout_ref,
                l1w_ref, l1b_ref, wr_ref, br_ref, x1_ref, h2_ref, lg_ref):
    m = m_ref[0]
    y = (jnp.dot(ym_ref[...], wout_ref[:D_MLSTM, :], preferred_element_type=F32)
         + jnp.dot(yc_ref[...], wout_ref[D_MLSTM:, :], preferred_element_type=F32))
    z = DEEPNORM_ALPHA * (x_ref[...] + pe_ref[...]) + m[2:3, :] * y
    x1 = _layer_norm_rows(z, l1w_ref[...], l1b_ref[...])
    x1_ref[...] = x1
    h2 = x1 * (1.0 + m[4:5, :]) + m[3:4, :]
    half = h2.shape[1] // 2
    h2_ref[...] = _pack_bf16_pairs(h2[:, :half], h2[:, half:])
    lg = jnp.dot(h2.astype(BF16), wr_ref[...], preferred_element_type=F32) + br_ref[...]

    lgt = lg.T
    bm = lg.shape[0]
    epg = EXPERTS_PER_GROUP
    rid = lax.broadcasted_iota(jnp.int32, (SUBLANES, bm), 0).astype(F32)
    neg = -jnp.inf
    far = float(SUBLANES)
    gl = lgt[N_EXPERTS:N_EXPERTS + SUBLANES, :]
    is_grp = rid < N_GROUPS
    m1 = jnp.max(jnp.where(is_grp, gl, neg), axis=0, keepdims=True)
    grp = jnp.min(jnp.where(is_grp & (gl == m1), rid, far), axis=0, keepdims=True)
    p_grp = 1.0 / jnp.sum(jnp.where(is_grp, jnp.exp(gl - m1), 0.0), axis=0, keepdims=True)
    l2 = lgt[0:epg, :]
    for g in range(1, N_GROUPS):
        l2 = jnp.where(grp == g, lgt[g * epg:(g + 1) * epg, :], l2)
    v0 = jnp.max(l2, axis=0, keepdims=True)
    i0 = jnp.min(jnp.where(l2 == v0, rid, far), axis=0, keepdims=True)
    l2m = jnp.where(rid == i0, neg, l2)
    v1 = jnp.max(l2m, axis=0, keepdims=True)
    i1 = jnp.min(jnp.where(l2m == v1, rid, far), axis=0, keepdims=True)
    s1 = jnp.exp(v1 - v0)
    g0 = p_grp / (1.0 + s1)
    res = jnp.where(rid == 0, g0, jnp.where(rid == 1, g0 * s1, jnp.where(
        rid == 2, grp * epg + i0, jnp.where(rid == 3, grp * epg + i1, 0.0))))
    lg_ref[...] = jnp.concatenate([res, jnp.zeros((LANE - SUBLANES, bm), F32)], axis=0).T


def _mixer_out(ym, yc, x2d, pe, mods3, wout, l1w, l1b, wr, br, seq, bm):
    rows, dm = x2d.shape
    pe_blocks = seq // bm
    row_blk = lambda n: pl.BlockSpec((bm, n), lambda i: (i, 0))
    return pl.pallas_call(
        _out_kernel,
        grid=(rows // bm,),
        in_specs=[row_blk(D_MLSTM), row_blk(D_CMLP), row_blk(dm),
                  pl.BlockSpec((bm, dm), lambda i: (i % pe_blocks, 0)),
                  pl.BlockSpec((1, N_MOD, dm), lambda i: (i // pe_blocks, 0, 0)),
                  _resident(wout.shape), _resident(l1w.shape), _resident(l1b.shape),
                  _resident(wr.shape), _resident(br.shape)],
        out_specs=[row_blk(dm), row_blk(dm // 2), row_blk(LANE)],
        out_shape=[jax.ShapeDtypeStruct((rows, dm), F32),
                   jax.ShapeDtypeStruct((rows, dm // 2), jnp.uint32),
                   jax.ShapeDtypeStruct((rows, LANE), F32)],
        compiler_params=_cparams(("parallel",)),
        name="mixer_out",
    )(ym, yc, x2d, pe, mods3, wout, l1w, l1b, wr, br)


def _moe_kernel(comp_ref, blk_ref, slot_ref, cast_ref, cslot_ref, cexp_ref, ctile_ref,
                gtab_ref, stab_ref, h2_hbm, wgf_ref, wuf_ref, wdf_ref, ys_hbm,
                wg_s, wu_s, wd_s, xbuf, xb_s, ybuf, gsem, ssem, pend, *, nt):
    del blk_ref, cexp_ref
    s = pl.program_id(0)
    n_steps = pl.num_programs(0)
    groups = MOE_ROWS // SUBLANES
    half = xbuf.shape[-1]
    dm = 2 * half
    n_asg = ys_hbm.shape[0] * SUBLANES - 2 * MOE_ROWS

    def gather_wait(p, n):
        pltpu.make_async_copy(h2_hbm.at[pl.ds(0, n)], xbuf.at[p, pl.ds(0, n)], gsem.at[p]).wait()

    def scatter_wait(p, n):
        pltpu.make_async_copy(ybuf.at[p, pl.ds(0, n)], ys_hbm.at[pl.ds(0, n)], ssem.at[p]).wait()

    @pl.when(s == 0)
    def _():
        pend[0] = 0
        pend[1] = 0
        xbuf[...] = jnp.zeros_like(xbuf)

    nxt = jnp.minimum(s + 1, n_steps - 1)

    @pl.when(jnp.logical_and(s + 1 < n_steps, comp_ref[nxt] > 0))
    def _():
        p = (s + 1) % 2

        for par in range(2):
            for g0 in range(0, groups, GROUP_STEP):
                @pl.when(jnp.logical_and(p == par, g0 < comp_ref[nxt]))
                def _():
                    for r in range(g0 * SUBLANES, (g0 + GROUP_STEP) * SUBLANES):
                        tok = gtab_ref[0, 0, r]
                        pltpu.make_async_copy(h2_hbm.at[tok >> 3, pl.ds(tok & 7, 1)],
                                              xbuf.at[par, r // SUBLANES, pl.ds(r % SUBLANES, 1)],
                                              gsem.at[par]).start()

    @pl.when(cast_ref[s] == 1)
    def _():
        cs = cslot_ref[s]
        t = ctile_ref[s]
        for k in range(nt):
            @pl.when(t == k)
            def _():
                wg_s[cs, :, k * MOE_FT:(k + 1) * MOE_FT] = wgf_ref[0].astype(BF16)
                wu_s[cs, :, k * MOE_FT:(k + 1) * MOE_FT] = wuf_ref[0].astype(BF16)
        wd_s[cs, t] = wdf_ref[0].astype(BF16)

    @pl.when(comp_ref[s] > 0)
    def _():
        p = s % 2
        sl = slot_ref[s]
        ng = comp_ref[s]
        gather_wait(p, ng)
        x_lo, x_hi = _unpack_bf16_pairs(xbuf[p].reshape(MOE_ROWS, half))
        xb_s[:, :half] = x_lo.astype(BF16)
        xb_s[:, half:] = x_hi.astype(BF16)
        x = xb_s[...]
        g = jnp.dot(x, wg_s[sl], preferred_element_type=F32)
        u = jnp.dot(x, wu_s[sl], preferred_element_type=F32)
        h = (_silu(g) * u).astype(BF16)
        y = jnp.dot(h, wd_s[sl].reshape(D_EXPERT, dm), preferred_element_type=F32)

        @pl.when(pend[p] > 0)
        def _():
            scatter_wait(p, pend[p])

        ybuf[p] = _pack_bf16_pairs(y[:, :half], y[:, half:]).reshape(groups, SUBLANES, half)

        for par in range(2):
            for g0 in range(0, groups, GROUP_STEP):
                @pl.when(jnp.logical_and(p == par, g0 < ng))
                def _():
                    for r in range(g0 * SUBLANES, (g0 + GROUP_STEP) * SUBLANES):
                        d = stab_ref[0, 0, r]
                        dst = jnp.where(d < 0, n_asg + par * MOE_ROWS + r, d)
                        pltpu.make_async_copy(ybuf.at[par, r // SUBLANES, pl.ds(r % SUBLANES, 1)],
                                              ys_hbm.at[dst >> 3, pl.ds(dst & 7, 1)], ssem.at[par]).start()

        pend[p] = ng

    @pl.when(s == n_steps - 1)
    def _():
        for p in range(2):
            @pl.when(pend[p] > 0)
            def _():
                scatter_wait(p, pend[p])
                pend[p] = 0

        xbuf[...] = jnp.zeros_like(xbuf)
        fills = [pltpu.make_async_copy(xbuf.at[p], ys_hbm.at[pl.ds(n_asg // SUBLANES + p * groups, groups)],
                                       gsem.at[p]) for p in range(2)]
        for cp in fills:
            cp.start()
        for cp in fills:
            cp.wait()


def _experts(h2, gtab, stab, sched, wg, wu, wd):
    n_tok, dm = h2.shape[0], 2 * h2.shape[1]
    nt = D_EXPERT // MOE_FT
    n_asg = 2 * n_tok
    n_steps = sched[0].shape[0]

    smem_rows = lambda imap: pl.BlockSpec((1, 1, MOE_ROWS), imap, memory_space=pltpu.SMEM)
    grid_spec = pltpu.PrefetchScalarGridSpec(
        num_scalar_prefetch=7,
        grid=(n_steps,),
        in_specs=[smem_rows(lambda s, comp, blk, *_: (blk[jnp.minimum(s + 1, n_steps - 1)], 0, 0)),
                  smem_rows(lambda s, comp, blk, *_: (blk[s], 0, 0)),
                  pl.BlockSpec(memory_space=pl.ANY),
                  pl.BlockSpec((1, dm, MOE_FT), lambda s, c, b, sl, ca, cs, ce, ct: (ce[s], 0, ct[s])),
                  pl.BlockSpec((1, dm, MOE_FT), lambda s, c, b, sl, ca, cs, ce, ct: (ce[s], 0, ct[s])),
                  pl.BlockSpec((1, MOE_FT, dm), lambda s, c, b, sl, ca, cs, ce, ct: (ce[s], ct[s], 0))],
        out_specs=pl.BlockSpec(memory_space=pl.ANY),
        scratch_shapes=[pltpu.VMEM((2, dm, D_EXPERT), BF16), pltpu.VMEM((2, dm, D_EXPERT), BF16),
                        pltpu.VMEM((2, nt, MOE_FT, dm), BF16),
                        pltpu.VMEM((2, MOE_ROWS // SUBLANES, SUBLANES, dm // 2), jnp.uint32),
                        pltpu.VMEM((MOE_ROWS, dm), BF16),
                        pltpu.VMEM((2, MOE_ROWS // SUBLANES, SUBLANES, dm // 2), jnp.uint32),
                        pltpu.SemaphoreType.DMA((2,)), pltpu.SemaphoreType.DMA((2,)),
                        pltpu.SMEM((2,), jnp.int32)],
    )
    return pl.pallas_call(
        functools.partial(_moe_kernel, nt=nt),
        grid_spec=grid_spec,
        out_shape=jax.ShapeDtypeStruct(((n_asg + 2 * MOE_ROWS) // SUBLANES, SUBLANES, dm // 2), jnp.uint32),
        compiler_params=_cparams(("arbitrary",)),
        name="experts",
    )(*sched, gtab, stab, h2.reshape(n_tok // SUBLANES, SUBLANES, dm // 2), wg, wu, wd)


def _final_kernel(x1_ref, y0_ref, y1_ref, g_ref, m_ref, w_ref, b_ref, o_ref):
    m = m_ref[0]
    lo0, hi0 = _unpack_bf16_pairs(y0_ref[...])
    lo1, hi1 = _unpack_bf16_pairs(y1_ref[...])
    g0 = g_ref[:, 0:1]
    g1 = g_ref[:, 1:2]
    y = jnp.concatenate([g0 * lo0 + g1 * lo1, g0 * hi0 + g1 * hi1], axis=-1)
    z = DEEPNORM_ALPHA * x1_ref[...] + m[5:6, :] * y
    o_ref[...] = _layer_norm_rows(z, w_ref[...], b_ref[...])


def _final(x1, ys, gates, mods3, w, b, seq, bm):
    rows, dm = x1.shape
    blocks_per_batch = seq // bm
    slot_blocks = rows // bm
    row_blk = lambda n: pl.BlockSpec((bm, n), lambda i: (i, 0))
    return pl.pallas_call(
        _final_kernel,
        grid=(rows // bm,),
        in_specs=[row_blk(dm), row_blk(dm // 2), pl.BlockSpec((bm, dm // 2), lambda i: (i + slot_blocks, 0)),
                  row_blk(LANE),
                  pl.BlockSpec((1, N_MOD, dm), lambda i: (i // blocks_per_batch, 0, 0)),
                  _resident(w.shape), _resident(b.shape)],
        out_specs=row_blk(dm),
        out_shape=jax.ShapeDtypeStruct((rows, dm), F32),
        compiler_params=_cparams(("parallel",)),
        name="final_ln",
    )(x1, ys, ys, gates, mods3, w, b)


def _route(route, n_tok):
    e_flat = route[:, 2:4].astype(jnp.int32).reshape(-1)
    n_asg = e_flat.shape[0]
    earange = jnp.arange(N_EXPERTS, dtype=jnp.int32)
    counts = jnp.sum((e_flat[:, None] == earange[None, :]).astype(jnp.int32), 0)
    nblk_e = (counts + MOE_ROWS - 1) // MOE_ROWS
    pad_end = jnp.cumsum(nblk_e * MOE_ROWS)
    pad_start = pad_end - nblk_e * MOE_ROWS
    n_blk = n_asg // MOE_ROWS + N_EXPERTS

    order = jnp.argsort(e_flat, stable=True).astype(jnp.int32)
    starts = jnp.cumsum(counts) - counts
    ridx = jnp.arange(n_blk * MOE_ROWS, dtype=jnp.int32)
    e_row = jnp.minimum(jnp.sum((pad_end[None, :] <= ridx[:, None]).astype(jnp.int32), -1), N_EXPERTS - 1)
    rsel = e_row[:, None] == earange[None, :]
    k_row = ridx - jnp.sum(jnp.where(rsel, pad_start[None, :], 0), -1)
    valid = k_row < jnp.sum(jnp.where(rsel, counts[None, :], 0), -1)
    src = jnp.clip(jnp.sum(jnp.where(rsel, starts[None, :], 0), -1) + k_row, 0, n_asg - 1)
    row_asg = jnp.where(valid, order[src], -1)
    gtab = (jnp.maximum(row_asg, 0) >> 1).reshape(n_blk, 1, MOE_ROWS)
    stab = jnp.where(row_asg < 0, -1, (row_asg & 1) * n_tok + (row_asg >> 1)).reshape(n_blk, 1, MOE_ROWS)

    nt = D_EXPERT // MOE_FT
    n_steps = n_blk + (nt - 1) * N_EXPERTS + nt
    has = nblk_e > 0
    n_visits = jnp.sum(has.astype(jnp.int32))
    e_of_visit = jnp.sort(jnp.where(has, earange, N_EXPERTS))
    vsel = e_of_visit[:, None] == earange[None, :]
    nb_v = jnp.sum(jnp.where(vsel, nblk_e[None, :], 0), -1)
    steps_v = jnp.where(nb_v > 0, jnp.maximum(nb_v, nt), 0)
    end_v = nt + jnp.cumsum(steps_v)
    start_v = end_v - steps_v
    first_blk_v = jnp.cumsum(nb_v) - nb_v
    sidx = jnp.arange(n_steps, dtype=jnp.int32)
    v = jnp.sum((end_v[None, :] <= sidx[:, None]).astype(jnp.int32), -1)
    pick = lambda arr, idx: jnp.sum(jnp.where(idx[:, None] == earange[None, :], arr[None, :], 0), -1)
    in_visit = (sidx >= nt) & (v < n_visits)
    k = sidx - pick(start_v, v)
    comp = in_visit & (k < pick(nb_v, v))
    cnt_v = jnp.sum(jnp.where(vsel, counts[None, :], 0), -1)
    rows_s = jnp.clip(pick(cnt_v, v) - k * MOE_ROWS, 0, MOE_ROWS)
    chunk_rows = GROUP_STEP * SUBLANES
    groups_s = jnp.where(comp, (rows_s + chunk_rows - 1) // chunk_rows * GROUP_STEP, 0)
    blk_s = lax.cummax(jnp.where(comp, pick(first_blk_v, v) + k, 0), axis=0)
    prologue = sidx < nt
    cast = prologue | (in_visit & (k < nt) & (v + 1 < n_visits))
    cexp = jnp.where(prologue, e_of_visit[0], pick(e_of_visit, v + 1))
    ctile = jnp.where(prologue, sidx, k)
    code = lax.cummax(jnp.where(cast, cexp * nt + ctile, 0), axis=0)
    i32 = lambda a: a.astype(jnp.int32)
    sched = (i32(groups_s), i32(blk_s), i32(v % 2), i32(cast), i32(jnp.where(prologue, 0, (v + 1) % 2)),
             i32(jnp.minimum(code // nt, N_EXPERTS - 1)), i32(code % nt))
    return sched, gtab, stab


def _grid_pos_embed(rows):
    quarter = D_MODEL // 4
    omega = 1.0 / (10000.0 ** (jnp.arange(quarter, dtype=F32) / quarter))
    ar = jnp.arange(rows, dtype=F32)[:, None] * omega
    ac = jnp.arange(GRID_W, dtype=F32)[:, None] * omega
    shape = (rows, GRID_W, quarter)
    parts = [jnp.broadcast_to(jnp.sin(ar)[:, None, :], shape), jnp.broadcast_to(jnp.cos(ar)[:, None, :], shape),
             jnp.broadcast_to(jnp.sin(ac)[None, :, :], shape), jnp.broadcast_to(jnp.cos(ac)[None, :, :], shape)]
    return jnp.concatenate(parts, -1).reshape(rows * GRID_W, D_MODEL)


def kernel(x, c, ctx, c_ctx, w_mod, b_mod, w_in, conv_w, conv_b, gate_bias, mlstm_norm_w, cmlp_norm_w,
           w_s, b_s, w_out, ln1_w, ln1_b, router1_w, router1_b, router2_w, router2_b, w_gate, w_up,
           w_down, ln2_w, ln2_b):
    bsz, seq, dm = x.shape
    ctx_len = ctx.shape[1]
    n_tok = bsz * seq
    assert w_mod.shape[0] == 1 and dm == D_MODEL and seq % MCHUNK == 0 and ctx_len == MCHUNK
    pe = _grid_pos_embed(seq // GRID_W).astype(x.dtype)
    x2d = x.reshape(n_tok, dm)
    ctx2d = ctx.reshape(bsz * ctx_len, dm)

    mod_rows = 16
    cc = jnp.concatenate([c, c_ctx[None, :], jnp.zeros((mod_rows - bsz - 1, dm), c.dtype)], 0)
    mods3 = _modulation(cc, w_mod[0], b_mod[0]).reshape(mod_rows, N_MOD, dm)

    dq = D_MLSTM
    wi = w_in[0]
    w_qkvo = wi[:, :4 * dq].astype(BF16)
    w_g = jnp.pad(wi[:, 4 * dq:4 * dq + N_GATE_COLS], ((0, 0), (0, LANE - N_GATE_COLS))).astype(BF16)
    w_uv = wi[:, 4 * dq + N_GATE_COLS:].astype(BF16)
    bm_proj = 256
    blocks_per_seq = seq // bm_proj
    cmlp = (cmlp_norm_w[0].reshape(1, -1), w_s[0].astype(BF16), b_s[0].T)
    qkvo, g_x, yc = _projection(x2d, pe, mods3, lambda i: i // blocks_per_seq,
                                [w_qkvo, w_g, w_uv], [BF16, F32, BF16], [False, False, True], bm_proj,
                                cmlp=cmlp)
    w_kv = wi[:, dq:3 * dq].astype(BF16)
    kv_c, g_c = _projection(ctx2d, None, mods3, lambda i: bsz, [w_kv, w_g], [BF16, F32],
                            [False, False], bm_proj)

    rowq, colq = _gate_stats(g_x, g_c, gate_bias[0], bsz, seq, ctx_len)
    ym = _mlstm(qkvo, kv_c, conv_w[0], conv_b[0].reshape(1, -1), rowq, colq,
                mlstm_norm_w[0].reshape(1, -1), bsz, seq, ctx_len)

    wr = jnp.pad(jnp.concatenate([router2_w[0], router1_w[0]], 1),
                 ((0, 0), (0, LANE - N_GROUPS - N_EXPERTS))).astype(BF16)
    br = jnp.pad(jnp.concatenate([router2_b[0], router1_b[0]], 0),
                 (0, LANE - N_GROUPS - N_EXPERTS)).reshape(1, LANE)
    x1, h2, route = _mixer_out(ym, yc, x2d, pe, mods3, w_out[0].astype(BF16),
                               ln1_w[0].reshape(1, -1), ln1_b[0].reshape(1, -1), wr, br, seq, 256)

    sched, gtab, stab = _route(route, n_tok)
    ys = _experts(h2, gtab, stab, sched, w_gate[0], w_up[0], w_down[0])
    out = _final(x1, ys.reshape(-1, dm // 2), route, mods3, ln2_w[0].reshape(1, -1),
                 ln2_b[0].reshape(1, -1), seq, 256)
    return out.reshape(bsz, seq, dm)
```

```python
import functools

import jax
import jax.numpy as jnp
from jax import lax
from jax.experimental import pallas as pl
from jax.experimental.pallas import tpu as pltpu

F32 = jnp.float32
BF16 = jnp.bfloat16

D_MODEL = 2048
GRID_W = 64
D_MLSTM = 1024
D_CMLP = 1024
HEADS = 4
HD = 256
CMLP_GROUPS = 4
CMLP_GD = 256
CMLP_CHUNK = 128
N_GROUPS = 4
EXPERTS_PER_GROUP = 8
N_EXPERTS = 32
D_EXPERT = 1024
N_MOD = 6
N_GATE_COLS = 16
DEEPNORM_ALPHA = 2.0 ** 0.25
LN_EPS = 1e-6

LANE = 128
SUBLANES = 8
MCHUNK = 256
MOE_ROWS = 256
MOE_FT = 256
GROUP_STEP = 4
ROW_LINES = D_MODEL // 2 // LANE
VMEM_LIMIT = 56 * 1024 * 1024


def _cparams(sem):
    return pltpu.CompilerParams(dimension_semantics=sem, vmem_limit_bytes=VMEM_LIMIT)


def _resident(shape):
    nd = len(shape)
    return pl.BlockSpec(shape, lambda *_: (0,) * nd, pipeline_mode=pl.Buffered(1))


def _sigmoid(x):
    return 0.5 * jnp.tanh(0.5 * x) + 0.5


def _silu(x):
    return x * _sigmoid(x)


def _log_sigmoid(x):
    return jnp.minimum(x, 0.0) - jnp.log1p(jnp.exp(-jnp.abs(x)))


def _gelu_tanh(x):
    c = 0.7978845608028654
    return 0.5 * x * (1.0 + jnp.tanh(c * (x + 0.044715 * (x * x * x))))


def _pack_bf16_pairs(lo, hi):
    lo_b = lax.bitcast_convert_type(lo.astype(BF16).astype(F32), jnp.uint32)
    hi_b = lax.bitcast_convert_type(hi.astype(BF16).astype(F32), jnp.uint32)
    return (lo_b >> 16) | (hi_b & jnp.uint32(0xFFFF0000))


def _unpack_bf16_pairs(w):
    lo = lax.bitcast_convert_type(w << 16, F32)
    hi = lax.bitcast_convert_type(w & jnp.uint32(0xFFFF0000), F32)
    return lo, hi


def _layer_norm_rows(z, w, b):
    mu = jnp.mean(z, axis=-1, keepdims=True)
    zc = z - mu
    var = jnp.mean(zc * zc, axis=-1, keepdims=True)
    return zc * lax.rsqrt(var + LN_EPS) * w + b


def _mod_kernel(c_ref, w_ref, b_ref, o_ref):
    s = _silu(c_ref[...]).astype(BF16)
    o_ref[...] = jnp.dot(s, w_ref[...].astype(BF16), preferred_element_type=F32) + b_ref[...]


def _modulation(cc, w_mod, b_mod):
    rows, dm = cc.shape
    n = w_mod.shape[1]
    tn = 1024
    return pl.pallas_call(
        _mod_kernel,
        grid=(n // tn,),
        in_specs=[pl.BlockSpec((rows, dm), lambda j: (0, 0)),
                  pl.BlockSpec((dm, tn), lambda j: (0, j)),
                  pl.BlockSpec((1, tn), lambda j: (0, j))],
        out_specs=pl.BlockSpec((rows, tn), lambda j: (0, j)),
        out_shape=jax.ShapeDtypeStruct((rows, n), F32),
        compiler_params=_cparams(("arbitrary",)),
        name="modulation",
    )(cc, w_mod, b_mod.reshape(1, n))


def _proj_kernel(*refs, n_w, has_pe, gelu_flags, has_cmlp, tn):
    x_ref = refs[0]
    k = 1
    pe_ref = None
    if has_pe:
        pe_ref = refs[k]
        k += 1
    m_ref = refs[k]
    k += 1
    w_refs = refs[k:k + n_w]
    k += n_w
    if has_cmlp:
        cnw_ref, ws_ref, bs_ref = refs[k:k + 3]
        k += 3
    o_refs = refs[k:k + n_w]
    hx_ref = refs[k + n_w]
    x = x_ref[...]
    if has_pe:
        x = x + pe_ref[...]
    m = m_ref[0]
    hx_ref[...] = (x * (1.0 + m[1:2, :]) + m[0:1, :]).astype(BF16)
    n_plain = n_w - 1 if has_cmlp else n_w
    for w_ref, o_ref, use_gelu in list(zip(w_refs, o_refs, gelu_flags))[:n_plain]:
        n = w_ref.shape[1]
        step = min(tn, n)
        for j in range(0, n, step):
            acc = jnp.dot(hx_ref[...], w_ref[:, j:j + step], preferred_element_type=F32)
            if use_gelu:
                acc = _gelu_tanh(acc)
            o_ref[:, j:j + step] = acc.astype(o_ref.dtype)
    if has_cmlp:
        w_ref, o_ref, s_ref = w_refs[-1], o_refs[-1], refs[k + n_w + 1]
        bm = x_ref.shape[0]
        vgate = _gelu_tanh(jnp.dot(hx_ref[...], w_ref[:, D_CMLP:], preferred_element_type=F32))
        for g in range(CMLP_GROUPS):
            gs = slice(g * CMLP_GD, (g + 1) * CMLP_GD)
            vg = vgate[:, gs]
            mu = jnp.mean(vg, axis=-1, keepdims=True)
            vc = vg - mu
            var = jnp.mean(vc * vc, axis=-1, keepdims=True)
            vn = (vc * lax.rsqrt(var + LN_EPS) * cnw_ref[:, gs]).astype(BF16)
            for p in range(bm // CMLP_CHUNK):
                ps = slice(p * CMLP_CHUNK, (p + 1) * CMLP_CHUNK)
                s_ref[ps, gs] = (jnp.dot(ws_ref[g], vn[ps, :], preferred_element_type=F32)
                                 + bs_ref[:, g:g + 1])
        u = _gelu_tanh(jnp.dot(hx_ref[...], w_ref[:, :D_CMLP], preferred_element_type=F32))
        o_ref[...] = (u * s_ref[...]).astype(o_ref.dtype)


def _projection(x2d, pe, mods3, mod_row_of_block, weights, out_dtypes, gelu_flags, bm, cmlp=None):
    rows, dm = x2d.shape
    has_pe = pe is not None
    n_w = len(weights)
    in_specs = [pl.BlockSpec((bm, dm), lambda i: (i, 0))]
    args = [x2d]
    if has_pe:
        pe_blocks = pe.shape[0] // bm
        in_specs.append(pl.BlockSpec((bm, dm), lambda i: (i % pe_blocks, 0)))
        args.append(pe)
    in_specs.append(pl.BlockSpec((1, N_MOD, dm), lambda i: (mod_row_of_block(i), 0, 0)))
    args.append(mods3)
    for w in weights:
        in_specs.append(_resident(w.shape))
        args.append(w)
    out_widths = [w.shape[1] for w in weights]
    scratch = [pltpu.VMEM((bm, dm), BF16)]
    if cmlp is not None:
        for a in cmlp:
            in_specs.append(_resident(a.shape))
            args.append(a)
        out_widths[-1] = D_CMLP
        scratch.append(pltpu.VMEM((bm, D_CMLP), F32))
    out_specs = [pl.BlockSpec((bm, n), lambda i: (i, 0)) for n in out_widths]
    out_shape = [jax.ShapeDtypeStruct((rows, n), dt) for n, dt in zip(out_widths, out_dtypes)]
    kern = functools.partial(_proj_kernel, n_w=n_w, has_pe=has_pe, gelu_flags=tuple(gelu_flags),
                             has_cmlp=cmlp is not None, tn=1024)
    return pl.pallas_call(
        kern,
        grid=(rows // bm,),
        in_specs=in_specs,
        out_specs=out_specs,
        out_shape=out_shape,
        scratch_shapes=scratch,
        compiler_params=_cparams(("parallel",)),
        name="projection",
    )(*args)


def _gate_kernel(lic_ref, lfc_ref, lir_ref, lfr_ref, row_ref, col_ref, *, nc):
    li = lic_ref[0, 0]
    lf = _log_sigmoid(lfc_ref[0, 0])
    length = li.shape[0]
    tid = lax.broadcasted_iota(jnp.int32, li.shape, 0)
    lane = lax.broadcasted_iota(jnp.int32, li.shape, 1)
    lane1 = lane[0:1, :]
    fwd = (lane < nc) | (lane == 2 * nc)

    def scan_sublanes(x, op, fill):
        p = x
        s = x
        k = 1
        while k < length:
            p = op(p, jnp.where(tid >= k, pltpu.roll(p, k, 0), fill))
            s = op(s, jnp.where(tid < length - k, pltpu.roll(s, length - k, 0), fill))
            k *= 2
        return jnp.where(fwd, p, s)

    b = scan_sublanes(lf, jnp.add, 0.0)
    btot = jnp.sum(lf, axis=0, keepdims=True)
    a = btot - b + li
    m_loc = jnp.max(a, axis=0, keepdims=True)
    r = li - b
    cm = scan_sublanes(r, jnp.maximum, -jnp.inf)

    m_ctx = jnp.maximum(btot, m_loc)
    m_in = jnp.where(lane1 == 0, pltpu.roll(m_ctx, LANE - 2 * nc, 1), pltpu.roll(m_ctx, LANE - 2, 1))
    for k in range(nc - 1):
        m_new = jnp.maximum(btot + m_in, m_loc)
        m_in = jnp.where(lane1 == k + 1, pltpu.roll(m_new, 1, 1),
                         jnp.where(lane1 == 2 * nc - 2 - k, pltpu.roll(m_new, LANE - 1, 1), m_in))
    is_ctx = lane1 >= 2 * nc
    m_in = jnp.where(is_ctx, 0.0, m_in)
    m_new = jnp.maximum(btot + m_in, m_loc)
    s_old = jnp.broadcast_to(jnp.exp(btot + m_in - m_new), li.shape)
    w = jnp.exp(a - m_new)
    big_m = jnp.maximum(m_in, cm)
    s_int = jnp.exp(m_in - big_m)
    e_neg = jnp.exp(-(b + big_m))
    g = 2 * nc
    col_ref[0, 0] = jnp.where(
        lane < g, w, jnp.where(
            lane < 2 * g, pltpu.roll(big_m, g, 1), jnp.where(
                lane < 3 * g, pltpu.roll(s_int, 2 * g, 1), jnp.where(
                    lane < 4 * g, pltpu.roll(e_neg, 3 * g, 1), jnp.where(
                        lane < 4 * g + 2, pltpu.roll(w, 3 * g, 1), pltpu.roll(s_old, 4 * g + 2, 1))))))

    lir = lir_ref[0, 0]
    lfr = _log_sigmoid(lfr_ref[0, 0])
    width = lir.shape[1]
    rid = lax.broadcasted_iota(jnp.int32, lir.shape, 0)
    pid = lax.broadcasted_iota(jnp.int32, lir.shape, 1)
    p = lfr
    s = lfr
    k = 1
    while k < width:
        p = p + jnp.where(pid >= k, pltpu.roll(p, k, 1), 0.0)
        s = s + jnp.where(pid < width - k, pltpu.roll(s, width - k, 1), 0.0)
        k *= 2
    row_ref[0, 0] = lir - jnp.where(rid < nc, p, s)


def _gate_stats(g_x, g_c, gate_bias, bsz, seq, ctx_len):
    nc = seq // MCHUNK
    gb = gate_bias.astype(F32)
    gx = g_x[:, :N_GATE_COLS].reshape(bsz, nc, MCHUNK, 2, 2, HEADS) + gb.reshape(2, 2, HEADS)
    gc = g_c[:, :N_GATE_COLS].reshape(bsz, ctx_len, 2, 2, HEADS) + gb.reshape(2, 2, HEADS)
    col_x = gx.transpose(4, 0, 5, 2, 3, 1).reshape(2, bsz, HEADS, MCHUNK, 2 * nc)
    col_c = gc.transpose(3, 0, 4, 1, 2)
    col = jnp.concatenate([col_x, col_c], -1)
    col = jnp.pad(col, ((0, 0),) * 4 + ((0, LANE - col.shape[-1]),))
    row = gx.transpose(4, 0, 5, 3, 1, 2).reshape(2, bsz, HEADS, 2 * nc, MCHUNK)
    blk_c = pl.BlockSpec((1, 1, MCHUNK, LANE), lambda b, h: (b, h, 0, 0))
    blk_r = pl.BlockSpec((1, 1, 2 * nc, MCHUNK), lambda b, h: (b, h, 0, 0))
    return pl.pallas_call(
        functools.partial(_gate_kernel, nc=nc),
        grid=(bsz, HEADS),
        in_specs=[blk_c, blk_c, blk_r, blk_r],
        out_specs=[blk_r, blk_c],
        out_shape=[jax.ShapeDtypeStruct((bsz, HEADS, 2 * nc, MCHUNK), F32),
                   jax.ShapeDtypeStruct((bsz, HEADS, MCHUNK, LANE), F32)],
        compiler_params=_cparams(("parallel", "parallel")),
        name="gate_stats",
    )(col[0], col[1], row[0], row[1])


def _mlstm_kernel(q_ref, k_ref, v_ref, o_ref, kc_ref, vc_ref, cwq_ref, cbq_ref, cwk_ref, cbk_ref,
                  row_ref, col_ref, nw_ref, band_ref, edge_ref, y_ref, q_s, k_s, kc_s, ct_s, n_s, *, nc):
    lc = MCHUNK
    halo_rows = edge_ref.shape[1]

    def conv_silu_chunk(x_ref, c, n_chunks, w, b, scale):
        x = x_ref[pl.ds(c * lc, lc), :]
        wb = w.astype(BF16)
        taps = jnp.concatenate([x * wb[0:1, :], x * wb[1:2, :], x * wb[2:3, :]], axis=0)
        y = jnp.dot(band_ref[...], taps, preferred_element_type=F32)
        if n_chunks > 1:
            rid = lax.broadcasted_iota(jnp.int32, (halo_rows, x.shape[1]), 0)
            wf = wb.astype(F32)
            halo = jnp.zeros((halo_rows, x.shape[1]), F32)
            if c > 0:
                prev = x_ref[pl.ds(c * lc - halo_rows, halo_rows), :].astype(F32)[halo_rows - 1:, :]
                halo = jnp.where(rid == 0, prev * wf[0:1, :], halo)
            if c < n_chunks - 1:
                nxt = x_ref[pl.ds((c + 1) * lc, halo_rows), :].astype(F32)[0:1, :]
                halo = jnp.where(rid == 1, nxt * wf[2:3, :], halo)
            y = y + jnp.dot(edge_ref[...], halo.astype(BF16), preferred_element_type=F32)
        y = _silu(y + b)
        if scale != 1.0:
            y = y * scale
        return y.astype(BF16)

    k_scale = HD ** -0.5
    for c in range(nc):
        sl = pl.ds(c * lc, lc)
        q_s[sl, :] = conv_silu_chunk(q_ref, c, nc, cwq_ref[...], cbq_ref[...], 1.0)
        k_s[sl, :] = conv_silu_chunk(k_ref, c, nc, cwk_ref[...], cbk_ref[...], k_scale)
    kc_s[...] = conv_silu_chunk(kc_ref, 0, 1, cwk_ref[...], cbk_ref[...], k_scale)

    def col(j):
        return col_ref[0, 0, :, j:j + 1]

    def local_state(kk, vv, wcol):
        vw = (vv.astype(F32) * wcol).astype(BF16)
        ct = lax.dot_general(kk, vw, (((0,), (0,)), ((), ())), preferred_element_type=F32)
        nn = jnp.sum(kk.astype(F32) * wcol, axis=0, keepdims=True)
        return ct, nn

    for d in range(2):
        ct, nn = local_state(kc_s[...], vc_ref[...], col(8 * nc + d))
        order = list(range(nc)) if d == 0 else list(range(nc - 1, -1, -1))
        for pos, c in enumerate(order):
            idx = d * nc + c
            ct_s[idx] = ct.astype(BF16)
            n_s[idx] = nn
            if pos == nc - 1:
                break
            sl = pl.ds(c * lc, lc)
            ctl, nl = local_state(k_s[sl, :], v_ref[sl, :], col(idx))
            s_old = col_ref[0, 0, 0:1, 8 * nc + 2 + idx:8 * nc + 3 + idx]
            ct = s_old * ct + ctl
            nn = s_old * nn + nl

    tid = lax.broadcasted_iota(jnp.int32, (lc, lc), 0)
    sid = lax.broadcasted_iota(jnp.int32, (lc, lc), 1)
    masks = (sid <= tid, sid >= tid)
    for c in range(nc):
        sl = pl.ds(c * lc, lc)
        q = q_s[sl, :]
        kk = k_s[sl, :]
        v = v_ref[sl, :]
        qf = q.astype(F32)
        s = lax.dot_general(q, kk, (((1,), (1,)), ((), ())), preferred_element_type=F32)
        h = None
        for d in range(2):
            idx = d * nc + c
            r = row_ref[0, 0, idx:idx + 1, :]
            big_m = col(2 * nc + idx)
            s_int = col(4 * nc + idx)
            e_neg = col(6 * nc + idx)
            p = jnp.where(masks[d], jnp.exp(r - big_m), 0.0) * s
            den = (jnp.sum(p, axis=-1, keepdims=True)
                   + s_int * jnp.sum(qf * n_s[idx], axis=-1, keepdims=True))
            num = (jnp.dot(p.astype(BF16), v, preferred_element_type=F32)
                   + s_int * jnp.dot(q, ct_s[idx], preferred_element_type=F32))
            hd = num * (1.0 / jnp.maximum(jnp.abs(den), e_neg))
            h = hd if h is None else h + hd
        mu = jnp.mean(h, axis=-1, keepdims=True)
        hc = h - mu
        var = jnp.mean(hc * hc, axis=-1, keepdims=True)
        hn = hc * lax.rsqrt(var + LN_EPS) * nw_ref[...]
        y_ref[sl, :] = (hn * _sigmoid(o_ref[sl, :].astype(F32))).astype(BF16)


def _mlstm(qkvo, kv_ctx, conv_w, conv_b, rowq, colq, norm_w, bsz, seq, ctx_len):
    nc = seq // MCHUNK
    hq = D_MLSTM // HD
    kern = functools.partial(_mlstm_kernel, nc=nc)
    ii = jnp.arange(MCHUNK)[:, None]
    jj = jnp.arange(MCHUNK)[None, :]
    band = jnp.concatenate([(jj == ii + t - 1) for t in range(3)], axis=1).astype(BF16)
    hh = jnp.arange(2 * SUBLANES)[None, :]
    edge = (((ii == 0) & (hh == 0)) | ((ii == MCHUNK - 1) & (hh == 1))).astype(BF16)
    seq_blk = lambda off: pl.BlockSpec((seq, HD), lambda b, h: (b, off + h))
    ctx_blk = lambda off: pl.BlockSpec((ctx_len, HD), lambda b, h: (b, off + h))
    return pl.pallas_call(
        kern,
        grid=(bsz, HEADS),
        in_specs=[seq_blk(0), seq_blk(hq), seq_blk(2 * hq), seq_blk(3 * hq),
                  ctx_blk(0), ctx_blk(hq),
                  pl.BlockSpec((3, HD), lambda b, h: (0, h)),
                  pl.BlockSpec((1, HD), lambda b, h: (0, h)),
                  pl.BlockSpec((3, HD), lambda b, h: (0, hq + h)),
                  pl.BlockSpec((1, HD), lambda b, h: (0, hq + h)),
                  pl.BlockSpec((1, 1, 2 * nc, MCHUNK), lambda b, h: (b, h, 0, 0)),
                  pl.BlockSpec((1, 1, MCHUNK, LANE), lambda b, h: (b, h, 0, 0)),
                  pl.BlockSpec((1, HD), lambda b, h: (0, h)),
                  _resident(band.shape), _resident(edge.shape)],
        out_specs=pl.BlockSpec((seq, HD), lambda b, h: (b, h)),
        out_shape=jax.ShapeDtypeStruct((bsz * seq, D_MLSTM), BF16),
        scratch_shapes=[pltpu.VMEM((seq, HD), BF16), pltpu.VMEM((seq, HD), BF16),
                        pltpu.VMEM((ctx_len, HD), BF16),
                        pltpu.VMEM((2 * nc, HD, HD), BF16), pltpu.VMEM((2 * nc, 1, HD), F32)],
        compiler_params=_cparams(("parallel", "parallel")),
        name="mlstm",
    )(qkvo, qkvo, qkvo, qkvo, kv_ctx, kv_ctx, conv_w, conv_b, conv_w, conv_b, rowq, colq, norm_w,
      band, edge)


def _out_kernel(ym_ref, yc_ref, x_ref, pe_ref, m_ref, wout_ref,
                l1w_ref, l1b_ref, wr_ref, br_ref, x1_ref, h2_ref, lg_ref):
    m = m_ref[0]
    y = (jnp.dot(ym_ref[...], wout_ref[:D_MLSTM, :], preferred_element_type=F32)
         + jnp.dot(yc_ref[...], wout_ref[D_MLSTM:, :], preferred_element_type=F32))
    z = DEEPNORM_ALPHA * (x_ref[...] + pe_ref[...]) + m[2:3, :] * y
    x1 = _layer_norm_rows(z, l1w_ref[...], l1b_ref[...])
    x1_ref[...] = x1
    h2 = x1 * (1.0 + m[4:5, :]) + m[3:4, :]
    half = h2.shape[1] // 2
    packed = _pack_bf16_pairs(h2[:, :half], h2[:, half:])
    for c in range(ROW_LINES):
        h2_ref[pl.ds(c, h2.shape[0], stride=ROW_LINES), :] = packed[:, c * LANE:(c + 1) * LANE]
    lg = jnp.dot(h2.astype(BF16), wr_ref[...], preferred_element_type=F32) + br_ref[...]

    lgt = lg.T
    bm = lg.shape[0]
    epg = EXPERTS_PER_GROUP
    rid = lax.broadcasted_iota(jnp.int32, (SUBLANES, bm), 0).astype(F32)
    neg = -jnp.inf
    far = float(SUBLANES)
    gl = lgt[N_EXPERTS:N_EXPERTS + SUBLANES, :]
    is_grp = rid < N_GROUPS
    m1 = jnp.max(jnp.where(is_grp, gl, neg), axis=0, keepdims=True)
    grp = jnp.min(jnp.where(is_grp & (gl == m1), rid, far), axis=0, keepdims=True)
    p_grp = 1.0 / jnp.sum(jnp.where(is_grp, jnp.exp(gl - m1), 0.0), axis=0, keepdims=True)
    l2 = lgt[0:epg, :]
    for g in range(1, N_GROUPS):
        l2 = jnp.where(grp == g, lgt[g * epg:(g + 1) * epg, :], l2)
    v0 = jnp.max(l2, axis=0, keepdims=True)
    i0 = jnp.min(jnp.where(l2 == v0, rid, far), axis=0, keepdims=True)
    l2m = jnp.where(rid == i0, neg, l2)
    v1 = jnp.max(l2m, axis=0, keepdims=True)
    i1 = jnp.min(jnp.where(l2m == v1, rid, far), axis=0, keepdims=True)
    s1 = jnp.exp(v1 - v0)
    g0 = p_grp / (1.0 + s1)
    res = jnp.where(rid == 0, g0, jnp.where(rid == 1, g0 * s1, jnp.where(
        rid == 2, grp * epg + i0, jnp.where(rid == 3, grp * epg + i1, 0.0))))
    lg_ref[...] = jnp.concatenate([res, jnp.zeros((LANE - SUBLANES, bm), F32)], axis=0).T


def _mixer_out(ym, yc, x2d, pe, mods3, wout, l1w, l1b, wr, br, seq, bm):
    rows, dm = x2d.shape
    pe_blocks = seq // bm
    row_blk = lambda n: pl.BlockSpec((bm, n), lambda i: (i, 0))
    return pl.pallas_call(
        _out_kernel,
        grid=(rows // bm,),
        in_specs=[row_blk(D_MLSTM), row_blk(D_CMLP), row_blk(dm),
                  pl.BlockSpec((bm, dm), lambda i: (i % pe_blocks, 0)),
                  pl.BlockSpec((1, N_MOD, dm), lambda i: (i // pe_blocks, 0, 0)),
                  _resident(wout.shape), _resident(l1w.shape), _resident(l1b.shape),
                  _resident(wr.shape), _resident(br.shape)],
        out_specs=[row_blk(dm), pl.BlockSpec((bm * ROW_LINES, LANE), lambda i: (i, 0)), row_blk(LANE)],
        out_shape=[jax.ShapeDtypeStruct((rows, dm), F32),
                   jax.ShapeDtypeStruct((rows * ROW_LINES, LANE), jnp.uint32),
                   jax.ShapeDtypeStruct((rows, LANE), F32)],
        compiler_params=_cparams(("parallel",)),
        name="mixer_out",
    )(ym, yc, x2d, pe, mods3, wout, l1w, l1b, wr, br)


def _moe_kernel(comp_ref, blk_ref, slot_ref, cast_ref, cslot_ref, cexp_ref, ctile_ref,
                gtab_ref, stab_ref, h2_hbm, wgf_ref, wuf_ref, wdf_ref, ys_hbm,
                wg_s, wu_s, wd_s, xbuf, xb_s, ybuf, gsem, ssem, pend, *, nt):
    del blk_ref, cexp_ref
    s = pl.program_id(0)
    n_steps = pl.num_programs(0)
    groups = MOE_ROWS // SUBLANES
    rl = ROW_LINES
    half = rl * LANE
    dm = 2 * half
    n_asg = ys_hbm.shape[0] // rl - 2 * MOE_ROWS
    group_lines = SUBLANES * rl

    def gather_wait(p, n):
        nl = n * group_lines
        pltpu.make_async_copy(h2_hbm.at[pl.ds(0, nl)], xbuf.at[p, pl.ds(0, nl)], gsem.at[p]).wait()

    def scatter_wait(p, n):
        nl = n * group_lines
        pltpu.make_async_copy(ybuf.at[p, pl.ds(0, nl)], ys_hbm.at[pl.ds(0, nl)], ssem.at[p]).wait()

    @pl.when(s == 0)
    def _():
        pend[0] = 0
        pend[1] = 0
        xbuf[...] = jnp.zeros_like(xbuf)

    nxt = jnp.minimum(s + 1, n_steps - 1)

    @pl.when(jnp.logical_and(s + 1 < n_steps, comp_ref[nxt] > 0))
    def _():
        p = (s + 1) % 2

        for par in range(2):
            for g0 in range(0, groups, GROUP_STEP):
                @pl.when(jnp.logical_and(p == par, g0 < comp_ref[nxt]))
                def _():
                    for r in range(g0 * SUBLANES, (g0 + GROUP_STEP) * SUBLANES):
                        src = pl.multiple_of(gtab_ref[0, 0, r] * rl, rl)
                        pltpu.make_async_copy(h2_hbm.at[pl.ds(src, rl)], xbuf.at[par, pl.ds(r * rl, rl)],
                                              gsem.at[par]).start()

    @pl.when(cast_ref[s] == 1)
    def _():
        cs = cslot_ref[s]
        t = ctile_ref[s]
        for k in range(nt):
            @pl.when(t == k)
            def _():
                wg_s[cs, :, k * MOE_FT:(k + 1) * MOE_FT] = wgf_ref[0].astype(BF16)
                wu_s[cs, :, k * MOE_FT:(k + 1) * MOE_FT] = wuf_ref[0].astype(BF16)
        wd_s[cs, t] = wdf_ref[0].astype(BF16)

    @pl.when(comp_ref[s] > 0)
    def _():
        p = s % 2
        sl = slot_ref[s]
        ng = comp_ref[s]
        gather_wait(p, ng)
        for c in range(rl):
            x_lo, x_hi = _unpack_bf16_pairs(xbuf[p, pl.ds(c, MOE_ROWS, stride=rl), :])
            xb_s[:, c * LANE:(c + 1) * LANE] = x_lo.astype(BF16)
            xb_s[:, half + c * LANE:half + (c + 1) * LANE] = x_hi.astype(BF16)
        x = xb_s[...]
        g = jnp.dot(x, wg_s[sl], preferred_element_type=F32)
        u = jnp.dot(x, wu_s[sl], preferred_element_type=F32)
        h = (_silu(g) * u).astype(BF16)
        y = jnp.dot(h, wd_s[sl].reshape(D_EXPERT, dm), preferred_element_type=F32)

        @pl.when(pend[p] > 0)
        def _():
            scatter_wait(p, pend[p])

        y_packed = _pack_bf16_pairs(y[:, :half], y[:, half:])
        for c in range(rl):
            ybuf[p, pl.ds(c, MOE_ROWS, stride=rl), :] = y_packed[:, c * LANE:(c + 1) * LANE]

        for par in range(2):
            for g0 in range(0, groups, GROUP_STEP):
                @pl.when(jnp.logical_and(p == par, g0 < ng))
                def _():
                    for r in range(g0 * SUBLANES, (g0 + GROUP_STEP) * SUBLANES):
                        d = stab_ref[0, 0, r]
                        dst = pl.multiple_of(jnp.where(d < 0, n_asg + par * MOE_ROWS + r, d) * rl, rl)
                        pltpu.make_async_copy(ybuf.at[par, pl.ds(r * rl, rl)], ys_hbm.at[pl.ds(dst, rl)],
                                              ssem.at[par]).start()

        pend[p] = ng

    @pl.when(s == n_steps - 1)
    def _():
        for p in range(2):
            @pl.when(pend[p] > 0)
            def _():
                scatter_wait(p, pend[p])
                pend[p] = 0

        xbuf[...] = jnp.zeros_like(xbuf)
        fills = [pltpu.make_async_copy(xbuf.at[p], ys_hbm.at[pl.ds((n_asg + p * MOE_ROWS) * rl, MOE_ROWS * rl)],
                                       gsem.at[p]) for p in range(2)]
        for cp in fills:
            cp.start()
        for cp in fills:
            cp.wait()


def _experts(h2, gtab, stab, sched, wg, wu, wd):
    n_tok, dm = h2.shape[0] // ROW_LINES, D_MODEL
    nt = D_EXPERT // MOE_FT
    n_asg = 2 * n_tok
    n_steps = sched[0].shape[0]

    smem_rows = lambda imap: pl.BlockSpec((1, 1, MOE_ROWS), imap, memory_space=pltpu.SMEM)
    grid_spec = pltpu.PrefetchScalarGridSpec(
        num_scalar_prefetch=7,
        grid=(n_steps,),
        in_specs=[smem_rows(lambda s, comp, blk, *_: (blk[jnp.minimum(s + 1, n_steps - 1)], 0, 0)),
                  smem_rows(lambda s, comp, blk, *_: (blk[s], 0, 0)),
                  pl.BlockSpec(memory_space=pl.ANY),
                  pl.BlockSpec((1, dm, MOE_FT), lambda s, c, b, sl, ca, cs, ce, ct: (ce[s], 0, ct[s])),
                  pl.BlockSpec((1, dm, MOE_FT), lambda s, c, b, sl, ca, cs, ce, ct: (ce[s], 0, ct[s])),
                  pl.BlockSpec((1, MOE_FT, dm), lambda s, c, b, sl, ca, cs, ce, ct: (ce[s], ct[s], 0))],
        out_specs=pl.BlockSpec(memory_space=pl.ANY),
        scratch_shapes=[pltpu.VMEM((2, dm, D_EXPERT), BF16), pltpu.VMEM((2, dm, D_EXPERT), BF16),
                        pltpu.VMEM((2, nt, MOE_FT, dm), BF16),
                        pltpu.VMEM((2, MOE_ROWS * ROW_LINES, LANE), jnp.uint32),
                        pltpu.VMEM((MOE_ROWS, dm), BF16),
                        pltpu.VMEM((2, MOE_ROWS * ROW_LINES, LANE), jnp.uint32),
                        pltpu.SemaphoreType.DMA((2,)), pltpu.SemaphoreType.DMA((2,)),
                        pltpu.SMEM((2,), jnp.int32)],
    )
    return pl.pallas_call(
        functools.partial(_moe_kernel, nt=nt),
        grid_spec=grid_spec,
        out_shape=jax.ShapeDtypeStruct(((n_asg + 2 * MOE_ROWS) * ROW_LINES, LANE), jnp.uint32),
        compiler_params=_cparams(("arbitrary",)),
        name="experts",
    )(*sched, gtab, stab, h2, wg, wu, wd)


def _final_kernel(x1_ref, y0_ref, y1_ref, g_ref, m_ref, w_ref, b_ref, o_ref, y_s):
    m = m_ref[0]
    bm, dm = x1_ref.shape
    half = dm // 2
    g0 = g_ref[:, 0:1]
    g1 = g_ref[:, 1:2]
    for c in range(ROW_LINES):
        rows_c = pl.ds(c, bm, stride=ROW_LINES)
        lo0, hi0 = _unpack_bf16_pairs(y0_ref[rows_c, :])
        lo1, hi1 = _unpack_bf16_pairs(y1_ref[rows_c, :])
        y_s[:, c * LANE:(c + 1) * LANE] = g0 * lo0 + g1 * lo1
        y_s[:, half + c * LANE:half + (c + 1) * LANE] = g0 * hi0 + g1 * hi1
    z = DEEPNORM_ALPHA * x1_ref[...] + m[5:6, :] * y_s[...]
    o_ref[...] = _layer_norm_rows(z, w_ref[...], b_ref[...])


def _final(x1, ys, gates, mods3, w, b, seq, bm):
    rows, dm = x1.shape
    blocks_per_batch = seq // bm
    slot_blocks = rows // bm
    row_blk = lambda n: pl.BlockSpec((bm, n), lambda i: (i, 0))
    return pl.pallas_call(
        _final_kernel,
        grid=(rows // bm,),
        in_specs=[row_blk(dm), pl.BlockSpec((bm * ROW_LINES, LANE), lambda i: (i, 0)),
                  pl.BlockSpec((bm * ROW_LINES, LANE), lambda i: (i + slot_blocks, 0)),
                  row_blk(LANE),
                  pl.BlockSpec((1, N_MOD, dm), lambda i: (i // blocks_per_batch, 0, 0)),
                  _resident(w.shape), _resident(b.shape)],
        out_specs=row_blk(dm),
        out_shape=jax.ShapeDtypeStruct((rows, dm), F32),
        scratch_shapes=[pltpu.VMEM((bm, dm), F32)],
        compiler_params=_cparams(("parallel",)),
        name="final_ln",
    )(x1, ys, ys, gates, mods3, w, b)


def _route(route, n_tok):
    e_flat = route[:, 2:4].astype(jnp.int32).reshape(-1)
    n_asg = e_flat.shape[0]
    earange = jnp.arange(N_EXPERTS, dtype=jnp.int32)
    counts = jnp.sum((e_flat[:, None] == earange[None, :]).astype(jnp.int32), 0)
    nblk_e = (counts + MOE_ROWS - 1) // MOE_ROWS
    pad_end = jnp.cumsum(nblk_e * MOE_ROWS)
    pad_start = pad_end - nblk_e * MOE_ROWS
    n_blk = n_asg // MOE_ROWS + N_EXPERTS

    order = jnp.argsort(e_flat, stable=True).astype(jnp.int32)
    starts = jnp.cumsum(counts) - counts
    ridx = jnp.arange(n_blk * MOE_ROWS, dtype=jnp.int32)
    e_row = jnp.minimum(jnp.sum((pad_end[None, :] <= ridx[:, None]).astype(jnp.int32), -1), N_EXPERTS - 1)
    rsel = e_row[:, None] == earange[None, :]
    k_row = ridx - jnp.sum(jnp.where(rsel, pad_start[None, :], 0), -1)
    valid = k_row < jnp.sum(jnp.where(rsel, counts[None, :], 0), -1)
    src = jnp.clip(jnp.sum(jnp.where(rsel, starts[None, :], 0), -1) + k_row, 0, n_asg - 1)
    row_asg = jnp.where(valid, order[src], -1)
    gtab = (jnp.maximum(row_asg, 0) >> 1).reshape(n_blk, 1, MOE_ROWS)
    stab = jnp.where(row_asg < 0, -1, (row_asg & 1) * n_tok + (row_asg >> 1)).reshape(n_blk, 1, MOE_ROWS)

    nt = D_EXPERT // MOE_FT
    n_steps = n_blk + (nt - 1) * N_EXPERTS + nt
    has = nblk_e > 0
    n_visits = jnp.sum(has.astype(jnp.int32))
    e_of_visit = jnp.sort(jnp.where(has, earange, N_EXPERTS))
    vsel = e_of_visit[:, None] == earange[None, :]
    nb_v = jnp.sum(jnp.where(vsel, nblk_e[None, :], 0), -1)
    steps_v = jnp.where(nb_v > 0, jnp.maximum(nb_v, nt), 0)
    end_v = nt + jnp.cumsum(steps_v)
    start_v = end_v - steps_v
    first_blk_v = jnp.cumsum(nb_v) - nb_v
    sidx = jnp.arange(n_steps, dtype=jnp.int32)
    v = jnp.sum((end_v[None, :] <= sidx[:, None]).astype(jnp.int32), -1)
    pick = lambda arr, idx: jnp.sum(jnp.where(idx[:, None] == earange[None, :], arr[None, :], 0), -1)
    in_visit = (sidx >= nt) & (v < n_visits)
    k = sidx - pick(start_v, v)
    comp = in_visit & (k < pick(nb_v, v))
    cnt_v = jnp.sum(jnp.where(vsel, counts[None, :], 0), -1)
    rows_s = jnp.clip(pick(cnt_v, v) - k * MOE_ROWS, 0, MOE_ROWS)
    chunk_rows = GROUP_STEP * SUBLANES
    groups_s = jnp.where(comp, (rows_s + chunk_rows - 1) // chunk_rows * GROUP_STEP, 0)
    blk_s = lax.cummax(jnp.where(comp, pick(first_blk_v, v) + k, 0), axis=0)
    prologue = sidx < nt
    cast = prologue | (in_visit & (k < nt) & (v + 1 < n_visits))
    cexp = jnp.where(prologue, e_of_visit[0], pick(e_of_visit, v + 1))
    ctile = jnp.where(prologue, sidx, k)
    code = lax.cummax(jnp.where(cast, cexp * nt + ctile, 0), axis=0)
    i32 = lambda a: a.astype(jnp.int32)
    sched = (i32(groups_s), i32(blk_s), i32(v % 2), i32(cast), i32(jnp.where(prologue, 0, (v + 1) % 2)),
             i32(jnp.minimum(code // nt, N_EXPERTS - 1)), i32(code % nt))
    return sched, gtab, stab


def _grid_pos_embed(rows):
    quarter = D_MODEL // 4
    omega = 1.0 / (10000.0 ** (jnp.arange(quarter, dtype=F32) / quarter))
    ar = jnp.arange(rows, dtype=F32)[:, None] * omega
    ac = jnp.arange(GRID_W, dtype=F32)[:, None] * omega
    shape = (rows, GRID_W, quarter)
    parts = [jnp.broadcast_to(jnp.sin(ar)[:, None, :], shape), jnp.broadcast_to(jnp.cos(ar)[:, None, :], shape),
             jnp.broadcast_to(jnp.sin(ac)[None, :, :], shape), jnp.broadcast_to(jnp.cos(ac)[None, :, :], shape)]
    return jnp.concatenate(parts, -1).reshape(rows * GRID_W, D_MODEL)


def kernel(x, c, ctx, c_ctx, w_mod, b_mod, w_in, conv_w, conv_b, gate_bias, mlstm_norm_w, cmlp_norm_w,
           w_s, b_s, w_out, ln1_w, ln1_b, router1_w, router1_b, router2_w, router2_b, w_gate, w_up,
           w_down, ln2_w, ln2_b):
    bsz, seq, dm = x.shape
    ctx_len = ctx.shape[1]
    n_tok = bsz * seq
    assert w_mod.shape[0] == 1 and dm == D_MODEL and seq % MCHUNK == 0 and ctx_len == MCHUNK
    pe = _grid_pos_embed(seq // GRID_W).astype(x.dtype)
    x2d = x.reshape(n_tok, dm)
    ctx2d = ctx.reshape(bsz * ctx_len, dm)

    mod_rows = 16
    cc = jnp.concatenate([c, c_ctx[None, :], jnp.zeros((mod_rows - bsz - 1, dm), c.dtype)], 0)
    mods3 = _modulation(cc, w_mod[0], b_mod[0]).reshape(mod_rows, N_MOD, dm)

    dq = D_MLSTM
    wi = w_in[0]
    w_qkvo = wi[:, :4 * dq].astype(BF16)
    w_g = jnp.pad(wi[:, 4 * dq:4 * dq + N_GATE_COLS], ((0, 0), (0, LANE - N_GATE_COLS))).astype(BF16)
    w_uv = wi[:, 4 * dq + N_GATE_COLS:].astype(BF16)
    bm_proj = 256
    blocks_per_seq = seq // bm_proj
    cmlp = (cmlp_norm_w[0].reshape(1, -1), w_s[0].astype(BF16), b_s[0].T)
    qkvo, g_x, yc = _projection(x2d, pe, mods3, lambda i: i // blocks_per_seq,
                                [w_qkvo, w_g, w_uv], [BF16, F32, BF16], [False, False, True], bm_proj,
                                cmlp=cmlp)
    w_kv = wi[:, dq:3 * dq].astype(BF16)
    kv_c, g_c = _projection(ctx2d, None, mods3, lambda i: bsz, [w_kv, w_g], [BF16, F32],
                            [False, False], bm_proj)

    rowq, colq = _gate_stats(g_x, g_c, gate_bias[0], bsz, seq, ctx_len)
    ym = _mlstm(qkvo, kv_c, conv_w[0], conv_b[0].reshape(1, -1), rowq, colq,
                mlstm_norm_w[0].reshape(1, -1), bsz, seq, ctx_len)

    wr = jnp.pad(jnp.concatenate([router2_w[0], router1_w[0]], 1),
                 ((0, 0), (0, LANE - N_GROUPS - N_EXPERTS))).astype(BF16)
    br = jnp.pad(jnp.concatenate([router2_b[0], router1_b[0]], 0),
                 (0, LANE - N_GROUPS - N_EXPERTS)).reshape(1, LANE)
    x1, h2, route = _mixer_out(ym, yc, x2d, pe, mods3, w_out[0].astype(BF16),
                               ln1_w[0].reshape(1, -1), ln1_b[0].reshape(1, -1), wr, br, seq, 256)

    sched, gtab, stab = _route(route, n_tok)
    ys = _experts(h2, gtab, stab, sched, w_gate[0], w_up[0], w_down[0])
    out = _final(x1, ys, route, mods3, ln2_w[0].reshape(1, -1),
                 ln2_b[0].reshape(1, -1), seq, 256)
    return out.reshape(bsz, seq, dm)
```

```python
import functools

import jax
import jax.numpy as jnp
from jax import lax
from jax.experimental import pallas as pl
from jax.experimental.pallas import tpu as pltpu

F32 = jnp.float32
BF16 = jnp.bfloat16

D_MODEL = 2048
GRID_W = 64
D_MLSTM = 1024
D_CMLP = 1024
HEADS = 4
HD = 256
CMLP_GROUPS = 4
CMLP_GD = 256
CMLP_CHUNK = 128
N_GROUPS = 4
EXPERTS_PER_GROUP = 8
N_EXPERTS = 32
D_EXPERT = 1024
N_MOD = 6
N_GATE_COLS = 16
DEEPNORM_ALPHA = 2.0 ** 0.25
LN_EPS = 1e-6

LANE = 128
SUBLANES = 8
MCHUNK = 256
MOE_ROWS = 256
MOE_SMALL_ROWS = 64
MOE_FT = 256
GROUP_STEP = 4
ROW_LINES = D_MODEL // 2 // LANE
VMEM_LIMIT = 56 * 1024 * 1024


def _cparams(sem):
    return pltpu.CompilerParams(dimension_semantics=sem, vmem_limit_bytes=VMEM_LIMIT)


def _resident(shape):
    nd = len(shape)
    return pl.BlockSpec(shape, lambda *_: (0,) * nd, pipeline_mode=pl.Buffered(1))


def _sigmoid(x):
    return 0.5 * jnp.tanh(0.5 * x) + 0.5


def _silu(x):
    return x * _sigmoid(x)


def _log_sigmoid(x):
    return jnp.minimum(x, 0.0) - jnp.log1p(jnp.exp(-jnp.abs(x)))


def _gelu_tanh(x):
    c = 0.7978845608028654
    return 0.5 * x * (1.0 + jnp.tanh(c * (x + 0.044715 * (x * x * x))))


def _pack_bf16_pairs(lo, hi):
    lo_b = lax.bitcast_convert_type(lo.astype(BF16).astype(F32), jnp.uint32)
    hi_b = lax.bitcast_convert_type(hi.astype(BF16).astype(F32), jnp.uint32)
    return (lo_b >> 16) | (hi_b & jnp.uint32(0xFFFF0000))


def _unpack_bf16_pairs(w):
    lo = lax.bitcast_convert_type(w << 16, F32)
    hi = lax.bitcast_convert_type(w & jnp.uint32(0xFFFF0000), F32)
    return lo, hi


def _layer_norm_rows(z, w, b):
    mu = jnp.mean(z, axis=-1, keepdims=True)
    zc = z - mu
    var = jnp.mean(zc * zc, axis=-1, keepdims=True)
    return zc * lax.rsqrt(var + LN_EPS) * w + b


def _mod_kernel(c_ref, w_ref, b_ref, o_ref):
    s = _silu(c_ref[...]).astype(BF16)
    o_ref[...] = jnp.dot(s, w_ref[...].astype(BF16), preferred_element_type=F32) + b_ref[...]


def _modulation(cc, w_mod, b_mod):
    rows, dm = cc.shape
    n = w_mod.shape[1]
    tn = 1024
    return pl.pallas_call(
        _mod_kernel,
        grid=(n // tn,),
        in_specs=[pl.BlockSpec((rows, dm), lambda j: (0, 0)),
                  pl.BlockSpec((dm, tn), lambda j: (0, j)),
                  pl.BlockSpec((1, tn), lambda j: (0, j))],
        out_specs=pl.BlockSpec((rows, tn), lambda j: (0, j)),
        out_shape=jax.ShapeDtypeStruct((rows, n), F32),
        compiler_params=_cparams(("arbitrary",)),
        name="modulation",
    )(cc, w_mod, b_mod.reshape(1, n))


def _proj_kernel(*refs, n_w, has_pe, gelu_flags, has_cmlp, tn):
    x_ref = refs[0]
    k = 1
    pe_ref = None
    if has_pe:
        pe_ref = refs[k]
        k += 1
    m_ref = refs[k]
    k += 1
    w_refs = refs[k:k + n_w]
    k += n_w
    if has_cmlp:
        cnw_ref, ws_ref, bs_ref = refs[k:k + 3]
        k += 3
    o_refs = refs[k:k + n_w]
    hx_ref = refs[k + n_w]
    x = x_ref[...]
    if has_pe:
        x = x + pe_ref[...]
    m = m_ref[0]
    hx_ref[...] = (x * (1.0 + m[1:2, :]) + m[0:1, :]).astype(BF16)
    n_plain = n_w - 1 if has_cmlp else n_w
    for w_ref, o_ref, use_gelu in list(zip(w_refs, o_refs, gelu_flags))[:n_plain]:
        n = w_ref.shape[1]
        step = min(tn, n)
        for j in range(0, n, step):
            acc = jnp.dot(hx_ref[...], w_ref[:, j:j + step], preferred_element_type=F32)
            if use_gelu:
                acc = _gelu_tanh(acc)
            o_ref[:, j:j + step] = acc.astype(o_ref.dtype)
    if has_cmlp:
        w_ref, o_ref, s_ref = w_refs[-1], o_refs[-1], refs[k + n_w + 1]
        bm = x_ref.shape[0]
        vgate = _gelu_tanh(jnp.dot(hx_ref[...], w_ref[:, D_CMLP:], preferred_element_type=F32))
        for g in range(CMLP_GROUPS):
            gs = slice(g * CMLP_GD, (g + 1) * CMLP_GD)
            vg = vgate[:, gs]
            mu = jnp.mean(vg, axis=-1, keepdims=True)
            vc = vg - mu
            var = jnp.mean(vc * vc, axis=-1, keepdims=True)
            vn = (vc * lax.rsqrt(var + LN_EPS) * cnw_ref[:, gs]).astype(BF16)
            for p in range(bm // CMLP_CHUNK):
                ps = slice(p * CMLP_CHUNK, (p + 1) * CMLP_CHUNK)
                s_ref[ps, gs] = (jnp.dot(ws_ref[g], vn[ps, :], preferred_element_type=F32)
                                 + bs_ref[:, g:g + 1])
        u = _gelu_tanh(jnp.dot(hx_ref[...], w_ref[:, :D_CMLP], preferred_element_type=F32))
        o_ref[...] = (u * s_ref[...]).astype(o_ref.dtype)


def _projection(x2d, pe, mods3, mod_row_of_block, weights, out_dtypes, gelu_flags, bm, cmlp=None):
    rows, dm = x2d.shape
    has_pe = pe is not None
    n_w = len(weights)
    in_specs = [pl.BlockSpec((bm, dm), lambda i: (i, 0))]
    args = [x2d]
    if has_pe:
        pe_blocks = pe.shape[0] // bm
        in_specs.append(pl.BlockSpec((bm, dm), lambda i: (i % pe_blocks, 0)))
        args.append(pe)
    in_specs.append(pl.BlockSpec((1, N_MOD, dm), lambda i: (mod_row_of_block(i), 0, 0)))
    args.append(mods3)
    for w in weights:
        in_specs.append(_resident(w.shape))
        args.append(w)
    out_widths = [w.shape[1] for w in weights]
    scratch = [pltpu.VMEM((bm, dm), BF16)]
    if cmlp is not None:
        for a in cmlp:
            in_specs.append(_resident(a.shape))
            args.append(a)
        out_widths[-1] = D_CMLP
        scratch.append(pltpu.VMEM((bm, D_CMLP), F32))
    out_specs = [pl.BlockSpec((bm, n), lambda i: (i, 0)) for n in out_widths]
    out_shape = [jax.ShapeDtypeStruct((rows, n), dt) for n, dt in zip(out_widths, out_dtypes)]
    kern = functools.partial(_proj_kernel, n_w=n_w, has_pe=has_pe, gelu_flags=tuple(gelu_flags),
                             has_cmlp=cmlp is not None, tn=1024)
    return pl.pallas_call(
        kern,
        grid=(rows // bm,),
        in_specs=in_specs,
        out_specs=out_specs,
        out_shape=out_shape,
        scratch_shapes=scratch,
        compiler_params=_cparams(("parallel",)),
        name="projection",
    )(*args)


def _gate_kernel(lic_ref, lfc_ref, lir_ref, lfr_ref, row_ref, col_ref, *, nc):
    li = lic_ref[0, 0]
    lf = _log_sigmoid(lfc_ref[0, 0])
    length = li.shape[0]
    tid = lax.broadcasted_iota(jnp.int32, li.shape, 0)
    lane = lax.broadcasted_iota(jnp.int32, li.shape, 1)
    lane1 = lane[0:1, :]
    fwd = (lane < nc) | (lane == 2 * nc)

    def scan_sublanes(x, op, fill):
        p = x
        s = x
        k = 1
        while k < length:
            p = op(p, jnp.where(tid >= k, pltpu.roll(p, k, 0), fill))
            s = op(s, jnp.where(tid < length - k, pltpu.roll(s, length - k, 0), fill))
            k *= 2
        return jnp.where(fwd, p, s)

    b = scan_sublanes(lf, jnp.add, 0.0)
    btot = jnp.sum(lf, axis=0, keepdims=True)
    a = btot - b + li
    m_loc = jnp.max(a, axis=0, keepdims=True)
    r = li - b
    cm = scan_sublanes(r, jnp.maximum, -jnp.inf)

    m_ctx = jnp.maximum(btot, m_loc)
    m_in = jnp.where(lane1 == 0, pltpu.roll(m_ctx, LANE - 2 * nc, 1), pltpu.roll(m_ctx, LANE - 2, 1))
    for k in range(nc - 1):
        m_new = jnp.maximum(btot + m_in, m_loc)
        m_in = jnp.where(lane1 == k + 1, pltpu.roll(m_new, 1, 1),
                         jnp.where(lane1 == 2 * nc - 2 - k, pltpu.roll(m_new, LANE - 1, 1), m_in))
    is_ctx = lane1 >= 2 * nc
    m_in = jnp.where(is_ctx, 0.0, m_in)
    m_new = jnp.maximum(btot + m_in, m_loc)
    s_old = jnp.broadcast_to(jnp.exp(btot + m_in - m_new), li.shape)
    w = jnp.exp(a - m_new)
    big_m = jnp.maximum(m_in, cm)
    s_int = jnp.exp(m_in - big_m)
    e_neg = jnp.exp(-(b + big_m))
    g = 2 * nc
    col_ref[0, 0] = jnp.where(
        lane < g, w, jnp.where(
            lane < 2 * g, pltpu.roll(big_m, g, 1), jnp.where(
                lane < 3 * g, pltpu.roll(s_int, 2 * g, 1), jnp.where(
                    lane < 4 * g, pltpu.roll(e_neg, 3 * g, 1), jnp.where(
                        lane < 4 * g + 2, pltpu.roll(w, 3 * g, 1), pltpu.roll(s_old, 4 * g + 2, 1))))))

    lir = lir_ref[0, 0]
    lfr = _log_sigmoid(lfr_ref[0, 0])
    width = lir.shape[1]
    rid = lax.broadcasted_iota(jnp.int32, lir.shape, 0)
    pid = lax.broadcasted_iota(jnp.int32, lir.shape, 1)
    p = lfr
    s = lfr
    k = 1
    while k < width:
        p = p + jnp.where(pid >= k, pltpu.roll(p, k, 1), 0.0)
        s = s + jnp.where(pid < width - k, pltpu.roll(s, width - k, 1), 0.0)
        k *= 2
    row_ref[0, 0] = lir - jnp.where(rid < nc, p, s)


def _gate_stats(g_x, g_c, gate_bias, bsz, seq, ctx_len):
    nc = seq // MCHUNK
    gb = gate_bias.astype(F32)
    gx = g_x[:, :N_GATE_COLS].reshape(bsz, nc, MCHUNK, 2, 2, HEADS) + gb.reshape(2, 2, HEADS)
    gc = g_c[:, :N_GATE_COLS].reshape(bsz, ctx_len, 2, 2, HEADS) + gb.reshape(2, 2, HEADS)
    col_x = gx.transpose(4, 0, 5, 2, 3, 1).reshape(2, bsz, HEADS, MCHUNK, 2 * nc)
    col_c = gc.transpose(3, 0, 4, 1, 2)
    col = jnp.concatenate([col_x, col_c], -1)
    col = jnp.pad(col, ((0, 0),) * 4 + ((0, LANE - col.shape[-1]),))
    row = gx.transpose(4, 0, 5, 3, 1, 2).reshape(2, bsz, HEADS, 2 * nc, MCHUNK)
    blk_c = pl.BlockSpec((1, 1, MCHUNK, LANE), lambda b, h: (b, h, 0, 0))
    blk_r = pl.BlockSpec((1, 1, 2 * nc, MCHUNK), lambda b, h: (b, h, 0, 0))
    return pl.pallas_call(
        functools.partial(_gate_kernel, nc=nc),
        grid=(bsz, HEADS),
        in_specs=[blk_c, blk_c, blk_r, blk_r],
        out_specs=[blk_r, blk_c],
        out_shape=[jax.ShapeDtypeStruct((bsz, HEADS, 2 * nc, MCHUNK), F32),
                   jax.ShapeDtypeStruct((bsz, HEADS, MCHUNK, LANE), F32)],
        compiler_params=_cparams(("parallel", "parallel")),
        name="gate_stats",
    )(col[0], col[1], row[0], row[1])


def _mlstm_kernel(q_ref, k_ref, v_ref, o_ref, kc_ref, vc_ref, cwq_ref, cbq_ref, cwk_ref, cbk_ref,
                  row_ref, col_ref, nw_ref, band_ref, edge_ref, y_ref, q_s, k_s, kc_s, ct_s, n_s, *, nc):
    lc = MCHUNK
    halo_rows = edge_ref.shape[1]

    def conv_silu_chunk(x_ref, c, n_chunks, w, b, scale):
        x = x_ref[pl.ds(c * lc, lc), :]
        wb = w.astype(BF16)
        taps = jnp.concatenate([x * wb[0:1, :], x * wb[1:2, :], x * wb[2:3, :]], axis=0)
        y = jnp.dot(band_ref[...], taps, preferred_element_type=F32)
        if n_chunks > 1:
            rid = lax.broadcasted_iota(jnp.int32, (halo_rows, x.shape[1]), 0)
            wf = wb.astype(F32)
            halo = jnp.zeros((halo_rows, x.shape[1]), F32)
            if c > 0:
                prev = x_ref[pl.ds(c * lc - halo_rows, halo_rows), :].astype(F32)[halo_rows - 1:, :]
                halo = jnp.where(rid == 0, prev * wf[0:1, :], halo)
            if c < n_chunks - 1:
                nxt = x_ref[pl.ds((c + 1) * lc, halo_rows), :].astype(F32)[0:1, :]
                halo = jnp.where(rid == 1, nxt * wf[2:3, :], halo)
            y = y + jnp.dot(edge_ref[...], halo.astype(BF16), preferred_element_type=F32)
        y = _silu(y + b)
        if scale != 1.0:
            y = y * scale
        return y.astype(BF16)

    k_scale = HD ** -0.5
    for c in range(nc):
        sl = pl.ds(c * lc, lc)
        q_s[sl, :] = conv_silu_chunk(q_ref, c, nc, cwq_ref[...], cbq_ref[...], 1.0)
        k_s[sl, :] = conv_silu_chunk(k_ref, c, nc, cwk_ref[...], cbk_ref[...], k_scale)
    kc_s[...] = conv_silu_chunk(kc_ref, 0, 1, cwk_ref[...], cbk_ref[...], k_scale)

    def col(j):
        return col_ref[0, 0, :, j:j + 1]

    def local_state(kk, vv, wcol):
        vw = (vv.astype(F32) * wcol).astype(BF16)
        ct = lax.dot_general(kk, vw, (((0,), (0,)), ((), ())), preferred_element_type=F32)
        nn = jnp.sum(kk.astype(F32) * wcol, axis=0, keepdims=True)
        return ct, nn

    for d in range(2):
        ct, nn = local_state(kc_s[...], vc_ref[...], col(8 * nc + d))
        order = list(range(nc)) if d == 0 else list(range(nc - 1, -1, -1))
        for pos, c in enumerate(order):
            idx = d * nc + c
            ct_s[idx] = ct.astype(BF16)
            n_s[idx] = nn
            if pos == nc - 1:
                break
            sl = pl.ds(c * lc, lc)
            ctl, nl = local_state(k_s[sl, :], v_ref[sl, :], col(idx))
            s_old = col_ref[0, 0, 0:1, 8 * nc + 2 + idx:8 * nc + 3 + idx]
            ct = s_old * ct + ctl
            nn = s_old * nn + nl

    tid = lax.broadcasted_iota(jnp.int32, (lc, lc), 0)
    sid = lax.broadcasted_iota(jnp.int32, (lc, lc), 1)
    masks = (sid <= tid, sid >= tid)
    for c in range(nc):
        sl = pl.ds(c * lc, lc)
        q = q_s[sl, :]
        kk = k_s[sl, :]
        v = v_ref[sl, :]
        qf = q.astype(F32)
        s = lax.dot_general(q, kk, (((1,), (1,)), ((), ())), preferred_element_type=F32)
        h = None
        for d in range(2):
            idx = d * nc + c
            r = row_ref[0, 0, idx:idx + 1, :]
            big_m = col(2 * nc + idx)
            s_int = col(4 * nc + idx)
            e_neg = col(6 * nc + idx)
            p = jnp.where(masks[d], jnp.exp(r - big_m), 0.0) * s
            den = (jnp.sum(p, axis=-1, keepdims=True)
                   + s_int * jnp.sum(qf * n_s[idx], axis=-1, keepdims=True))
            num = (jnp.dot(p.astype(BF16), v, preferred_element_type=F32)
                   + s_int * jnp.dot(q, ct_s[idx], preferred_element_type=F32))
            hd = num * (1.0 / jnp.maximum(jnp.abs(den), e_neg))
            h = hd if h is None else h + hd
        mu = jnp.mean(h, axis=-1, keepdims=True)
        hc = h - mu
        var = jnp.mean(hc * hc, axis=-1, keepdims=True)
        hn = hc * lax.rsqrt(var + LN_EPS) * nw_ref[...]
        y_ref[sl, :] = (hn * _sigmoid(o_ref[sl, :].astype(F32))).astype(BF16)


def _mlstm(qkvo, kv_ctx, conv_w, conv_b, rowq, colq, norm_w, bsz, seq, ctx_len):
    nc = seq // MCHUNK
    hq = D_MLSTM // HD
    kern = functools.partial(_mlstm_kernel, nc=nc)
    ii = jnp.arange(MCHUNK)[:, None]
    jj = jnp.arange(MCHUNK)[None, :]
    band = jnp.concatenate([(jj == ii + t - 1) for t in range(3)], axis=1).astype(BF16)
    hh = jnp.arange(2 * SUBLANES)[None, :]
    edge = (((ii == 0) & (hh == 0)) | ((ii == MCHUNK - 1) & (hh == 1))).astype(BF16)
    seq_blk = lambda off: pl.BlockSpec((seq, HD), lambda b, h: (b, off + h))
    ctx_blk = lambda off: pl.BlockSpec((ctx_len, HD), lambda b, h: (b, off + h))
    return pl.pallas_call(
        kern,
        grid=(bsz, HEADS),
        in_specs=[seq_blk(0), seq_blk(hq), seq_blk(2 * hq), seq_blk(3 * hq),
                  ctx_blk(0), ctx_blk(hq),
                  pl.BlockSpec((3, HD), lambda b, h: (0, h)),
                  pl.BlockSpec((1, HD), lambda b, h: (0, h)),
                  pl.BlockSpec((3, HD), lambda b, h: (0, hq + h)),
                  pl.BlockSpec((1, HD), lambda b, h: (0, hq + h)),
                  pl.BlockSpec((1, 1, 2 * nc, MCHUNK), lambda b, h: (b, h, 0, 0)),
                  pl.BlockSpec((1, 1, MCHUNK, LANE), lambda b, h: (b, h, 0, 0)),
                  pl.BlockSpec((1, HD), lambda b, h: (0, h)),
                  _resident(band.shape), _resident(edge.shape)],
        out_specs=pl.BlockSpec((seq, HD), lambda b, h: (b, h)),
        out_shape=jax.ShapeDtypeStruct((bsz * seq, D_MLSTM), BF16),
        scratch_shapes=[pltpu.VMEM((seq, HD), BF16), pltpu.VMEM((seq, HD), BF16),
                        pltpu.VMEM((ctx_len, HD), BF16),
                        pltpu.VMEM((2 * nc, HD, HD), BF16), pltpu.VMEM((2 * nc, 1, HD), F32)],
        compiler_params=_cparams(("parallel", "parallel")),
        name="mlstm",
    )(qkvo, qkvo, qkvo, qkvo, kv_ctx, kv_ctx, conv_w, conv_b, conv_w, conv_b, rowq, colq, norm_w,
      band, edge)


def _out_kernel(ym_ref, yc_ref, x_ref, pe_ref, m_ref, wout_ref,
                l1w_ref, l1b_ref, wr_ref, br_ref, x1_ref, h2_ref, lg_ref):
    m = m_ref[0]
    y = (jnp.dot(ym_ref[...], wout_ref[:D_MLSTM, :], preferred_element_type=F32)
         + jnp.dot(yc_ref[...], wout_ref[D_MLSTM:, :], preferred_element_type=F32))
    z = DEEPNORM_ALPHA * (x_ref[...] + pe_ref[...]) + m[2:3, :] * y
    x1 = _layer_norm_rows(z, l1w_ref[...], l1b_ref[...])
    x1_ref[...] = x1
    h2 = x1 * (1.0 + m[4:5, :]) + m[3:4, :]
    half = h2.shape[1] // 2
    packed = _pack_bf16_pairs(h2[:, :half], h2[:, half:])
    for c in range(ROW_LINES):
        h2_ref[pl.ds(c, h2.shape[0], stride=ROW_LINES), :] = packed[:, c * LANE:(c + 1) * LANE]
    lg = jnp.dot(h2.astype(BF16), wr_ref[...], preferred_element_type=F32) + br_ref[...]

    lgt = lg.T
    bm = lg.shape[0]
    epg = EXPERTS_PER_GROUP
    rid = lax.broadcasted_iota(jnp.int32, (SUBLANES, bm), 0).astype(F32)
    neg = -jnp.inf
    far = float(SUBLANES)
    gl = lgt[N_EXPERTS:N_EXPERTS + SUBLANES, :]
    is_grp = rid < N_GROUPS
    m1 = jnp.max(jnp.where(is_grp, gl, neg), axis=0, keepdims=True)
    grp = jnp.min(jnp.where(is_grp & (gl == m1), rid, far), axis=0, keepdims=True)
    p_grp = 1.0 / jnp.sum(jnp.where(is_grp, jnp.exp(gl - m1), 0.0), axis=0, keepdims=True)
    l2 = lgt[0:epg, :]
    for g in range(1, N_GROUPS):
        l2 = jnp.where(grp == g, lgt[g * epg:(g + 1) * epg, :], l2)
    v0 = jnp.max(l2, axis=0, keepdims=True)
    i0 = jnp.min(jnp.where(l2 == v0, rid, far), axis=0, keepdims=True)
    l2m = jnp.where(rid == i0, neg, l2)
    v1 = jnp.max(l2m, axis=0, keepdims=True)
    i1 = jnp.min(jnp.where(l2m == v1, rid, far), axis=0, keepdims=True)
    s1 = jnp.exp(v1 - v0)
    g0 = p_grp / (1.0 + s1)
    res = jnp.where(rid == 0, g0, jnp.where(rid == 1, g0 * s1, jnp.where(
        rid == 2, grp * epg + i0, jnp.where(rid == 3, grp * epg + i1, 0.0))))
    lg_ref[...] = jnp.concatenate([res, jnp.zeros((LANE - SUBLANES, bm), F32)], axis=0).T


def _mixer_out(ym, yc, x2d, pe, mods3, wout, l1w, l1b, wr, br, seq, bm):
    rows, dm = x2d.shape
    pe_blocks = seq // bm
    row_blk = lambda n: pl.BlockSpec((bm, n), lambda i: (i, 0))
    return pl.pallas_call(
        _out_kernel,
        grid=(rows // bm,),
        in_specs=[row_blk(D_MLSTM), row_blk(D_CMLP), row_blk(dm),
                  pl.BlockSpec((bm, dm), lambda i: (i % pe_blocks, 0)),
                  pl.BlockSpec((1, N_MOD, dm), lambda i: (i // pe_blocks, 0, 0)),
                  _resident(wout.shape), _resident(l1w.shape), _resident(l1b.shape),
                  _resident(wr.shape), _resident(br.shape)],
        out_specs=[row_blk(dm), pl.BlockSpec((bm * ROW_LINES, LANE), lambda i: (i, 0)), row_blk(LANE)],
        out_shape=[jax.ShapeDtypeStruct((rows, dm), F32),
                   jax.ShapeDtypeStruct((rows * ROW_LINES, LANE), jnp.uint32),
                   jax.ShapeDtypeStruct((rows, LANE), F32)],
        compiler_params=_cparams(("parallel",)),
        name="mixer_out",
    )(ym, yc, x2d, pe, mods3, wout, l1w, l1b, wr, br)


def _moe_kernel(comp_ref, blk_ref, slot_ref, cast_ref, cslot_ref, cexp_ref, ctile_ref,
                gtab_ref, stab_ref, h2_hbm, wgf_ref, wuf_ref, wdf_ref, ys_hbm,
                wg_s, wu_s, wd_s, xbuf, xb_s, ybuf, gsem, ssem, pend, *, nt):
    del blk_ref, cexp_ref
    s = pl.program_id(0)
    n_steps = pl.num_programs(0)
    groups = MOE_ROWS // SUBLANES
    rl = ROW_LINES
    half = rl * LANE
    dm = 2 * half
    n_asg = ys_hbm.shape[0] // rl - 2 * MOE_ROWS
    group_lines = SUBLANES * rl

    def gather_wait(p, n):
        nl = n * group_lines
        pltpu.make_async_copy(h2_hbm.at[pl.ds(0, nl)], xbuf.at[p, pl.ds(0, nl)], gsem.at[p]).wait()

    def scatter_wait(p, n):
        nl = n * group_lines
        pltpu.make_async_copy(ybuf.at[p, pl.ds(0, nl)], ys_hbm.at[pl.ds(0, nl)], ssem.at[p]).wait()

    @pl.when(s == 0)
    def _():
        pend[0] = 0
        pend[1] = 0
        xbuf[...] = jnp.zeros_like(xbuf)

    nxt = jnp.minimum(s + 1, n_steps - 1)

    @pl.when(jnp.logical_and(s + 1 < n_steps, comp_ref[nxt] > 0))
    def _():
        p = (s + 1) % 2

        for par in range(2):
            for g0 in range(0, groups, GROUP_STEP):
                @pl.when(jnp.logical_and(p == par, g0 < comp_ref[nxt]))
                def _():
                    for r in range(g0 * SUBLANES, (g0 + GROUP_STEP) * SUBLANES):
                        src = pl.multiple_of(gtab_ref[0, 0, r] * rl, rl)
                        pltpu.make_async_copy(h2_hbm.at[pl.ds(src, rl)], xbuf.at[par, pl.ds(r * rl, rl)],
                                              gsem.at[par]).start()

    @pl.when(cast_ref[s] == 1)
    def _():
        cs = cslot_ref[s]
        t = ctile_ref[s]
        for k in range(nt):
            @pl.when(t == k)
            def _():
                wg_s[cs, :, k * MOE_FT:(k + 1) * MOE_FT] = wgf_ref[0].astype(BF16)
                wu_s[cs, :, k * MOE_FT:(k + 1) * MOE_FT] = wuf_ref[0].astype(BF16)
        wd_s[cs, t] = wdf_ref[0].astype(BF16)

    @pl.when(comp_ref[s] > 0)
    def _():
        p = s % 2
        sl = slot_ref[s]
        ng = comp_ref[s]
        gather_wait(p, ng)

        @pl.when(pend[p] > 0)
        def _():
            scatter_wait(p, pend[p])

        def run_block(rows):
            for c in range(rl):
                x_lo, x_hi = _unpack_bf16_pairs(xbuf[p, pl.ds(c, rows, stride=rl), :])
                xb_s[:rows, c * LANE:(c + 1) * LANE] = x_lo.astype(BF16)
                xb_s[:rows, half + c * LANE:half + (c + 1) * LANE] = x_hi.astype(BF16)
            x = xb_s[:rows, :]
            g = jnp.dot(x, wg_s[sl], preferred_element_type=F32)
            u = jnp.dot(x, wu_s[sl], preferred_element_type=F32)
            h = (_silu(g) * u).astype(BF16)
            y = jnp.dot(h, wd_s[sl].reshape(D_EXPERT, dm), preferred_element_type=F32)
            y_packed = _pack_bf16_pairs(y[:, :half], y[:, half:])
            for c in range(rl):
                ybuf[p, pl.ds(c, rows, stride=rl), :] = y_packed[:, c * LANE:(c + 1) * LANE]

        small_groups = MOE_SMALL_ROWS // SUBLANES

        @pl.when(ng > small_groups)
        def _():
            run_block(MOE_ROWS)

        @pl.when(ng <= small_groups)
        def _():
            run_block(MOE_SMALL_ROWS)

        for par in range(2):
            for g0 in range(0, groups, GROUP_STEP):
                @pl.when(jnp.logical_and(p == par, g0 < ng))
                def _():
                    for r in range(g0 * SUBLANES, (g0 + GROUP_STEP) * SUBLANES):
                        d = stab_ref[0, 0, r]
                        dst = pl.multiple_of(jnp.where(d < 0, n_asg + par * MOE_ROWS + r, d) * rl, rl)
                        pltpu.make_async_copy(ybuf.at[par, pl.ds(r * rl, rl)], ys_hbm.at[pl.ds(dst, rl)],
                                              ssem.at[par]).start()

        pend[p] = ng

    @pl.when(s == n_steps - 1)
    def _():
        for p in range(2):
            @pl.when(pend[p] > 0)
            def _():
                scatter_wait(p, pend[p])
                pend[p] = 0

        xbuf[...] = jnp.zeros_like(xbuf)
        fills = [pltpu.make_async_copy(xbuf.at[p], ys_hbm.at[pl.ds((n_asg + p * MOE_ROWS) * rl, MOE_ROWS * rl)],
                                       gsem.at[p]) for p in range(2)]
        for cp in fills:
            cp.start()
        for cp in fills:
            cp.wait()


def _experts(h2, gtab, stab, sched, wg, wu, wd):
    n_tok, dm = h2.shape[0] // ROW_LINES, D_MODEL
    nt = D_EXPERT // MOE_FT
    n_asg = 2 * n_tok
    n_steps = sched[0].shape[0]

    smem_rows = lambda imap: pl.BlockSpec((1, 1, MOE_ROWS), imap, memory_space=pltpu.SMEM)
    grid_spec = pltpu.PrefetchScalarGridSpec(
        num_scalar_prefetch=7,
        grid=(n_steps,),
        in_specs=[smem_rows(lambda s, comp, blk, *_: (blk[jnp.minimum(s + 1, n_steps - 1)], 0, 0)),
                  smem_rows(lambda s, comp, blk, *_: (blk[s], 0, 0)),
                  pl.BlockSpec(memory_space=pl.ANY),
                  pl.BlockSpec((1, dm, MOE_FT), lambda s, c, b, sl, ca, cs, ce, ct: (ce[s], 0, ct[s])),
                  pl.BlockSpec((1, dm, MOE_FT), lambda s, c, b, sl, ca, cs, ce, ct: (ce[s], 0, ct[s])),
                  pl.BlockSpec((1, MOE_FT, dm), lambda s, c, b, sl, ca, cs, ce, ct: (ce[s], ct[s], 0))],
        out_specs=pl.BlockSpec(memory_space=pl.ANY),
        scratch_shapes=[pltpu.VMEM((2, dm, D_EXPERT), BF16), pltpu.VMEM((2, dm, D_EXPERT), BF16),
                        pltpu.VMEM((2, nt, MOE_FT, dm), BF16),
                        pltpu.VMEM((2, MOE_ROWS * ROW_LINES, LANE), jnp.uint32),
                        pltpu.VMEM((MOE_ROWS, dm), BF16),
                        pltpu.VMEM((2, MOE_ROWS * ROW_LINES, LANE), jnp.uint32),
                        pltpu.SemaphoreType.DMA((2,)), pltpu.SemaphoreType.DMA((2,)),
                        pltpu.SMEM((2,), jnp.int32)],
    )
    return pl.pallas_call(
        functools.partial(_moe_kernel, nt=nt),
        grid_spec=grid_spec,
        out_shape=jax.ShapeDtypeStruct(((n_asg + 2 * MOE_ROWS) * ROW_LINES, LANE), jnp.uint32),
        compiler_params=_cparams(("arbitrary",)),
        name="experts",
    )(*sched, gtab, stab, h2, wg, wu, wd)


def _final_kernel(x1_ref, y0_ref, y1_ref, g_ref, m_ref, w_ref, b_ref, o_ref, y_s):
    m = m_ref[0]
    bm, dm = x1_ref.shape
    half = dm // 2
    g0 = g_ref[:, 0:1]
    g1 = g_ref[:, 1:2]
    for c in range(ROW_LINES):
        rows_c = pl.ds(c, bm, stride=ROW_LINES)
        lo0, hi0 = _unpack_bf16_pairs(y0_ref[rows_c, :])
        lo1, hi1 = _unpack_bf16_pairs(y1_ref[rows_c, :])
        y_s[:, c * LANE:(c + 1) * LANE] = g0 * lo0 + g1 * lo1
        y_s[:, half + c * LANE:half + (c + 1) * LANE] = g0 * hi0 + g1 * hi1
    z = DEEPNORM_ALPHA * x1_ref[...] + m[5:6, :] * y_s[...]
    o_ref[...] = _layer_norm_rows(z, w_ref[...], b_ref[...])


def _final(x1, ys, gates, mods3, w, b, seq, bm):
    rows, dm = x1.shape
    blocks_per_batch = seq // bm
    slot_blocks = rows // bm
    row_blk = lambda n: pl.BlockSpec((bm, n), lambda i: (i, 0))
    return pl.pallas_call(
        _final_kernel,
        grid=(rows // bm,),
        in_specs=[row_blk(dm), pl.BlockSpec((bm * ROW_LINES, LANE), lambda i: (i, 0)),
                  pl.BlockSpec((bm * ROW_LINES, LANE), lambda i: (i + slot_blocks, 0)),
                  row_blk(LANE),
                  pl.BlockSpec((1, N_MOD, dm), lambda i: (i // blocks_per_batch, 0, 0)),
                  _resident(w.shape), _resident(b.shape)],
        out_specs=row_blk(dm),
        out_shape=jax.ShapeDtypeStruct((rows, dm), F32),
        scratch_shapes=[pltpu.VMEM((bm, dm), F32)],
        compiler_params=_cparams(("parallel",)),
        name="final_ln",
    )(x1, ys, ys, gates, mods3, w, b)


def _route(route, n_tok):
    e_flat = route[:, 2:4].astype(jnp.int32).reshape(-1)
    n_asg = e_flat.shape[0]
    earange = jnp.arange(N_EXPERTS, dtype=jnp.int32)
    counts = jnp.sum((e_flat[:, None] == earange[None, :]).astype(jnp.int32), 0)
    nblk_e = (counts + MOE_ROWS - 1) // MOE_ROWS
    pad_end = jnp.cumsum(nblk_e * MOE_ROWS)
    pad_start = pad_end - nblk_e * MOE_ROWS
    n_blk = n_asg // MOE_ROWS + N_EXPERTS

    order = jnp.argsort(e_flat, stable=True).astype(jnp.int32)
    starts = jnp.cumsum(counts) - counts
    ridx = jnp.arange(n_blk * MOE_ROWS, dtype=jnp.int32)
    e_row = jnp.minimum(jnp.sum((pad_end[None, :] <= ridx[:, None]).astype(jnp.int32), -1), N_EXPERTS - 1)
    rsel = e_row[:, None] == earange[None, :]
    k_row = ridx - jnp.sum(jnp.where(rsel, pad_start[None, :], 0), -1)
    valid = k_row < jnp.sum(jnp.where(rsel, counts[None, :], 0), -1)
    src = jnp.clip(jnp.sum(jnp.where(rsel, starts[None, :], 0), -1) + k_row, 0, n_asg - 1)
    row_asg = jnp.where(valid, order[src], -1)
    gtab = (jnp.maximum(row_asg, 0) >> 1).reshape(n_blk, 1, MOE_ROWS)
    stab = jnp.where(row_asg < 0, -1, (row_asg & 1) * n_tok + (row_asg >> 1)).reshape(n_blk, 1, MOE_ROWS)

    nt = D_EXPERT // MOE_FT
    n_steps = n_blk + (nt - 1) * N_EXPERTS + nt
    has = nblk_e > 0
    n_visits = jnp.sum(has.astype(jnp.int32))
    e_of_visit = jnp.sort(jnp.where(has, earange, N_EXPERTS))
    vsel = e_of_visit[:, None] == earange[None, :]
    nb_v = jnp.sum(jnp.where(vsel, nblk_e[None, :], 0), -1)
    steps_v = jnp.where(nb_v > 0, jnp.maximum(nb_v, nt), 0)
    end_v = nt + jnp.cumsum(steps_v)
    start_v = end_v - steps_v
    first_blk_v = jnp.cumsum(nb_v) - nb_v
    sidx = jnp.arange(n_steps, dtype=jnp.int32)
    v = jnp.sum((end_v[None, :] <= sidx[:, None]).astype(jnp.int32), -1)
    pick = lambda arr, idx: jnp.sum(jnp.where(idx[:, None] == earange[None, :], arr[None, :], 0), -1)
    in_visit = (sidx >= nt) & (v < n_visits)
    k = sidx - pick(start_v, v)
    comp = in_visit & (k < pick(nb_v, v))
    cnt_v = jnp.sum(jnp.where(vsel, counts[None, :], 0), -1)
    rows_s = jnp.clip(pick(cnt_v, v) - k * MOE_ROWS, 0, MOE_ROWS)
    chunk_rows = GROUP_STEP * SUBLANES
    groups_s = jnp.where(comp, (rows_s + chunk_rows - 1) // chunk_rows * GROUP_STEP, 0)
    blk_s = lax.cummax(jnp.where(comp, pick(first_blk_v, v) + k, 0), axis=0)
    prologue = sidx < nt
    cast = prologue | (in_visit & (k < nt) & (v + 1 < n_visits))
    cexp = jnp.where(prologue, e_of_visit[0], pick(e_of_visit, v + 1))
    ctile = jnp.where(prologue, sidx, k)
    code = lax.cummax(jnp.where(cast, cexp * nt + ctile, 0), axis=0)
    i32 = lambda a: a.astype(jnp.int32)
    sched = (i32(groups_s), i32(blk_s), i32(v % 2), i32(cast), i32(jnp.where(prologue, 0, (v + 1) % 2)),
             i32(jnp.minimum(code // nt, N_EXPERTS - 1)), i32(code % nt))
    return sched, gtab, stab


def _grid_pos_embed(rows):
    quarter = D_MODEL // 4
    omega = 1.0 / (10000.0 ** (jnp.arange(quarter, dtype=F32) / quarter))
    ar = jnp.arange(rows, dtype=F32)[:, None] * omega
    ac = jnp.arange(GRID_W, dtype=F32)[:, None] * omega
    shape = (rows, GRID_W, quarter)
    parts = [jnp.broadcast_to(jnp.sin(ar)[:, None, :], shape), jnp.broadcast_to(jnp.cos(ar)[:, None, :], shape),
             jnp.broadcast_to(jnp.sin(ac)[None, :, :], shape), jnp.broadcast_to(jnp.cos(ac)[None, :, :], shape)]
    return jnp.concatenate(parts, -1).reshape(rows * GRID_W, D_MODEL)


def kernel(x, c, ctx, c_ctx, w_mod, b_mod, w_in, conv_w, conv_b, gate_bias, mlstm_norm_w, cmlp_norm_w,
           w_s, b_s, w_out, ln1_w, ln1_b, router1_w, router1_b, router2_w, router2_b, w_gate, w_up,
           w_down, ln2_w, ln2_b):
    bsz, seq, dm = x.shape
    ctx_len = ctx.shape[1]
    n_tok = bsz * seq
    assert w_mod.shape[0] == 1 and dm == D_MODEL and seq % MCHUNK == 0 and ctx_len == MCHUNK
    pe = _grid_pos_embed(seq // GRID_W).astype(x.dtype)
    x2d = x.reshape(n_tok, dm)
    ctx2d = ctx.reshape(bsz * ctx_len, dm)

    mod_rows = 16
    cc = jnp.concatenate([c, c_ctx[None, :], jnp.zeros((mod_rows - bsz - 1, dm), c.dtype)], 0)
    mods3 = _modulation(cc, w_mod[0], b_mod[0]).reshape(mod_rows, N_MOD, dm)

    dq = D_MLSTM
    wi = w_in[0]
    w_qkvo = wi[:, :4 * dq].astype(BF16)
    w_g = jnp.pad(wi[:, 4 * dq:4 * dq + N_GATE_COLS], ((0, 0), (0, LANE - N_GATE_COLS))).astype(BF16)
    w_uv = wi[:, 4 * dq + N_GATE_COLS:].astype(BF16)
    bm_proj = 256
    blocks_per_seq = seq // bm_proj
    cmlp = (cmlp_norm_w[0].reshape(1, -1), w_s[0].astype(BF16), b_s[0].T)
    qkvo, g_x, yc = _projection(x2d, pe, mods3, lambda i: i // blocks_per_seq,
                                [w_qkvo, w_g, w_uv], [BF16, F32, BF16], [False, False, True], bm_proj,
                                cmlp=cmlp)
    w_kv = wi[:, dq:3 * dq].astype(BF16)
    kv_c, g_c = _projection(ctx2d, None, mods3, lambda i: bsz, [w_kv, w_g], [BF16, F32],
                            [False, False], bm_proj)

    rowq, colq = _gate_stats(g_x, g_c, gate_bias[0], bsz, seq, ctx_len)
    ym = _mlstm(qkvo, kv_c, conv_w[0], conv_b[0].reshape(1, -1), rowq, colq,
                mlstm_norm_w[0].reshape(1, -1), bsz, seq, ctx_len)

    wr = jnp.pad(jnp.concatenate([router2_w[0], router1_w[0]], 1),
                 ((0, 0), (0, LANE - N_GROUPS - N_EXPERTS))).astype(BF16)
    br = jnp.pad(jnp.concatenate([router2_b[0], router1_b[0]], 0),
                 (0, LANE - N_GROUPS - N_EXPERTS)).reshape(1, LANE)
    x1, h2, route = _mixer_out(ym, yc, x2d, pe, mods3, w_out[0].astype(BF16),
                               ln1_w[0].reshape(1, -1), ln1_b[0].reshape(1, -1), wr, br, seq, 256)

    sched, gtab, stab = _route(route, n_tok)
    ys = _experts(h2, gtab, stab, sched, w_gate[0], w_up[0], w_down[0])
    out = _final(x1, ys, route, mods3, ln2_w[0].reshape(1, -1),
                 ln2_b[0].reshape(1, -1), seq, 512)
    return out.reshape(bsz, seq, dm)
```

```python
import functools

import jax
import jax.numpy as jnp
from jax import lax
from jax.experimental import pallas as pl
from jax.experimental.pallas import tpu as pltpu

F32 = jnp.float32
BF16 = jnp.bfloat16

D_MODEL = 2048
GRID_W = 64
D_MLSTM = 1024
D_CMLP = 1024
HEADS = 4
HD = 256
CMLP_GROUPS = 4
CMLP_GD = 256
CMLP_CHUNK = 128
N_GROUPS = 4
EXPERTS_PER_GROUP = 8
N_EXPERTS = 32
D_EXPERT = 1024
N_MOD = 6
N_GATE_COLS = 16
DEEPNORM_ALPHA = 2.0 ** 0.25
LN_EPS = 1e-6

LANE = 128
SUBLANES = 8
MCHUNK = 256
MOE_ROWS = 256
MOE_SMALL_ROWS = 64
MOE_FT = 256
GROUP_STEP = 4
ROW_LINES = D_MODEL // 2 // LANE
VMEM_LIMIT = 56 * 1024 * 1024


def _cparams(sem):
    return pltpu.CompilerParams(dimension_semantics=sem, vmem_limit_bytes=VMEM_LIMIT)


def _resident(shape):
    nd = len(shape)
    return pl.BlockSpec(shape, lambda *_: (0,) * nd, pipeline_mode=pl.Buffered(1))


def _sigmoid(x):
    return 0.5 * jnp.tanh(0.5 * x) + 0.5


def _silu(x):
    return x * _sigmoid(x)


def _log_sigmoid(x):
    return jnp.minimum(x, 0.0) - jnp.log1p(jnp.exp(-jnp.abs(x)))


def _gelu_tanh(x):
    c = 0.7978845608028654
    return 0.5 * x * (1.0 + jnp.tanh(c * (x + 0.044715 * (x * x * x))))


def _pack_bf16_pairs(lo, hi):
    lo_b = lax.bitcast_convert_type(lo.astype(BF16).astype(F32), jnp.uint32)
    hi_b = lax.bitcast_convert_type(hi.astype(BF16).astype(F32), jnp.uint32)
    return (lo_b >> 16) | (hi_b & jnp.uint32(0xFFFF0000))


def _unpack_bf16_pairs(w):
    lo = lax.bitcast_convert_type(w << 16, F32)
    hi = lax.bitcast_convert_type(w & jnp.uint32(0xFFFF0000), F32)
    return lo, hi


def _layer_norm_rows(z, w, b):
    mu = jnp.mean(z, axis=-1, keepdims=True)
    zc = z - mu
    var = jnp.mean(zc * zc, axis=-1, keepdims=True)
    return zc * lax.rsqrt(var + LN_EPS) * w + b


def _mod_kernel(c_ref, w_ref, b_ref, o_ref):
    s = _silu(c_ref[...]).astype(BF16)
    o_ref[...] = jnp.dot(s, w_ref[...].astype(BF16), preferred_element_type=F32) + b_ref[...]


def _modulation(cc, w_mod, b_mod):
    rows, dm = cc.shape
    n = w_mod.shape[1]
    tn = 1024
    return pl.pallas_call(
        _mod_kernel,
        grid=(n // tn,),
        in_specs=[pl.BlockSpec((rows, dm), lambda j: (0, 0)),
                  pl.BlockSpec((dm, tn), lambda j: (0, j)),
                  pl.BlockSpec((1, tn), lambda j: (0, j))],
        out_specs=pl.BlockSpec((rows, tn), lambda j: (0, j)),
        out_shape=jax.ShapeDtypeStruct((rows, n), F32),
        compiler_params=_cparams(("arbitrary",)),
        name="modulation",
    )(cc, w_mod, b_mod.reshape(1, n))


def _proj_kernel(*refs, n_w, has_pe, gelu_flags, has_cmlp, tn):
    x_ref = refs[0]
    k = 1
    pe_ref = None
    if has_pe:
        pe_ref = refs[k]
        k += 1
    m_ref = refs[k]
    k += 1
    w_refs = refs[k:k + n_w]
    k += n_w
    if has_cmlp:
        cnw_ref, ws_ref, bs_ref = refs[k:k + 3]
        k += 3
    o_refs = refs[k:k + n_w]
    hx_ref = refs[k + n_w]
    x = x_ref[...]
    if has_pe:
        x = x + pe_ref[...]
    m = m_ref[0]
    hx_ref[...] = (x * (1.0 + m[1:2, :]) + m[0:1, :]).astype(BF16)
    n_plain = n_w - 1 if has_cmlp else n_w
    for w_ref, o_ref, use_gelu in list(zip(w_refs, o_refs, gelu_flags))[:n_plain]:
        n = w_ref.shape[1]
        step = min(tn, n)
        for j in range(0, n, step):
            acc = jnp.dot(hx_ref[...], w_ref[:, j:j + step], preferred_element_type=F32)
            if use_gelu:
                acc = _gelu_tanh(acc)
            o_ref[:, j:j + step] = acc.astype(o_ref.dtype)
    if has_cmlp:
        w_ref, o_ref, s_ref = w_refs[-1], o_refs[-1], refs[k + n_w + 1]
        bm = x_ref.shape[0]
        vgate = _gelu_tanh(jnp.dot(hx_ref[...], w_ref[:, D_CMLP:], preferred_element_type=F32))
        for g in range(CMLP_GROUPS):
            gs = slice(g * CMLP_GD, (g + 1) * CMLP_GD)
            vg = vgate[:, gs]
            mu = jnp.mean(vg, axis=-1, keepdims=True)
            vc = vg - mu
            var = jnp.mean(vc * vc, axis=-1, keepdims=True)
            vn = (vc * lax.rsqrt(var + LN_EPS) * cnw_ref[:, gs]).astype(BF16)
            for p in range(bm // CMLP_CHUNK):
                ps = slice(p * CMLP_CHUNK, (p + 1) * CMLP_CHUNK)
                s_ref[ps, gs] = (jnp.dot(ws_ref[g], vn[ps, :], preferred_element_type=F32)
                                 + bs_ref[:, g:g + 1])
        u = _gelu_tanh(jnp.dot(hx_ref[...], w_ref[:, :D_CMLP], preferred_element_type=F32))
        o_ref[...] = (u * s_ref[...]).astype(o_ref.dtype)


def _projection(x2d, pe, mods3, mod_row_of_block, weights, out_dtypes, gelu_flags, bm, cmlp=None):
    rows, dm = x2d.shape
    has_pe = pe is not None
    n_w = len(weights)
    in_specs = [pl.BlockSpec((bm, dm), lambda i: (i, 0))]
    args = [x2d]
    if has_pe:
        pe_blocks = pe.shape[0] // bm
        in_specs.append(pl.BlockSpec((bm, dm), lambda i: (i % pe_blocks, 0)))
        args.append(pe)
    in_specs.append(pl.BlockSpec((1, N_MOD, dm), lambda i: (mod_row_of_block(i), 0, 0)))
    args.append(mods3)
    for w in weights:
        in_specs.append(_resident(w.shape))
        args.append(w)
    out_widths = [w.shape[1] for w in weights]
    scratch = [pltpu.VMEM((bm, dm), BF16)]
    if cmlp is not None:
        for a in cmlp:
            in_specs.append(_resident(a.shape))
            args.append(a)
        out_widths[-1] = D_CMLP
        scratch.append(pltpu.VMEM((bm, D_CMLP), F32))
    out_specs = [pl.BlockSpec((bm, n), lambda i: (i, 0)) for n in out_widths]
    out_shape = [jax.ShapeDtypeStruct((rows, n), dt) for n, dt in zip(out_widths, out_dtypes)]
    kern = functools.partial(_proj_kernel, n_w=n_w, has_pe=has_pe, gelu_flags=tuple(gelu_flags),
                             has_cmlp=cmlp is not None, tn=1024)
    return pl.pallas_call(
        kern,
        grid=(rows // bm,),
        in_specs=in_specs,
        out_specs=out_specs,
        out_shape=out_shape,
        scratch_shapes=scratch,
        compiler_params=_cparams(("parallel",)),
        name="projection",
    )(*args)


def _gate_kernel(lic_ref, lfc_ref, lir_ref, lfr_ref, row_ref, col_ref, *, nc):
    li = lic_ref[0, 0]
    lf = _log_sigmoid(lfc_ref[0, 0])
    length = li.shape[0]
    tid = lax.broadcasted_iota(jnp.int32, li.shape, 0)
    lane = lax.broadcasted_iota(jnp.int32, li.shape, 1)
    lane1 = lane[0:1, :]
    fwd = (lane < nc) | (lane == 2 * nc)

    def scan_sublanes(x, op, fill):
        p = x
        s = x
        k = 1
        while k < length:
            p = op(p, jnp.where(tid >= k, pltpu.roll(p, k, 0), fill))
            s = op(s, jnp.where(tid < length - k, pltpu.roll(s, length - k, 0), fill))
            k *= 2
        return jnp.where(fwd, p, s)

    b = scan_sublanes(lf, jnp.add, 0.0)
    btot = jnp.sum(lf, axis=0, keepdims=True)
    a = btot - b + li
    m_loc = jnp.max(a, axis=0, keepdims=True)
    r = li - b
    cm = scan_sublanes(r, jnp.maximum, -jnp.inf)

    m_ctx = jnp.maximum(btot, m_loc)
    m_in = jnp.where(lane1 == 0, pltpu.roll(m_ctx, LANE - 2 * nc, 1), pltpu.roll(m_ctx, LANE - 2, 1))
    for k in range(nc - 1):
        m_new = jnp.maximum(btot + m_in, m_loc)
        m_in = jnp.where(lane1 == k + 1, pltpu.roll(m_new, 1, 1),
                         jnp.where(lane1 == 2 * nc - 2 - k, pltpu.roll(m_new, LANE - 1, 1), m_in))
    is_ctx = lane1 >= 2 * nc
    m_in = jnp.where(is_ctx, 0.0, m_in)
    m_new = jnp.maximum(btot + m_in, m_loc)
    s_old = jnp.broadcast_to(jnp.exp(btot + m_in - m_new), li.shape)
    w = jnp.exp(a - m_new)
    big_m = jnp.maximum(m_in, cm)
    s_int = jnp.exp(m_in - big_m)
    e_neg = jnp.exp(-(b + big_m))
    g = 2 * nc
    col_ref[0, 0] = jnp.where(
        lane < g, w, jnp.where(
            lane < 2 * g, pltpu.roll(big_m, g, 1), jnp.where(
                lane < 3 * g, pltpu.roll(s_int, 2 * g, 1), jnp.where(
                    lane < 4 * g, pltpu.roll(e_neg, 3 * g, 1), jnp.where(
                        lane < 4 * g + 2, pltpu.roll(w, 3 * g, 1), pltpu.roll(s_old, 4 * g + 2, 1))))))

    lir = lir_ref[0, 0]
    lfr = _log_sigmoid(lfr_ref[0, 0])
    width = lir.shape[1]
    rid = lax.broadcasted_iota(jnp.int32, lir.shape, 0)
    pid = lax.broadcasted_iota(jnp.int32, lir.shape, 1)
    p = lfr
    s = lfr
    k = 1
    while k < width:
        p = p + jnp.where(pid >= k, pltpu.roll(p, k, 1), 0.0)
        s = s + jnp.where(pid < width - k, pltpu.roll(s, width - k, 1), 0.0)
        k *= 2
    row_ref[0, 0] = lir - jnp.where(rid < nc, p, s)


def _gate_stats(g_x, g_c, gate_bias, bsz, seq, ctx_len):
    nc = seq // MCHUNK
    gb = gate_bias.astype(F32)
    gx = g_x[:, :N_GATE_COLS].reshape(bsz, nc, MCHUNK, 2, 2, HEADS) + gb.reshape(2, 2, HEADS)
    gc = g_c[:, :N_GATE_COLS].reshape(bsz, ctx_len, 2, 2, HEADS) + gb.reshape(2, 2, HEADS)
    col_x = gx.transpose(4, 0, 5, 2, 3, 1).reshape(2, bsz, HEADS, MCHUNK, 2 * nc)
    col_c = gc.transpose(3, 0, 4, 1, 2)
    col = jnp.concatenate([col_x, col_c], -1)
    col = jnp.pad(col, ((0, 0),) * 4 + ((0, LANE - col.shape[-1]),))
    row = gx.transpose(4, 0, 5, 3, 1, 2).reshape(2, bsz, HEADS, 2 * nc, MCHUNK)
    blk_c = pl.BlockSpec((1, 1, MCHUNK, LANE), lambda b, h: (b, h, 0, 0))
    blk_r = pl.BlockSpec((1, 1, 2 * nc, MCHUNK), lambda b, h: (b, h, 0, 0))
    return pl.pallas_call(
        functools.partial(_gate_kernel, nc=nc),
        grid=(bsz, HEADS),
        in_specs=[blk_c, blk_c, blk_r, blk_r],
        out_specs=[blk_r, blk_c],
        out_shape=[jax.ShapeDtypeStruct((bsz, HEADS, 2 * nc, MCHUNK), F32),
                   jax.ShapeDtypeStruct((bsz, HEADS, MCHUNK, LANE), F32)],
        compiler_params=_cparams(("parallel", "parallel")),
        name="gate_stats",
    )(col[0], col[1], row[0], row[1])


def _mlstm_kernel(q_ref, k_ref, v_ref, o_ref, kc_ref, vc_ref, cwq_ref, cbq_ref, cwk_ref, cbk_ref,
                  row_ref, col_ref, nw_ref, band_ref, edge_ref, y_ref, q_s, k_s, kc_s, ct_s, n_s, *, nc):
    lc = MCHUNK
    halo_rows = edge_ref.shape[1]

    def conv_silu_chunk(x_ref, c, n_chunks, w, b, scale):
        x = x_ref[pl.ds(c * lc, lc), :]
        wb = w.astype(BF16)
        taps = jnp.concatenate([x * wb[0:1, :], x * wb[1:2, :], x * wb[2:3, :]], axis=0)
        y = jnp.dot(band_ref[...], taps, preferred_element_type=F32)
        if n_chunks > 1:
            rid = lax.broadcasted_iota(jnp.int32, (halo_rows, x.shape[1]), 0)
            wf = wb.astype(F32)
            halo = jnp.zeros((halo_rows, x.shape[1]), F32)
            if c > 0:
                prev = x_ref[pl.ds(c * lc - halo_rows, halo_rows), :].astype(F32)[halo_rows - 1:, :]
                halo = jnp.where(rid == 0, prev * wf[0:1, :], halo)
            if c < n_chunks - 1:
                nxt = x_ref[pl.ds((c + 1) * lc, halo_rows), :].astype(F32)[0:1, :]
                halo = jnp.where(rid == 1, nxt * wf[2:3, :], halo)
            y = y + jnp.dot(edge_ref[...], halo.astype(BF16), preferred_element_type=F32)
        y = _silu(y + b)
        if scale != 1.0:
            y = y * scale
        return y.astype(BF16)

    k_scale = HD ** -0.5
    for c in range(nc):
        sl = pl.ds(c * lc, lc)
        q_s[sl, :] = conv_silu_chunk(q_ref, c, nc, cwq_ref[...], cbq_ref[...], 1.0)
        k_s[sl, :] = conv_silu_chunk(k_ref, c, nc, cwk_ref[...], cbk_ref[...], k_scale)
    kc_s[...] = conv_silu_chunk(kc_ref, 0, 1, cwk_ref[...], cbk_ref[...], k_scale)

    def col(j):
        return col_ref[0, 0, :, j:j + 1]

    stats_t = col_ref[0, 0].T

    def wrow(j):
        return stats_t[j:j + 1, :].astype(BF16)

    def local_state(kk, vv, w_row):
        ktw = kk.T * w_row
        ct = jnp.dot(ktw, vv, preferred_element_type=F32)
        nn = jnp.dot(jnp.broadcast_to(w_row, (SUBLANES, lc)), kk, preferred_element_type=F32)[0:1, :]
        return ct, nn

    for d in range(2):
        ct, nn = local_state(kc_s[...], vc_ref[...], wrow(8 * nc + d))
        order = list(range(nc)) if d == 0 else list(range(nc - 1, -1, -1))
        for pos, c in enumerate(order):
            idx = d * nc + c
            ct_s[idx] = ct.astype(BF16)
            n_s[idx] = nn
            if pos == nc - 1:
                break
            sl = pl.ds(c * lc, lc)
            ctl, nl = local_state(k_s[sl, :], v_ref[sl, :], wrow(idx))
            s_old = col_ref[0, 0, 0:1, 8 * nc + 2 + idx:8 * nc + 3 + idx]
            ct = s_old * ct + ctl
            nn = s_old * nn + nl

    tid = lax.broadcasted_iota(jnp.int32, (lc, lc), 0)
    sid = lax.broadcasted_iota(jnp.int32, (lc, lc), 1)
    masks = (sid <= tid, sid >= tid)
    for c in range(nc):
        sl = pl.ds(c * lc, lc)
        q = q_s[sl, :]
        kk = k_s[sl, :]
        v = v_ref[sl, :]
        qf = q.astype(F32)
        s = lax.dot_general(q, kk, (((1,), (1,)), ((), ())), preferred_element_type=F32)
        h = None
        for d in range(2):
            idx = d * nc + c
            r = row_ref[0, 0, idx:idx + 1, :]
            big_m = col(2 * nc + idx)
            s_int = col(4 * nc + idx)
            e_neg = col(6 * nc + idx)
            p = jnp.where(masks[d], jnp.exp(r - big_m), 0.0) * s
            den = (jnp.sum(p, axis=-1, keepdims=True)
                   + s_int * jnp.sum(qf * n_s[idx], axis=-1, keepdims=True))
            num = (jnp.dot(p.astype(BF16), v, preferred_element_type=F32)
                   + s_int * jnp.dot(q, ct_s[idx], preferred_element_type=F32))
            hd = num * (1.0 / jnp.maximum(jnp.abs(den), e_neg))
            h = hd if h is None else h + hd
        mu = jnp.mean(h, axis=-1, keepdims=True)
        hc = h - mu
        var = jnp.mean(hc * hc, axis=-1, keepdims=True)
        hn = hc * lax.rsqrt(var + LN_EPS) * nw_ref[...]
        y_ref[sl, :] = (hn * _sigmoid(o_ref[sl, :].astype(F32))).astype(BF16)


def _mlstm(qkvo, kv_ctx, conv_w, conv_b, rowq, colq, norm_w, bsz, seq, ctx_len):
    nc = seq // MCHUNK
    hq = D_MLSTM // HD
    kern = functools.partial(_mlstm_kernel, nc=nc)
    ii = jnp.arange(MCHUNK)[:, None]
    jj = jnp.arange(MCHUNK)[None, :]
    band = jnp.concatenate([(jj == ii + t - 1) for t in range(3)], axis=1).astype(BF16)
    hh = jnp.arange(2 * SUBLANES)[None, :]
    edge = (((ii == 0) & (hh == 0)) | ((ii == MCHUNK - 1) & (hh == 1))).astype(BF16)
    seq_blk = lambda off: pl.BlockSpec((seq, HD), lambda b, h: (b, off + h))
    ctx_blk = lambda off: pl.BlockSpec((ctx_len, HD), lambda b, h: (b, off + h))
    return pl.pallas_call(
        kern,
        grid=(bsz, HEADS),
        in_specs=[seq_blk(0), seq_blk(hq), seq_blk(2 * hq), seq_blk(3 * hq),
                  ctx_blk(0), ctx_blk(hq),
                  pl.BlockSpec((3, HD), lambda b, h: (0, h)),
                  pl.BlockSpec((1, HD), lambda b, h: (0, h)),
                  pl.BlockSpec((3, HD), lambda b, h: (0, hq + h)),
                  pl.BlockSpec((1, HD), lambda b, h: (0, hq + h)),
                  pl.BlockSpec((1, 1, 2 * nc, MCHUNK), lambda b, h: (b, h, 0, 0)),
                  pl.BlockSpec((1, 1, MCHUNK, LANE), lambda b, h: (b, h, 0, 0)),
                  pl.BlockSpec((1, HD), lambda b, h: (0, h)),
                  _resident(band.shape), _resident(edge.shape)],
        out_specs=pl.BlockSpec((seq, HD), lambda b, h: (b, h)),
        out_shape=jax.ShapeDtypeStruct((bsz * seq, D_MLSTM), BF16),
        scratch_shapes=[pltpu.VMEM((seq, HD), BF16), pltpu.VMEM((seq, HD), BF16),
                        pltpu.VMEM((ctx_len, HD), BF16),
                        pltpu.VMEM((2 * nc, HD, HD), BF16), pltpu.VMEM((2 * nc, 1, HD), F32)],
        compiler_params=_cparams(("parallel", "parallel")),
        name="mlstm",
    )(qkvo, qkvo, qkvo, qkvo, kv_ctx, kv_ctx, conv_w, conv_b, conv_w, conv_b, rowq, colq, norm_w,
      band, edge)


def _out_kernel(ym_ref, yc_ref, x_ref, pe_ref, m_ref, wout_ref,
                l1w_ref, l1b_ref, wr_ref, br_ref, x1_ref, h2_ref, lg_ref):
    m = m_ref[0]
    y = (jnp.dot(ym_ref[...], wout_ref[:D_MLSTM, :], preferred_element_type=F32)
         + jnp.dot(yc_ref[...], wout_ref[D_MLSTM:, :], preferred_element_type=F32))
    z = DEEPNORM_ALPHA * (x_ref[...] + pe_ref[...]) + m[2:3, :] * y
    x1 = _layer_norm_rows(z, l1w_ref[...], l1b_ref[...])
    x1_ref[...] = x1
    h2 = x1 * (1.0 + m[4:5, :]) + m[3:4, :]
    half = h2.shape[1] // 2
    packed = _pack_bf16_pairs(h2[:, :half], h2[:, half:])
    for c in range(ROW_LINES):
        h2_ref[pl.ds(c, h2.shape[0], stride=ROW_LINES), :] = packed[:, c * LANE:(c + 1) * LANE]
    lg = jnp.dot(h2.astype(BF16), wr_ref[...], preferred_element_type=F32) + br_ref[...]

    lgt = lg.T
    bm = lg.shape[0]
    epg = EXPERTS_PER_GROUP
    rid = lax.broadcasted_iota(jnp.int32, (SUBLANES, bm), 0).astype(F32)
    neg = -jnp.inf
    far = float(SUBLANES)
    gl = lgt[N_EXPERTS:N_EXPERTS + SUBLANES, :]
    is_grp = rid < N_GROUPS
    m1 = jnp.max(jnp.where(is_grp, gl, neg), axis=0, keepdims=True)
    grp = jnp.min(jnp.where(is_grp & (gl == m1), rid, far), axis=0, keepdims=True)
    p_grp = 1.0 / jnp.sum(jnp.where(is_grp, jnp.exp(gl - m1), 0.0), axis=0, keepdims=True)
    l2 = lgt[0:epg, :]
    for g in range(1, N_GROUPS):
        l2 = jnp.where(grp == g, lgt[g * epg:(g + 1) * epg, :], l2)
    v0 = jnp.max(l2, axis=0, keepdims=True)
    i0 = jnp.min(jnp.where(l2 == v0, rid, far), axis=0, keepdims=True)
    l2m = jnp.where(rid == i0, neg, l2)
    v1 = jnp.max(l2m, axis=0, keepdims=True)
    i1 = jnp.min(jnp.where(l2m == v1, rid, far), axis=0, keepdims=True)
    s1 = jnp.exp(v1 - v0)
    g0 = p_grp / (1.0 + s1)
    res = jnp.where(rid == 0, g0, jnp.where(rid == 1, g0 * s1, jnp.where(
        rid == 2, grp * epg + i0, jnp.where(rid == 3, grp * epg + i1, 0.0))))
    lg_ref[...] = jnp.concatenate([res, jnp.zeros((LANE - SUBLANES, bm), F32)], axis=0).T


def _mixer_out(ym, yc, x2d, pe, mods3, wout, l1w, l1b, wr, br, seq, bm):
    rows, dm = x2d.shape
    pe_blocks = seq // bm
    row_blk = lambda n: pl.BlockSpec((bm, n), lambda i: (i, 0))
    return pl.pallas_call(
        _out_kernel,
        grid=(rows // bm,),
        in_specs=[row_blk(D_MLSTM), row_blk(D_CMLP), row_blk(dm),
                  pl.BlockSpec((bm, dm), lambda i: (i % pe_blocks, 0)),
                  pl.BlockSpec((1, N_MOD, dm), lambda i: (i // pe_blocks, 0, 0)),
                  _resident(wout.shape), _resident(l1w.shape), _resident(l1b.shape),
                  _resident(wr.shape), _resident(br.shape)],
        out_specs=[row_blk(dm), pl.BlockSpec((bm * ROW_LINES, LANE), lambda i: (i, 0)), row_blk(LANE)],
        out_shape=[jax.ShapeDtypeStruct((rows, dm), F32),
                   jax.ShapeDtypeStruct((rows * ROW_LINES, LANE), jnp.uint32),
                   jax.ShapeDtypeStruct((rows, LANE), F32)],
        compiler_params=_cparams(("parallel",)),
        name="mixer_out",
    )(ym, yc, x2d, pe, mods3, wout, l1w, l1b, wr, br)


def _moe_kernel(comp_ref, blk_ref, slot_ref, cast_ref, cslot_ref, cexp_ref, ctile_ref,
                gtab_ref, stab_ref, h2_hbm, wgf_ref, wuf_ref, wdf_ref, ys_hbm,
                wg_s, wu_s, wd_s, xbuf, xb_s, ybuf, gsem, ssem, pend, *, nt):
    del blk_ref, cexp_ref
    s = pl.program_id(0)
    n_steps = pl.num_programs(0)
    groups = MOE_ROWS // SUBLANES
    rl = ROW_LINES
    half = rl * LANE
    dm = 2 * half
    n_asg = ys_hbm.shape[0] // rl - 2 * MOE_ROWS
    group_lines = SUBLANES * rl

    def gather_wait(p, n):
        nl = n * group_lines
        pltpu.make_async_copy(h2_hbm.at[pl.ds(0, nl)], xbuf.at[p, pl.ds(0, nl)], gsem.at[p]).wait()

    def scatter_wait(p, n):
        nl = n * group_lines
        pltpu.make_async_copy(ybuf.at[p, pl.ds(0, nl)], ys_hbm.at[pl.ds(0, nl)], ssem.at[p]).wait()

    @pl.when(s == 0)
    def _():
        pend[0] = 0
        pend[1] = 0
        xbuf[...] = jnp.zeros_like(xbuf)

    nxt = jnp.minimum(s + 1, n_steps - 1)

    @pl.when(jnp.logical_and(s + 1 < n_steps, comp_ref[nxt] > 0))
    def _():
        p = (s + 1) % 2

        for par in range(2):
            for g0 in range(0, groups, GROUP_STEP):
                @pl.when(jnp.logical_and(p == par, g0 < comp_ref[nxt]))
                def _():
                    for r in range(g0 * SUBLANES, (g0 + GROUP_STEP) * SUBLANES):
                        src = pl.multiple_of(gtab_ref[0, 0, r] * rl, rl)
                        pltpu.make_async_copy(h2_hbm.at[pl.ds(src, rl)], xbuf.at[par, pl.ds(r * rl, rl)],
                                              gsem.at[par]).start()

    @pl.when(cast_ref[s] == 1)
    def _():
        cs = cslot_ref[s]
        t = ctile_ref[s]
        for k in range(nt):
            @pl.when(t == k)
            def _():
                wg_s[cs, :, k * MOE_FT:(k + 1) * MOE_FT] = wgf_ref[0].astype(BF16)
                wu_s[cs, :, k * MOE_FT:(k + 1) * MOE_FT] = wuf_ref[0].astype(BF16)
        wd_s[cs, t] = wdf_ref[0].astype(BF16)

    @pl.when(comp_ref[s] > 0)
    def _():
        p = s % 2
        sl = slot_ref[s]
        ng = comp_ref[s]
        gather_wait(p, ng)

        @pl.when(pend[p] > 0)
        def _():
            scatter_wait(p, pend[p])

        def run_block(rows):
            for c in range(rl):
                x_lo, x_hi = _unpack_bf16_pairs(xbuf[p, pl.ds(c, rows, stride=rl), :])
                xb_s[:rows, c * LANE:(c + 1) * LANE] = x_lo.astype(BF16)
                xb_s[:rows, half + c * LANE:half + (c + 1) * LANE] = x_hi.astype(BF16)
            x = xb_s[:rows, :]
            g = jnp.dot(x, wg_s[sl], preferred_element_type=F32)
            u = jnp.dot(x, wu_s[sl], preferred_element_type=F32)
            h = (_silu(g) * u).astype(BF16)
            y = jnp.dot(h, wd_s[sl].reshape(D_EXPERT, dm), preferred_element_type=F32)
            y_packed = _pack_bf16_pairs(y[:, :half], y[:, half:])
            for c in range(rl):
                ybuf[p, pl.ds(c, rows, stride=rl), :] = y_packed[:, c * LANE:(c + 1) * LANE]

        small_groups = MOE_SMALL_ROWS // SUBLANES

        @pl.when(ng > small_groups)
        def _():
            run_block(MOE_ROWS)

        @pl.when(ng <= small_groups)
        def _():
            run_block(MOE_SMALL_ROWS)

        for par in range(2):
            for g0 in range(0, groups, GROUP_STEP):
                @pl.when(jnp.logical_and(p == par, g0 < ng))
                def _():
                    for r in range(g0 * SUBLANES, (g0 + GROUP_STEP) * SUBLANES):
                        d = stab_ref[0, 0, r]
                        dst = pl.multiple_of(jnp.where(d < 0, n_asg + par * MOE_ROWS + r, d) * rl, rl)
                        pltpu.make_async_copy(ybuf.at[par, pl.ds(r * rl, rl)], ys_hbm.at[pl.ds(dst, rl)],
                                              ssem.at[par]).start()

        pend[p] = ng

    @pl.when(s == n_steps - 1)
    def _():
        for p in range(2):
            @pl.when(pend[p] > 0)
            def _():
                scatter_wait(p, pend[p])
                pend[p] = 0

        xbuf[...] = jnp.zeros_like(xbuf)
        fills = [pltpu.make_async_copy(xbuf.at[p], ys_hbm.at[pl.ds((n_asg + p * MOE_ROWS) * rl, MOE_ROWS * rl)],
                                       gsem.at[p]) for p in range(2)]
        for cp in fills:
            cp.start()
        for cp in fills:
            cp.wait()


def _experts(h2, gtab, stab, sched, wg, wu, wd):
    n_tok, dm = h2.shape[0] // ROW_LINES, D_MODEL
    nt = D_EXPERT // MOE_FT
    n_asg = 2 * n_tok
    n_steps = sched[0].shape[0]

    smem_rows = lambda imap: pl.BlockSpec((1, 1, MOE_ROWS), imap, memory_space=pltpu.SMEM)
    grid_spec = pltpu.PrefetchScalarGridSpec(
        num_scalar_prefetch=7,
        grid=(n_steps,),
        in_specs=[smem_rows(lambda s, comp, blk, *_: (blk[jnp.minimum(s + 1, n_steps - 1)], 0, 0)),
                  smem_rows(lambda s, comp, blk, *_: (blk[s], 0, 0)),
                  pl.BlockSpec(memory_space=pl.ANY),
                  pl.BlockSpec((1, dm, MOE_FT), lambda s, c, b, sl, ca, cs, ce, ct: (ce[s], 0, ct[s])),
                  pl.BlockSpec((1, dm, MOE_FT), lambda s, c, b, sl, ca, cs, ce, ct: (ce[s], 0, ct[s])),
                  pl.BlockSpec((1, MOE_FT, dm), lambda s, c, b, sl, ca, cs, ce, ct: (ce[s], ct[s], 0))],
        out_specs=pl.BlockSpec(memory_space=pl.ANY),
        scratch_shapes=[pltpu.VMEM((2, dm, D_EXPERT), BF16), pltpu.VMEM((2, dm, D_EXPERT), BF16),
                        pltpu.VMEM((2, nt, MOE_FT, dm), BF16),
                        pltpu.VMEM((2, MOE_ROWS * ROW_LINES, LANE), jnp.uint32),
                        pltpu.VMEM((MOE_ROWS, dm), BF16),
                        pltpu.VMEM((2, MOE_ROWS * ROW_LINES, LANE), jnp.uint32),
                        pltpu.SemaphoreType.DMA((2,)), pltpu.SemaphoreType.DMA((2,)),
                        pltpu.SMEM((2,), jnp.int32)],
    )
    return pl.pallas_call(
        functools.partial(_moe_kernel, nt=nt),
        grid_spec=grid_spec,
        out_shape=jax.ShapeDtypeStruct(((n_asg + 2 * MOE_ROWS) * ROW_LINES, LANE), jnp.uint32),
        compiler_params=_cparams(("arbitrary",)),
        name="experts",
    )(*sched, gtab, stab, h2, wg, wu, wd)


def _final_kernel(x1_ref, y0_ref, y1_ref, g_ref, m_ref, w_ref, b_ref, o_ref, y_s):
    m = m_ref[0]
    bm, dm = x1_ref.shape
    half = dm // 2
    g0 = g_ref[:, 0:1]
    g1 = g_ref[:, 1:2]
    for c in range(ROW_LINES):
        rows_c = pl.ds(c, bm, stride=ROW_LINES)
        lo0, hi0 = _unpack_bf16_pairs(y0_ref[rows_c, :])
        lo1, hi1 = _unpack_bf16_pairs(y1_ref[rows_c, :])
        y_s[:, c * LANE:(c + 1) * LANE] = g0 * lo0 + g1 * lo1
        y_s[:, half + c * LANE:half + (c + 1) * LANE] = g0 * hi0 + g1 * hi1
    z = DEEPNORM_ALPHA * x1_ref[...] + m[5:6, :] * y_s[...]
    o_ref[...] = _layer_norm_rows(z, w_ref[...], b_ref[...])


def _final(x1, ys, gates, mods3, w, b, seq, bm):
    rows, dm = x1.shape
    blocks_per_batch = seq // bm
    slot_blocks = rows // bm
    row_blk = lambda n: pl.BlockSpec((bm, n), lambda i: (i, 0))
    return pl.pallas_call(
        _final_kernel,
        grid=(rows // bm,),
        in_specs=[row_blk(dm), pl.BlockSpec((bm * ROW_LINES, LANE), lambda i: (i, 0)),
                  pl.BlockSpec((bm * ROW_LINES, LANE), lambda i: (i + slot_blocks, 0)),
                  row_blk(LANE),
                  pl.BlockSpec((1, N_MOD, dm), lambda i: (i // blocks_per_batch, 0, 0)),
                  _resident(w.shape), _resident(b.shape)],
        out_specs=row_blk(dm),
        out_shape=jax.ShapeDtypeStruct((rows, dm), F32),
        scratch_shapes=[pltpu.VMEM((bm, dm), F32)],
        compiler_params=_cparams(("parallel",)),
        name="final_ln",
    )(x1, ys, ys, gates, mods3, w, b)


def _route(route, n_tok):
    e_flat = route[:, 2:4].astype(jnp.int32).reshape(-1)
    n_asg = e_flat.shape[0]
    earange = jnp.arange(N_EXPERTS, dtype=jnp.int32)
    counts = jnp.sum((e_flat[:, None] == earange[None, :]).astype(jnp.int32), 0)
    nblk_e = (counts + MOE_ROWS - 1) // MOE_ROWS
    pad_end = jnp.cumsum(nblk_e * MOE_ROWS)
    pad_start = pad_end - nblk_e * MOE_ROWS
    n_blk = n_asg // MOE_ROWS + N_EXPERTS

    order = jnp.argsort(e_flat, stable=True).astype(jnp.int32)
    starts = jnp.cumsum(counts) - counts
    ridx = jnp.arange(n_blk * MOE_ROWS, dtype=jnp.int32)
    e_row = jnp.minimum(jnp.sum((pad_end[None, :] <= ridx[:, None]).astype(jnp.int32), -1), N_EXPERTS - 1)
    rsel = e_row[:, None] == earange[None, :]
    k_row = ridx - jnp.sum(jnp.where(rsel, pad_start[None, :], 0), -1)
    valid = k_row < jnp.sum(jnp.where(rsel, counts[None, :], 0), -1)
    src = jnp.clip(jnp.sum(jnp.where(rsel, starts[None, :], 0), -1) + k_row, 0, n_asg - 1)
    row_asg = jnp.where(valid, order[src], -1)
    gtab = (jnp.maximum(row_asg, 0) >> 1).reshape(n_blk, 1, MOE_ROWS)
    stab = jnp.where(row_asg < 0, -1, (row_asg & 1) * n_tok + (row_asg >> 1)).reshape(n_blk, 1, MOE_ROWS)

    nt = D_EXPERT // MOE_FT
    n_steps = n_blk + (nt - 1) * N_EXPERTS + nt
    has = nblk_e > 0
    n_visits = jnp.sum(has.astype(jnp.int32))
    e_of_visit = jnp.sort(jnp.where(has, earange, N_EXPERTS))
    vsel = e_of_visit[:, None] == earange[None, :]
    nb_v = jnp.sum(jnp.where(vsel, nblk_e[None, :], 0), -1)
    steps_v = jnp.where(nb_v > 0, jnp.maximum(nb_v, nt), 0)
    end_v = nt + jnp.cumsum(steps_v)
    start_v = end_v - steps_v
    first_blk_v = jnp.cumsum(nb_v) - nb_v
    sidx = jnp.arange(n_steps, dtype=jnp.int32)
    v = jnp.sum((end_v[None, :] <= sidx[:, None]).astype(jnp.int32), -1)
    pick = lambda arr, idx: jnp.sum(jnp.where(idx[:, None] == earange[None, :], arr[None, :], 0), -1)
    in_visit = (sidx >= nt) & (v < n_visits)
    k = sidx - pick(start_v, v)
    comp = in_visit & (k < pick(nb_v, v))
    cnt_v = jnp.sum(jnp.where(vsel, counts[None, :], 0), -1)
    rows_s = jnp.clip(pick(cnt_v, v) - k * MOE_ROWS, 0, MOE_ROWS)
    chunk_rows = GROUP_STEP * SUBLANES
    groups_s = jnp.where(comp, (rows_s + chunk_rows - 1) // chunk_rows * GROUP_STEP, 0)
    blk_s = lax.cummax(jnp.where(comp, pick(first_blk_v, v) + k, 0), axis=0)
    prologue = sidx < nt
    cast = prologue | (in_visit & (k < nt) & (v + 1 < n_visits))
    cexp = jnp.where(prologue, e_of_visit[0], pick(e_of_visit, v + 1))
    ctile = jnp.where(prologue, sidx, k)
    code = lax.cummax(jnp.where(cast, cexp * nt + ctile, 0), axis=0)
    i32 = lambda a: a.astype(jnp.int32)
    sched = (i32(groups_s), i32(blk_s), i32(v % 2), i32(cast), i32(jnp.where(prologue, 0, (v + 1) % 2)),
             i32(jnp.minimum(code // nt, N_EXPERTS - 1)), i32(code % nt))
    return sched, gtab, stab


def _grid_pos_embed(rows):
    quarter = D_MODEL // 4
    omega = 1.0 / (10000.0 ** (jnp.arange(quarter, dtype=F32) / quarter))
    ar = jnp.arange(rows, dtype=F32)[:, None] * omega
    ac = jnp.arange(GRID_W, dtype=F32)[:, None] * omega
    shape = (rows, GRID_W, quarter)
    parts = [jnp.broadcast_to(jnp.sin(ar)[:, None, :], shape), jnp.broadcast_to(jnp.cos(ar)[:, None, :], shape),
             jnp.broadcast_to(jnp.sin(ac)[None, :, :], shape), jnp.broadcast_to(jnp.cos(ac)[None, :, :], shape)]
    return jnp.concatenate(parts, -1).reshape(rows * GRID_W, D_MODEL)


def kernel(x, c, ctx, c_ctx, w_mod, b_mod, w_in, conv_w, conv_b, gate_bias, mlstm_norm_w, cmlp_norm_w,
           w_s, b_s, w_out, ln1_w, ln1_b, router1_w, router1_b, router2_w, router2_b, w_gate, w_up,
           w_down, ln2_w, ln2_b):
    bsz, seq, dm = x.shape
    ctx_len = ctx.shape[1]
    n_tok = bsz * seq
    assert w_mod.shape[0] == 1 and dm == D_MODEL and seq % MCHUNK == 0 and ctx_len == MCHUNK
    pe = _grid_pos_embed(seq // GRID_W).astype(x.dtype)
    x2d = x.reshape(n_tok, dm)
    ctx2d = ctx.reshape(bsz * ctx_len, dm)

    mod_rows = 16
    cc = jnp.concatenate([c, c_ctx[None, :], jnp.zeros((mod_rows - bsz - 1, dm), c.dtype)], 0)
    mods3 = _modulation(cc, w_mod[0], b_mod[0]).reshape(mod_rows, N_MOD, dm)

    dq = D_MLSTM
    wi = w_in[0]
    w_qkvo = wi[:, :4 * dq].astype(BF16)
    w_g = jnp.pad(wi[:, 4 * dq:4 * dq + N_GATE_COLS], ((0, 0), (0, LANE - N_GATE_COLS))).astype(BF16)
    w_uv = wi[:, 4 * dq + N_GATE_COLS:].astype(BF16)
    bm_proj = 256
    blocks_per_seq = seq // bm_proj
    cmlp = (cmlp_norm_w[0].reshape(1, -1), w_s[0].astype(BF16), b_s[0].T)
    qkvo, g_x, yc = _projection(x2d, pe, mods3, lambda i: i // blocks_per_seq,
                                [w_qkvo, w_g, w_uv], [BF16, F32, BF16], [False, False, True], bm_proj,
                                cmlp=cmlp)
    w_kv = wi[:, dq:3 * dq].astype(BF16)
    kv_c, g_c = _projection(ctx2d, None, mods3, lambda i: bsz, [w_kv, w_g], [BF16, F32],
                            [False, False], bm_proj)

    rowq, colq = _gate_stats(g_x, g_c, gate_bias[0], bsz, seq, ctx_len)
    ym = _mlstm(qkvo, kv_c, conv_w[0], conv_b[0].reshape(1, -1), rowq, colq,
                mlstm_norm_w[0].reshape(1, -1), bsz, seq, ctx_len)

    wr = jnp.pad(jnp.concatenate([router2_w[0], router1_w[0]], 1),
                 ((0, 0), (0, LANE - N_GROUPS - N_EXPERTS))).astype(BF16)
    br = jnp.pad(jnp.concatenate([router2_b[0], router1_b[0]], 0),
                 (0, LANE - N_GROUPS - N_EXPERTS)).reshape(1, LANE)
    x1, h2, route = _mixer_out(ym, yc, x2d, pe, mods3, w_out[0].astype(BF16),
                               ln1_w[0].reshape(1, -1), ln1_b[0].reshape(1, -1), wr, br, seq, 512)

    sched, gtab, stab = _route(route, n_tok)
    ys = _experts(h2, gtab, stab, sched, w_gate[0], w_up[0], w_down[0])
    out = _final(x1, ys, route, mods3, ln2_w[0].reshape(1, -1),
                 ln2_b[0].reshape(1, -1), seq, 512)
    return out.reshape(bsz, seq, dm)
```

```python
import functools

import jax
import jax.numpy as jnp
from jax import lax
from jax.experimental import pallas as pl
from jax.experimental.pallas import tpu as pltpu

F32 = jnp.float32
BF16 = jnp.bfloat16

D_MODEL = 2048
GRID_W = 64
D_MLSTM = 1024
D_CMLP = 1024
HEADS = 4
HD = 256
CMLP_GROUPS = 4
CMLP_GD = 256
CMLP_CHUNK = 128
N_GROUPS = 4
EXPERTS_PER_GROUP = 8
N_EXPERTS = 32
D_EXPERT = 1024
N_MOD = 6
N_GATE_COLS = 16
DEEPNORM_ALPHA = 2.0 ** 0.25
LN_EPS = 1e-6

LANE = 128
SUBLANES = 8
MCHUNK = 256
MOE_ROWS = 256
MOE_SMALL_ROWS = 64
MOE_FT = 256
GROUP_STEP = 4
HEAD_LANES = LANE // HEADS
ROW_LINES = D_MODEL // 2 // LANE
VMEM_LIMIT = 56 * 1024 * 1024


def _cparams(sem):
    return pltpu.CompilerParams(dimension_semantics=sem, vmem_limit_bytes=VMEM_LIMIT)


def _resident(shape):
    nd = len(shape)
    return pl.BlockSpec(shape, lambda *_: (0,) * nd, pipeline_mode=pl.Buffered(1))


def _sigmoid(x):
    return 0.5 * jnp.tanh(0.5 * x) + 0.5


def _silu(x):
    return x * _sigmoid(x)


def _log_sigmoid(x):
    return jnp.minimum(x, 0.0) - jnp.log1p(jnp.exp(-jnp.abs(x)))


def _gelu_tanh(x):
    c = 0.7978845608028654
    return 0.5 * x * (1.0 + jnp.tanh(c * (x + 0.044715 * (x * x * x))))


def _pack_bf16_pairs(lo, hi):
    lo_b = lax.bitcast_convert_type(lo.astype(BF16).astype(F32), jnp.uint32)
    hi_b = lax.bitcast_convert_type(hi.astype(BF16).astype(F32), jnp.uint32)
    return (lo_b >> 16) | (hi_b & jnp.uint32(0xFFFF0000))


def _unpack_bf16_pairs(w):
    lo = lax.bitcast_convert_type(w << 16, F32)
    hi = lax.bitcast_convert_type(w & jnp.uint32(0xFFFF0000), F32)
    return lo, hi


def _layer_norm_rows(z, w, b):
    mu = jnp.mean(z, axis=-1, keepdims=True)
    zc = z - mu
    var = jnp.mean(zc * zc, axis=-1, keepdims=True)
    return zc * lax.rsqrt(var + LN_EPS) * w + b


def _mod_kernel(c_ref, w_ref, b_ref, o_ref):
    s = _silu(c_ref[...]).astype(BF16)
    o_ref[...] = jnp.dot(s, w_ref[...].astype(BF16), preferred_element_type=F32) + b_ref[...]


def _modulation(cc, w_mod, b_mod):
    rows, dm = cc.shape
    n = w_mod.shape[1]
    tn = 1024
    return pl.pallas_call(
        _mod_kernel,
        grid=(n // tn,),
        in_specs=[pl.BlockSpec((rows, dm), lambda j: (0, 0)),
                  pl.BlockSpec((dm, tn), lambda j: (0, j)),
                  pl.BlockSpec((1, tn), lambda j: (0, j))],
        out_specs=pl.BlockSpec((rows, tn), lambda j: (0, j)),
        out_shape=jax.ShapeDtypeStruct((rows, n), F32),
        compiler_params=_cparams(("arbitrary",)),
        name="modulation",
    )(cc, w_mod, b_mod.reshape(1, n))


def _proj_kernel(*refs, n_w, has_pe, gelu_flags, has_cmlp, tn):
    x_ref = refs[0]
    k = 1
    pe_ref = None
    if has_pe:
        pe_ref = refs[k]
        k += 1
    m_ref = refs[k]
    k += 1
    w_refs = refs[k:k + n_w]
    k += n_w
    if has_cmlp:
        cnw_ref, ws_ref, bs_ref = refs[k:k + 3]
        k += 3
    o_refs = refs[k:k + n_w]
    hx_ref = refs[k + n_w]
    x = x_ref[...]
    if has_pe:
        x = x + pe_ref[...]
    m = m_ref[0]
    hx_ref[...] = (x * (1.0 + m[1:2, :]) + m[0:1, :]).astype(BF16)
    n_plain = n_w - 1 if has_cmlp else n_w
    for w_ref, o_ref, use_gelu in list(zip(w_refs, o_refs, gelu_flags))[:n_plain]:
        n = w_ref.shape[1]
        step = min(tn, n)
        for j in range(0, n, step):
            acc = jnp.dot(hx_ref[...], w_ref[:, j:j + step], preferred_element_type=F32)
            if use_gelu:
                acc = _gelu_tanh(acc)
            o_ref[:, j:j + step] = acc.astype(o_ref.dtype)
    if has_cmlp:
        w_ref, o_ref, s_ref = w_refs[-1], o_refs[-1], refs[k + n_w + 1]
        bm = x_ref.shape[0]
        vgate = _gelu_tanh(jnp.dot(hx_ref[...], w_ref[:, D_CMLP:], preferred_element_type=F32))
        for g in range(CMLP_GROUPS):
            gs = slice(g * CMLP_GD, (g + 1) * CMLP_GD)
            vg = vgate[:, gs]
            mu = jnp.mean(vg, axis=-1, keepdims=True)
            vc = vg - mu
            var = jnp.mean(vc * vc, axis=-1, keepdims=True)
            vn = (vc * lax.rsqrt(var + LN_EPS) * cnw_ref[:, gs]).astype(BF16)
            for p in range(bm // CMLP_CHUNK):
                ps = slice(p * CMLP_CHUNK, (p + 1) * CMLP_CHUNK)
                s_ref[ps, gs] = (jnp.dot(ws_ref[g], vn[ps, :], preferred_element_type=F32)
                                 + bs_ref[:, g:g + 1])
        u = _gelu_tanh(jnp.dot(hx_ref[...], w_ref[:, :D_CMLP], preferred_element_type=F32))
        o_ref[...] = (u * s_ref[...]).astype(o_ref.dtype)


def _projection(x2d, pe, mods3, mod_row_of_block, weights, out_dtypes, gelu_flags, bm, cmlp=None):
    rows, dm = x2d.shape
    has_pe = pe is not None
    n_w = len(weights)
    in_specs = [pl.BlockSpec((bm, dm), lambda i: (i, 0))]
    args = [x2d]
    if has_pe:
        pe_blocks = pe.shape[0] // bm
        in_specs.append(pl.BlockSpec((bm, dm), lambda i: (i % pe_blocks, 0)))
        args.append(pe)
    in_specs.append(pl.BlockSpec((1, N_MOD, dm), lambda i: (mod_row_of_block(i), 0, 0)))
    args.append(mods3)
    for w in weights:
        in_specs.append(_resident(w.shape))
        args.append(w)
    out_widths = [w.shape[1] for w in weights]
    scratch = [pltpu.VMEM((bm, dm), BF16)]
    if cmlp is not None:
        for a in cmlp:
            in_specs.append(_resident(a.shape))
            args.append(a)
        out_widths[-1] = D_CMLP
        scratch.append(pltpu.VMEM((bm, D_CMLP), F32))
    out_specs = [pl.BlockSpec((bm, n), lambda i: (i, 0)) for n in out_widths]
    out_shape = [jax.ShapeDtypeStruct((rows, n), dt) for n, dt in zip(out_widths, out_dtypes)]
    kern = functools.partial(_proj_kernel, n_w=n_w, has_pe=has_pe, gelu_flags=tuple(gelu_flags),
                             has_cmlp=cmlp is not None, tn=1024)
    return pl.pallas_call(
        kern,
        grid=(rows // bm,),
        in_specs=in_specs,
        out_specs=out_specs,
        out_shape=out_shape,
        scratch_shapes=scratch,
        compiler_params=_cparams(("parallel",)),
        name="projection",
    )(*args)


def _gate_kernel(lic_ref, lfc_ref, lir_ref, lfr_ref, row_ref, col_ref, *, nc):
    hl = HEAD_LANES
    li = lic_ref[0]
    lf = _log_sigmoid(lfc_ref[0])
    length = li.shape[0]
    tid = lax.broadcasted_iota(jnp.int32, li.shape, 0)
    lane = lax.broadcasted_iota(jnp.int32, li.shape, 1)
    lane_l = lane & (hl - 1)
    lane1 = lane_l[0:1, :]
    fwd = (lane_l < nc) | (lane_l == 2 * nc)

    def scan_sublanes(x, op, fill):
        p = x
        s = x
        k = 1
        while k < length:
            p = op(p, jnp.where(tid >= k, pltpu.roll(p, k, 0), fill))
            s = op(s, jnp.where(tid < length - k, pltpu.roll(s, length - k, 0), fill))
            k *= 2
        return jnp.where(fwd, p, s)

    b = scan_sublanes(lf, jnp.add, 0.0)
    btot = jnp.sum(lf, axis=0, keepdims=True)
    a = btot - b + li
    m_loc = jnp.max(a, axis=0, keepdims=True)
    r = li - b
    cm = scan_sublanes(r, jnp.maximum, -jnp.inf)

    m_ctx = jnp.maximum(btot, m_loc)
    m_in = jnp.where(lane1 == 0, pltpu.roll(m_ctx, LANE - 2 * nc, 1), pltpu.roll(m_ctx, LANE - 2, 1))
    for k in range(nc - 1):
        m_new = jnp.maximum(btot + m_in, m_loc)
        m_in = jnp.where(lane1 == k + 1, pltpu.roll(m_new, 1, 1),
                         jnp.where(lane1 == 2 * nc - 2 - k, pltpu.roll(m_new, LANE - 1, 1), m_in))
    is_ctx = lane1 >= 2 * nc
    m_in = jnp.where(is_ctx, 0.0, m_in)
    m_new = jnp.maximum(btot + m_in, m_loc)
    s_old = jnp.broadcast_to(jnp.exp(btot + m_in - m_new), li.shape)
    w = jnp.exp(a - m_new)
    big_m = jnp.maximum(m_in, cm)
    s_int = jnp.exp(m_in - big_m)
    e_neg = jnp.exp(-(b + big_m))
    g = 2 * nc
    for h in range(HEADS):
        off = h * hl
        rot = lambda x, to: pltpu.roll(x, (to - off) % LANE, 1)
        col_ref[0, h] = jnp.where(
            lane < g, rot(w, 0), jnp.where(
                lane < 2 * g, rot(big_m, g), jnp.where(
                    lane < 3 * g, rot(s_int, 2 * g), jnp.where(
                        lane < 4 * g, rot(e_neg, 3 * g), jnp.where(
                            lane < 4 * g + 2, rot(w, 3 * g), rot(s_old, 4 * g + 2))))))

    lir = lir_ref[0]
    lfr = _log_sigmoid(lfr_ref[0])
    width = lir.shape[1]
    rid = lax.broadcasted_iota(jnp.int32, lir.shape, 0) & (2 * nc - 1)
    pid = lax.broadcasted_iota(jnp.int32, lir.shape, 1)
    p = lfr
    s = lfr
    k = 1
    while k < width:
        p = p + jnp.where(pid >= k, pltpu.roll(p, k, 1), 0.0)
        s = s + jnp.where(pid < width - k, pltpu.roll(s, width - k, 1), 0.0)
        k *= 2
    row_ref[0] = lir - jnp.where(rid < nc, p, s)


def _gate_stats(g_x, g_c, gate_bias, bsz, seq, ctx_len):
    nc = seq // MCHUNK
    assert HEADS * HEAD_LANES == LANE and 2 * nc + 2 <= HEAD_LANES and (2 * nc) & (2 * nc - 1) == 0
    gb = gate_bias.astype(F32)
    gx = g_x[:, :N_GATE_COLS].reshape(bsz, nc, MCHUNK, 2, 2, HEADS) + gb.reshape(2, 2, HEADS)
    gc = g_c[:, :N_GATE_COLS].reshape(bsz, ctx_len, 2, 2, HEADS) + gb.reshape(2, 2, HEADS)
    col_x = gx.transpose(4, 0, 2, 5, 3, 1).reshape(2, bsz, MCHUNK, HEADS, 2 * nc)
    col_c = gc.transpose(3, 0, 1, 4, 2)
    col = jnp.concatenate([col_x, col_c], -1)
    col = jnp.pad(col, ((0, 0),) * 4 + ((0, HEAD_LANES - col.shape[-1]),)).reshape(2, bsz, MCHUNK, LANE)
    row = gx.transpose(4, 0, 5, 3, 1, 2).reshape(2, bsz, HEADS * 2 * nc, MCHUNK)
    blk_c = pl.BlockSpec((1, MCHUNK, LANE), lambda b: (b, 0, 0))
    blk_r = pl.BlockSpec((1, HEADS * 2 * nc, MCHUNK), lambda b: (b, 0, 0))
    rowq, colq = pl.pallas_call(
        functools.partial(_gate_kernel, nc=nc),
        grid=(bsz,),
        in_specs=[blk_c, blk_c, blk_r, blk_r],
        out_specs=[blk_r, pl.BlockSpec((1, HEADS, MCHUNK, LANE), lambda b: (b, 0, 0, 0))],
        out_shape=[jax.ShapeDtypeStruct((bsz, HEADS * 2 * nc, MCHUNK), F32),
                   jax.ShapeDtypeStruct((bsz, HEADS, MCHUNK, LANE), F32)],
        compiler_params=_cparams(("parallel",)),
        name="gate_stats",
    )(col[0], col[1], row[0], row[1])
    return rowq.reshape(bsz, HEADS, 2 * nc, MCHUNK), colq


def _mlstm_kernel(q_ref, k_ref, v_ref, o_ref, kc_ref, vc_ref, cwq_ref, cbq_ref, cwk_ref, cbk_ref,
                  row_ref, col_ref, nw_ref, band_ref, edge_ref, y_ref, q_s, k_s, kc_s, ct_s, n_s, *, nc):
    lc = MCHUNK
    halo_rows = edge_ref.shape[1]

    def conv_silu_chunk(x_ref, c, n_chunks, w, b, scale):
        x = x_ref[pl.ds(c * lc, lc), :]
        wb = w.astype(BF16)
        taps = jnp.concatenate([x * wb[0:1, :], x * wb[1:2, :], x * wb[2:3, :]], axis=0)
        y = jnp.dot(band_ref[...], taps, preferred_element_type=F32)
        if n_chunks > 1:
            rid = lax.broadcasted_iota(jnp.int32, (halo_rows, x.shape[1]), 0)
            wf = wb.astype(F32)
            halo = jnp.zeros((halo_rows, x.shape[1]), F32)
            if c > 0:
                prev = x_ref[pl.ds(c * lc - halo_rows, halo_rows), :].astype(F32)[halo_rows - 1:, :]
                halo = jnp.where(rid == 0, prev * wf[0:1, :], halo)
            if c < n_chunks - 1:
                nxt = x_ref[pl.ds((c + 1) * lc, halo_rows), :].astype(F32)[0:1, :]
                halo = jnp.where(rid == 1, nxt * wf[2:3, :], halo)
            y = y + jnp.dot(edge_ref[...], halo.astype(BF16), preferred_element_type=F32)
        y = _silu(y + b)
        if scale != 1.0:
            y = y * scale
        return y.astype(BF16)

    k_scale = HD ** -0.5
    for c in range(nc):
        sl = pl.ds(c * lc, lc)
        q_s[sl, :] = conv_silu_chunk(q_ref, c, nc, cwq_ref[...], cbq_ref[...], 1.0)
        k_s[sl, :] = conv_silu_chunk(k_ref, c, nc, cwk_ref[...], cbk_ref[...], k_scale)
    kc_s[...] = conv_silu_chunk(kc_ref, 0, 1, cwk_ref[...], cbk_ref[...], k_scale)

    def col(j):
        return col_ref[0, 0, :, j:j + 1]

    stats_t = col_ref[0, 0].T

    def wrow(j):
        return stats_t[j:j + 1, :].astype(BF16)

    def local_state(kk, vv, w_row):
        ktw = kk.T * w_row
        ct = jnp.dot(ktw, vv, preferred_element_type=F32)
        nn = jnp.dot(jnp.broadcast_to(w_row, (SUBLANES, lc)), kk, preferred_element_type=F32)[0:1, :]
        return ct, nn

    for d in range(2):
        ct, nn = local_state(kc_s[...], vc_ref[...], wrow(8 * nc + d))
        order = list(range(nc)) if d == 0 else list(range(nc - 1, -1, -1))
        for pos, c in enumerate(order):
            idx = d * nc + c
            ct_s[idx] = ct.astype(BF16)
            n_s[idx] = nn
            if pos == nc - 1:
                break
            sl = pl.ds(c * lc, lc)
            ctl, nl = local_state(k_s[sl, :], v_ref[sl, :], wrow(idx))
            s_old = col_ref[0, 0, 0:1, 8 * nc + 2 + idx:8 * nc + 3 + idx]
            ct = s_old * ct + ctl
            nn = s_old * nn + nl

    tid = lax.broadcasted_iota(jnp.int32, (lc, lc), 0)
    sid = lax.broadcasted_iota(jnp.int32, (lc, lc), 1)
    masks = (sid <= tid, sid >= tid)
    for c in range(nc):
        sl = pl.ds(c * lc, lc)
        q = q_s[sl, :]
        kk = k_s[sl, :]
        v = v_ref[sl, :]
        qf = q.astype(F32)
        s = lax.dot_general(q, kk, (((1,), (1,)), ((), ())), preferred_element_type=F32)
        h = None
        for d in range(2):
            idx = d * nc + c
            r = row_ref[0, 0, idx:idx + 1, :]
            big_m = col(2 * nc + idx)
            s_int = col(4 * nc + idx)
            e_neg = col(6 * nc + idx)
            p = jnp.where(masks[d], jnp.exp(r - big_m), 0.0) * s
            den = (jnp.sum(p, axis=-1, keepdims=True)
                   + s_int * jnp.sum(qf * n_s[idx], axis=-1, keepdims=True))
            num = (jnp.dot(p.astype(BF16), v, preferred_element_type=F32)
                   + s_int * jnp.dot(q, ct_s[idx], preferred_element_type=F32))
            hd = num * (1.0 / jnp.maximum(jnp.abs(den), e_neg))
            h = hd if h is None else h + hd
        mu = jnp.mean(h, axis=-1, keepdims=True)
        hc = h - mu
        var = jnp.mean(hc * hc, axis=-1, keepdims=True)
        hn = hc * lax.rsqrt(var + LN_EPS) * nw_ref[...]
        y_ref[sl, :] = (hn * _sigmoid(o_ref[sl, :].astype(F32))).astype(BF16)


def _mlstm(qkvo, kv_ctx, conv_w, conv_b, rowq, colq, norm_w, bsz, seq, ctx_len):
    nc = seq // MCHUNK
    hq = D_MLSTM // HD
    kern = functools.partial(_mlstm_kernel, nc=nc)
    ii = jnp.arange(MCHUNK)[:, None]
    jj = jnp.arange(MCHUNK)[None, :]
    band = jnp.concatenate([(jj == ii + t - 1) for t in range(3)], axis=1).astype(BF16)
    hh = jnp.arange(2 * SUBLANES)[None, :]
    edge = (((ii == 0) & (hh == 0)) | ((ii == MCHUNK - 1) & (hh == 1))).astype(BF16)
    seq_blk = lambda off: pl.BlockSpec((seq, HD), lambda b, h: (b, off + h))
    ctx_blk = lambda off: pl.BlockSpec((ctx_len, HD), lambda b, h: (b, off + h))
    return pl.pallas_call(
        kern,
        grid=(bsz, HEADS),
        in_specs=[seq_blk(0), seq_blk(hq), seq_blk(2 * hq), seq_blk(3 * hq),
                  ctx_blk(0), ctx_blk(hq),
                  pl.BlockSpec((3, HD), lambda b, h: (0, h)),
                  pl.BlockSpec((1, HD), lambda b, h: (0, h)),
                  pl.BlockSpec((3, HD), lambda b, h: (0, hq + h)),
                  pl.BlockSpec((1, HD), lambda b, h: (0, hq + h)),
                  pl.BlockSpec((1, 1, 2 * nc, MCHUNK), lambda b, h: (b, h, 0, 0)),
                  pl.BlockSpec((1, 1, MCHUNK, LANE), lambda b, h: (b, h, 0, 0)),
                  pl.BlockSpec((1, HD), lambda b, h: (0, h)),
                  _resident(band.shape), _resident(edge.shape)],
        out_specs=pl.BlockSpec((seq, HD), lambda b, h: (b, h)),
        out_shape=jax.ShapeDtypeStruct((bsz * seq, D_MLSTM), BF16),
        scratch_shapes=[pltpu.VMEM((seq, HD), BF16), pltpu.VMEM((seq, HD), BF16),
                        pltpu.VMEM((ctx_len, HD), BF16),
                        pltpu.VMEM((2 * nc, HD, HD), BF16), pltpu.VMEM((2 * nc, 1, HD), F32)],
        compiler_params=_cparams(("parallel", "parallel")),
        name="mlstm",
    )(qkvo, qkvo, qkvo, qkvo, kv_ctx, kv_ctx, conv_w, conv_b, conv_w, conv_b, rowq, colq, norm_w,
      band, edge)


def _out_kernel(ym_ref, yc_ref, x_ref, pe_ref, m_ref, wout_ref,
                l1w_ref, l1b_ref, wr_ref, br_ref, x1_ref, h2_ref, lg_ref):
    m = m_ref[0]
    y = (jnp.dot(ym_ref[...], wout_ref[:D_MLSTM, :], preferred_element_type=F32)
         + jnp.dot(yc_ref[...], wout_ref[D_MLSTM:, :], preferred_element_type=F32))
    z = DEEPNORM_ALPHA * (x_ref[...] + pe_ref[...]) + m[2:3, :] * y
    x1 = _layer_norm_rows(z, l1w_ref[...], l1b_ref[...])
    x1_ref[...] = x1
    h2 = x1 * (1.0 + m[4:5, :]) + m[3:4, :]
    half = h2.shape[1] // 2
    packed = _pack_bf16_pairs(h2[:, :half], h2[:, half:])
    for c in range(ROW_LINES):
        h2_ref[pl.ds(c, h2.shape[0], stride=ROW_LINES), :] = packed[:, c * LANE:(c + 1) * LANE]
    lg = jnp.dot(h2.astype(BF16), wr_ref[...], preferred_element_type=F32) + br_ref[...]

    lgt = lg.T
    bm = lg.shape[0]
    epg = EXPERTS_PER_GROUP
    rid = lax.broadcasted_iota(jnp.int32, (SUBLANES, bm), 0).astype(F32)
    neg = -jnp.inf
    far = float(SUBLANES)
    gl = lgt[N_EXPERTS:N_EXPERTS + SUBLANES, :]
    is_grp = rid < N_GROUPS
    m1 = jnp.max(jnp.where(is_grp, gl, neg), axis=0, keepdims=True)
    grp = jnp.min(jnp.where(is_grp & (gl == m1), rid, far), axis=0, keepdims=True)
    p_grp = 1.0 / jnp.sum(jnp.where(is_grp, jnp.exp(gl - m1), 0.0), axis=0, keepdims=True)
    l2 = lgt[0:epg, :]
    for g in range(1, N_GROUPS):
        l2 = jnp.where(grp == g, lgt[g * epg:(g + 1) * epg, :], l2)
    v0 = jnp.max(l2, axis=0, keepdims=True)
    i0 = jnp.min(jnp.where(l2 == v0, rid, far), axis=0, keepdims=True)
    l2m = jnp.where(rid == i0, neg, l2)
    v1 = jnp.max(l2m, axis=0, keepdims=True)
    i1 = jnp.min(jnp.where(l2m == v1, rid, far), axis=0, keepdims=True)
    s1 = jnp.exp(v1 - v0)
    g0 = p_grp / (1.0 + s1)
    res = jnp.where(rid == 0, g0, jnp.where(rid == 1, g0 * s1, jnp.where(
        rid == 2, grp * epg + i0, jnp.where(rid == 3, grp * epg + i1, 0.0))))
    lg_ref[...] = jnp.concatenate([res, jnp.zeros((LANE - SUBLANES, bm), F32)], axis=0).T


def _mixer_out(ym, yc, x2d, pe, mods3, wout, l1w, l1b, wr, br, seq, bm):
    rows, dm = x2d.shape
    pe_blocks = seq // bm
    row_blk = lambda n: pl.BlockSpec((bm, n), lambda i: (i, 0))
    return pl.pallas_call(
        _out_kernel,
        grid=(rows // bm,),
        in_specs=[row_blk(D_MLSTM), row_blk(D_CMLP), row_blk(dm),
                  pl.BlockSpec((bm, dm), lambda i: (i % pe_blocks, 0)),
                  pl.BlockSpec((1, N_MOD, dm), lambda i: (i // pe_blocks, 0, 0)),
                  _resident(wout.shape), _resident(l1w.shape), _resident(l1b.shape),
                  _resident(wr.shape), _resident(br.shape)],
        out_specs=[row_blk(dm), pl.BlockSpec((bm * ROW_LINES, LANE), lambda i: (i, 0)), row_blk(LANE)],
        out_shape=[jax.ShapeDtypeStruct((rows, dm), F32),
                   jax.ShapeDtypeStruct((rows * ROW_LINES, LANE), jnp.uint32),
                   jax.ShapeDtypeStruct((rows, LANE), F32)],
        compiler_params=_cparams(("parallel",)),
        name="mixer_out",
    )(ym, yc, x2d, pe, mods3, wout, l1w, l1b, wr, br)


def _moe_kernel(comp_ref, blk_ref, slot_ref, cast_ref, cslot_ref, cexp_ref, ctile_ref,
                gtab_ref, stab_ref, h2_hbm, wgf_ref, wuf_ref, wdf_ref, ys_hbm,
                wg_s, wu_s, wd_s, xbuf, xb_s, ybuf, gsem, ssem, pend, *, nt):
    del blk_ref, cexp_ref
    s = pl.program_id(0)
    n_steps = pl.num_programs(0)
    groups = MOE_ROWS // SUBLANES
    rl = ROW_LINES
    half = rl * LANE
    dm = 2 * half
    n_asg = ys_hbm.shape[0] // rl - 2 * MOE_ROWS
    group_lines = SUBLANES * rl

    def gather_wait(p, n):
        nl = n * group_lines
        pltpu.make_async_copy(h2_hbm.at[pl.ds(0, nl)], xbuf.at[p, pl.ds(0, nl)], gsem.at[p]).wait()

    def scatter_wait(p, n):
        nl = n * group_lines
        pltpu.make_async_copy(ybuf.at[p, pl.ds(0, nl)], ys_hbm.at[pl.ds(0, nl)], ssem.at[p]).wait()

    @pl.when(s == 0)
    def _():
        pend[0] = 0
        pend[1] = 0
        xbuf[...] = jnp.zeros_like(xbuf)

    nxt = jnp.minimum(s + 1, n_steps - 1)

    @pl.when(jnp.logical_and(s + 1 < n_steps, comp_ref[nxt] > 0))
    def _():
        p = (s + 1) % 2

        for par in range(2):
            for g0 in range(0, groups, GROUP_STEP):
                @pl.when(jnp.logical_and(p == par, g0 < comp_ref[nxt]))
                def _():
                    for r in range(g0 * SUBLANES, (g0 + GROUP_STEP) * SUBLANES):
                        src = pl.multiple_of(gtab_ref[0, 0, r] * rl, rl)
                        pltpu.make_async_copy(h2_hbm.at[pl.ds(src, rl)], xbuf.at[par, pl.ds(r * rl, rl)],
                                              gsem.at[par]).start()

    @pl.when(cast_ref[s] == 1)
    def _():
        cs = cslot_ref[s]
        t = ctile_ref[s]
        for k in range(nt):
            @pl.when(t == k)
            def _():
                wg_s[cs, :, k * MOE_FT:(k + 1) * MOE_FT] = wgf_ref[0].astype(BF16)
                wu_s[cs, :, k * MOE_FT:(k + 1) * MOE_FT] = wuf_ref[0].astype(BF16)
        wd_s[cs, t] = wdf_ref[0].astype(BF16)

    @pl.when(comp_ref[s] > 0)
    def _():
        p = s % 2
        sl = slot_ref[s]
        ng = comp_ref[s]
        gather_wait(p, ng)

        @pl.when(pend[p] > 0)
        def _():
            scatter_wait(p, pend[p])

        def run_block(rows):
            for c in range(rl):
                x_lo, x_hi = _unpack_bf16_pairs(xbuf[p, pl.ds(c, rows, stride=rl), :])
                xb_s[:rows, c * LANE:(c + 1) * LANE] = x_lo.astype(BF16)
                xb_s[:rows, half + c * LANE:half + (c + 1) * LANE] = x_hi.astype(BF16)
            x = xb_s[:rows, :]
            g = jnp.dot(x, wg_s[sl], preferred_element_type=F32)
            u = jnp.dot(x, wu_s[sl], preferred_element_type=F32)
            h = (_silu(g) * u).astype(BF16)
            y = jnp.dot(h, wd_s[sl].reshape(D_EXPERT, dm), preferred_element_type=F32)
            y_packed = _pack_bf16_pairs(y[:, :half], y[:, half:])
            for c in range(rl):
                ybuf[p, pl.ds(c, rows, stride=rl), :] = y_packed[:, c * LANE:(c + 1) * LANE]

        small_groups = MOE_SMALL_ROWS // SUBLANES

        @pl.when(ng > small_groups)
        def _():
            run_block(MOE_ROWS)

        @pl.when(ng <= small_groups)
        def _():
            run_block(MOE_SMALL_ROWS)

        for par in range(2):
            for g0 in range(0, groups, GROUP_STEP):
                @pl.when(jnp.logical_and(p == par, g0 < ng))
                def _():
                    for r in range(g0 * SUBLANES, (g0 + GROUP_STEP) * SUBLANES):
                        d = stab_ref[0, 0, r]
                        dst = pl.multiple_of(jnp.where(d < 0, n_asg + par * MOE_ROWS + r, d) * rl, rl)
                        pltpu.make_async_copy(ybuf.at[par, pl.ds(r * rl, rl)], ys_hbm.at[pl.ds(dst, rl)],
                                              ssem.at[par]).start()

        pend[p] = ng

    @pl.when(s == n_steps - 1)
    def _():
        for p in range(2):
            @pl.when(pend[p] > 0)
            def _():
                scatter_wait(p, pend[p])
                pend[p] = 0

        xbuf[...] = jnp.zeros_like(xbuf)
        fills = [pltpu.make_async_copy(xbuf.at[p], ys_hbm.at[pl.ds((n_asg + p * MOE_ROWS) * rl, MOE_ROWS * rl)],
                                       gsem.at[p]) for p in range(2)]
        for cp in fills:
            cp.start()
        for cp in fills:
            cp.wait()


def _experts(h2, gtab, stab, sched, wg, wu, wd):
    n_tok, dm = h2.shape[0] // ROW_LINES, D_MODEL
    nt = D_EXPERT // MOE_FT
    n_asg = 2 * n_tok
    n_steps = sched[0].shape[0]

    smem_rows = lambda imap: pl.BlockSpec((1, 1, MOE_ROWS), imap, memory_space=pltpu.SMEM)
    grid_spec = pltpu.PrefetchScalarGridSpec(
        num_scalar_prefetch=7,
        grid=(n_steps,),
        in_specs=[smem_rows(lambda s, comp, blk, *_: (blk[jnp.minimum(s + 1, n_steps - 1)], 0, 0)),
                  smem_rows(lambda s, comp, blk, *_: (blk[s], 0, 0)),
                  pl.BlockSpec(memory_space=pl.ANY),
                  pl.BlockSpec((1, dm, MOE_FT), lambda s, c, b, sl, ca, cs, ce, ct: (ce[s], 0, ct[s])),
                  pl.BlockSpec((1, dm, MOE_FT), lambda s, c, b, sl, ca, cs, ce, ct: (ce[s], 0, ct[s])),
                  pl.BlockSpec((1, MOE_FT, dm), lambda s, c, b, sl, ca, cs, ce, ct: (ce[s], ct[s], 0))],
        out_specs=pl.BlockSpec(memory_space=pl.ANY),
        scratch_shapes=[pltpu.VMEM((2, dm, D_EXPERT), BF16), pltpu.VMEM((2, dm, D_EXPERT), BF16),
                        pltpu.VMEM((2, nt, MOE_FT, dm), BF16),
                        pltpu.VMEM((2, MOE_ROWS * ROW_LINES, LANE), jnp.uint32),
                        pltpu.VMEM((MOE_ROWS, dm), BF16),
                        pltpu.VMEM((2, MOE_ROWS * ROW_LINES, LANE), jnp.uint32),
                        pltpu.SemaphoreType.DMA((2,)), pltpu.SemaphoreType.DMA((2,)),
                        pltpu.SMEM((2,), jnp.int32)],
    )
    return pl.pallas_call(
        functools.partial(_moe_kernel, nt=nt),
        grid_spec=grid_spec,
        out_shape=jax.ShapeDtypeStruct(((n_asg + 2 * MOE_ROWS) * ROW_LINES, LANE), jnp.uint32),
        compiler_params=_cparams(("arbitrary",)),
        name="experts",
    )(*sched, gtab, stab, h2, wg, wu, wd)


def _final_kernel(x1_ref, y0_ref, y1_ref, g_ref, m_ref, w_ref, b_ref, o_ref, y_s):
    m = m_ref[0]
    bm, dm = x1_ref.shape
    half = dm // 2
    g0 = g_ref[:, 0:1]
    g1 = g_ref[:, 1:2]
    for c in range(ROW_LINES):
        rows_c = pl.ds(c, bm, stride=ROW_LINES)
        lo0, hi0 = _unpack_bf16_pairs(y0_ref[rows_c, :])
        lo1, hi1 = _unpack_bf16_pairs(y1_ref[rows_c, :])
        y_s[:, c * LANE:(c + 1) * LANE] = g0 * lo0 + g1 * lo1
        y_s[:, half + c * LANE:half + (c + 1) * LANE] = g0 * hi0 + g1 * hi1
    z = DEEPNORM_ALPHA * x1_ref[...] + m[5:6, :] * y_s[...]
    o_ref[...] = _layer_norm_rows(z, w_ref[...], b_ref[...])


def _final(x1, ys, gates, mods3, w, b, seq, bm):
    rows, dm = x1.shape
    blocks_per_batch = seq // bm
    slot_blocks = rows // bm
    row_blk = lambda n: pl.BlockSpec((bm, n), lambda i: (i, 0))
    return pl.pallas_call(
        _final_kernel,
        grid=(rows // bm,),
        in_specs=[row_blk(dm), pl.BlockSpec((bm * ROW_LINES, LANE), lambda i: (i, 0)),
                  pl.BlockSpec((bm * ROW_LINES, LANE), lambda i: (i + slot_blocks, 0)),
                  row_blk(LANE),
                  pl.BlockSpec((1, N_MOD, dm), lambda i: (i // blocks_per_batch, 0, 0)),
                  _resident(w.shape), _resident(b.shape)],
        out_specs=row_blk(dm),
        out_shape=jax.ShapeDtypeStruct((rows, dm), F32),
        scratch_shapes=[pltpu.VMEM((bm, dm), F32)],
        compiler_params=_cparams(("parallel",)),
        name="final_ln",
    )(x1, ys, ys, gates, mods3, w, b)


def _route(route, n_tok):
    e_flat = route[:, 2:4].astype(jnp.int32).reshape(-1)
    n_asg = e_flat.shape[0]
    earange = jnp.arange(N_EXPERTS, dtype=jnp.int32)
    counts = jnp.sum((e_flat[:, None] == earange[None, :]).astype(jnp.int32), 0)
    nblk_e = (counts + MOE_ROWS - 1) // MOE_ROWS
    pad_end = jnp.cumsum(nblk_e * MOE_ROWS)
    pad_start = pad_end - nblk_e * MOE_ROWS
    n_blk = n_asg // MOE_ROWS + N_EXPERTS

    order = jnp.argsort(e_flat, stable=True).astype(jnp.int32)
    starts = jnp.cumsum(counts) - counts
    ridx = jnp.arange(n_blk * MOE_ROWS, dtype=jnp.int32)
    e_row = jnp.minimum(jnp.sum((pad_end[None, :] <= ridx[:, None]).astype(jnp.int32), -1), N_EXPERTS - 1)
    rsel = e_row[:, None] == earange[None, :]
    k_row = ridx - jnp.sum(jnp.where(rsel, pad_start[None, :], 0), -1)
    valid = k_row < jnp.sum(jnp.where(rsel, counts[None, :], 0), -1)
    src = jnp.clip(jnp.sum(jnp.where(rsel, starts[None, :], 0), -1) + k_row, 0, n_asg - 1)
    row_asg = jnp.where(valid, order[src], -1)
    gtab = (jnp.maximum(row_asg, 0) >> 1).reshape(n_blk, 1, MOE_ROWS)
    stab = jnp.where(row_asg < 0, -1, (row_asg & 1) * n_tok + (row_asg >> 1)).reshape(n_blk, 1, MOE_ROWS)

    nt = D_EXPERT // MOE_FT
    n_steps = n_blk + (nt - 1) * N_EXPERTS + nt
    has = nblk_e > 0
    n_visits = jnp.sum(has.astype(jnp.int32))
    e_of_visit = jnp.sort(jnp.where(has, earange, N_EXPERTS))
    vsel = e_of_visit[:, None] == earange[None, :]
    nb_v = jnp.sum(jnp.where(vsel, nblk_e[None, :], 0), -1)
    steps_v = jnp.where(nb_v > 0, jnp.maximum(nb_v, nt), 0)
    end_v = nt + jnp.cumsum(steps_v)
    start_v = end_v - steps_v
    first_blk_v = jnp.cumsum(nb_v) - nb_v
    sidx = jnp.arange(n_steps, dtype=jnp.int32)
    v = jnp.sum((end_v[None, :] <= sidx[:, None]).astype(jnp.int32), -1)
    pick = lambda arr, idx: jnp.sum(jnp.where(idx[:, None] == earange[None, :], arr[None, :], 0), -1)
    in_visit = (sidx >= nt) & (v < n_visits)
    k = sidx - pick(start_v, v)
    comp = in_visit & (k < pick(nb_v, v))
    cnt_v = jnp.sum(jnp.where(vsel, counts[None, :], 0), -1)
    rows_s = jnp.clip(pick(cnt_v, v) - k * MOE_ROWS, 0, MOE_ROWS)
    chunk_rows = GROUP_STEP * SUBLANES
    groups_s = jnp.where(comp, (rows_s + chunk_rows - 1) // chunk_rows * GROUP_STEP, 0)
    blk_s = lax.cummax(jnp.where(comp, pick(first_blk_v, v) + k, 0), axis=0)
    prologue = sidx < nt
    cast = prologue | (in_visit & (k < nt) & (v + 1 < n_visits))
    cexp = jnp.where(prologue, e_of_visit[0], pick(e_of_visit, v + 1))
    ctile = jnp.where(prologue, sidx, k)
    code = lax.cummax(jnp.where(cast, cexp * nt + ctile, 0), axis=0)
    i32 = lambda a: a.astype(jnp.int32)
    sched = (i32(groups_s), i32(blk_s), i32(v % 2), i32(cast), i32(jnp.where(prologue, 0, (v + 1) % 2)),
             i32(jnp.minimum(code // nt, N_EXPERTS - 1)), i32(code % nt))
    return sched, gtab, stab


def _grid_pos_embed(rows):
    quarter = D_MODEL // 4
    omega = 1.0 / (10000.0 ** (jnp.arange(quarter, dtype=F32) / quarter))
    ar = jnp.arange(rows, dtype=F32)[:, None] * omega
    ac = jnp.arange(GRID_W, dtype=F32)[:, None] * omega
    shape = (rows, GRID_W, quarter)
    parts = [jnp.broadcast_to(jnp.sin(ar)[:, None, :], shape), jnp.broadcast_to(jnp.cos(ar)[:, None, :], shape),
             jnp.broadcast_to(jnp.sin(ac)[None, :, :], shape), jnp.broadcast_to(jnp.cos(ac)[None, :, :], shape)]
    return jnp.concatenate(parts, -1).reshape(rows * GRID_W, D_MODEL)


def kernel(x, c, ctx, c_ctx, w_mod, b_mod, w_in, conv_w, conv_b, gate_bias, mlstm_norm_w, cmlp_norm_w,
           w_s, b_s, w_out, ln1_w, ln1_b, router1_w, router1_b, router2_w, router2_b, w_gate, w_up,
           w_down, ln2_w, ln2_b):
    bsz, seq, dm = x.shape
    ctx_len = ctx.shape[1]
    n_tok = bsz * seq
    assert w_mod.shape[0] == 1 and dm == D_MODEL and seq % MCHUNK == 0 and ctx_len == MCHUNK
    pe = _grid_pos_embed(seq // GRID_W).astype(x.dtype)
    x2d = x.reshape(n_tok, dm)
    ctx2d = ctx.reshape(bsz * ctx_len, dm)

    mod_rows = 16
    cc = jnp.concatenate([c, c_ctx[None, :], jnp.zeros((mod_rows - bsz - 1, dm), c.dtype)], 0)
    mods3 = _modulation(cc, w_mod[0], b_mod[0]).reshape(mod_rows, N_MOD, dm)

    dq = D_MLSTM
    wi = w_in[0]
    w_qkvo = wi[:, :4 * dq].astype(BF16)
    w_g = jnp.pad(wi[:, 4 * dq:4 * dq + N_GATE_COLS], ((0, 0), (0, LANE - N_GATE_COLS))).astype(BF16)
    w_uv = wi[:, 4 * dq + N_GATE_COLS:].astype(BF16)
    bm_proj = 256
    blocks_per_seq = seq // bm_proj
    cmlp = (cmlp_norm_w[0].reshape(1, -1), w_s[0].astype(BF16), b_s[0].T)
    qkvo, g_x, yc = _projection(x2d, pe, mods3, lambda i: i // blocks_per_seq,
                                [w_qkvo, w_g, w_uv], [BF16, F32, BF16], [False, False, True], bm_proj,
                                cmlp=cmlp)
    w_kv = wi[:, dq:3 * dq].astype(BF16)
    kv_c, g_c = _projection(ctx2d, None, mods3, lambda i: bsz, [w_kv, w_g], [BF16, F32],
                            [False, False], bm_proj)

    rowq, colq = _gate_stats(g_x, g_c, gate_bias[0], bsz, seq, ctx_len)
    ym = _mlstm(qkvo, kv_c, conv_w[0], conv_b[0].reshape(1, -1), rowq, colq,
                mlstm_norm_w[0].reshape(1, -1), bsz, seq, ctx_len)

    wr = jnp.pad(jnp.concatenate([router2_w[0], router1_w[0]], 1),
                 ((0, 0), (0, LANE - N_GROUPS - N_EXPERTS))).astype(BF16)
    br = jnp.pad(jnp.concatenate([router2_b[0], router1_b[0]], 0),
                 (0, LANE - N_GROUPS - N_EXPERTS)).reshape(1, LANE)
    x1, h2, route = _mixer_out(ym, yc, x2d, pe, mods3, w_out[0].astype(BF16),
                               ln1_w[0].reshape(1, -1), ln1_b[0].reshape(1, -1), wr, br, seq, 512)

    sched, gtab, stab = _route(route, n_tok)
    ys = _experts(h2, gtab, stab, sched, w_gate[0], w_up[0], w_down[0])
    out = _final(x1, ys, route, mods3, ln2_w[0].reshape(1, -1),
                 ln2_b[0].reshape(1, -1), seq, 512)
    return out.reshape(bsz, seq, dm)
```

```python
import functools

import jax
import jax.numpy as jnp
from jax import lax
from jax.experimental import pallas as pl
from jax.experimental.pallas import tpu as pltpu

F32 = jnp.float32
BF16 = jnp.bfloat16

D_MODEL = 2048
GRID_W = 64
D_MLSTM = 1024
D_CMLP = 1024
HEADS = 4
HD = 256
CMLP_GROUPS = 4
CMLP_GD = 256
CMLP_CHUNK = 128
N_GROUPS = 4
EXPERTS_PER_GROUP = 8
N_EXPERTS = 32
D_EXPERT = 1024
N_MOD = 6
N_GATE_COLS = 16
DEEPNORM_ALPHA = 2.0 ** 0.25
LN_EPS = 1e-6

LANE = 128
SUBLANES = 8
MCHUNK = 256
MOE_ROWS = 256
MOE_SMALL_ROWS = 64
MOE_FT = 256
GROUP_STEP = 4
HEAD_LANES = LANE // HEADS
ROW_LINES = D_MODEL // 2 // LANE
VMEM_LIMIT = 56 * 1024 * 1024


def _cparams(sem):
    return pltpu.CompilerParams(dimension_semantics=sem, vmem_limit_bytes=VMEM_LIMIT)


def _resident(shape):
    nd = len(shape)
    return pl.BlockSpec(shape, lambda *_: (0,) * nd, pipeline_mode=pl.Buffered(1))


def _sigmoid(x):
    return 0.5 * jnp.tanh(0.5 * x) + 0.5


def _silu(x):
    return x * _sigmoid(x)


def _log_sigmoid(x):
    return jnp.minimum(x, 0.0) - jnp.log1p(jnp.exp(-jnp.abs(x)))


def _gelu_tanh(x):
    c = 0.7978845608028654
    return 0.5 * x * (1.0 + jnp.tanh(c * (x + 0.044715 * (x * x * x))))


def _pack_bf16_pairs(lo, hi):
    lo_b = lax.bitcast_convert_type(lo.astype(BF16).astype(F32), jnp.uint32)
    hi_b = lax.bitcast_convert_type(hi.astype(BF16).astype(F32), jnp.uint32)
    return (lo_b >> 16) | (hi_b & jnp.uint32(0xFFFF0000))


def _unpack_bf16_pairs(w):
    lo = lax.bitcast_convert_type(w << 16, F32)
    hi = lax.bitcast_convert_type(w & jnp.uint32(0xFFFF0000), F32)
    return lo, hi


def _layer_norm_rows(z, w, b):
    mu = jnp.mean(z, axis=-1, keepdims=True)
    zc = z - mu
    var = jnp.mean(zc * zc, axis=-1, keepdims=True)
    return zc * lax.rsqrt(var + LN_EPS) * w + b


def _mod_kernel(c_ref, w_ref, b_ref, o_ref):
    s = _silu(c_ref[...]).astype(BF16)
    o_ref[...] = jnp.dot(s, w_ref[...].astype(BF16), preferred_element_type=F32) + b_ref[...]


def _modulation(cc, w_mod, b_mod):
    rows, dm = cc.shape
    n = w_mod.shape[1]
    tn = 1024
    return pl.pallas_call(
        _mod_kernel,
        grid=(n // tn,),
        in_specs=[pl.BlockSpec((rows, dm), lambda j: (0, 0)),
                  pl.BlockSpec((dm, tn), lambda j: (0, j)),
                  pl.BlockSpec((1, tn), lambda j: (0, j))],
        out_specs=pl.BlockSpec((rows, tn), lambda j: (0, j)),
        out_shape=jax.ShapeDtypeStruct((rows, n), F32),
        compiler_params=_cparams(("arbitrary",)),
        name="modulation",
    )(cc, w_mod, b_mod.reshape(1, n))


def _proj_kernel(*refs, n_w, n_out, has_pe, gelu_flags, has_cmlp, tn):
    x_ref = refs[0]
    k = 1
    pe_ref = None
    if has_pe:
        pe_ref = refs[k]
        k += 1
    m_ref = refs[k]
    k += 1
    w_refs = refs[k:k + n_w]
    k += n_w
    if has_cmlp:
        cnw_ref, ws_ref, bs_ref = refs[k:k + 3]
        k += 3
    o_refs = refs[k:k + n_out]
    hx_ref = refs[k + n_out]
    x = x_ref[...]
    if has_pe:
        x = x + pe_ref[...]
    m = m_ref[0]
    hx_ref[...] = (x * (1.0 + m[1:2, :]) + m[0:1, :]).astype(BF16)
    n_plain = n_w - 1 if has_cmlp else n_w
    for w_ref, o_ref, use_gelu in list(zip(w_refs, o_refs, gelu_flags))[:n_plain]:
        n = w_ref.shape[1]
        step = min(tn, n)
        for j in range(0, n, step):
            acc = jnp.dot(hx_ref[...], w_ref[:, j:j + step], preferred_element_type=F32)
            if use_gelu:
                acc = _gelu_tanh(acc)
            o_ref[:, j:j + step] = acc.astype(o_ref.dtype)
    if has_cmlp:
        w_all, g_ref, o_ref = w_refs[-1], o_refs[-2], o_refs[-1]
        s_ref, w_ref = refs[k + n_out + 1], refs[k + n_out + 2]
        bm = x_ref.shape[0]

        @pl.when(pl.program_id(0) == 0)
        def _():
            w_ref[...] = w_all[:, N_GATE_COLS:N_GATE_COLS + 2 * D_CMLP]

        g_ref[...] = jnp.dot(hx_ref[...], w_all[:, :LANE], preferred_element_type=F32)
        vgate = _gelu_tanh(jnp.dot(hx_ref[...], w_ref[:, D_CMLP:], preferred_element_type=F32))
        for g in range(CMLP_GROUPS):
            gs = slice(g * CMLP_GD, (g + 1) * CMLP_GD)
            vg = vgate[:, gs]
            mu = jnp.mean(vg, axis=-1, keepdims=True)
            vc = vg - mu
            var = jnp.mean(vc * vc, axis=-1, keepdims=True)
            vn = (vc * lax.rsqrt(var + LN_EPS) * cnw_ref[:, gs]).astype(BF16)
            for p in range(bm // CMLP_CHUNK):
                ps = slice(p * CMLP_CHUNK, (p + 1) * CMLP_CHUNK)
                s_ref[ps, gs] = (jnp.dot(ws_ref[g], vn[ps, :], preferred_element_type=F32)
                                 + bs_ref[:, g:g + 1])
        u = _gelu_tanh(jnp.dot(hx_ref[...], w_ref[:, :D_CMLP], preferred_element_type=F32))
        o_ref[...] = (u * s_ref[...]).astype(o_ref.dtype)


def _projection(x2d, pe, mods3, mod_row_of_block, weights, out_dtypes, gelu_flags, bm, cmlp=None):
    rows, dm = x2d.shape
    has_pe = pe is not None
    n_w = len(weights)
    in_specs = [pl.BlockSpec((bm, dm), lambda i: (i, 0))]
    args = [x2d]
    if has_pe:
        pe_blocks = pe.shape[0] // bm
        in_specs.append(pl.BlockSpec((bm, dm), lambda i: (i % pe_blocks, 0)))
        args.append(pe)
    in_specs.append(pl.BlockSpec((1, N_MOD, dm), lambda i: (mod_row_of_block(i), 0, 0)))
    args.append(mods3)
    for w in weights:
        in_specs.append(_resident(w.shape))
        args.append(w)
    out_widths = [w.shape[1] for w in weights]
    scratch = [pltpu.VMEM((bm, dm), BF16)]
    if cmlp is not None:
        for a in cmlp:
            in_specs.append(_resident(a.shape))
            args.append(a)
        out_widths[-1:] = [LANE, D_CMLP]
        scratch.append(pltpu.VMEM((bm, D_CMLP), F32))
        scratch.append(pltpu.VMEM((dm, 2 * D_CMLP), BF16))
    out_specs = [pl.BlockSpec((bm, n), lambda i: (i, 0)) for n in out_widths]
    out_shape = [jax.ShapeDtypeStruct((rows, n), dt) for n, dt in zip(out_widths, out_dtypes)]
    kern = functools.partial(_proj_kernel, n_w=n_w, n_out=len(out_widths), has_pe=has_pe,
                             gelu_flags=tuple(gelu_flags), has_cmlp=cmlp is not None, tn=1024)
    return pl.pallas_call(
        kern,
        grid=(rows // bm,),
        in_specs=in_specs,
        out_specs=out_specs,
        out_shape=out_shape,
        scratch_shapes=scratch,
        compiler_params=_cparams(("arbitrary",)),
        name="projection",
    )(*args)


def _gate_kernel(lic_ref, lfc_ref, lir_ref, lfr_ref, row_ref, col_ref, *, nc):
    hl = HEAD_LANES
    li = lic_ref[0]
    lf = _log_sigmoid(lfc_ref[0])
    length = li.shape[0]
    tid = lax.broadcasted_iota(jnp.int32, li.shape, 0)
    lane = lax.broadcasted_iota(jnp.int32, li.shape, 1)
    lane_l = lane & (hl - 1)
    lane1 = lane_l[0:1, :]
    fwd = (lane_l < nc) | (lane_l == 2 * nc)

    def scan_sublanes(x, op, fill):
        p = x
        s = x
        k = 1
        while k < length:
            p = op(p, jnp.where(tid >= k, pltpu.roll(p, k, 0), fill))
            s = op(s, jnp.where(tid < length - k, pltpu.roll(s, length - k, 0), fill))
            k *= 2
        return jnp.where(fwd, p, s)

    b = scan_sublanes(lf, jnp.add, 0.0)
    btot = jnp.sum(lf, axis=0, keepdims=True)
    a = btot - b + li
    m_loc = jnp.max(a, axis=0, keepdims=True)
    r = li - b
    cm = scan_sublanes(r, jnp.maximum, -jnp.inf)

    m_ctx = jnp.maximum(btot, m_loc)
    m_in = jnp.where(lane1 == 0, pltpu.roll(m_ctx, LANE - 2 * nc, 1), pltpu.roll(m_ctx, LANE - 2, 1))
    for k in range(nc - 1):
        m_new = jnp.maximum(btot + m_in, m_loc)
        m_in = jnp.where(lane1 == k + 1, pltpu.roll(m_new, 1, 1),
                         jnp.where(lane1 == 2 * nc - 2 - k, pltpu.roll(m_new, LANE - 1, 1), m_in))
    is_ctx = lane1 >= 2 * nc
    m_in = jnp.where(is_ctx, 0.0, m_in)
    m_new = jnp.maximum(btot + m_in, m_loc)
    s_old = jnp.broadcast_to(jnp.exp(btot + m_in - m_new), li.shape)
    w = jnp.exp(a - m_new)
    big_m = jnp.maximum(m_in, cm)
    s_int = jnp.exp(m_in - big_m)
    e_neg = jnp.exp(-(b + big_m))
    g = 2 * nc
    for h in range(HEADS):
        off = h * hl
        rot = lambda x, to: pltpu.roll(x, (to - off) % LANE, 1)
        col_ref[0, h] = jnp.where(
            lane < g, rot(w, 0), jnp.where(
                lane < 2 * g, rot(big_m, g), jnp.where(
                    lane < 3 * g, rot(s_int, 2 * g), jnp.where(
                        lane < 4 * g, rot(e_neg, 3 * g), jnp.where(
                            lane < 4 * g + 2, rot(w, 3 * g), rot(s_old, 4 * g + 2))))))

    lir = lir_ref[0]
    lfr = _log_sigmoid(lfr_ref[0])
    width = lir.shape[1]
    rid = lax.broadcasted_iota(jnp.int32, lir.shape, 0) & (2 * nc - 1)
    pid = lax.broadcasted_iota(jnp.int32, lir.shape, 1)
    p = lfr
    s = lfr
    k = 1
    while k < width:
        p = p + jnp.where(pid >= k, pltpu.roll(p, k, 1), 0.0)
        s = s + jnp.where(pid < width - k, pltpu.roll(s, width - k, 1), 0.0)
        k *= 2
    row_ref[0] = lir - jnp.where(rid < nc, p, s)


def _gate_stats(g_x, g_c, gate_bias, bsz, seq, ctx_len):
    nc = seq // MCHUNK
    assert HEADS * HEAD_LANES == LANE and 2 * nc + 2 <= HEAD_LANES and (2 * nc) & (2 * nc - 1) == 0
    gb = gate_bias.astype(F32)
    gx = g_x[:, :N_GATE_COLS].reshape(bsz, nc, MCHUNK, 2, 2, HEADS) + gb.reshape(2, 2, HEADS)
    gc = g_c[:, :N_GATE_COLS].reshape(bsz, ctx_len, 2, 2, HEADS) + gb.reshape(2, 2, HEADS)
    col_x = gx.transpose(4, 0, 2, 5, 3, 1).reshape(2, bsz, MCHUNK, HEADS, 2 * nc)
    col_c = gc.transpose(3, 0, 1, 4, 2)
    col = jnp.concatenate([col_x, col_c], -1)
    col = jnp.pad(col, ((0, 0),) * 4 + ((0, HEAD_LANES - col.shape[-1]),)).reshape(2, bsz, MCHUNK, LANE)
    row = gx.transpose(4, 0, 5, 3, 1, 2).reshape(2, bsz, HEADS * 2 * nc, MCHUNK)
    blk_c = pl.BlockSpec((1, MCHUNK, LANE), lambda b: (b, 0, 0))
    blk_r = pl.BlockSpec((1, HEADS * 2 * nc, MCHUNK), lambda b: (b, 0, 0))
    rowq, colq = pl.pallas_call(
        functools.partial(_gate_kernel, nc=nc),
        grid=(bsz,),
        in_specs=[blk_c, blk_c, blk_r, blk_r],
        out_specs=[blk_r, pl.BlockSpec((1, HEADS, MCHUNK, LANE), lambda b: (b, 0, 0, 0))],
        out_shape=[jax.ShapeDtypeStruct((bsz, HEADS * 2 * nc, MCHUNK), F32),
                   jax.ShapeDtypeStruct((bsz, HEADS, MCHUNK, LANE), F32)],
        compiler_params=_cparams(("parallel",)),
        name="gate_stats",
    )(col[0], col[1], row[0], row[1])
    return rowq.reshape(bsz, HEADS, 2 * nc, MCHUNK), colq


def _mlstm_kernel(q_ref, k_ref, v_ref, o_ref, kc_ref, vc_ref, cwq_ref, cbq_ref, cwk_ref, cbk_ref,
                  row_ref, col_ref, nw_ref, band_ref, edge_ref, y_ref, q_s, k_s, kc_s, ct_s, n_s, *, nc):
    lc = MCHUNK
    halo_rows = edge_ref.shape[1]

    def conv_silu_chunk(x_ref, c, n_chunks, w, b, scale):
        x = x_ref[pl.ds(c * lc, lc), :]
        wb = w.astype(BF16)
        taps = jnp.concatenate([x * wb[0:1, :], x * wb[1:2, :], x * wb[2:3, :]], axis=0)
        y = jnp.dot(band_ref[...], taps, preferred_element_type=F32)
        if n_chunks > 1:
            rid = lax.broadcasted_iota(jnp.int32, (halo_rows, x.shape[1]), 0)
            wf = wb.astype(F32)
            halo = jnp.zeros((halo_rows, x.shape[1]), F32)
            if c > 0:
                prev = x_ref[pl.ds(c * lc - halo_rows, halo_rows), :].astype(F32)[halo_rows - 1:, :]
                halo = jnp.where(rid == 0, prev * wf[0:1, :], halo)
            if c < n_chunks - 1:
                nxt = x_ref[pl.ds((c + 1) * lc, halo_rows), :].astype(F32)[0:1, :]
                halo = jnp.where(rid == 1, nxt * wf[2:3, :], halo)
            y = y + jnp.dot(edge_ref[...], halo.astype(BF16), preferred_element_type=F32)
        y = _silu(y + b)
        if scale != 1.0:
            y = y * scale
        return y.astype(BF16)

    k_scale = HD ** -0.5
    for c in range(nc):
        sl = pl.ds(c * lc, lc)
        q_s[sl, :] = conv_silu_chunk(q_ref, c, nc, cwq_ref[...], cbq_ref[...], 1.0)
        k_s[sl, :] = conv_silu_chunk(k_ref, c, nc, cwk_ref[...], cbk_ref[...], k_scale)
    kc_s[...] = conv_silu_chunk(kc_ref, 0, 1, cwk_ref[...], cbk_ref[...], k_scale)

    def col(j):
        return col_ref[0, 0, :, j:j + 1]

    stats_t = col_ref[0, 0].T

    def wrow(j):
        return stats_t[j:j + 1, :].astype(BF16)

    def local_state(kk, vv, w_row):
        ktw = kk.T * w_row
        ct = jnp.dot(ktw, vv, preferred_element_type=F32)
        nn = jnp.dot(jnp.broadcast_to(w_row, (SUBLANES, lc)), kk, preferred_element_type=F32)[0:1, :]
        return ct, nn

    for d in range(2):
        ct, nn = local_state(kc_s[...], vc_ref[...], wrow(8 * nc + d))
        order = list(range(nc)) if d == 0 else list(range(nc - 1, -1, -1))
        for pos, c in enumerate(order):
            idx = d * nc + c
            ct_s[idx] = ct.astype(BF16)
            n_s[idx] = nn
            if pos == nc - 1:
                break
            sl = pl.ds(c * lc, lc)
            ctl, nl = local_state(k_s[sl, :], v_ref[sl, :], wrow(idx))
            s_old = col_ref[0, 0, 0:1, 8 * nc + 2 + idx:8 * nc + 3 + idx]
            ct = s_old * ct + ctl
            nn = s_old * nn + nl

    tid = lax.broadcasted_iota(jnp.int32, (lc, lc), 0)
    sid = lax.broadcasted_iota(jnp.int32, (lc, lc), 1)
    masks = (sid <= tid, sid >= tid)
    for c in range(nc):
        sl = pl.ds(c * lc, lc)
        q = q_s[sl, :]
        kk = k_s[sl, :]
        v = v_ref[sl, :]
        qf = q.astype(F32)
        s = lax.dot_general(q, kk, (((1,), (1,)), ((), ())), preferred_element_type=F32)
        h = None
        for d in range(2):
            idx = d * nc + c
            r = row_ref[0, 0, idx:idx + 1, :]
            big_m = col(2 * nc + idx)
            s_int = col(4 * nc + idx)
            e_neg = col(6 * nc + idx)
            p = jnp.where(masks[d], jnp.exp(r - big_m), 0.0) * s
            den = (jnp.sum(p, axis=-1, keepdims=True)
                   + s_int * jnp.sum(qf * n_s[idx], axis=-1, keepdims=True))
            num = (jnp.dot(p.astype(BF16), v, preferred_element_type=F32)
                   + s_int * jnp.dot(q, ct_s[idx], preferred_element_type=F32))
            hd = num * (1.0 / jnp.maximum(jnp.abs(den), e_neg))
            h = hd if h is None else h + hd
        mu = jnp.mean(h, axis=-1, keepdims=True)
        hc = h - mu
        var = jnp.mean(hc * hc, axis=-1, keepdims=True)
        hn = hc * lax.rsqrt(var + LN_EPS) * nw_ref[...]
        y_ref[sl, :] = (hn * _sigmoid(o_ref[sl, :].astype(F32))).astype(BF16)


def _mlstm(qkvo, kv_ctx, conv_w, conv_b, rowq, colq, norm_w, bsz, seq, ctx_len):
    nc = seq // MCHUNK
    hq = D_MLSTM // HD
    kern = functools.partial(_mlstm_kernel, nc=nc)
    ii = jnp.arange(MCHUNK)[:, None]
    jj = jnp.arange(MCHUNK)[None, :]
    band = jnp.concatenate([(jj == ii + t - 1) for t in range(3)], axis=1).astype(BF16)
    hh = jnp.arange(2 * SUBLANES)[None, :]
    edge = (((ii == 0) & (hh == 0)) | ((ii == MCHUNK - 1) & (hh == 1))).astype(BF16)
    seq_blk = lambda off: pl.BlockSpec((seq, HD), lambda b, h: (b, off + h))
    ctx_blk = lambda off: pl.BlockSpec((ctx_len, HD), lambda b, h: (b, off + h))
    return pl.pallas_call(
        kern,
        grid=(bsz, HEADS),
        in_specs=[seq_blk(0), seq_blk(hq), seq_blk(2 * hq), seq_blk(3 * hq),
                  ctx_blk(0), ctx_blk(hq),
                  pl.BlockSpec((3, HD), lambda b, h: (0, h)),
                  pl.BlockSpec((1, HD), lambda b, h: (0, h)),
                  pl.BlockSpec((3, HD), lambda b, h: (0, hq + h)),
                  pl.BlockSpec((1, HD), lambda b, h: (0, hq + h)),
                  pl.BlockSpec((1, 1, 2 * nc, MCHUNK), lambda b, h: (b, h, 0, 0)),
                  pl.BlockSpec((1, 1, MCHUNK, LANE), lambda b, h: (b, h, 0, 0)),
                  pl.BlockSpec((1, HD), lambda b, h: (0, h)),
                  _resident(band.shape), _resident(edge.shape)],
        out_specs=pl.BlockSpec((seq, HD), lambda b, h: (b, h)),
        out_shape=jax.ShapeDtypeStruct((bsz * seq, D_MLSTM), BF16),
        scratch_shapes=[pltpu.VMEM((seq, HD), BF16), pltpu.VMEM((seq, HD), BF16),
                        pltpu.VMEM((ctx_len, HD), BF16),
                        pltpu.VMEM((2 * nc, HD, HD), BF16), pltpu.VMEM((2 * nc, 1, HD), F32)],
        compiler_params=_cparams(("parallel", "parallel")),
        name="mlstm",
    )(qkvo, qkvo, qkvo, qkvo, kv_ctx, kv_ctx, conv_w, conv_b, conv_w, conv_b, rowq, colq, norm_w,
      band, edge)


def _out_kernel(ym_ref, yc_ref, x_ref, pe_ref, m_ref, wout_ref,
                l1w_ref, l1b_ref, wr_ref, br_ref, x1_ref, h2_ref, lg_ref):
    m = m_ref[0]
    y = (jnp.dot(ym_ref[...], wout_ref[:D_MLSTM, :], preferred_element_type=F32)
         + jnp.dot(yc_ref[...], wout_ref[D_MLSTM:, :], preferred_element_type=F32))
    z = DEEPNORM_ALPHA * (x_ref[...] + pe_ref[...]) + m[2:3, :] * y
    x1 = _layer_norm_rows(z, l1w_ref[...], l1b_ref[...])
    x1_ref[...] = x1
    h2 = x1 * (1.0 + m[4:5, :]) + m[3:4, :]
    half = h2.shape[1] // 2
    packed = _pack_bf16_pairs(h2[:, :half], h2[:, half:])
    for c in range(ROW_LINES):
        h2_ref[pl.ds(c, h2.shape[0], stride=ROW_LINES), :] = packed[:, c * LANE:(c + 1) * LANE]
    lg = jnp.dot(h2.astype(BF16), wr_ref[...], preferred_element_type=F32) + br_ref[...]

    lgt = lg.T
    bm = lg.shape[0]
    epg = EXPERTS_PER_GROUP
    rid = lax.broadcasted_iota(jnp.int32, (SUBLANES, bm), 0).astype(F32)
    neg = -jnp.inf
    far = float(SUBLANES)
    gl = lgt[N_EXPERTS:N_EXPERTS + SUBLANES, :]
    is_grp = rid < N_GROUPS
    m1 = jnp.max(jnp.where(is_grp, gl, neg), axis=0, keepdims=True)
    grp = jnp.min(jnp.where(is_grp & (gl == m1), rid, far), axis=0, keepdims=True)
    p_grp = 1.0 / jnp.sum(jnp.where(is_grp, jnp.exp(gl - m1), 0.0), axis=0, keepdims=True)
    l2 = lgt[0:epg, :]
    for g in range(1, N_GROUPS):
        l2 = jnp.where(grp == g, lgt[g * epg:(g + 1) * epg, :], l2)
    v0 = jnp.max(l2, axis=0, keepdims=True)
    i0 = jnp.min(jnp.where(l2 == v0, rid, far), axis=0, keepdims=True)
    l2m = jnp.where(rid == i0, neg, l2)
    v1 = jnp.max(l2m, axis=0, keepdims=True)
    i1 = jnp.min(jnp.where(l2m == v1, rid, far), axis=0, keepdims=True)
    s1 = jnp.exp(v1 - v0)
    g0 = p_grp / (1.0 + s1)
    res = jnp.where(rid == 0, g0, jnp.where(rid == 1, g0 * s1, jnp.where(
        rid == 2, grp * epg + i0, jnp.where(rid == 3, grp * epg + i1, 0.0))))
    lg_ref[...] = jnp.concatenate([res, jnp.zeros((LANE - SUBLANES, bm), F32)], axis=0).T


def _mixer_out(ym, yc, x2d, pe, mods3, wout, l1w, l1b, wr, br, seq, bm):
    rows, dm = x2d.shape
    pe_blocks = seq // bm
    row_blk = lambda n: pl.BlockSpec((bm, n), lambda i: (i, 0))
    return pl.pallas_call(
        _out_kernel,
        grid=(rows // bm,),
        in_specs=[row_blk(D_MLSTM), row_blk(D_CMLP), row_blk(dm),
                  pl.BlockSpec((bm, dm), lambda i: (i % pe_blocks, 0)),
                  pl.BlockSpec((1, N_MOD, dm), lambda i: (i // pe_blocks, 0, 0)),
                  _resident(wout.shape), _resident(l1w.shape), _resident(l1b.shape),
                  _resident(wr.shape), _resident(br.shape)],
        out_specs=[row_blk(dm), pl.BlockSpec((bm * ROW_LINES, LANE), lambda i: (i, 0)), row_blk(LANE)],
        out_shape=[jax.ShapeDtypeStruct((rows, dm), F32),
                   jax.ShapeDtypeStruct((rows * ROW_LINES, LANE), jnp.uint32),
                   jax.ShapeDtypeStruct((rows, LANE), F32)],
        compiler_params=_cparams(("parallel",)),
        name="mixer_out",
    )(ym, yc, x2d, pe, mods3, wout, l1w, l1b, wr, br)


def _moe_kernel(comp_ref, blk_ref, slot_ref, cast_ref, cslot_ref, cexp_ref, ctile_ref,
                gtab_ref, stab_ref, h2_hbm, wgf_ref, wuf_ref, wdf_ref, ys_hbm,
                wg_s, wu_s, wd_s, xbuf, xb_s, ybuf, gsem, ssem, pend, *, nt):
    del blk_ref, cexp_ref
    s = pl.program_id(0)
    n_steps = pl.num_programs(0)
    groups = MOE_ROWS // SUBLANES
    rl = ROW_LINES
    half = rl * LANE
    dm = 2 * half
    n_asg = ys_hbm.shape[0] // rl - 2 * MOE_ROWS
    group_lines = SUBLANES * rl

    def gather_wait(p, n):
        nl = n * group_lines
        pltpu.make_async_copy(h2_hbm.at[pl.ds(0, nl)], xbuf.at[p, pl.ds(0, nl)], gsem.at[p]).wait()

    def scatter_wait(p, n):
        nl = n * group_lines
        pltpu.make_async_copy(ybuf.at[p, pl.ds(0, nl)], ys_hbm.at[pl.ds(0, nl)], ssem.at[p]).wait()

    @pl.when(s == 0)
    def _():
        pend[0] = 0
        pend[1] = 0
        xbuf[...] = jnp.zeros_like(xbuf)

    nxt = jnp.minimum(s + 1, n_steps - 1)

    @pl.when(jnp.logical_and(s + 1 < n_steps, comp_ref[nxt] > 0))
    def _():
        p = (s + 1) % 2

        for par in range(2):
            for g0 in range(0, groups, GROUP_STEP):
                @pl.when(jnp.logical_and(p == par, g0 < comp_ref[nxt]))
                def _():
                    for r in range(g0 * SUBLANES, (g0 + GROUP_STEP) * SUBLANES):
                        src = pl.multiple_of(gtab_ref[0, 0, r] * rl, rl)
                        pltpu.make_async_copy(h2_hbm.at[pl.ds(src, rl)], xbuf.at[par, pl.ds(r * rl, rl)],
                                              gsem.at[par]).start()

    @pl.when(cast_ref[s] == 1)
    def _():
        cs = cslot_ref[s]
        t = ctile_ref[s]
        for k in range(nt):
            @pl.when(t == k)
            def _():
                wg_s[cs, :, k * MOE_FT:(k + 1) * MOE_FT] = wgf_ref[0].astype(BF16)
                wu_s[cs, :, k * MOE_FT:(k + 1) * MOE_FT] = wuf_ref[0].astype(BF16)
        wd_s[cs, t] = wdf_ref[0].astype(BF16)

    @pl.when(comp_ref[s] > 0)
    def _():
        p = s % 2
        sl = slot_ref[s]
        ng = comp_ref[s]
        gather_wait(p, ng)

        @pl.when(pend[p] > 0)
        def _():
            scatter_wait(p, pend[p])

        def run_block(rows):
            for c in range(rl):
                x_lo, x_hi = _unpack_bf16_pairs(xbuf[p, pl.ds(c, rows, stride=rl), :])
                xb_s[:rows, c * LANE:(c + 1) * LANE] = x_lo.astype(BF16)
                xb_s[:rows, half + c * LANE:half + (c + 1) * LANE] = x_hi.astype(BF16)
            x = xb_s[:rows, :]
            g = jnp.dot(x, wg_s[sl], preferred_element_type=F32)
            u = jnp.dot(x, wu_s[sl], preferred_element_type=F32)
            h = (_silu(g) * u).astype(BF16)
            y = jnp.dot(h, wd_s[sl].reshape(D_EXPERT, dm), preferred_element_type=F32)
            y_packed = _pack_bf16_pairs(y[:, :half], y[:, half:])
            for c in range(rl):
                ybuf[p, pl.ds(c, rows, stride=rl), :] = y_packed[:, c * LANE:(c + 1) * LANE]

        small_groups = MOE_SMALL_ROWS // SUBLANES

        @pl.when(ng > small_groups)
        def _():
            run_block(MOE_ROWS)

        @pl.when(ng <= small_groups)
        def _():
            run_block(MOE_SMALL_ROWS)

        for par in range(2):
            for g0 in range(0, groups, GROUP_STEP):
                @pl.when(jnp.logical_and(p == par, g0 < ng))
                def _():
                    for r in range(g0 * SUBLANES, (g0 + GROUP_STEP) * SUBLANES):
                        d = stab_ref[0, 0, r]
                        dst = pl.multiple_of(jnp.where(d < 0, n_asg + par * MOE_ROWS + r, d) * rl, rl)
                        pltpu.make_async_copy(ybuf.at[par, pl.ds(r * rl, rl)], ys_hbm.at[pl.ds(dst, rl)],
                                              ssem.at[par]).start()

        pend[p] = ng

    @pl.when(s == n_steps - 1)
    def _():
        for p in range(2):
            @pl.when(pend[p] > 0)
            def _():
                scatter_wait(p, pend[p])
                pend[p] = 0

        xbuf[...] = jnp.zeros_like(xbuf)
        fills = [pltpu.make_async_copy(xbuf.at[p], ys_hbm.at[pl.ds((n_asg + p * MOE_ROWS) * rl, MOE_ROWS * rl)],
                                       gsem.at[p]) for p in range(2)]
        for cp in fills:
            cp.start()
        for cp in fills:
            cp.wait()


def _experts(h2, gtab, stab, sched, wg, wu, wd):
    n_tok, dm = h2.shape[0] // ROW_LINES, D_MODEL
    nt = D_EXPERT // MOE_FT
    n_asg = 2 * n_tok
    n_steps = sched[0].shape[0]

    smem_rows = lambda imap: pl.BlockSpec((1, 1, MOE_ROWS), imap, memory_space=pltpu.SMEM)
    grid_spec = pltpu.PrefetchScalarGridSpec(
        num_scalar_prefetch=7,
        grid=(n_steps,),
        in_specs=[smem_rows(lambda s, comp, blk, *_: (blk[jnp.minimum(s + 1, n_steps - 1)], 0, 0)),
                  smem_rows(lambda s, comp, blk, *_: (blk[s], 0, 0)),
                  pl.BlockSpec(memory_space=pl.ANY),
                  pl.BlockSpec((1, dm, MOE_FT), lambda s, c, b, sl, ca, cs, ce, ct: (ce[s], 0, ct[s])),
                  pl.BlockSpec((1, dm, MOE_FT), lambda s, c, b, sl, ca, cs, ce, ct: (ce[s], 0, ct[s])),
                  pl.BlockSpec((1, MOE_FT, dm), lambda s, c, b, sl, ca, cs, ce, ct: (ce[s], ct[s], 0))],
        out_specs=pl.BlockSpec(memory_space=pl.ANY),
        scratch_shapes=[pltpu.VMEM((2, dm, D_EXPERT), BF16), pltpu.VMEM((2, dm, D_EXPERT), BF16),
                        pltpu.VMEM((2, nt, MOE_FT, dm), BF16),
                        pltpu.VMEM((2, MOE_ROWS * ROW_LINES, LANE), jnp.uint32),
                        pltpu.VMEM((MOE_ROWS, dm), BF16),
                        pltpu.VMEM((2, MOE_ROWS * ROW_LINES, LANE), jnp.uint32),
                        pltpu.SemaphoreType.DMA((2,)), pltpu.SemaphoreType.DMA((2,)),
                        pltpu.SMEM((2,), jnp.int32)],
    )
    return pl.pallas_call(
        functools.partial(_moe_kernel, nt=nt),
        grid_spec=grid_spec,
        out_shape=jax.ShapeDtypeStruct(((n_asg + 2 * MOE_ROWS) * ROW_LINES, LANE), jnp.uint32),
        compiler_params=_cparams(("arbitrary",)),
        name="experts",
    )(*sched, gtab, stab, h2, wg, wu, wd)


def _final_kernel(x1_ref, y0_ref, y1_ref, g_ref, m_ref, w_ref, b_ref, o_ref, y_s):
    m = m_ref[0]
    bm, dm = x1_ref.shape
    half = dm // 2
    g0 = g_ref[:, 0:1]
    g1 = g_ref[:, 1:2]
    for c in range(ROW_LINES):
        rows_c = pl.ds(c, bm, stride=ROW_LINES)
        lo0, hi0 = _unpack_bf16_pairs(y0_ref[rows_c, :])
        lo1, hi1 = _unpack_bf16_pairs(y1_ref[rows_c, :])
        y_s[:, c * LANE:(c + 1) * LANE] = g0 * lo0 + g1 * lo1
        y_s[:, half + c * LANE:half + (c + 1) * LANE] = g0 * hi0 + g1 * hi1
    z = DEEPNORM_ALPHA * x1_ref[...] + m[5:6, :] * y_s[...]
    o_ref[...] = _layer_norm_rows(z, w_ref[...], b_ref[...])


def _final(x1, ys, gates, mods3, w, b, seq, bm):
    rows, dm = x1.shape
    blocks_per_batch = seq // bm
    slot_blocks = rows // bm
    row_blk = lambda n: pl.BlockSpec((bm, n), lambda i: (i, 0))
    return pl.pallas_call(
        _final_kernel,
        grid=(rows // bm,),
        in_specs=[row_blk(dm), pl.BlockSpec((bm * ROW_LINES, LANE), lambda i: (i, 0)),
                  pl.BlockSpec((bm * ROW_LINES, LANE), lambda i: (i + slot_blocks, 0)),
                  row_blk(LANE),
                  pl.BlockSpec((1, N_MOD, dm), lambda i: (i // blocks_per_batch, 0, 0)),
                  _resident(w.shape), _resident(b.shape)],
        out_specs=row_blk(dm),
        out_shape=jax.ShapeDtypeStruct((rows, dm), F32),
        scratch_shapes=[pltpu.VMEM((bm, dm), F32)],
        compiler_params=_cparams(("parallel",)),
        name="final_ln",
    )(x1, ys, ys, gates, mods3, w, b)


def _route(route, n_tok):
    e_flat = route[:, 2:4].astype(jnp.int32).reshape(-1)
    n_asg = e_flat.shape[0]
    earange = jnp.arange(N_EXPERTS, dtype=jnp.int32)
    counts = jnp.sum((e_flat[:, None] == earange[None, :]).astype(jnp.int32), 0)
    nblk_e = (counts + MOE_ROWS - 1) // MOE_ROWS
    pad_end = jnp.cumsum(nblk_e * MOE_ROWS)
    pad_start = pad_end - nblk_e * MOE_ROWS
    n_blk = n_asg // MOE_ROWS + N_EXPERTS

    order = jnp.argsort(e_flat, stable=True).astype(jnp.int32)
    starts = jnp.cumsum(counts) - counts
    ridx = jnp.arange(n_blk * MOE_ROWS, dtype=jnp.int32)
    e_row = jnp.minimum(jnp.sum((pad_end[None, :] <= ridx[:, None]).astype(jnp.int32), -1), N_EXPERTS - 1)
    rsel = e_row[:, None] == earange[None, :]
    k_row = ridx - jnp.sum(jnp.where(rsel, pad_start[None, :], 0), -1)
    valid = k_row < jnp.sum(jnp.where(rsel, counts[None, :], 0), -1)
    src = jnp.clip(jnp.sum(jnp.where(rsel, starts[None, :], 0), -1) + k_row, 0, n_asg - 1)
    row_asg = jnp.where(valid, order[src], -1)
    gtab = (jnp.maximum(row_asg, 0) >> 1).reshape(n_blk, 1, MOE_ROWS)
    stab = jnp.where(row_asg < 0, -1, (row_asg & 1) * n_tok + (row_asg >> 1)).reshape(n_blk, 1, MOE_ROWS)

    nt = D_EXPERT // MOE_FT
    n_steps = n_blk + (nt - 1) * N_EXPERTS + nt
    has = nblk_e > 0
    n_visits = jnp.sum(has.astype(jnp.int32))
    e_of_visit = jnp.sort(jnp.where(has, earange, N_EXPERTS))
    vsel = e_of_visit[:, None] == earange[None, :]
    nb_v = jnp.sum(jnp.where(vsel, nblk_e[None, :], 0), -1)
    steps_v = jnp.where(nb_v > 0, jnp.maximum(nb_v, nt), 0)
    end_v = nt + jnp.cumsum(steps_v)
    start_v = end_v - steps_v
    first_blk_v = jnp.cumsum(nb_v) - nb_v
    sidx = jnp.arange(n_steps, dtype=jnp.int32)
    v = jnp.sum((end_v[None, :] <= sidx[:, None]).astype(jnp.int32), -1)
    pick = lambda arr, idx: jnp.sum(jnp.where(idx[:, None] == earange[None, :], arr[None, :], 0), -1)
    in_visit = (sidx >= nt) & (v < n_visits)
    k = sidx - pick(start_v, v)
    comp = in_visit & (k < pick(nb_v, v))
    cnt_v = jnp.sum(jnp.where(vsel, counts[None, :], 0), -1)
    rows_s = jnp.clip(pick(cnt_v, v) - k * MOE_ROWS, 0, MOE_ROWS)
    chunk_rows = GROUP_STEP * SUBLANES
    groups_s = jnp.where(comp, (rows_s + chunk_rows - 1) // chunk_rows * GROUP_STEP, 0)
    blk_s = lax.cummax(jnp.where(comp, pick(first_blk_v, v) + k, 0), axis=0)
    prologue = sidx < nt
    cast = prologue | (in_visit & (k < nt) & (v + 1 < n_visits))
    cexp = jnp.where(prologue, e_of_visit[0], pick(e_of_visit, v + 1))
    ctile = jnp.where(prologue, sidx, k)
    code = lax.cummax(jnp.where(cast, cexp * nt + ctile, 0), axis=0)
    i32 = lambda a: a.astype(jnp.int32)
    sched = (i32(groups_s), i32(blk_s), i32(v % 2), i32(cast), i32(jnp.where(prologue, 0, (v + 1) % 2)),
             i32(jnp.minimum(code // nt, N_EXPERTS - 1)), i32(code % nt))
    return sched, gtab, stab


def _grid_pos_embed(rows):
    quarter = D_MODEL // 4
    omega = 1.0 / (10000.0 ** (jnp.arange(quarter, dtype=F32) / quarter))
    ar = jnp.arange(rows, dtype=F32)[:, None] * omega
    ac = jnp.arange(GRID_W, dtype=F32)[:, None] * omega
    shape = (rows, GRID_W, quarter)
    parts = [jnp.broadcast_to(jnp.sin(ar)[:, None, :], shape), jnp.broadcast_to(jnp.cos(ar)[:, None, :], shape),
             jnp.broadcast_to(jnp.sin(ac)[None, :, :], shape), jnp.broadcast_to(jnp.cos(ac)[None, :, :], shape)]
    return jnp.concatenate(parts, -1).reshape(rows * GRID_W, D_MODEL)


def kernel(x, c, ctx, c_ctx, w_mod, b_mod, w_in, conv_w, conv_b, gate_bias, mlstm_norm_w, cmlp_norm_w,
           w_s, b_s, w_out, ln1_w, ln1_b, router1_w, router1_b, router2_w, router2_b, w_gate, w_up,
           w_down, ln2_w, ln2_b):
    bsz, seq, dm = x.shape
    ctx_len = ctx.shape[1]
    n_tok = bsz * seq
    assert w_mod.shape[0] == 1 and dm == D_MODEL and seq % MCHUNK == 0 and ctx_len == MCHUNK
    pe = _grid_pos_embed(seq // GRID_W).astype(x.dtype)
    x2d = x.reshape(n_tok, dm)
    ctx2d = ctx.reshape(bsz * ctx_len, dm)

    mod_rows = 16
    cc = jnp.concatenate([c, c_ctx[None, :], jnp.zeros((mod_rows - bsz - 1, dm), c.dtype)], 0)
    mods3 = _modulation(cc, w_mod[0], b_mod[0]).reshape(mod_rows, N_MOD, dm)

    dq = D_MLSTM
    wi = w_in[0]
    w_qkvo = wi[:, :4 * dq].astype(BF16)
    w_g = jnp.pad(wi[:, 4 * dq:4 * dq + N_GATE_COLS], ((0, 0), (0, LANE - N_GATE_COLS))).astype(BF16)
    n_guv = N_GATE_COLS + 2 * D_CMLP
    w_guv = jnp.pad(wi[:, 4 * dq:], ((0, 0), (0, -n_guv % LANE))).astype(BF16)
    bm_proj = 256
    blocks_per_seq = seq // bm_proj
    cmlp = (cmlp_norm_w[0].reshape(1, -1), w_s[0].astype(BF16), b_s[0].T)
    qkvo, g_x, yc = _projection(x2d, pe, mods3, lambda i: i // blocks_per_seq,
                                [w_qkvo, w_guv], [BF16, F32, BF16], [False, False], bm_proj, cmlp=cmlp)
    w_kv = wi[:, dq:3 * dq].astype(BF16)
    kv_c, g_c = _projection(ctx2d, None, mods3, lambda i: bsz, [w_kv, w_g], [BF16, F32],
                            [False, False], bm_proj)

    rowq, colq = _gate_stats(g_x, g_c, gate_bias[0], bsz, seq, ctx_len)
    ym = _mlstm(qkvo, kv_c, conv_w[0], conv_b[0].reshape(1, -1), rowq, colq,
                mlstm_norm_w[0].reshape(1, -1), bsz, seq, ctx_len)

    wr = jnp.pad(jnp.concatenate([router2_w[0], router1_w[0]], 1),
                 ((0, 0), (0, LANE - N_GROUPS - N_EXPERTS))).astype(BF16)
    br = jnp.pad(jnp.concatenate([router2_b[0], router1_b[0]], 0),
                 (0, LANE - N_GROUPS - N_EXPERTS)).reshape(1, LANE)
    x1, h2, route = _mixer_out(ym, yc, x2d, pe, mods3, w_out[0].astype(BF16),
                               ln1_w[0].reshape(1, -1), ln1_b[0].reshape(1, -1), wr, br, seq, 512)

    sched, gtab, stab = _route(route, n_tok)
    ys = _experts(h2, gtab, stab, sched, w_gate[0], w_up[0], w_down[0])
    out = _final(x1, ys, route, mods3, ln2_w[0].reshape(1, -1),
                 ln2_b[0].reshape(1, -1), seq, 512)
    return out.reshape(bsz, seq, dm)
```

```python
import functools

import jax
import jax.numpy as jnp
from jax import lax
from jax.experimental import pallas as pl
from jax.experimental.pallas import tpu as pltpu

F32 = jnp.float32
BF16 = jnp.bfloat16

D_MODEL = 2048
GRID_W = 64
D_MLSTM = 1024
D_CMLP = 1024
HEADS = 4
HD = 256
CMLP_GROUPS = 4
CMLP_GD = 256
CMLP_CHUNK = 128
N_GROUPS = 4
EXPERTS_PER_GROUP = 8
N_EXPERTS = 32
D_EXPERT = 1024
N_MOD = 6
N_GATE_COLS = 16
DEEPNORM_ALPHA = 2.0 ** 0.25
LN_EPS = 1e-6

LANE = 128
SUBLANES = 8
MCHUNK = 256
MOE_ROWS = 256
MOE_SMALL_ROWS = 64
MOE_FT = 256
GROUP_STEP = 4
HEAD_LANES = LANE // HEADS
ROW_LINES = D_MODEL // 2 // LANE
VMEM_LIMIT = 62 * 1024 * 1024


def _cparams(sem):
    return pltpu.CompilerParams(dimension_semantics=sem, vmem_limit_bytes=VMEM_LIMIT)


def _resident(shape):
    nd = len(shape)
    return pl.BlockSpec(shape, lambda *_: (0,) * nd, pipeline_mode=pl.Buffered(1))


def _sigmoid(x):
    return 0.5 * jnp.tanh(0.5 * x) + 0.5


def _silu(x):
    return x * _sigmoid(x)


def _log_sigmoid(x):
    return jnp.minimum(x, 0.0) - jnp.log1p(jnp.exp(-jnp.abs(x)))


def _gelu_tanh(x):
    c = 0.7978845608028654
    return 0.5 * x * (1.0 + jnp.tanh(c * (x + 0.044715 * (x * x * x))))


def _pack_bf16_pairs(lo, hi):
    lo_b = lax.bitcast_convert_type(lo.astype(BF16).astype(F32), jnp.uint32)
    hi_b = lax.bitcast_convert_type(hi.astype(BF16).astype(F32), jnp.uint32)
    return (lo_b >> 16) | (hi_b & jnp.uint32(0xFFFF0000))


def _unpack_bf16_pairs(w):
    lo = lax.bitcast_convert_type(w << 16, F32)
    hi = lax.bitcast_convert_type(w & jnp.uint32(0xFFFF0000), F32)
    return lo, hi


def _layer_norm_rows(z, w, b):
    mu = jnp.mean(z, axis=-1, keepdims=True)
    zc = z - mu
    var = jnp.mean(zc * zc, axis=-1, keepdims=True)
    return zc * lax.rsqrt(var + LN_EPS) * w + b


def _mod_kernel(c_ref, w_ref, b_ref, o_ref):
    s = _silu(c_ref[...]).astype(BF16)
    o_ref[...] = jnp.dot(s, w_ref[...].astype(BF16), preferred_element_type=F32) + b_ref[...]


def _modulation(cc, w_mod, b_mod):
    rows, dm = cc.shape
    n = w_mod.shape[1]
    tn = 1024
    return pl.pallas_call(
        _mod_kernel,
        grid=(n // tn,),
        in_specs=[pl.BlockSpec((rows, dm), lambda j: (0, 0)),
                  pl.BlockSpec((dm, tn), lambda j: (0, j)),
                  pl.BlockSpec((1, tn), lambda j: (0, j))],
        out_specs=pl.BlockSpec((rows, tn), lambda j: (0, j)),
        out_shape=jax.ShapeDtypeStruct((rows, n), F32),
        compiler_params=_cparams(("arbitrary",)),
        name="modulation",
    )(cc, w_mod, b_mod.reshape(1, n))


def _proj_kernel(*refs, n_w, has_pe, gelu_flags, has_cmlp, tn):
    x_ref = refs[0]
    k = 1
    pe_ref = None
    if has_pe:
        pe_ref = refs[k]
        k += 1
    m_ref = refs[k]
    k += 1
    w_refs = refs[k:k + n_w]
    k += n_w
    if has_cmlp:
        cnw_ref, ws_ref, bs_ref = refs[k:k + 3]
        k += 3
    o_refs = refs[k:k + n_w]
    hx_ref = refs[k + n_w]
    x = x_ref[...]
    if has_pe:
        x = x + pe_ref[...]
    m = m_ref[0]
    hx_ref[...] = (x * (1.0 + m[1:2, :]) + m[0:1, :]).astype(BF16)
    n_plain = n_w - 1 if has_cmlp else n_w
    for w_ref, o_ref, use_gelu in list(zip(w_refs, o_refs, gelu_flags))[:n_plain]:
        n = w_ref.shape[1]
        step = min(tn, n)
        for j in range(0, n, step):
            acc = jnp.dot(hx_ref[...], w_ref[:, j:j + step], preferred_element_type=F32)
            if use_gelu:
                acc = _gelu_tanh(acc)
            o_ref[:, j:j + step] = acc.astype(o_ref.dtype)
    if has_cmlp:
        w_ref, o_ref, s_ref = w_refs[-1], o_refs[-1], refs[k + n_w + 1]
        bm = x_ref.shape[0]
        vgate = _gelu_tanh(jnp.dot(hx_ref[...], w_ref[:, D_CMLP:], preferred_element_type=F32))
        for g in range(CMLP_GROUPS):
            gs = slice(g * CMLP_GD, (g + 1) * CMLP_GD)
            vg = vgate[:, gs]
            mu = jnp.mean(vg, axis=-1, keepdims=True)
            vc = vg - mu
            var = jnp.mean(vc * vc, axis=-1, keepdims=True)
            vn = (vc * lax.rsqrt(var + LN_EPS) * cnw_ref[:, gs]).astype(BF16)
            for p in range(bm // CMLP_CHUNK):
                ps = slice(p * CMLP_CHUNK, (p + 1) * CMLP_CHUNK)
                s_ref[ps, gs] = (jnp.dot(ws_ref[g], vn[ps, :], preferred_element_type=F32)
                                 + bs_ref[:, g:g + 1])
        u = _gelu_tanh(jnp.dot(hx_ref[...], w_ref[:, :D_CMLP], preferred_element_type=F32))
        o_ref[...] = (u * s_ref[...]).astype(o_ref.dtype)


def _projection(x2d, pe, mods3, mod_row_of_block, weights, out_dtypes, gelu_flags, bm, cmlp=None):
    rows, dm = x2d.shape
    has_pe = pe is not None
    n_w = len(weights)
    in_specs = [pl.BlockSpec((bm, dm), lambda i: (i, 0))]
    args = [x2d]
    if has_pe:
        pe_blocks = pe.shape[0] // bm
        in_specs.append(pl.BlockSpec((bm, dm), lambda i: (i % pe_blocks, 0)))
        args.append(pe)
    in_specs.append(pl.BlockSpec((1, N_MOD, dm), lambda i: (mod_row_of_block(i), 0, 0)))
    args.append(mods3)
    for w in weights:
        in_specs.append(_resident(w.shape))
        args.append(w)
    out_widths = [w.shape[1] for w in weights]
    scratch = [pltpu.VMEM((bm, dm), BF16)]
    if cmlp is not None:
        for a in cmlp:
            in_specs.append(_resident(a.shape))
            args.append(a)
        out_widths[-1] = D_CMLP
        scratch.append(pltpu.VMEM((bm, D_CMLP), F32))
    out_specs = [pl.BlockSpec((bm, n), lambda i: (i, 0)) for n in out_widths]
    out_shape = [jax.ShapeDtypeStruct((rows, n), dt) for n, dt in zip(out_widths, out_dtypes)]
    kern = functools.partial(_proj_kernel, n_w=n_w, has_pe=has_pe, gelu_flags=tuple(gelu_flags),
                             has_cmlp=cmlp is not None, tn=1024)
    return pl.pallas_call(
        kern,
        grid=(rows // bm,),
        in_specs=in_specs,
        out_specs=out_specs,
        out_shape=out_shape,
        scratch_shapes=scratch,
        compiler_params=_cparams(("parallel",)),
        name="projection",
    )(*args)


def _gate_kernel(lic_ref, lfc_ref, lir_ref, lfr_ref, row_ref, col_ref, *, nc):
    hl = HEAD_LANES
    li = lic_ref[0]
    lf = _log_sigmoid(lfc_ref[0])
    length = li.shape[0]
    tid = lax.broadcasted_iota(jnp.int32, li.shape, 0)
    lane = lax.broadcasted_iota(jnp.int32, li.shape, 1)
    lane_l = lane & (hl - 1)
    lane1 = lane_l[0:1, :]
    fwd = (lane_l < nc) | (lane_l == 2 * nc)

    def scan_sublanes(x, op, fill):
        p = x
        s = x
        k = 1
        while k < length:
            p = op(p, jnp.where(tid >= k, pltpu.roll(p, k, 0), fill))
            s = op(s, jnp.where(tid < length - k, pltpu.roll(s, length - k, 0), fill))
            k *= 2
        return jnp.where(fwd, p, s)

    b = scan_sublanes(lf, jnp.add, 0.0)
    btot = jnp.sum(lf, axis=0, keepdims=True)
    a = btot - b + li
    m_loc = jnp.max(a, axis=0, keepdims=True)
    r = li - b
    cm = scan_sublanes(r, jnp.maximum, -jnp.inf)

    m_ctx = jnp.maximum(btot, m_loc)
    m_in = jnp.where(lane1 == 0, pltpu.roll(m_ctx, LANE - 2 * nc, 1), pltpu.roll(m_ctx, LANE - 2, 1))
    for k in range(nc - 1):
        m_new = jnp.maximum(btot + m_in, m_loc)
        m_in = jnp.where(lane1 == k + 1, pltpu.roll(m_new, 1, 1),
                         jnp.where(lane1 == 2 * nc - 2 - k, pltpu.roll(m_new, LANE - 1, 1), m_in))
    is_ctx = lane1 >= 2 * nc
    m_in = jnp.where(is_ctx, 0.0, m_in)
    m_new = jnp.maximum(btot + m_in, m_loc)
    s_old = jnp.broadcast_to(jnp.exp(btot + m_in - m_new), li.shape)
    w = jnp.exp(a - m_new)
    big_m = jnp.maximum(m_in, cm)
    s_int = jnp.exp(m_in - big_m)
    e_neg = jnp.exp(-(b + big_m))
    g = 2 * nc
    for h in range(HEADS):
        off = h * hl
        rot = lambda x, to: pltpu.roll(x, (to - off) % LANE, 1)
        col_ref[0, h] = jnp.where(
            lane < g, rot(w, 0), jnp.where(
                lane < 2 * g, rot(big_m, g), jnp.where(
                    lane < 3 * g, rot(s_int, 2 * g), jnp.where(
                        lane < 4 * g, rot(e_neg, 3 * g), jnp.where(
                            lane < 4 * g + 2, rot(w, 3 * g), rot(s_old, 4 * g + 2))))))

    lir = lir_ref[0]
    lfr = _log_sigmoid(lfr_ref[0])
    width = lir.shape[1]
    rid = lax.broadcasted_iota(jnp.int32, lir.shape, 0) & (2 * nc - 1)
    pid = lax.broadcasted_iota(jnp.int32, lir.shape, 1)
    p = lfr
    s = lfr
    k = 1
    while k < width:
        p = p + jnp.where(pid >= k, pltpu.roll(p, k, 1), 0.0)
        s = s + jnp.where(pid < width - k, pltpu.roll(s, width - k, 1), 0.0)
        k *= 2
    row_ref[0] = lir - jnp.where(rid < nc, p, s)


def _gate_stats(g_x, g_c, gate_bias, bsz, seq, ctx_len):
    nc = seq // MCHUNK
    assert HEADS * HEAD_LANES == LANE and 2 * nc + 2 <= HEAD_LANES and (2 * nc) & (2 * nc - 1) == 0
    gb = gate_bias.astype(F32)
    gx = g_x[:, :N_GATE_COLS].reshape(bsz, nc, MCHUNK, 2, 2, HEADS) + gb.reshape(2, 2, HEADS)
    gc = g_c[:, :N_GATE_COLS].reshape(bsz, ctx_len, 2, 2, HEADS) + gb.reshape(2, 2, HEADS)
    col_x = gx.transpose(4, 0, 2, 5, 3, 1).reshape(2, bsz, MCHUNK, HEADS, 2 * nc)
    col_c = gc.transpose(3, 0, 1, 4, 2)
    col = jnp.concatenate([col_x, col_c], -1)
    col = jnp.pad(col, ((0, 0),) * 4 + ((0, HEAD_LANES - col.shape[-1]),)).reshape(2, bsz, MCHUNK, LANE)
    row = gx.transpose(4, 0, 5, 3, 1, 2).reshape(2, bsz, HEADS * 2 * nc, MCHUNK)
    blk_c = pl.BlockSpec((1, MCHUNK, LANE), lambda b: (b, 0, 0))
    blk_r = pl.BlockSpec((1, HEADS * 2 * nc, MCHUNK), lambda b: (b, 0, 0))
    rowq, colq = pl.pallas_call(
        functools.partial(_gate_kernel, nc=nc),
        grid=(bsz,),
        in_specs=[blk_c, blk_c, blk_r, blk_r],
        out_specs=[blk_r, pl.BlockSpec((1, HEADS, MCHUNK, LANE), lambda b: (b, 0, 0, 0))],
        out_shape=[jax.ShapeDtypeStruct((bsz, HEADS * 2 * nc, MCHUNK), F32),
                   jax.ShapeDtypeStruct((bsz, HEADS, MCHUNK, LANE), F32)],
        compiler_params=_cparams(("parallel",)),
        name="gate_stats",
    )(col[0], col[1], row[0], row[1])
    return rowq.reshape(bsz, HEADS, 2 * nc, MCHUNK), colq


def _mlstm_kernel(q_ref, k_ref, v_ref, o_ref, kc_ref, vc_ref, cwq_ref, cbq_ref, cwk_ref, cbk_ref,
                  row_ref, col_ref, nw_ref, band_ref, edge_ref, y_ref, q_s, k_s, kc_s, ct_s, n_s, *, nc):
    lc = MCHUNK
    halo_rows = edge_ref.shape[1]

    def conv_silu_chunk(x_ref, c, n_chunks, w, b, scale):
        x = x_ref[pl.ds(c * lc, lc), :]
        wb = w.astype(BF16)
        taps = jnp.concatenate([x * wb[0:1, :], x * wb[1:2, :], x * wb[2:3, :]], axis=0)
        y = jnp.dot(band_ref[...], taps, preferred_element_type=F32)
        if n_chunks > 1:
            rid = lax.broadcasted_iota(jnp.int32, (halo_rows, x.shape[1]), 0)
            wf = wb.astype(F32)
            halo = jnp.zeros((halo_rows, x.shape[1]), F32)
            if c > 0:
                prev = x_ref[pl.ds(c * lc - halo_rows, halo_rows), :].astype(F32)[halo_rows - 1:, :]
                halo = jnp.where(rid == 0, prev * wf[0:1, :], halo)
            if c < n_chunks - 1:
                nxt = x_ref[pl.ds((c + 1) * lc, halo_rows), :].astype(F32)[0:1, :]
                halo = jnp.where(rid == 1, nxt * wf[2:3, :], halo)
            y = y + jnp.dot(edge_ref[...], halo.astype(BF16), preferred_element_type=F32)
        y = _silu(y + b)
        if scale != 1.0:
            y = y * scale
        return y.astype(BF16)

    k_scale = HD ** -0.5
    for c in range(nc):
        sl = pl.ds(c * lc, lc)
        q_s[sl, :] = conv_silu_chunk(q_ref, c, nc, cwq_ref[...], cbq_ref[...], 1.0)
        k_s[sl, :] = conv_silu_chunk(k_ref, c, nc, cwk_ref[...], cbk_ref[...], k_scale)
    kc_s[...] = conv_silu_chunk(kc_ref, 0, 1, cwk_ref[...], cbk_ref[...], k_scale)

    def col(j):
        return col_ref[0, 0, :, j:j + 1]

    stats_t = col_ref[0, 0].T

    def wrow(j):
        return stats_t[j:j + 1, :].astype(BF16)

    def local_state(kk, vv, w_row):
        ktw = kk.T * w_row
        ct = jnp.dot(ktw, vv, preferred_element_type=F32)
        nn = jnp.dot(jnp.broadcast_to(w_row, (SUBLANES, lc)), kk, preferred_element_type=F32)[0:1, :]
        return ct, nn

    for d in range(2):
        ct, nn = local_state(kc_s[...], vc_ref[...], wrow(8 * nc + d))
        order = list(range(nc)) if d == 0 else list(range(nc - 1, -1, -1))
        for pos, c in enumerate(order):
            idx = d * nc + c
            ct_s[idx] = ct.astype(BF16)
            n_s[idx] = nn
            if pos == nc - 1:
                break
            sl = pl.ds(c * lc, lc)
            ctl, nl = local_state(k_s[sl, :], v_ref[sl, :], wrow(idx))
            s_old = col_ref[0, 0, 0:1, 8 * nc + 2 + idx:8 * nc + 3 + idx]
            ct = s_old * ct + ctl
            nn = s_old * nn + nl

    tid = lax.broadcasted_iota(jnp.int32, (lc, lc), 0)
    sid = lax.broadcasted_iota(jnp.int32, (lc, lc), 1)
    masks = (sid <= tid, sid >= tid)
    for c in range(nc):
        sl = pl.ds(c * lc, lc)
        q = q_s[sl, :]
        kk = k_s[sl, :]
        v = v_ref[sl, :]
        qf = q.astype(F32)
        s = lax.dot_general(q, kk, (((1,), (1,)), ((), ())), preferred_element_type=F32)
        h = None
        for d in range(2):
            idx = d * nc + c
            r = row_ref[0, 0, idx:idx + 1, :]
            big_m = col(2 * nc + idx)
            s_int = col(4 * nc + idx)
            e_neg = col(6 * nc + idx)
            p = jnp.where(masks[d], jnp.exp(r - big_m), 0.0) * s
            den = (jnp.sum(p, axis=-1, keepdims=True)
                   + s_int * jnp.sum(qf * n_s[idx], axis=-1, keepdims=True))
            num = (jnp.dot(p.astype(BF16), v, preferred_element_type=F32)
                   + s_int * jnp.dot(q, ct_s[idx], preferred_element_type=F32))
            hd = num * (1.0 / jnp.maximum(jnp.abs(den), e_neg))
            h = hd if h is None else h + hd
        mu = jnp.mean(h, axis=-1, keepdims=True)
        hc = h - mu
        var = jnp.mean(hc * hc, axis=-1, keepdims=True)
        hn = hc * lax.rsqrt(var + LN_EPS) * nw_ref[...]
        y_ref[sl, :] = (hn * _sigmoid(o_ref[sl, :].astype(F32))).astype(BF16)


def _mlstm(qkvo, kv_ctx, conv_w, conv_b, rowq, colq, norm_w, bsz, seq, ctx_len):
    nc = seq // MCHUNK
    hq = D_MLSTM // HD
    kern = functools.partial(_mlstm_kernel, nc=nc)
    ii = jnp.arange(MCHUNK)[:, None]
    jj = jnp.arange(MCHUNK)[None, :]
    band = jnp.concatenate([(jj == ii + t - 1) for t in range(3)], axis=1).astype(BF16)
    hh = jnp.arange(2 * SUBLANES)[None, :]
    edge = (((ii == 0) & (hh == 0)) | ((ii == MCHUNK - 1) & (hh == 1))).astype(BF16)
    seq_blk = lambda off: pl.BlockSpec((seq, HD), lambda b, h: (b, off + h))
    ctx_blk = lambda off: pl.BlockSpec((ctx_len, HD), lambda b, h: (b, off + h))
    return pl.pallas_call(
        kern,
        grid=(bsz, HEADS),
        in_specs=[seq_blk(0), seq_blk(hq), seq_blk(2 * hq), seq_blk(3 * hq),
                  ctx_blk(0), ctx_blk(hq),
                  pl.BlockSpec((3, HD), lambda b, h: (0, h)),
                  pl.BlockSpec((1, HD), lambda b, h: (0, h)),
                  pl.BlockSpec((3, HD), lambda b, h: (0, hq + h)),
                  pl.BlockSpec((1, HD), lambda b, h: (0, hq + h)),
                  pl.BlockSpec((1, 1, 2 * nc, MCHUNK), lambda b, h: (b, h, 0, 0)),
                  pl.BlockSpec((1, 1, MCHUNK, LANE), lambda b, h: (b, h, 0, 0)),
                  pl.BlockSpec((1, HD), lambda b, h: (0, h)),
                  _resident(band.shape), _resident(edge.shape)],
        out_specs=pl.BlockSpec((seq, HD), lambda b, h: (b, h)),
        out_shape=jax.ShapeDtypeStruct((bsz * seq, D_MLSTM), BF16),
        scratch_shapes=[pltpu.VMEM((seq, HD), BF16), pltpu.VMEM((seq, HD), BF16),
                        pltpu.VMEM((ctx_len, HD), BF16),
                        pltpu.VMEM((2 * nc, HD, HD), BF16), pltpu.VMEM((2 * nc, 1, HD), F32)],
        compiler_params=_cparams(("parallel", "parallel")),
        name="mlstm",
    )(qkvo, qkvo, qkvo, qkvo, kv_ctx, kv_ctx, conv_w, conv_b, conv_w, conv_b, rowq, colq, norm_w,
      band, edge)


def _out_kernel(ym_ref, yc_ref, x_ref, pe_ref, m_ref, wout_ref,
                l1w_ref, l1b_ref, wr_ref, br_ref, x1_ref, h2_ref, lg_ref):
    m = m_ref[0]
    y = (jnp.dot(ym_ref[...], wout_ref[:D_MLSTM, :], preferred_element_type=F32)
         + jnp.dot(yc_ref[...], wout_ref[D_MLSTM:, :], preferred_element_type=F32))
    z = DEEPNORM_ALPHA * (x_ref[...] + pe_ref[...]) + m[2:3, :] * y
    x1 = _layer_norm_rows(z, l1w_ref[...], l1b_ref[...])
    x1_ref[...] = x1
    h2 = x1 * (1.0 + m[4:5, :]) + m[3:4, :]
    half = h2.shape[1] // 2
    packed = _pack_bf16_pairs(h2[:, :half], h2[:, half:])
    for c in range(ROW_LINES):
        h2_ref[pl.ds(c, h2.shape[0], stride=ROW_LINES), :] = packed[:, c * LANE:(c + 1) * LANE]
    lg = jnp.dot(h2.astype(BF16), wr_ref[...], preferred_element_type=F32) + br_ref[...]

    lgt = lg.T
    bm = lg.shape[0]
    epg = EXPERTS_PER_GROUP
    rid = lax.broadcasted_iota(jnp.int32, (SUBLANES, bm), 0).astype(F32)
    neg = -jnp.inf
    far = float(SUBLANES)
    gl = lgt[N_EXPERTS:N_EXPERTS + SUBLANES, :]
    is_grp = rid < N_GROUPS
    m1 = jnp.max(jnp.where(is_grp, gl, neg), axis=0, keepdims=True)
    grp = jnp.min(jnp.where(is_grp & (gl == m1), rid, far), axis=0, keepdims=True)
    p_grp = 1.0 / jnp.sum(jnp.where(is_grp, jnp.exp(gl - m1), 0.0), axis=0, keepdims=True)
    l2 = lgt[0:epg, :]
    for g in range(1, N_GROUPS):
        l2 = jnp.where(grp == g, lgt[g * epg:(g + 1) * epg, :], l2)
    v0 = jnp.max(l2, axis=0, keepdims=True)
    i0 = jnp.min(jnp.where(l2 == v0, rid, far), axis=0, keepdims=True)
    l2m = jnp.where(rid == i0, neg, l2)
    v1 = jnp.max(l2m, axis=0, keepdims=True)
    i1 = jnp.min(jnp.where(l2m == v1, rid, far), axis=0, keepdims=True)
    s1 = jnp.exp(v1 - v0)
    g0 = p_grp / (1.0 + s1)
    res = jnp.where(rid == 0, g0, jnp.where(rid == 1, g0 * s1, jnp.where(
        rid == 2, grp * epg + i0, jnp.where(rid == 3, grp * epg + i1, 0.0))))
    lg_ref[...] = jnp.concatenate([res, jnp.zeros((LANE - SUBLANES, bm), F32)], axis=0).T


def _mixer_out(ym, yc, x2d, pe, mods3, wout, l1w, l1b, wr, br, seq, bm):
    rows, dm = x2d.shape
    pe_blocks = seq // bm
    row_blk = lambda n: pl.BlockSpec((bm, n), lambda i: (i, 0))
    return pl.pallas_call(
        _out_kernel,
        grid=(rows // bm,),
        in_specs=[row_blk(D_MLSTM), row_blk(D_CMLP), row_blk(dm),
                  pl.BlockSpec((bm, dm), lambda i: (i % pe_blocks, 0)),
                  pl.BlockSpec((1, N_MOD, dm), lambda i: (i // pe_blocks, 0, 0)),
                  _resident(wout.shape), _resident(l1w.shape), _resident(l1b.shape),
                  _resident(wr.shape), _resident(br.shape)],
        out_specs=[row_blk(dm), pl.BlockSpec((bm * ROW_LINES, LANE), lambda i: (i, 0)), row_blk(LANE)],
        out_shape=[jax.ShapeDtypeStruct((rows, dm), F32),
                   jax.ShapeDtypeStruct((rows * ROW_LINES, LANE), jnp.uint32),
                   jax.ShapeDtypeStruct((rows, LANE), F32)],
        compiler_params=_cparams(("parallel",)),
        name="mixer_out",
    )(ym, yc, x2d, pe, mods3, wout, l1w, l1b, wr, br)


def _moe_kernel(comp_ref, blk_ref, slot_ref, cast_ref, cslot_ref, cexp_ref, ctile_ref,
                gtab_ref, stab_ref, h2_hbm, wgf_ref, wuf_ref, wdf_ref, ys_hbm,
                wg_s, wu_s, wd_s, xbuf, xb_s, ybuf, gsem, ssem, pend, *, nt):
    del blk_ref, cexp_ref
    s = pl.program_id(0)
    n_steps = pl.num_programs(0)
    groups = MOE_ROWS // SUBLANES
    rl = ROW_LINES
    half = rl * LANE
    dm = 2 * half
    n_asg = ys_hbm.shape[0] // rl - 2 * MOE_ROWS
    group_lines = SUBLANES * rl

    def gather_wait(p, n):
        nl = n * group_lines
        pltpu.make_async_copy(h2_hbm.at[pl.ds(0, nl)], xbuf.at[p, pl.ds(0, nl)], gsem.at[p]).wait()

    def scatter_wait(p, n):
        nl = n * group_lines
        pltpu.make_async_copy(ybuf.at[p, pl.ds(0, nl)], ys_hbm.at[pl.ds(0, nl)], ssem.at[p]).wait()

    @pl.when(s == 0)
    def _():
        pend[0] = 0
        pend[1] = 0
        xbuf[...] = jnp.zeros_like(xbuf)

    nxt = jnp.minimum(s + 1, n_steps - 1)

    @pl.when(jnp.logical_and(s + 1 < n_steps, comp_ref[nxt] > 0))
    def _():
        p = (s + 1) % 2

        for par in range(2):
            for g0 in range(0, groups, GROUP_STEP):
                @pl.when(jnp.logical_and(p == par, g0 < comp_ref[nxt]))
                def _():
                    for r in range(g0 * SUBLANES, (g0 + GROUP_STEP) * SUBLANES):
                        src = pl.multiple_of(gtab_ref[0, 0, r] * rl, rl)
                        pltpu.make_async_copy(h2_hbm.at[pl.ds(src, rl)], xbuf.at[par, pl.ds(r * rl, rl)],
                                              gsem.at[par]).start()

    @pl.when(cast_ref[s] == 1)
    def _():
        cs = cslot_ref[s]
        t = ctile_ref[s]
        for k in range(nt):
            @pl.when(t == k)
            def _():
                wg_s[cs, :, k * MOE_FT:(k + 1) * MOE_FT] = wgf_ref[0].astype(BF16)
                wu_s[cs, :, k * MOE_FT:(k + 1) * MOE_FT] = wuf_ref[0].astype(BF16)
        wd_s[cs, t] = wdf_ref[0].astype(BF16)

    @pl.when(comp_ref[s] > 0)
    def _():
        p = s % 2
        sl = slot_ref[s]
        ng = comp_ref[s]
        gather_wait(p, ng)

        @pl.when(pend[p] > 0)
        def _():
            scatter_wait(p, pend[p])

        def run_block(rows):
            for c in range(rl):
                x_lo, x_hi = _unpack_bf16_pairs(xbuf[p, pl.ds(c, rows, stride=rl), :])
                xb_s[:rows, c * LANE:(c + 1) * LANE] = x_lo.astype(BF16)
                xb_s[:rows, half + c * LANE:half + (c + 1) * LANE] = x_hi.astype(BF16)
            x = xb_s[:rows, :]
            g = jnp.dot(x, wg_s[sl], preferred_element_type=F32)
            u = jnp.dot(x, wu_s[sl], preferred_element_type=F32)
            h = (_silu(g) * u).astype(BF16)
            y = jnp.dot(h, wd_s[sl].reshape(D_EXPERT, dm), preferred_element_type=F32)
            y_packed = _pack_bf16_pairs(y[:, :half], y[:, half:])
            for c in range(rl):
                ybuf[p, pl.ds(c, rows, stride=rl), :] = y_packed[:, c * LANE:(c + 1) * LANE]

        small_groups = MOE_SMALL_ROWS // SUBLANES

        @pl.when(ng > small_groups)
        def _():
            run_block(MOE_ROWS)

        @pl.when(ng <= small_groups)
        def _():
            run_block(MOE_SMALL_ROWS)

        for par in range(2):
            for g0 in range(0, groups, GROUP_STEP):
                @pl.when(jnp.logical_and(p == par, g0 < ng))
                def _():
                    for r in range(g0 * SUBLANES, (g0 + GROUP_STEP) * SUBLANES):
                        d = stab_ref[0, 0, r]
                        dst = pl.multiple_of(jnp.where(d < 0, n_asg + par * MOE_ROWS + r, d) * rl, rl)
                        pltpu.make_async_copy(ybuf.at[par, pl.ds(r * rl, rl)], ys_hbm.at[pl.ds(dst, rl)],
                                              ssem.at[par]).start()

        pend[p] = ng

    @pl.when(s == n_steps - 1)
    def _():
        for p in range(2):
            @pl.when(pend[p] > 0)
            def _():
                scatter_wait(p, pend[p])
                pend[p] = 0

        xbuf[...] = jnp.zeros_like(xbuf)
        fills = [pltpu.make_async_copy(xbuf.at[p], ys_hbm.at[pl.ds((n_asg + p * MOE_ROWS) * rl, MOE_ROWS * rl)],
                                       gsem.at[p]) for p in range(2)]
        for cp in fills:
            cp.start()
        for cp in fills:
            cp.wait()


def _experts(h2, gtab, stab, sched, wg, wu, wd):
    n_tok, dm = h2.shape[0] // ROW_LINES, D_MODEL
    nt = D_EXPERT // MOE_FT
    n_asg = 2 * n_tok
    n_steps = sched[0].shape[0]

    smem_rows = lambda imap: pl.BlockSpec((1, 1, MOE_ROWS), imap, memory_space=pltpu.SMEM)
    grid_spec = pltpu.PrefetchScalarGridSpec(
        num_scalar_prefetch=7,
        grid=(n_steps,),
        in_specs=[smem_rows(lambda s, comp, blk, *_: (blk[jnp.minimum(s + 1, n_steps - 1)], 0, 0)),
                  smem_rows(lambda s, comp, blk, *_: (blk[s], 0, 0)),
                  pl.BlockSpec(memory_space=pl.ANY),
                  pl.BlockSpec((1, dm, MOE_FT), lambda s, c, b, sl, ca, cs, ce, ct: (ce[s], 0, ct[s])),
                  pl.BlockSpec((1, dm, MOE_FT), lambda s, c, b, sl, ca, cs, ce, ct: (ce[s], 0, ct[s])),
                  pl.BlockSpec((1, MOE_FT, dm), lambda s, c, b, sl, ca, cs, ce, ct: (ce[s], ct[s], 0))],
        out_specs=pl.BlockSpec(memory_space=pl.ANY),
        scratch_shapes=[pltpu.VMEM((2, dm, D_EXPERT), BF16), pltpu.VMEM((2, dm, D_EXPERT), BF16),
                        pltpu.VMEM((2, nt, MOE_FT, dm), BF16),
                        pltpu.VMEM((2, MOE_ROWS * ROW_LINES, LANE), jnp.uint32),
                        pltpu.VMEM((MOE_ROWS, dm), BF16),
                        pltpu.VMEM((2, MOE_ROWS * ROW_LINES, LANE), jnp.uint32),
                        pltpu.SemaphoreType.DMA((2,)), pltpu.SemaphoreType.DMA((2,)),
                        pltpu.SMEM((2,), jnp.int32)],
    )
    return pl.pallas_call(
        functools.partial(_moe_kernel, nt=nt),
        grid_spec=grid_spec,
        out_shape=jax.ShapeDtypeStruct(((n_asg + 2 * MOE_ROWS) * ROW_LINES, LANE), jnp.uint32),
        compiler_params=_cparams(("arbitrary",)),
        name="experts",
    )(*sched, gtab, stab, h2, wg, wu, wd)


def _final_kernel(x1_ref, y0_ref, y1_ref, g_ref, m_ref, w_ref, b_ref, o_ref, y_s):
    m = m_ref[0]
    bm, dm = x1_ref.shape
    half = dm // 2
    g0 = g_ref[:, 0:1]
    g1 = g_ref[:, 1:2]
    for c in range(ROW_LINES):
        rows_c = pl.ds(c, bm, stride=ROW_LINES)
        lo0, hi0 = _unpack_bf16_pairs(y0_ref[rows_c, :])
        lo1, hi1 = _unpack_bf16_pairs(y1_ref[rows_c, :])
        y_s[:, c * LANE:(c + 1) * LANE] = g0 * lo0 + g1 * lo1
        y_s[:, half + c * LANE:half + (c + 1) * LANE] = g0 * hi0 + g1 * hi1
    z = DEEPNORM_ALPHA * x1_ref[...] + m[5:6, :] * y_s[...]
    o_ref[...] = _layer_norm_rows(z, w_ref[...], b_ref[...])


def _final(x1, ys, gates, mods3, w, b, seq, bm):
    rows, dm = x1.shape
    blocks_per_batch = seq // bm
    slot_blocks = rows // bm
    row_blk = lambda n: pl.BlockSpec((bm, n), lambda i: (i, 0))
    return pl.pallas_call(
        _final_kernel,
        grid=(rows // bm,),
        in_specs=[row_blk(dm), pl.BlockSpec((bm * ROW_LINES, LANE), lambda i: (i, 0)),
                  pl.BlockSpec((bm * ROW_LINES, LANE), lambda i: (i + slot_blocks, 0)),
                  row_blk(LANE),
                  pl.BlockSpec((1, N_MOD, dm), lambda i: (i // blocks_per_batch, 0, 0)),
                  _resident(w.shape), _resident(b.shape)],
        out_specs=row_blk(dm),
        out_shape=jax.ShapeDtypeStruct((rows, dm), F32),
        scratch_shapes=[pltpu.VMEM((bm, dm), F32)],
        compiler_params=_cparams(("parallel",)),
        name="final_ln",
    )(x1, ys, ys, gates, mods3, w, b)


def _route(route, n_tok):
    e_flat = route[:, 2:4].astype(jnp.int32).reshape(-1)
    n_asg = e_flat.shape[0]
    earange = jnp.arange(N_EXPERTS, dtype=jnp.int32)
    counts = jnp.sum((e_flat[:, None] == earange[None, :]).astype(jnp.int32), 0)
    nblk_e = (counts + MOE_ROWS - 1) // MOE_ROWS
    pad_end = jnp.cumsum(nblk_e * MOE_ROWS)
    pad_start = pad_end - nblk_e * MOE_ROWS
    n_blk = n_asg // MOE_ROWS + N_EXPERTS

    order = jnp.argsort(e_flat, stable=True).astype(jnp.int32)
    starts = jnp.cumsum(counts) - counts
    ridx = jnp.arange(n_blk * MOE_ROWS, dtype=jnp.int32)
    e_row = jnp.minimum(jnp.sum((pad_end[None, :] <= ridx[:, None]).astype(jnp.int32), -1), N_EXPERTS - 1)
    rsel = e_row[:, None] == earange[None, :]
    k_row = ridx - jnp.sum(jnp.where(rsel, pad_start[None, :], 0), -1)
    valid = k_row < jnp.sum(jnp.where(rsel, counts[None, :], 0), -1)
    src = jnp.clip(jnp.sum(jnp.where(rsel, starts[None, :], 0), -1) + k_row, 0, n_asg - 1)
    row_asg = jnp.where(valid, order[src], -1)
    gtab = (jnp.maximum(row_asg, 0) >> 1).reshape(n_blk, 1, MOE_ROWS)
    stab = jnp.where(row_asg < 0, -1, (row_asg & 1) * n_tok + (row_asg >> 1)).reshape(n_blk, 1, MOE_ROWS)

    nt = D_EXPERT // MOE_FT
    n_steps = n_blk + (nt - 1) * N_EXPERTS + nt
    has = nblk_e > 0
    n_visits = jnp.sum(has.astype(jnp.int32))
    e_of_visit = jnp.sort(jnp.where(has, earange, N_EXPERTS))
    vsel = e_of_visit[:, None] == earange[None, :]
    nb_v = jnp.sum(jnp.where(vsel, nblk_e[None, :], 0), -1)
    steps_v = jnp.where(nb_v > 0, jnp.maximum(nb_v, nt), 0)
    end_v = nt + jnp.cumsum(steps_v)
    start_v = end_v - steps_v
    first_blk_v = jnp.cumsum(nb_v) - nb_v
    sidx = jnp.arange(n_steps, dtype=jnp.int32)
    v = jnp.sum((end_v[None, :] <= sidx[:, None]).astype(jnp.int32), -1)
    pick = lambda arr, idx: jnp.sum(jnp.where(idx[:, None] == earange[None, :], arr[None, :], 0), -1)
    in_visit = (sidx >= nt) & (v < n_visits)
    k = sidx - pick(start_v, v)
    comp = in_visit & (k < pick(nb_v, v))
    cnt_v = jnp.sum(jnp.where(vsel, counts[None, :], 0), -1)
    rows_s = jnp.clip(pick(cnt_v, v) - k * MOE_ROWS, 0, MOE_ROWS)
    chunk_rows = GROUP_STEP * SUBLANES
    groups_s = jnp.where(comp, (rows_s + chunk_rows - 1) // chunk_rows * GROUP_STEP, 0)
    blk_s = lax.cummax(jnp.where(comp, pick(first_blk_v, v) + k, 0), axis=0)
    prologue = sidx < nt
    cast = prologue | (in_visit & (k < nt) & (v + 1 < n_visits))
    cexp = jnp.where(prologue, e_of_visit[0], pick(e_of_visit, v + 1))
    ctile = jnp.where(prologue, sidx, k)
    code = lax.cummax(jnp.where(cast, cexp * nt + ctile, 0), axis=0)
    i32 = lambda a: a.astype(jnp.int32)
    sched = (i32(groups_s), i32(blk_s), i32(v % 2), i32(cast), i32(jnp.where(prologue, 0, (v + 1) % 2)),
             i32(jnp.minimum(code // nt, N_EXPERTS - 1)), i32(code % nt))
    return sched, gtab, stab


def _grid_pos_embed(rows):
    quarter = D_MODEL // 4
    omega = 1.0 / (10000.0 ** (jnp.arange(quarter, dtype=F32) / quarter))
    ar = jnp.arange(rows, dtype=F32)[:, None] * omega
    ac = jnp.arange(GRID_W, dtype=F32)[:, None] * omega
    shape = (rows, GRID_W, quarter)
    parts = [jnp.broadcast_to(jnp.sin(ar)[:, None, :], shape), jnp.broadcast_to(jnp.cos(ar)[:, None, :], shape),
             jnp.broadcast_to(jnp.sin(ac)[None, :, :], shape), jnp.broadcast_to(jnp.cos(ac)[None, :, :], shape)]
    return jnp.concatenate(parts, -1).reshape(rows * GRID_W, D_MODEL)


def kernel(x, c, ctx, c_ctx, w_mod, b_mod, w_in, conv_w, conv_b, gate_bias, mlstm_norm_w, cmlp_norm_w,
           w_s, b_s, w_out, ln1_w, ln1_b, router1_w, router1_b, router2_w, router2_b, w_gate, w_up,
           w_down, ln2_w, ln2_b):
    bsz, seq, dm = x.shape
    ctx_len = ctx.shape[1]
    n_tok = bsz * seq
    assert w_mod.shape[0] == 1 and dm == D_MODEL and seq % MCHUNK == 0 and ctx_len == MCHUNK
    pe = _grid_pos_embed(seq // GRID_W).astype(x.dtype)
    x2d = x.reshape(n_tok, dm)
    ctx2d = ctx.reshape(bsz * ctx_len, dm)

    mod_rows = 16
    cc = jnp.concatenate([c, c_ctx[None, :], jnp.zeros((mod_rows - bsz - 1, dm), c.dtype)], 0)
    mods3 = _modulation(cc, w_mod[0], b_mod[0]).reshape(mod_rows, N_MOD, dm)

    dq = D_MLSTM
    wi = w_in[0]
    w_qkvo = wi[:, :4 * dq].astype(BF16)
    w_g = jnp.pad(wi[:, 4 * dq:4 * dq + N_GATE_COLS], ((0, 0), (0, LANE - N_GATE_COLS))).astype(BF16)
    w_uv = wi[:, 4 * dq + N_GATE_COLS:].astype(BF16)
    bm_proj = 512
    blocks_per_seq = seq // bm_proj
    cmlp = (cmlp_norm_w[0].reshape(1, -1), w_s[0].astype(BF16), b_s[0].T)
    qkvo, g_x, yc = _projection(x2d, pe, mods3, lambda i: i // blocks_per_seq,
                                [w_qkvo, w_g, w_uv], [BF16, F32, BF16], [False, False, True], bm_proj,
                                cmlp=cmlp)
    w_kv = wi[:, dq:3 * dq].astype(BF16)
    kv_c, g_c = _projection(ctx2d, None, mods3, lambda i: bsz, [w_kv, w_g], [BF16, F32],
                            [False, False], bm_proj)

    rowq, colq = _gate_stats(g_x, g_c, gate_bias[0], bsz, seq, ctx_len)
    ym = _mlstm(qkvo, kv_c, conv_w[0], conv_b[0].reshape(1, -1), rowq, colq,
                mlstm_norm_w[0].reshape(1, -1), bsz, seq, ctx_len)

    wr = jnp.pad(jnp.concatenate([router2_w[0], router1_w[0]], 1),
                 ((0, 0), (0, LANE - N_GROUPS - N_EXPERTS))).astype(BF16)
    br = jnp.pad(jnp.concatenate([router2_b[0], router1_b[0]], 0),
                 (0, LANE - N_GROUPS - N_EXPERTS)).reshape(1, LANE)
    x1, h2, route = _mixer_out(ym, yc, x2d, pe, mods3, w_out[0].astype(BF16),
                               ln1_w[0].reshape(1, -1), ln1_b[0].reshape(1, -1), wr, br, seq, 512)

    sched, gtab, stab = _route(route, n_tok)
    ys = _experts(h2, gtab, stab, sched, w_gate[0], w_up[0], w_down[0])
    out = _final(x1, ys, route, mods3, ln2_w[0].reshape(1, -1),
                 ln2_b[0].reshape(1, -1), seq, 512)
    return out.reshape(bsz, seq, dm)
```

```python
import functools

import jax
import jax.numpy as jnp
from jax import lax
from jax.experimental import pallas as pl
from jax.experimental.pallas import tpu as pltpu

F32 = jnp.float32
BF16 = jnp.bfloat16

D_MODEL = 2048
GRID_W = 64
D_MLSTM = 1024
D_CMLP = 1024
HEADS = 4
HD = 256
CMLP_GROUPS = 4
CMLP_GD = 256
CMLP_CHUNK = 128
N_GROUPS = 4
EXPERTS_PER_GROUP = 8
N_EXPERTS = 32
D_EXPERT = 1024
N_MOD = 6
N_GATE_COLS = 16
DEEPNORM_ALPHA = 2.0 ** 0.25
LN_EPS = 1e-6

LANE = 128
SUBLANES = 8
MCHUNK = 256
MOE_ROWS = 256
MOE_SMALL_ROWS = 64
MOE_FT = 512
GROUP_STEP = 4
HEAD_LANES = LANE // HEADS
ROW_LINES = D_MODEL // 2 // LANE
VMEM_LIMIT = 62 * 1024 * 1024


def _cparams(sem):
    return pltpu.CompilerParams(dimension_semantics=sem, vmem_limit_bytes=VMEM_LIMIT)


def _resident(shape):
    nd = len(shape)
    return pl.BlockSpec(shape, lambda *_: (0,) * nd, pipeline_mode=pl.Buffered(1))


def _sigmoid(x):
    return 0.5 * jnp.tanh(0.5 * x) + 0.5


def _silu(x):
    return x * _sigmoid(x)


def _log_sigmoid(x):
    return jnp.minimum(x, 0.0) - jnp.log1p(jnp.exp(-jnp.abs(x)))


def _gelu_tanh(x):
    c = 0.7978845608028654
    return 0.5 * x * (1.0 + jnp.tanh(c * (x + 0.044715 * (x * x * x))))


def _pack_bf16_pairs(lo, hi):
    lo_b = lax.bitcast_convert_type(lo.astype(BF16).astype(F32), jnp.uint32)
    hi_b = lax.bitcast_convert_type(hi.astype(BF16).astype(F32), jnp.uint32)
    return (lo_b >> 16) | (hi_b & jnp.uint32(0xFFFF0000))


def _unpack_bf16_pairs(w):
    lo = lax.bitcast_convert_type(w << 16, F32)
    hi = lax.bitcast_convert_type(w & jnp.uint32(0xFFFF0000), F32)
    return lo, hi


def _layer_norm_rows(z, w, b):
    mu = jnp.mean(z, axis=-1, keepdims=True)
    zc = z - mu
    var = jnp.mean(zc * zc, axis=-1, keepdims=True)
    return zc * lax.rsqrt(var + LN_EPS) * w + b


def _mod_kernel(c_ref, w_ref, b_ref, o_ref):
    s = _silu(c_ref[...]).astype(BF16)
    o_ref[...] = jnp.dot(s, w_ref[...].astype(BF16), preferred_element_type=F32) + b_ref[...]


def _modulation(cc, w_mod, b_mod):
    rows, dm = cc.shape
    n = w_mod.shape[1]
    tn = 1024
    return pl.pallas_call(
        _mod_kernel,
        grid=(n // tn,),
        in_specs=[pl.BlockSpec((rows, dm), lambda j: (0, 0)),
                  pl.BlockSpec((dm, tn), lambda j: (0, j)),
                  pl.BlockSpec((1, tn), lambda j: (0, j))],
        out_specs=pl.BlockSpec((rows, tn), lambda j: (0, j)),
        out_shape=jax.ShapeDtypeStruct((rows, n), F32),
        compiler_params=_cparams(("arbitrary",)),
        name="modulation",
    )(cc, w_mod, b_mod.reshape(1, n))


def _proj_kernel(*refs, n_w, has_pe, gelu_flags, has_cmlp, tn):
    x_ref = refs[0]
    k = 1
    pe_ref = None
    if has_pe:
        pe_ref = refs[k]
        k += 1
    m_ref = refs[k]
    k += 1
    w_refs = refs[k:k + n_w]
    k += n_w
    if has_cmlp:
        cnw_ref, ws_ref, bs_ref = refs[k:k + 3]
        k += 3
    o_refs = refs[k:k + n_w]
    hx_ref = refs[k + n_w]
    x = x_ref[...]
    if has_pe:
        x = x + pe_ref[...]
    m = m_ref[0]
    hx_ref[...] = (x * (1.0 + m[1:2, :]) + m[0:1, :]).astype(BF16)
    n_plain = n_w - 1 if has_cmlp else n_w
    for w_ref, o_ref, use_gelu in list(zip(w_refs, o_refs, gelu_flags))[:n_plain]:
        n = w_ref.shape[1]
        step = min(tn, n)
        for j in range(0, n, step):
            acc = jnp.dot(hx_ref[...], w_ref[:, j:j + step], preferred_element_type=F32)
            if use_gelu:
                acc = _gelu_tanh(acc)
            o_ref[:, j:j + step] = acc.astype(o_ref.dtype)
    if has_cmlp:
        w_ref, o_ref, s_ref = w_refs[-1], o_refs[-1], refs[k + n_w + 1]
        bm = x_ref.shape[0]
        vgate = _gelu_tanh(jnp.dot(hx_ref[...], w_ref[:, D_CMLP:], preferred_element_type=F32))
        for g in range(CMLP_GROUPS):
            gs = slice(g * CMLP_GD, (g + 1) * CMLP_GD)
            vg = vgate[:, gs]
            mu = jnp.mean(vg, axis=-1, keepdims=True)
            vc = vg - mu
            var = jnp.mean(vc * vc, axis=-1, keepdims=True)
            vn = (vc * lax.rsqrt(var + LN_EPS) * cnw_ref[:, gs]).astype(BF16)
            for p in range(bm // CMLP_CHUNK):
                ps = slice(p * CMLP_CHUNK, (p + 1) * CMLP_CHUNK)
                s_ref[ps, gs] = (jnp.dot(ws_ref[g], vn[ps, :], preferred_element_type=F32)
                                 + bs_ref[:, g:g + 1])
        u = _gelu_tanh(jnp.dot(hx_ref[...], w_ref[:, :D_CMLP], preferred_element_type=F32))
        o_ref[...] = (u * s_ref[...]).astype(o_ref.dtype)


def _projection(x2d, pe, mods3, mod_row_of_block, weights, out_dtypes, gelu_flags, bm, cmlp=None):
    rows, dm = x2d.shape
    has_pe = pe is not None
    n_w = len(weights)
    in_specs = [pl.BlockSpec((bm, dm), lambda i: (i, 0))]
    args = [x2d]
    if has_pe:
        pe_blocks = pe.shape[0] // bm
        in_specs.append(pl.BlockSpec((bm, dm), lambda i: (i % pe_blocks, 0)))
        args.append(pe)
    in_specs.append(pl.BlockSpec((1, N_MOD, dm), lambda i: (mod_row_of_block(i), 0, 0)))
    args.append(mods3)
    for w in weights:
        in_specs.append(_resident(w.shape))
        args.append(w)
    out_widths = [w.shape[1] for w in weights]
    scratch = [pltpu.VMEM((bm, dm), BF16)]
    if cmlp is not None:
        for a in cmlp:
            in_specs.append(_resident(a.shape))
            args.append(a)
        out_widths[-1] = D_CMLP
        scratch.append(pltpu.VMEM((bm, D_CMLP), F32))
    out_specs = [pl.BlockSpec((bm, n), lambda i: (i, 0)) for n in out_widths]
    out_shape = [jax.ShapeDtypeStruct((rows, n), dt) for n, dt in zip(out_widths, out_dtypes)]
    kern = functools.partial(_proj_kernel, n_w=n_w, has_pe=has_pe, gelu_flags=tuple(gelu_flags),
                             has_cmlp=cmlp is not None, tn=1024)
    return pl.pallas_call(
        kern,
        grid=(rows // bm,),
        in_specs=in_specs,
        out_specs=out_specs,
        out_shape=out_shape,
        scratch_shapes=scratch,
        compiler_params=_cparams(("parallel",)),
        name="projection",
    )(*args)


def _gate_kernel(lic_ref, lfc_ref, lir_ref, lfr_ref, row_ref, col_ref, *, nc):
    hl = HEAD_LANES
    li = lic_ref[0]
    lf = _log_sigmoid(lfc_ref[0])
    length = li.shape[0]
    tid = lax.broadcasted_iota(jnp.int32, li.shape, 0)
    lane = lax.broadcasted_iota(jnp.int32, li.shape, 1)
    lane_l = lane & (hl - 1)
    lane1 = lane_l[0:1, :]
    fwd = (lane_l < nc) | (lane_l == 2 * nc)

    def scan_sublanes(x, op, fill):
        p = x
        s = x
        k = 1
        while k < length:
            p = op(p, jnp.where(tid >= k, pltpu.roll(p, k, 0), fill))
            s = op(s, jnp.where(tid < length - k, pltpu.roll(s, length - k, 0), fill))
            k *= 2
        return jnp.where(fwd, p, s)

    b = scan_sublanes(lf, jnp.add, 0.0)
    btot = jnp.sum(lf, axis=0, keepdims=True)
    a = btot - b + li
    m_loc = jnp.max(a, axis=0, keepdims=True)
    r = li - b
    cm = scan_sublanes(r, jnp.maximum, -jnp.inf)

    m_ctx = jnp.maximum(btot, m_loc)
    m_in = jnp.where(lane1 == 0, pltpu.roll(m_ctx, LANE - 2 * nc, 1), pltpu.roll(m_ctx, LANE - 2, 1))
    for k in range(nc - 1):
        m_new = jnp.maximum(btot + m_in, m_loc)
        m_in = jnp.where(lane1 == k + 1, pltpu.roll(m_new, 1, 1),
                         jnp.where(lane1 == 2 * nc - 2 - k, pltpu.roll(m_new, LANE - 1, 1), m_in))
    is_ctx = lane1 >= 2 * nc
    m_in = jnp.where(is_ctx, 0.0, m_in)
    m_new = jnp.maximum(btot + m_in, m_loc)
    s_old = jnp.broadcast_to(jnp.exp(btot + m_in - m_new), li.shape)
    w = jnp.exp(a - m_new)
    big_m = jnp.maximum(m_in, cm)
    s_int = jnp.exp(m_in - big_m)
    e_neg = jnp.exp(-(b + big_m))
    g = 2 * nc
    for h in range(HEADS):
        off = h * hl
        rot = lambda x, to: pltpu.roll(x, (to - off) % LANE, 1)
        col_ref[0, h] = jnp.where(
            lane < g, rot(w, 0), jnp.where(
                lane < 2 * g, rot(big_m, g), jnp.where(
                    lane < 3 * g, rot(s_int, 2 * g), jnp.where(
                        lane < 4 * g, rot(e_neg, 3 * g), jnp.where(
                            lane < 4 * g + 2, rot(w, 3 * g), rot(s_old, 4 * g + 2))))))

    lir = lir_ref[0]
    lfr = _log_sigmoid(lfr_ref[0])
    width = lir.shape[1]
    rid = lax.broadcasted_iota(jnp.int32, lir.shape, 0) & (2 * nc - 1)
    pid = lax.broadcasted_iota(jnp.int32, lir.shape, 1)
    p = lfr
    s = lfr
    k = 1
    while k < width:
        p = p + jnp.where(pid >= k, pltpu.roll(p, k, 1), 0.0)
        s = s + jnp.where(pid < width - k, pltpu.roll(s, width - k, 1), 0.0)
        k *= 2
    row_ref[0] = lir - jnp.where(rid < nc, p, s)


def _gate_stats(g_x, g_c, gate_bias, bsz, seq, ctx_len):
    nc = seq // MCHUNK
    assert HEADS * HEAD_LANES == LANE and 2 * nc + 2 <= HEAD_LANES and (2 * nc) & (2 * nc - 1) == 0
    gb = gate_bias.astype(F32)
    gx = g_x[:, :N_GATE_COLS].reshape(bsz, nc, MCHUNK, 2, 2, HEADS) + gb.reshape(2, 2, HEADS)
    gc = g_c[:, :N_GATE_COLS].reshape(bsz, ctx_len, 2, 2, HEADS) + gb.reshape(2, 2, HEADS)
    col_x = gx.transpose(4, 0, 2, 5, 3, 1).reshape(2, bsz, MCHUNK, HEADS, 2 * nc)
    col_c = gc.transpose(3, 0, 1, 4, 2)
    col = jnp.concatenate([col_x, col_c], -1)
    col = jnp.pad(col, ((0, 0),) * 4 + ((0, HEAD_LANES - col.shape[-1]),)).reshape(2, bsz, MCHUNK, LANE)
    row = gx.transpose(4, 0, 5, 3, 1, 2).reshape(2, bsz, HEADS * 2 * nc, MCHUNK)
    blk_c = pl.BlockSpec((1, MCHUNK, LANE), lambda b: (b, 0, 0))
    blk_r = pl.BlockSpec((1, HEADS * 2 * nc, MCHUNK), lambda b: (b, 0, 0))
    rowq, colq = pl.pallas_call(
        functools.partial(_gate_kernel, nc=nc),
        grid=(bsz,),
        in_specs=[blk_c, blk_c, blk_r, blk_r],
        out_specs=[blk_r, pl.BlockSpec((1, HEADS, MCHUNK, LANE), lambda b: (b, 0, 0, 0))],
        out_shape=[jax.ShapeDtypeStruct((bsz, HEADS * 2 * nc, MCHUNK), F32),
                   jax.ShapeDtypeStruct((bsz, HEADS, MCHUNK, LANE), F32)],
        compiler_params=_cparams(("parallel",)),
        name="gate_stats",
    )(col[0], col[1], row[0], row[1])
    return rowq.reshape(bsz, HEADS, 2 * nc, MCHUNK), colq


def _mlstm_kernel(q_ref, k_ref, v_ref, o_ref, kc_ref, vc_ref, cwq_ref, cbq_ref, cwk_ref, cbk_ref,
                  row_ref, col_ref, nw_ref, band_ref, edge_ref, y_ref, q_s, k_s, kc_s, ct_s, n_s, *, nc):
    lc = MCHUNK
    halo_rows = edge_ref.shape[1]

    def conv_silu_chunk(x_ref, c, n_chunks, w, b, scale):
        x = x_ref[pl.ds(c * lc, lc), :]
        wb = w.astype(BF16)
        taps = jnp.concatenate([x * wb[0:1, :], x * wb[1:2, :], x * wb[2:3, :]], axis=0)
        y = jnp.dot(band_ref[...], taps, preferred_element_type=F32)
        if n_chunks > 1:
            rid = lax.broadcasted_iota(jnp.int32, (halo_rows, x.shape[1]), 0)
            wf = wb.astype(F32)
            halo = jnp.zeros((halo_rows, x.shape[1]), F32)
            if c > 0:
                prev = x_ref[pl.ds(c * lc - halo_rows, halo_rows), :].astype(F32)[halo_rows - 1:, :]
                halo = jnp.where(rid == 0, prev * wf[0:1, :], halo)
            if c < n_chunks - 1:
                nxt = x_ref[pl.ds((c + 1) * lc, halo_rows), :].astype(F32)[0:1, :]
                halo = jnp.where(rid == 1, nxt * wf[2:3, :], halo)
            y = y + jnp.dot(edge_ref[...], halo.astype(BF16), preferred_element_type=F32)
        y = _silu(y + b)
        if scale != 1.0:
            y = y * scale
        return y.astype(BF16)

    k_scale = HD ** -0.5
    for c in range(nc):
        sl = pl.ds(c * lc, lc)
        q_s[sl, :] = conv_silu_chunk(q_ref, c, nc, cwq_ref[...], cbq_ref[...], 1.0)
        k_s[sl, :] = conv_silu_chunk(k_ref, c, nc, cwk_ref[...], cbk_ref[...], k_scale)
    kc_s[...] = conv_silu_chunk(kc_ref, 0, 1, cwk_ref[...], cbk_ref[...], k_scale)

    def col(j):
        return col_ref[0, 0, :, j:j + 1]

    stats_t = col_ref[0, 0].T

    def wrow(j):
        return stats_t[j:j + 1, :].astype(BF16)

    def local_state(kk, vv, w_row):
        ktw = kk.T * w_row
        ct = jnp.dot(ktw, vv, preferred_element_type=F32)
        nn = jnp.dot(jnp.broadcast_to(w_row, (SUBLANES, lc)), kk, preferred_element_type=F32)[0:1, :]
        return ct, nn

    for d in range(2):
        ct, nn = local_state(kc_s[...], vc_ref[...], wrow(8 * nc + d))
        order = list(range(nc)) if d == 0 else list(range(nc - 1, -1, -1))
        for pos, c in enumerate(order):
            idx = d * nc + c
            ct_s[idx] = ct.astype(BF16)
            n_s[idx] = nn
            if pos == nc - 1:
                break
            sl = pl.ds(c * lc, lc)
            ctl, nl = local_state(k_s[sl, :], v_ref[sl, :], wrow(idx))
            s_old = col_ref[0, 0, 0:1, 8 * nc + 2 + idx:8 * nc + 3 + idx]
            ct = s_old * ct + ctl
            nn = s_old * nn + nl

    tid = lax.broadcasted_iota(jnp.int32, (lc, lc), 0)
    sid = lax.broadcasted_iota(jnp.int32, (lc, lc), 1)
    masks = (sid <= tid, sid >= tid)
    for c in range(nc):
        sl = pl.ds(c * lc, lc)
        q = q_s[sl, :]
        kk = k_s[sl, :]
        v = v_ref[sl, :]
        qf = q.astype(F32)
        s = lax.dot_general(q, kk, (((1,), (1,)), ((), ())), preferred_element_type=F32)
        h = None
        for d in range(2):
            idx = d * nc + c
            r = row_ref[0, 0, idx:idx + 1, :]
            big_m = col(2 * nc + idx)
            s_int = col(4 * nc + idx)
            e_neg = col(6 * nc + idx)
            p = jnp.where(masks[d], jnp.exp(r - big_m), 0.0) * s
            den = (jnp.sum(p, axis=-1, keepdims=True)
                   + s_int * jnp.sum(qf * n_s[idx], axis=-1, keepdims=True))
            num = (jnp.dot(p.astype(BF16), v, preferred_element_type=F32)
                   + s_int * jnp.dot(q, ct_s[idx], preferred_element_type=F32))
            hd = num * (1.0 / jnp.maximum(jnp.abs(den), e_neg))
            h = hd if h is None else h + hd
        mu = jnp.mean(h, axis=-1, keepdims=True)
        hc = h - mu
        var = jnp.mean(hc * hc, axis=-1, keepdims=True)
        hn = hc * lax.rsqrt(var + LN_EPS) * nw_ref[...]
        y_ref[sl, :] = (hn * _sigmoid(o_ref[sl, :].astype(F32))).astype(BF16)


def _mlstm(qkvo, kv_ctx, conv_w, conv_b, rowq, colq, norm_w, bsz, seq, ctx_len):
    nc = seq // MCHUNK
    hq = D_MLSTM // HD
    kern = functools.partial(_mlstm_kernel, nc=nc)
    ii = jnp.arange(MCHUNK)[:, None]
    jj = jnp.arange(MCHUNK)[None, :]
    band = jnp.concatenate([(jj == ii + t - 1) for t in range(3)], axis=1).astype(BF16)
    hh = jnp.arange(2 * SUBLANES)[None, :]
    edge = (((ii == 0) & (hh == 0)) | ((ii == MCHUNK - 1) & (hh == 1))).astype(BF16)
    seq_blk = lambda off: pl.BlockSpec((seq, HD), lambda b, h: (b, off + h))
    ctx_blk = lambda off: pl.BlockSpec((ctx_len, HD), lambda b, h: (b, off + h))
    return pl.pallas_call(
        kern,
        grid=(bsz, HEADS),
        in_specs=[seq_blk(0), seq_blk(hq), seq_blk(2 * hq), seq_blk(3 * hq),
                  ctx_blk(0), ctx_blk(hq),
                  pl.BlockSpec((3, HD), lambda b, h: (0, h)),
                  pl.BlockSpec((1, HD), lambda b, h: (0, h)),
                  pl.BlockSpec((3, HD), lambda b, h: (0, hq + h)),
                  pl.BlockSpec((1, HD), lambda b, h: (0, hq + h)),
                  pl.BlockSpec((1, 1, 2 * nc, MCHUNK), lambda b, h: (b, h, 0, 0)),
                  pl.BlockSpec((1, 1, MCHUNK, LANE), lambda b, h: (b, h, 0, 0)),
                  pl.BlockSpec((1, HD), lambda b, h: (0, h)),
                  _resident(band.shape), _resident(edge.shape)],
        out_specs=pl.BlockSpec((seq, HD), lambda b, h: (b, h)),
        out_shape=jax.ShapeDtypeStruct((bsz * seq, D_MLSTM), BF16),
        scratch_shapes=[pltpu.VMEM((seq, HD), BF16), pltpu.VMEM((seq, HD), BF16),
                        pltpu.VMEM((ctx_len, HD), BF16),
                        pltpu.VMEM((2 * nc, HD, HD), BF16), pltpu.VMEM((2 * nc, 1, HD), F32)],
        compiler_params=_cparams(("parallel", "parallel")),
        name="mlstm",
    )(qkvo, qkvo, qkvo, qkvo, kv_ctx, kv_ctx, conv_w, conv_b, conv_w, conv_b, rowq, colq, norm_w,
      band, edge)


def _out_kernel(ym_ref, yc_ref, x_ref, pe_ref, m_ref, wout_ref,
                l1w_ref, l1b_ref, wr_ref, br_ref, x1_ref, h2_ref, lg_ref):
    m = m_ref[0]
    y = (jnp.dot(ym_ref[...], wout_ref[:D_MLSTM, :], preferred_element_type=F32)
         + jnp.dot(yc_ref[...], wout_ref[D_MLSTM:, :], preferred_element_type=F32))
    z = DEEPNORM_ALPHA * (x_ref[...] + pe_ref[...]) + m[2:3, :] * y
    x1 = _layer_norm_rows(z, l1w_ref[...], l1b_ref[...])
    x1_ref[...] = x1
    h2 = x1 * (1.0 + m[4:5, :]) + m[3:4, :]
    half = h2.shape[1] // 2
    packed = _pack_bf16_pairs(h2[:, :half], h2[:, half:])
    for c in range(ROW_LINES):
        h2_ref[pl.ds(c, h2.shape[0], stride=ROW_LINES), :] = packed[:, c * LANE:(c + 1) * LANE]
    lg = jnp.dot(h2.astype(BF16), wr_ref[...], preferred_element_type=F32) + br_ref[...]

    lgt = lg.T
    bm = lg.shape[0]
    epg = EXPERTS_PER_GROUP
    rid = lax.broadcasted_iota(jnp.int32, (SUBLANES, bm), 0).astype(F32)
    neg = -jnp.inf
    far = float(SUBLANES)
    gl = lgt[N_EXPERTS:N_EXPERTS + SUBLANES, :]
    is_grp = rid < N_GROUPS
    m1 = jnp.max(jnp.where(is_grp, gl, neg), axis=0, keepdims=True)
    grp = jnp.min(jnp.where(is_grp & (gl == m1), rid, far), axis=0, keepdims=True)
    p_grp = 1.0 / jnp.sum(jnp.where(is_grp, jnp.exp(gl - m1), 0.0), axis=0, keepdims=True)
    l2 = lgt[0:epg, :]
    for g in range(1, N_GROUPS):
        l2 = jnp.where(grp == g, lgt[g * epg:(g + 1) * epg, :], l2)
    v0 = jnp.max(l2, axis=0, keepdims=True)
    i0 = jnp.min(jnp.where(l2 == v0, rid, far), axis=0, keepdims=True)
    l2m = jnp.where(rid == i0, neg, l2)
    v1 = jnp.max(l2m, axis=0, keepdims=True)
    i1 = jnp.min(jnp.where(l2m == v1, rid, far), axis=0, keepdims=True)
    s1 = jnp.exp(v1 - v0)
    g0 = p_grp / (1.0 + s1)
    res = jnp.where(rid == 0, g0, jnp.where(rid == 1, g0 * s1, jnp.where(
        rid == 2, grp * epg + i0, jnp.where(rid == 3, grp * epg + i1, 0.0))))
    lg_ref[...] = jnp.concatenate([res, jnp.zeros((LANE - SUBLANES, bm), F32)], axis=0).T


def _mixer_out(ym, yc, x2d, pe, mods3, wout, l1w, l1b, wr, br, seq, bm):
    rows, dm = x2d.shape
    pe_blocks = seq // bm
    row_blk = lambda n: pl.BlockSpec((bm, n), lambda i: (i, 0))
    return pl.pallas_call(
        _out_kernel,
        grid=(rows // bm,),
        in_specs=[row_blk(D_MLSTM), row_blk(D_CMLP), row_blk(dm),
                  pl.BlockSpec((bm, dm), lambda i: (i % pe_blocks, 0)),
                  pl.BlockSpec((1, N_MOD, dm), lambda i: (i // pe_blocks, 0, 0)),
                  _resident(wout.shape), _resident(l1w.shape), _resident(l1b.shape),
                  _resident(wr.shape), _resident(br.shape)],
        out_specs=[row_blk(dm), pl.BlockSpec((bm * ROW_LINES, LANE), lambda i: (i, 0)), row_blk(LANE)],
        out_shape=[jax.ShapeDtypeStruct((rows, dm), F32),
                   jax.ShapeDtypeStruct((rows * ROW_LINES, LANE), jnp.uint32),
                   jax.ShapeDtypeStruct((rows, LANE), F32)],
        compiler_params=_cparams(("parallel",)),
        name="mixer_out",
    )(ym, yc, x2d, pe, mods3, wout, l1w, l1b, wr, br)


def _moe_kernel(comp_ref, blk_ref, slot_ref, cast_ref, cslot_ref, cexp_ref, ctile_ref,
                gtab_ref, stab_ref, h2_hbm, wgf_ref, wuf_ref, wdf_ref, ys_hbm,
                wg_s, wu_s, wd_s, xbuf, xb_s, ybuf, gsem, ssem, pend, *, nt):
    del blk_ref, cexp_ref
    s = pl.program_id(0)
    n_steps = pl.num_programs(0)
    groups = MOE_ROWS // SUBLANES
    rl = ROW_LINES
    half = rl * LANE
    dm = 2 * half
    n_asg = ys_hbm.shape[0] // rl - 2 * MOE_ROWS
    group_lines = SUBLANES * rl

    def gather_wait(p, n):
        nl = n * group_lines
        pltpu.make_async_copy(h2_hbm.at[pl.ds(0, nl)], xbuf.at[p, pl.ds(0, nl)], gsem.at[p]).wait()

    def scatter_wait(p, n):
        nl = n * group_lines
        pltpu.make_async_copy(ybuf.at[p, pl.ds(0, nl)], ys_hbm.at[pl.ds(0, nl)], ssem.at[p]).wait()

    @pl.when(s == 0)
    def _():
        pend[0] = 0
        pend[1] = 0
        xbuf[...] = jnp.zeros_like(xbuf)

    nxt = jnp.minimum(s + 1, n_steps - 1)

    @pl.when(jnp.logical_and(s + 1 < n_steps, comp_ref[nxt] > 0))
    def _():
        p = (s + 1) % 2

        for par in range(2):
            for g0 in range(0, groups, GROUP_STEP):
                @pl.when(jnp.logical_and(p == par, g0 < comp_ref[nxt]))
                def _():
                    for r in range(g0 * SUBLANES, (g0 + GROUP_STEP) * SUBLANES):
                        src = pl.multiple_of(gtab_ref[0, 0, r] * rl, rl)
                        pltpu.make_async_copy(h2_hbm.at[pl.ds(src, rl)], xbuf.at[par, pl.ds(r * rl, rl)],
                                              gsem.at[par]).start()

    @pl.when(cast_ref[s] == 1)
    def _():
        cs = cslot_ref[s]
        t = ctile_ref[s]
        for k in range(nt):
            @pl.when(t == k)
            def _():
                wg_s[cs, :, k * MOE_FT:(k + 1) * MOE_FT] = wgf_ref[0].astype(BF16)
                wu_s[cs, :, k * MOE_FT:(k + 1) * MOE_FT] = wuf_ref[0].astype(BF16)
        wd_s[cs, t] = wdf_ref[0].astype(BF16)

    @pl.when(comp_ref[s] > 0)
    def _():
        p = s % 2
        sl = slot_ref[s]
        ng = comp_ref[s]
        gather_wait(p, ng)

        @pl.when(pend[p] > 0)
        def _():
            scatter_wait(p, pend[p])

        def run_block(rows):
            for c in range(rl):
                x_lo, x_hi = _unpack_bf16_pairs(xbuf[p, pl.ds(c, rows, stride=rl), :])
                xb_s[:rows, c * LANE:(c + 1) * LANE] = x_lo.astype(BF16)
                xb_s[:rows, half + c * LANE:half + (c + 1) * LANE] = x_hi.astype(BF16)
            x = xb_s[:rows, :]
            g = jnp.dot(x, wg_s[sl], preferred_element_type=F32)
            u = jnp.dot(x, wu_s[sl], preferred_element_type=F32)
            h = (_silu(g) * u).astype(BF16)
            y = jnp.dot(h, wd_s[sl].reshape(D_EXPERT, dm), preferred_element_type=F32)
            y_packed = _pack_bf16_pairs(y[:, :half], y[:, half:])
            for c in range(rl):
                ybuf[p, pl.ds(c, rows, stride=rl), :] = y_packed[:, c * LANE:(c + 1) * LANE]

        small_groups = MOE_SMALL_ROWS // SUBLANES

        @pl.when(ng > small_groups)
        def _():
            run_block(MOE_ROWS)

        @pl.when(ng <= small_groups)
        def _():
            run_block(MOE_SMALL_ROWS)

        for par in range(2):
            for g0 in range(0, groups, GROUP_STEP):
                @pl.when(jnp.logical_and(p == par, g0 < ng))
                def _():
                    for r in range(g0 * SUBLANES, (g0 + GROUP_STEP) * SUBLANES):
                        d = stab_ref[0, 0, r]
                        dst = pl.multiple_of(jnp.where(d < 0, n_asg + par * MOE_ROWS + r, d) * rl, rl)
                        pltpu.make_async_copy(ybuf.at[par, pl.ds(r * rl, rl)], ys_hbm.at[pl.ds(dst, rl)],
                                              ssem.at[par]).start()

        pend[p] = ng

    @pl.when(s == n_steps - 1)
    def _():
        for p in range(2):
            @pl.when(pend[p] > 0)
            def _():
                scatter_wait(p, pend[p])
                pend[p] = 0

        xbuf[...] = jnp.zeros_like(xbuf)
        fills = [pltpu.make_async_copy(xbuf.at[p], ys_hbm.at[pl.ds((n_asg + p * MOE_ROWS) * rl, MOE_ROWS * rl)],
                                       gsem.at[p]) for p in range(2)]
        for cp in fills:
            cp.start()
        for cp in fills:
            cp.wait()


def _experts(h2, gtab, stab, sched, wg, wu, wd):
    n_tok, dm = h2.shape[0] // ROW_LINES, D_MODEL
    nt = D_EXPERT // MOE_FT
    n_asg = 2 * n_tok
    n_steps = sched[0].shape[0]

    smem_rows = lambda imap: pl.BlockSpec((1, 1, MOE_ROWS), imap, memory_space=pltpu.SMEM)
    grid_spec = pltpu.PrefetchScalarGridSpec(
        num_scalar_prefetch=7,
        grid=(n_steps,),
        in_specs=[smem_rows(lambda s, comp, blk, *_: (blk[jnp.minimum(s + 1, n_steps - 1)], 0, 0)),
                  smem_rows(lambda s, comp, blk, *_: (blk[s], 0, 0)),
                  pl.BlockSpec(memory_space=pl.ANY),
                  pl.BlockSpec((1, dm, MOE_FT), lambda s, c, b, sl, ca, cs, ce, ct: (ce[s], 0, ct[s])),
                  pl.BlockSpec((1, dm, MOE_FT), lambda s, c, b, sl, ca, cs, ce, ct: (ce[s], 0, ct[s])),
                  pl.BlockSpec((1, MOE_FT, dm), lambda s, c, b, sl, ca, cs, ce, ct: (ce[s], ct[s], 0))],
        out_specs=pl.BlockSpec(memory_space=pl.ANY),
        scratch_shapes=[pltpu.VMEM((2, dm, D_EXPERT), BF16), pltpu.VMEM((2, dm, D_EXPERT), BF16),
                        pltpu.VMEM((2, nt, MOE_FT, dm), BF16),
                        pltpu.VMEM((2, MOE_ROWS * ROW_LINES, LANE), jnp.uint32),
                        pltpu.VMEM((MOE_ROWS, dm), BF16),
                        pltpu.VMEM((2, MOE_ROWS * ROW_LINES, LANE), jnp.uint32),
                        pltpu.SemaphoreType.DMA((2,)), pltpu.SemaphoreType.DMA((2,)),
                        pltpu.SMEM((2,), jnp.int32)],
    )
    return pl.pallas_call(
        functools.partial(_moe_kernel, nt=nt),
        grid_spec=grid_spec,
        out_shape=jax.ShapeDtypeStruct(((n_asg + 2 * MOE_ROWS) * ROW_LINES, LANE), jnp.uint32),
        compiler_params=_cparams(("arbitrary",)),
        name="experts",
    )(*sched, gtab, stab, h2, wg, wu, wd)


def _final_kernel(x1_ref, y0_ref, y1_ref, g_ref, m_ref, w_ref, b_ref, o_ref, y_s):
    m = m_ref[0]
    bm, dm = x1_ref.shape
    half = dm // 2
    g0 = g_ref[:, 0:1]
    g1 = g_ref[:, 1:2]
    for c in range(ROW_LINES):
        rows_c = pl.ds(c, bm, stride=ROW_LINES)
        lo0, hi0 = _unpack_bf16_pairs(y0_ref[rows_c, :])
        lo1, hi1 = _unpack_bf16_pairs(y1_ref[rows_c, :])
        y_s[:, c * LANE:(c + 1) * LANE] = g0 * lo0 + g1 * lo1
        y_s[:, half + c * LANE:half + (c + 1) * LANE] = g0 * hi0 + g1 * hi1
    z = DEEPNORM_ALPHA * x1_ref[...] + m[5:6, :] * y_s[...]
    o_ref[...] = _layer_norm_rows(z, w_ref[...], b_ref[...])


def _final(x1, ys, gates, mods3, w, b, seq, bm):
    rows, dm = x1.shape
    blocks_per_batch = seq // bm
    slot_blocks = rows // bm
    row_blk = lambda n: pl.BlockSpec((bm, n), lambda i: (i, 0))
    return pl.pallas_call(
        _final_kernel,
        grid=(rows // bm,),
        in_specs=[row_blk(dm), pl.BlockSpec((bm * ROW_LINES, LANE), lambda i: (i, 0)),
                  pl.BlockSpec((bm * ROW_LINES, LANE), lambda i: (i + slot_blocks, 0)),
                  row_blk(LANE),
                  pl.BlockSpec((1, N_MOD, dm), lambda i: (i // blocks_per_batch, 0, 0)),
                  _resident(w.shape), _resident(b.shape)],
        out_specs=row_blk(dm),
        out_shape=jax.ShapeDtypeStruct((rows, dm), F32),
        scratch_shapes=[pltpu.VMEM((bm, dm), F32)],
        compiler_params=_cparams(("parallel",)),
        name="final_ln",
    )(x1, ys, ys, gates, mods3, w, b)


def _route(route, n_tok):
    e_flat = route[:, 2:4].astype(jnp.int32).reshape(-1)
    n_asg = e_flat.shape[0]
    earange = jnp.arange(N_EXPERTS, dtype=jnp.int32)
    counts = jnp.sum((e_flat[:, None] == earange[None, :]).astype(jnp.int32), 0)
    nblk_e = (counts + MOE_ROWS - 1) // MOE_ROWS
    pad_end = jnp.cumsum(nblk_e * MOE_ROWS)
    pad_start = pad_end - nblk_e * MOE_ROWS
    n_blk = n_asg // MOE_ROWS + N_EXPERTS

    order = jnp.argsort(e_flat, stable=True).astype(jnp.int32)
    starts = jnp.cumsum(counts) - counts
    ridx = jnp.arange(n_blk * MOE_ROWS, dtype=jnp.int32)
    e_row = jnp.minimum(jnp.sum((pad_end[None, :] <= ridx[:, None]).astype(jnp.int32), -1), N_EXPERTS - 1)
    rsel = e_row[:, None] == earange[None, :]
    k_row = ridx - jnp.sum(jnp.where(rsel, pad_start[None, :], 0), -1)
    valid = k_row < jnp.sum(jnp.where(rsel, counts[None, :], 0), -1)
    src = jnp.clip(jnp.sum(jnp.where(rsel, starts[None, :], 0), -1) + k_row, 0, n_asg - 1)
    row_asg = jnp.where(valid, order[src], -1)
    gtab = (jnp.maximum(row_asg, 0) >> 1).reshape(n_blk, 1, MOE_ROWS)
    stab = jnp.where(row_asg < 0, -1, (row_asg & 1) * n_tok + (row_asg >> 1)).reshape(n_blk, 1, MOE_ROWS)

    nt = D_EXPERT // MOE_FT
    n_steps = n_blk + (nt - 1) * N_EXPERTS + nt
    has = nblk_e > 0
    n_visits = jnp.sum(has.astype(jnp.int32))
    e_of_visit = jnp.sort(jnp.where(has, earange, N_EXPERTS))
    vsel = e_of_visit[:, None] == earange[None, :]
    nb_v = jnp.sum(jnp.where(vsel, nblk_e[None, :], 0), -1)
    steps_v = jnp.where(nb_v > 0, jnp.maximum(nb_v, nt), 0)
    end_v = nt + jnp.cumsum(steps_v)
    start_v = end_v - steps_v
    first_blk_v = jnp.cumsum(nb_v) - nb_v
    sidx = jnp.arange(n_steps, dtype=jnp.int32)
    v = jnp.sum((end_v[None, :] <= sidx[:, None]).astype(jnp.int32), -1)
    pick = lambda arr, idx: jnp.sum(jnp.where(idx[:, None] == earange[None, :], arr[None, :], 0), -1)
    in_visit = (sidx >= nt) & (v < n_visits)
    k = sidx - pick(start_v, v)
    comp = in_visit & (k < pick(nb_v, v))
    cnt_v = jnp.sum(jnp.where(vsel, counts[None, :], 0), -1)
    rows_s = jnp.clip(pick(cnt_v, v) - k * MOE_ROWS, 0, MOE_ROWS)
    chunk_rows = GROUP_STEP * SUBLANES
    groups_s = jnp.where(comp, (rows_s + chunk_rows - 1) // chunk_rows * GROUP_STEP, 0)
    blk_s = lax.cummax(jnp.where(comp, pick(first_blk_v, v) + k, 0), axis=0)
    prologue = sidx < nt
    cast = prologue | (in_visit & (k < nt) & (v + 1 < n_visits))
    cexp = jnp.where(prologue, e_of_visit[0], pick(e_of_visit, v + 1))
    ctile = jnp.where(prologue, sidx, k)
    code = lax.cummax(jnp.where(cast, cexp * nt + ctile, 0), axis=0)
    i32 = lambda a: a.astype(jnp.int32)
    sched = (i32(groups_s), i32(blk_s), i32(v % 2), i32(cast), i32(jnp.where(prologue, 0, (v + 1) % 2)),
             i32(jnp.minimum(code // nt, N_EXPERTS - 1)), i32(code % nt))
    return sched, gtab, stab


def _grid_pos_embed(rows):
    quarter = D_MODEL // 4
    omega = 1.0 / (10000.0 ** (jnp.arange(quarter, dtype=F32) / quarter))
    ar = jnp.arange(rows, dtype=F32)[:, None] * omega
    ac = jnp.arange(GRID_W, dtype=F32)[:, None] * omega
    shape = (rows, GRID_W, quarter)
    parts = [jnp.broadcast_to(jnp.sin(ar)[:, None, :], shape), jnp.broadcast_to(jnp.cos(ar)[:, None, :], shape),
             jnp.broadcast_to(jnp.sin(ac)[None, :, :], shape), jnp.broadcast_to(jnp.cos(ac)[None, :, :], shape)]
    return jnp.concatenate(parts, -1).reshape(rows * GRID_W, D_MODEL)


def kernel(x, c, ctx, c_ctx, w_mod, b_mod, w_in, conv_w, conv_b, gate_bias, mlstm_norm_w, cmlp_norm_w,
           w_s, b_s, w_out, ln1_w, ln1_b, router1_w, router1_b, router2_w, router2_b, w_gate, w_up,
           w_down, ln2_w, ln2_b):
    bsz, seq, dm = x.shape
    ctx_len = ctx.shape[1]
    n_tok = bsz * seq
    assert w_mod.shape[0] == 1 and dm == D_MODEL and seq % MCHUNK == 0 and ctx_len == MCHUNK
    pe = _grid_pos_embed(seq // GRID_W).astype(x.dtype)
    x2d = x.reshape(n_tok, dm)
    ctx2d = ctx.reshape(bsz * ctx_len, dm)

    mod_rows = 16
    cc = jnp.concatenate([c, c_ctx[None, :], jnp.zeros((mod_rows - bsz - 1, dm), c.dtype)], 0)
    mods3 = _modulation(cc, w_mod[0], b_mod[0]).reshape(mod_rows, N_MOD, dm)

    dq = D_MLSTM
    wi = w_in[0]
    w_qkvo = wi[:, :4 * dq].astype(BF16)
    w_g = jnp.pad(wi[:, 4 * dq:4 * dq + N_GATE_COLS], ((0, 0), (0, LANE - N_GATE_COLS))).astype(BF16)
    w_uv = wi[:, 4 * dq + N_GATE_COLS:].astype(BF16)
    bm_proj = 512
    blocks_per_seq = seq // bm_proj
    cmlp = (cmlp_norm_w[0].reshape(1, -1), w_s[0].astype(BF16), b_s[0].T)
    qkvo, g_x, yc = _projection(x2d, pe, mods3, lambda i: i // blocks_per_seq,
                                [w_qkvo, w_g, w_uv], [BF16, F32, BF16], [False, False, True], bm_proj,
                                cmlp=cmlp)
    w_kv = wi[:, dq:3 * dq].astype(BF16)
    kv_c, g_c = _projection(ctx2d, None, mods3, lambda i: bsz, [w_kv, w_g], [BF16, F32],
                            [False, False], bm_proj)

    rowq, colq = _gate_stats(g_x, g_c, gate_bias[0], bsz, seq, ctx_len)
    ym = _mlstm(qkvo, kv_c, conv_w[0], conv_b[0].reshape(1, -1), rowq, colq,
                mlstm_norm_w[0].reshape(1, -1), bsz, seq, ctx_len)

    wr = jnp.pad(jnp.concatenate([router2_w[0], router1_w[0]], 1),
                 ((0, 0), (0, LANE - N_GROUPS - N_EXPERTS))).astype(BF16)
    br = jnp.pad(jnp.concatenate([router2_b[0], router1_b[0]], 0),
                 (0, LANE - N_GROUPS - N_EXPERTS)).reshape(1, LANE)
    x1, h2, route = _mixer_out(ym, yc, x2d, pe, mods3, w_out[0].astype(BF16),
                               ln1_w[0].reshape(1, -1), ln1_b[0].reshape(1, -1), wr, br, seq, 512)

    sched, gtab, stab = _route(route, n_tok)
    ys = _experts(h2, gtab, stab, sched, w_gate[0], w_up[0], w_down[0])
    out = _final(x1, ys, route, mods3, ln2_w[0].reshape(1, -1),
                 ln2_b[0].reshape(1, -1), seq, 512)
    return out.reshape(bsz, seq, dm)
```

```python
import functools

import jax
import jax.numpy as jnp
from jax import lax
from jax.experimental import pallas as pl
from jax.experimental.pallas import tpu as pltpu

F32 = jnp.float32
BF16 = jnp.bfloat16

D_MODEL = 2048
GRID_W = 64
D_MLSTM = 1024
D_CMLP = 1024
HEADS = 4
HD = 256
CMLP_GROUPS = 4
CMLP_GD = 256
CMLP_CHUNK = 128
N_GROUPS = 4
EXPERTS_PER_GROUP = 8
N_EXPERTS = 32
D_EXPERT = 1024
N_MOD = 6
N_GATE_COLS = 16
DEEPNORM_ALPHA = 2.0 ** 0.25
LN_EPS = 1e-6

LANE = 128
SUBLANES = 8
MCHUNK = 256
MOE_ROWS = 256
MOE_SMALL_ROWS = 64
MOE_FT = 512
GROUP_STEP = 8
HEAD_LANES = LANE // HEADS
ROW_LINES = D_MODEL // 2 // LANE
VMEM_LIMIT = 62 * 1024 * 1024


def _cparams(sem):
    return pltpu.CompilerParams(dimension_semantics=sem, vmem_limit_bytes=VMEM_LIMIT)


def _resident(shape):
    nd = len(shape)
    return pl.BlockSpec(shape, lambda *_: (0,) * nd, pipeline_mode=pl.Buffered(1))


def _sigmoid(x):
    return 0.5 * jnp.tanh(0.5 * x) + 0.5


def _silu(x):
    return x * _sigmoid(x)


def _log_sigmoid(x):
    return jnp.minimum(x, 0.0) - jnp.log1p(jnp.exp(-jnp.abs(x)))


def _gelu_tanh(x):
    c = 0.7978845608028654
    return 0.5 * x * (1.0 + jnp.tanh(c * (x + 0.044715 * (x * x * x))))


def _pack_bf16_pairs(lo, hi):
    lo_b = lax.bitcast_convert_type(lo.astype(BF16).astype(F32), jnp.uint32)
    hi_b = lax.bitcast_convert_type(hi.astype(BF16).astype(F32), jnp.uint32)
    return (lo_b >> 16) | (hi_b & jnp.uint32(0xFFFF0000))


def _unpack_bf16_pairs(w):
    lo = lax.bitcast_convert_type(w << 16, F32)
    hi = lax.bitcast_convert_type(w & jnp.uint32(0xFFFF0000), F32)
    return lo, hi


def _layer_norm_rows(z, w, b):
    mu = jnp.mean(z, axis=-1, keepdims=True)
    zc = z - mu
    var = jnp.mean(zc * zc, axis=-1, keepdims=True)
    return zc * lax.rsqrt(var + LN_EPS) * w + b


def _mod_kernel(c_ref, w_ref, b_ref, o_ref):
    s = _silu(c_ref[...]).astype(BF16)
    o_ref[...] = jnp.dot(s, w_ref[...].astype(BF16), preferred_element_type=F32) + b_ref[...]


def _modulation(cc, w_mod, b_mod):
    rows, dm = cc.shape
    n = w_mod.shape[1]
    tn = 1024
    return pl.pallas_call(
        _mod_kernel,
        grid=(n // tn,),
        in_specs=[pl.BlockSpec((rows, dm), lambda j: (0, 0)),
                  pl.BlockSpec((dm, tn), lambda j: (0, j)),
                  pl.BlockSpec((1, tn), lambda j: (0, j))],
        out_specs=pl.BlockSpec((rows, tn), lambda j: (0, j)),
        out_shape=jax.ShapeDtypeStruct((rows, n), F32),
        compiler_params=_cparams(("arbitrary",)),
        name="modulation",
    )(cc, w_mod, b_mod.reshape(1, n))


def _proj_kernel(*refs, n_w, has_pe, gelu_flags, has_cmlp, tn):
    x_ref = refs[0]
    k = 1
    pe_ref = None
    if has_pe:
        pe_ref = refs[k]
        k += 1
    m_ref = refs[k]
    k += 1
    w_refs = refs[k:k + n_w]
    k += n_w
    if has_cmlp:
        cnw_ref, ws_ref, bs_ref = refs[k:k + 3]
        k += 3
    o_refs = refs[k:k + n_w]
    hx_ref = refs[k + n_w]
    x = x_ref[...]
    if has_pe:
        x = x + pe_ref[...]
    m = m_ref[0]
    hx_ref[...] = (x * (1.0 + m[1:2, :]) + m[0:1, :]).astype(BF16)
    n_plain = n_w - 1 if has_cmlp else n_w
    for w_ref, o_ref, use_gelu in list(zip(w_refs, o_refs, gelu_flags))[:n_plain]:
        n = w_ref.shape[1]
        step = min(tn, n)
        for j in range(0, n, step):
            acc = jnp.dot(hx_ref[...], w_ref[:, j:j + step], preferred_element_type=F32)
            if use_gelu:
                acc = _gelu_tanh(acc)
            o_ref[:, j:j + step] = acc.astype(o_ref.dtype)
    if has_cmlp:
        w_ref, o_ref, s_ref = w_refs[-1], o_refs[-1], refs[k + n_w + 1]
        bm = x_ref.shape[0]
        vgate = _gelu_tanh(jnp.dot(hx_ref[...], w_ref[:, D_CMLP:], preferred_element_type=F32))
        for g in range(CMLP_GROUPS):
            gs = slice(g * CMLP_GD, (g + 1) * CMLP_GD)
            vg = vgate[:, gs]
            mu = jnp.mean(vg, axis=-1, keepdims=True)
            vc = vg - mu
            var = jnp.mean(vc * vc, axis=-1, keepdims=True)
            vn = (vc * lax.rsqrt(var + LN_EPS) * cnw_ref[:, gs]).astype(BF16)
            for p in range(bm // CMLP_CHUNK):
                ps = slice(p * CMLP_CHUNK, (p + 1) * CMLP_CHUNK)
                s_ref[ps, gs] = (jnp.dot(ws_ref[g], vn[ps, :], preferred_element_type=F32)
                                 + bs_ref[:, g:g + 1])
        u = _gelu_tanh(jnp.dot(hx_ref[...], w_ref[:, :D_CMLP], preferred_element_type=F32))
        o_ref[...] = (u * s_ref[...]).astype(o_ref.dtype)


def _projection(x2d, pe, mods3, mod_row_of_block, weights, out_dtypes, gelu_flags, bm, cmlp=None):
    rows, dm = x2d.shape
    has_pe = pe is not None
    n_w = len(weights)
    blk = lambda i: i
    if has_pe:
        pe_blocks = pe.shape[0] // bm
        bsz = rows // pe.shape[0]
        blk = lambda i: (i % bsz) * pe_blocks + i // bsz
    in_specs = [pl.BlockSpec((bm, dm), lambda i: (blk(i), 0))]
    args = [x2d]
    if has_pe:
        in_specs.append(pl.BlockSpec((bm, dm), lambda i: (i // bsz, 0)))
        args.append(pe)
    in_specs.append(pl.BlockSpec((1, N_MOD, dm), lambda i: (mod_row_of_block(i), 0, 0)))
    args.append(mods3)
    for w in weights:
        in_specs.append(_resident(w.shape))
        args.append(w)
    out_widths = [w.shape[1] for w in weights]
    scratch = [pltpu.VMEM((bm, dm), BF16)]
    if cmlp is not None:
        for a in cmlp:
            in_specs.append(_resident(a.shape))
            args.append(a)
        out_widths[-1] = D_CMLP
        scratch.append(pltpu.VMEM((bm, D_CMLP), F32))
    out_specs = [pl.BlockSpec((bm, n), lambda i: (blk(i), 0)) for n in out_widths]
    out_shape = [jax.ShapeDtypeStruct((rows, n), dt) for n, dt in zip(out_widths, out_dtypes)]
    kern = functools.partial(_proj_kernel, n_w=n_w, has_pe=has_pe, gelu_flags=tuple(gelu_flags),
                             has_cmlp=cmlp is not None, tn=1024)
    return pl.pallas_call(
        kern,
        grid=(rows // bm,),
        in_specs=in_specs,
        out_specs=out_specs,
        out_shape=out_shape,
        scratch_shapes=scratch,
        compiler_params=_cparams(("parallel",)),
        name="projection",
    )(*args)


def _gate_kernel(lic_ref, lfc_ref, lir_ref, lfr_ref, row_ref, col_ref, *, nc):
    hl = HEAD_LANES
    li = lic_ref[0]
    lf = _log_sigmoid(lfc_ref[0])
    length = li.shape[0]
    tid = lax.broadcasted_iota(jnp.int32, li.shape, 0)
    lane = lax.broadcasted_iota(jnp.int32, li.shape, 1)
    lane_l = lane & (hl - 1)
    lane1 = lane_l[0:1, :]
    fwd = (lane_l < nc) | (lane_l == 2 * nc)

    def scan_sublanes(x, op, fill):
        p = x
        s = x
        k = 1
        while k < length:
            p = op(p, jnp.where(tid >= k, pltpu.roll(p, k, 0), fill))
            s = op(s, jnp.where(tid < length - k, pltpu.roll(s, length - k, 0), fill))
            k *= 2
        return jnp.where(fwd, p, s)

    b = scan_sublanes(lf, jnp.add, 0.0)
    btot = jnp.sum(lf, axis=0, keepdims=True)
    a = btot - b + li
    m_loc = jnp.max(a, axis=0, keepdims=True)
    r = li - b
    cm = scan_sublanes(r, jnp.maximum, -jnp.inf)

    m_ctx = jnp.maximum(btot, m_loc)
    m_in = jnp.where(lane1 == 0, pltpu.roll(m_ctx, LANE - 2 * nc, 1), pltpu.roll(m_ctx, LANE - 2, 1))
    for k in range(nc - 1):
        m_new = jnp.maximum(btot + m_in, m_loc)
        m_in = jnp.where(lane1 == k + 1, pltpu.roll(m_new, 1, 1),
                         jnp.where(lane1 == 2 * nc - 2 - k, pltpu.roll(m_new, LANE - 1, 1), m_in))
    is_ctx = lane1 >= 2 * nc
    m_in = jnp.where(is_ctx, 0.0, m_in)
    m_new = jnp.maximum(btot + m_in, m_loc)
    s_old = jnp.broadcast_to(jnp.exp(btot + m_in - m_new), li.shape)
    w = jnp.exp(a - m_new)
    big_m = jnp.maximum(m_in, cm)
    s_int = jnp.exp(m_in - big_m)
    e_neg = jnp.exp(-(b + big_m))
    g = 2 * nc
    for h in range(HEADS):
        off = h * hl
        rot = lambda x, to: pltpu.roll(x, (to - off) % LANE, 1)
        col_ref[0, h] = jnp.where(
            lane < g, rot(w, 0), jnp.where(
                lane < 2 * g, rot(big_m, g), jnp.where(
                    lane < 3 * g, rot(s_int, 2 * g), jnp.where(
                        lane < 4 * g, rot(e_neg, 3 * g), jnp.where(
                            lane < 4 * g + 2, rot(w, 3 * g), rot(s_old, 4 * g + 2))))))

    lir = lir_ref[0]
    lfr = _log_sigmoid(lfr_ref[0])
    width = lir.shape[1]
    rid = lax.broadcasted_iota(jnp.int32, lir.shape, 0) & (2 * nc - 1)
    pid = lax.broadcasted_iota(jnp.int32, lir.shape, 1)
    p = lfr
    s = lfr
    k = 1
    while k < width:
        p = p + jnp.where(pid >= k, pltpu.roll(p, k, 1), 0.0)
        s = s + jnp.where(pid < width - k, pltpu.roll(s, width - k, 1), 0.0)
        k *= 2
    row_ref[0] = lir - jnp.where(rid < nc, p, s)


def _gate_stats(g_x, g_c, gate_bias, bsz, seq, ctx_len):
    nc = seq // MCHUNK
    assert HEADS * HEAD_LANES == LANE and 2 * nc + 2 <= HEAD_LANES and (2 * nc) & (2 * nc - 1) == 0
    gb = gate_bias.astype(F32)
    gx = g_x[:, :N_GATE_COLS].reshape(bsz, nc, MCHUNK, 2, 2, HEADS) + gb.reshape(2, 2, HEADS)
    gc = g_c[:, :N_GATE_COLS].reshape(bsz, ctx_len, 2, 2, HEADS) + gb.reshape(2, 2, HEADS)
    col_x = gx.transpose(4, 0, 2, 5, 3, 1).reshape(2, bsz, MCHUNK, HEADS, 2 * nc)
    col_c = gc.transpose(3, 0, 1, 4, 2)
    col = jnp.concatenate([col_x, col_c], -1)
    col = jnp.pad(col, ((0, 0),) * 4 + ((0, HEAD_LANES - col.shape[-1]),)).reshape(2, bsz, MCHUNK, LANE)
    row = gx.transpose(4, 0, 5, 3, 1, 2).reshape(2, bsz, HEADS * 2 * nc, MCHUNK)
    blk_c = pl.BlockSpec((1, MCHUNK, LANE), lambda b: (b, 0, 0))
    blk_r = pl.BlockSpec((1, HEADS * 2 * nc, MCHUNK), lambda b: (b, 0, 0))
    rowq, colq = pl.pallas_call(
        functools.partial(_gate_kernel, nc=nc),
        grid=(bsz,),
        in_specs=[blk_c, blk_c, blk_r, blk_r],
        out_specs=[blk_r, pl.BlockSpec((1, HEADS, MCHUNK, LANE), lambda b: (b, 0, 0, 0))],
        out_shape=[jax.ShapeDtypeStruct((bsz, HEADS * 2 * nc, MCHUNK), F32),
                   jax.ShapeDtypeStruct((bsz, HEADS, MCHUNK, LANE), F32)],
        compiler_params=_cparams(("parallel",)),
        name="gate_stats",
    )(col[0], col[1], row[0], row[1])
    return rowq.reshape(bsz, HEADS, 2 * nc, MCHUNK), colq


def _mlstm_kernel(q_ref, k_ref, v_ref, o_ref, kc_ref, vc_ref, cwq_ref, cbq_ref, cwk_ref, cbk_ref,
                  row_ref, col_ref, nw_ref, band_ref, edge_ref, y_ref, q_s, k_s, kc_s, ct_s, n_s, *, nc):
    lc = MCHUNK
    halo_rows = edge_ref.shape[1]

    def conv_silu_chunk(x_ref, c, n_chunks, w, b, scale):
        x = x_ref[pl.ds(c * lc, lc), :]
        wb = w.astype(BF16)
        taps = jnp.concatenate([x * wb[0:1, :], x * wb[1:2, :], x * wb[2:3, :]], axis=0)
        y = jnp.dot(band_ref[...], taps, preferred_element_type=F32)
        if n_chunks > 1:
            rid = lax.broadcasted_iota(jnp.int32, (halo_rows, x.shape[1]), 0)
            wf = wb.astype(F32)
            halo = jnp.zeros((halo_rows, x.shape[1]), F32)
            if c > 0:
                prev = x_ref[pl.ds(c * lc - halo_rows, halo_rows), :].astype(F32)[halo_rows - 1:, :]
                halo = jnp.where(rid == 0, prev * wf[0:1, :], halo)
            if c < n_chunks - 1:
                nxt = x_ref[pl.ds((c + 1) * lc, halo_rows), :].astype(F32)[0:1, :]
                halo = jnp.where(rid == 1, nxt * wf[2:3, :], halo)
            y = y + jnp.dot(edge_ref[...], halo.astype(BF16), preferred_element_type=F32)
        y = _silu(y + b)
        if scale != 1.0:
            y = y * scale
        return y.astype(BF16)

    k_scale = HD ** -0.5
    for c in range(nc):
        sl = pl.ds(c * lc, lc)
        q_s[sl, :] = conv_silu_chunk(q_ref, c, nc, cwq_ref[...], cbq_ref[...], 1.0)
        k_s[sl, :] = conv_silu_chunk(k_ref, c, nc, cwk_ref[...], cbk_ref[...], k_scale)
    kc_s[...] = conv_silu_chunk(kc_ref, 0, 1, cwk_ref[...], cbk_ref[...], k_scale)

    def col(j):
        return col_ref[0, 0, :, j:j + 1]

    stats_t = col_ref[0, 0].T

    def wrow(j):
        return stats_t[j:j + 1, :].astype(BF16)

    def local_state(kk, vv, w_row):
        ktw = kk.T * w_row
        ct = jnp.dot(ktw, vv, preferred_element_type=F32)
        nn = jnp.dot(jnp.broadcast_to(w_row, (SUBLANES, lc)), kk, preferred_element_type=F32)[0:1, :]
        return ct, nn

    for d in range(2):
        ct, nn = local_state(kc_s[...], vc_ref[...], wrow(8 * nc + d))
        order = list(range(nc)) if d == 0 else list(range(nc - 1, -1, -1))
        for pos, c in enumerate(order):
            idx = d * nc + c
            ct_s[idx] = ct.astype(BF16)
            n_s[idx] = nn
            if pos == nc - 1:
                break
            sl = pl.ds(c * lc, lc)
            ctl, nl = local_state(k_s[sl, :], v_ref[sl, :], wrow(idx))
            s_old = col_ref[0, 0, 0:1, 8 * nc + 2 + idx:8 * nc + 3 + idx]
            ct = s_old * ct + ctl
            nn = s_old * nn + nl

    tid = lax.broadcasted_iota(jnp.int32, (lc, lc), 0)
    sid = lax.broadcasted_iota(jnp.int32, (lc, lc), 1)
    masks = (sid <= tid, sid >= tid)
    for c in range(nc):
        sl = pl.ds(c * lc, lc)
        q = q_s[sl, :]
        kk = k_s[sl, :]
        v = v_ref[sl, :]
        qf = q.astype(F32)
        s = lax.dot_general(q, kk, (((1,), (1,)), ((), ())), preferred_element_type=F32)
        h = None
        for d in range(2):
            idx = d * nc + c
            r = row_ref[0, 0, idx:idx + 1, :]
            big_m = col(2 * nc + idx)
            s_int = col(4 * nc + idx)
            e_neg = col(6 * nc + idx)
            p = jnp.where(masks[d], jnp.exp(r - big_m), 0.0) * s
            den = (jnp.sum(p, axis=-1, keepdims=True)
                   + s_int * jnp.sum(qf * n_s[idx], axis=-1, keepdims=True))
            num = (jnp.dot(p.astype(BF16), v, preferred_element_type=F32)
                   + s_int * jnp.dot(q, ct_s[idx], preferred_element_type=F32))
            hd = num * (1.0 / jnp.maximum(jnp.abs(den), e_neg))
            h = hd if h is None else h + hd
        mu = jnp.mean(h, axis=-1, keepdims=True)
        hc = h - mu
        var = jnp.mean(hc * hc, axis=-1, keepdims=True)
        hn = hc * lax.rsqrt(var + LN_EPS) * nw_ref[...]
        y_ref[sl, :] = (hn * _sigmoid(o_ref[sl, :].astype(F32))).astype(BF16)


def _mlstm(qkvo, kv_ctx, conv_w, conv_b, rowq, colq, norm_w, bsz, seq, ctx_len):
    nc = seq // MCHUNK
    hq = D_MLSTM // HD
    kern = functools.partial(_mlstm_kernel, nc=nc)
    ii = jnp.arange(MCHUNK)[:, None]
    jj = jnp.arange(MCHUNK)[None, :]
    band = jnp.concatenate([(jj == ii + t - 1) for t in range(3)], axis=1).astype(BF16)
    hh = jnp.arange(2 * SUBLANES)[None, :]
    edge = (((ii == 0) & (hh == 0)) | ((ii == MCHUNK - 1) & (hh == 1))).astype(BF16)
    seq_blk = lambda off: pl.BlockSpec((seq, HD), lambda b, h: (b, off + h))
    ctx_blk = lambda off: pl.BlockSpec((ctx_len, HD), lambda b, h: (b, off + h))
    return pl.pallas_call(
        kern,
        grid=(bsz, HEADS),
        in_specs=[seq_blk(0), seq_blk(hq), seq_blk(2 * hq), seq_blk(3 * hq),
                  ctx_blk(0), ctx_blk(hq),
                  pl.BlockSpec((3, HD), lambda b, h: (0, h)),
                  pl.BlockSpec((1, HD), lambda b, h: (0, h)),
                  pl.BlockSpec((3, HD), lambda b, h: (0, hq + h)),
                  pl.BlockSpec((1, HD), lambda b, h: (0, hq + h)),
                  pl.BlockSpec((1, 1, 2 * nc, MCHUNK), lambda b, h: (b, h, 0, 0)),
                  pl.BlockSpec((1, 1, MCHUNK, LANE), lambda b, h: (b, h, 0, 0)),
                  pl.BlockSpec((1, HD), lambda b, h: (0, h)),
                  _resident(band.shape), _resident(edge.shape)],
        out_specs=pl.BlockSpec((seq, HD), lambda b, h: (b, h)),
        out_shape=jax.ShapeDtypeStruct((bsz * seq, D_MLSTM), BF16),
        scratch_shapes=[pltpu.VMEM((seq, HD), BF16), pltpu.VMEM((seq, HD), BF16),
                        pltpu.VMEM((ctx_len, HD), BF16),
                        pltpu.VMEM((2 * nc, HD, HD), BF16), pltpu.VMEM((2 * nc, 1, HD), F32)],
        compiler_params=_cparams(("parallel", "parallel")),
        name="mlstm",
    )(qkvo, qkvo, qkvo, qkvo, kv_ctx, kv_ctx, conv_w, conv_b, conv_w, conv_b, rowq, colq, norm_w,
      band, edge)


def _out_kernel(ym_ref, yc_ref, x_ref, pe_ref, m_ref, wout_ref,
                l1w_ref, l1b_ref, wr_ref, br_ref, x1_ref, h2_ref, lg_ref):
    m = m_ref[0]
    y = (jnp.dot(ym_ref[...], wout_ref[:D_MLSTM, :], preferred_element_type=F32)
         + jnp.dot(yc_ref[...], wout_ref[D_MLSTM:, :], preferred_element_type=F32))
    z = DEEPNORM_ALPHA * (x_ref[...] + pe_ref[...]) + m[2:3, :] * y
    x1 = _layer_norm_rows(z, l1w_ref[...], l1b_ref[...])
    x1_ref[...] = x1
    h2 = x1 * (1.0 + m[4:5, :]) + m[3:4, :]
    half = h2.shape[1] // 2
    packed = _pack_bf16_pairs(h2[:, :half], h2[:, half:])
    for c in range(ROW_LINES):
        h2_ref[pl.ds(c, h2.shape[0], stride=ROW_LINES), :] = packed[:, c * LANE:(c + 1) * LANE]
    lg = jnp.dot(h2.astype(BF16), wr_ref[...], preferred_element_type=F32) + br_ref[...]

    lgt = lg.T
    bm = lg.shape[0]
    epg = EXPERTS_PER_GROUP
    rid = lax.broadcasted_iota(jnp.int32, (SUBLANES, bm), 0).astype(F32)
    neg = -jnp.inf
    far = float(SUBLANES)
    gl = lgt[N_EXPERTS:N_EXPERTS + SUBLANES, :]
    is_grp = rid < N_GROUPS
    m1 = jnp.max(jnp.where(is_grp, gl, neg), axis=0, keepdims=True)
    grp = jnp.min(jnp.where(is_grp & (gl == m1), rid, far), axis=0, keepdims=True)
    p_grp = 1.0 / jnp.sum(jnp.where(is_grp, jnp.exp(gl - m1), 0.0), axis=0, keepdims=True)
    l2 = lgt[0:epg, :]
    for g in range(1, N_GROUPS):
        l2 = jnp.where(grp == g, lgt[g * epg:(g + 1) * epg, :], l2)
    v0 = jnp.max(l2, axis=0, keepdims=True)
    i0 = jnp.min(jnp.where(l2 == v0, rid, far), axis=0, keepdims=True)
    l2m = jnp.where(rid == i0, neg, l2)
    v1 = jnp.max(l2m, axis=0, keepdims=True)
    i1 = jnp.min(jnp.where(l2m == v1, rid, far), axis=0, keepdims=True)
    s1 = jnp.exp(v1 - v0)
    g0 = p_grp / (1.0 + s1)
    res = jnp.where(rid == 0, g0, jnp.where(rid == 1, g0 * s1, jnp.where(
        rid == 2, grp * epg + i0, jnp.where(rid == 3, grp * epg + i1, 0.0))))
    lg_ref[...] = jnp.concatenate([res, jnp.zeros((LANE - SUBLANES, bm), F32)], axis=0).T


def _mixer_out(ym, yc, x2d, pe, mods3, wout, l1w, l1b, wr, br, seq, bm):
    rows, dm = x2d.shape
    pe_blocks = seq // bm
    bsz = rows // seq
    blk = lambda i: (i % bsz) * pe_blocks + i // bsz
    row_blk = lambda n: pl.BlockSpec((bm, n), lambda i: (blk(i), 0))
    return pl.pallas_call(
        _out_kernel,
        grid=(rows // bm,),
        in_specs=[row_blk(D_MLSTM), row_blk(D_CMLP), row_blk(dm),
                  pl.BlockSpec((bm, dm), lambda i: (i // bsz, 0)),
                  pl.BlockSpec((1, N_MOD, dm), lambda i: (i % bsz, 0, 0)),
                  _resident(wout.shape), _resident(l1w.shape), _resident(l1b.shape),
                  _resident(wr.shape), _resident(br.shape)],
        out_specs=[row_blk(dm), pl.BlockSpec((bm * ROW_LINES, LANE), lambda i: (blk(i), 0)), row_blk(LANE)],
        out_shape=[jax.ShapeDtypeStruct((rows, dm), F32),
                   jax.ShapeDtypeStruct((rows * ROW_LINES, LANE), jnp.uint32),
                   jax.ShapeDtypeStruct((rows, LANE), F32)],
        compiler_params=_cparams(("parallel",)),
        name="mixer_out",
    )(ym, yc, x2d, pe, mods3, wout, l1w, l1b, wr, br)


def _moe_kernel(comp_ref, blk_ref, slot_ref, cast_ref, cslot_ref, cexp_ref, ctile_ref,
                gtab_ref, stab_ref, h2_hbm, wgf_ref, wuf_ref, wdf_ref, ys_hbm,
                wg_s, wu_s, wd_s, xbuf, xb_s, ybuf, gsem, ssem, pend, *, nt):
    del blk_ref, cexp_ref
    s = pl.program_id(0)
    n_steps = pl.num_programs(0)
    groups = MOE_ROWS // SUBLANES
    rl = ROW_LINES
    half = rl * LANE
    dm = 2 * half
    n_asg = ys_hbm.shape[0] // rl - 2 * MOE_ROWS
    group_lines = SUBLANES * rl

    def gather_wait(p, n):
        nl = n * group_lines
        pltpu.make_async_copy(h2_hbm.at[pl.ds(0, nl)], xbuf.at[p, pl.ds(0, nl)], gsem.at[p]).wait()

    def scatter_wait(p, n):
        nl = n * group_lines
        pltpu.make_async_copy(ybuf.at[p, pl.ds(0, nl)], ys_hbm.at[pl.ds(0, nl)], ssem.at[p]).wait()

    @pl.when(s == 0)
    def _():
        pend[0] = 0
        pend[1] = 0
        xbuf[...] = jnp.zeros_like(xbuf)

    nxt = jnp.minimum(s + 1, n_steps - 1)

    @pl.when(jnp.logical_and(s + 1 < n_steps, comp_ref[nxt] > 0))
    def _():
        p = (s + 1) % 2

        for par in range(2):
            for g0 in range(0, groups, GROUP_STEP):
                @pl.when(jnp.logical_and(p == par, g0 < comp_ref[nxt]))
                def _():
                    for r in range(g0 * SUBLANES, (g0 + GROUP_STEP) * SUBLANES):
                        src = pl.multiple_of(gtab_ref[0, 0, r] * rl, rl)
                        pltpu.make_async_copy(h2_hbm.at[pl.ds(src, rl)], xbuf.at[par, pl.ds(r * rl, rl)],
                                              gsem.at[par]).start()

    @pl.when(cast_ref[s] == 1)
    def _():
        cs = cslot_ref[s]
        t = ctile_ref[s]
        for k in range(nt):
            @pl.when(t == k)
            def _():
                wg_s[cs, :, k * MOE_FT:(k + 1) * MOE_FT] = wgf_ref[0].astype(BF16)
                wu_s[cs, :, k * MOE_FT:(k + 1) * MOE_FT] = wuf_ref[0].astype(BF16)
        wd_s[cs, t] = wdf_ref[0].astype(BF16)

    @pl.when(comp_ref[s] > 0)
    def _():
        p = s % 2
        sl = slot_ref[s]
        ng = comp_ref[s]
        gather_wait(p, ng)

        @pl.when(pend[p] > 0)
        def _():
            scatter_wait(p, pend[p])

        def run_block(rows):
            for c in range(rl):
                x_lo, x_hi = _unpack_bf16_pairs(xbuf[p, pl.ds(c, rows, stride=rl), :])
                xb_s[:rows, c * LANE:(c + 1) * LANE] = x_lo.astype(BF16)
                xb_s[:rows, half + c * LANE:half + (c + 1) * LANE] = x_hi.astype(BF16)
            x = xb_s[:rows, :]
            g = jnp.dot(x, wg_s[sl], preferred_element_type=F32)
            u = jnp.dot(x, wu_s[sl], preferred_element_type=F32)
            h = (_silu(g) * u).astype(BF16)
            y = jnp.dot(h, wd_s[sl].reshape(D_EXPERT, dm), preferred_element_type=F32)
            y_packed = _pack_bf16_pairs(y[:, :half], y[:, half:])
            for c in range(rl):
                ybuf[p, pl.ds(c, rows, stride=rl), :] = y_packed[:, c * LANE:(c + 1) * LANE]

        small_groups = MOE_SMALL_ROWS // SUBLANES

        @pl.when(ng > small_groups)
        def _():
            run_block(MOE_ROWS)

        @pl.when(ng <= small_groups)
        def _():
            run_block(MOE_SMALL_ROWS)

        for par in range(2):
            for g0 in range(0, groups, GROUP_STEP):
                @pl.when(jnp.logical_and(p == par, g0 < ng))
                def _():
                    for r in range(g0 * SUBLANES, (g0 + GROUP_STEP) * SUBLANES):
                        d = stab_ref[0, 0, r]
                        dst = pl.multiple_of(jnp.where(d < 0, n_asg + par * MOE_ROWS + r, d) * rl, rl)
                        pltpu.make_async_copy(ybuf.at[par, pl.ds(r * rl, rl)], ys_hbm.at[pl.ds(dst, rl)],
                                              ssem.at[par]).start()

        pend[p] = ng

    @pl.when(s == n_steps - 1)
    def _():
        for p in range(2):
            @pl.when(pend[p] > 0)
            def _():
                scatter_wait(p, pend[p])
                pend[p] = 0

        xbuf[...] = jnp.zeros_like(xbuf)
        fills = [pltpu.make_async_copy(xbuf.at[p], ys_hbm.at[pl.ds((n_asg + p * MOE_ROWS) * rl, MOE_ROWS * rl)],
                                       gsem.at[p]) for p in range(2)]
        for cp in fills:
            cp.start()
        for cp in fills:
            cp.wait()


def _experts(h2, gtab, stab, sched, wg, wu, wd):
    n_tok, dm = h2.shape[0] // ROW_LINES, D_MODEL
    nt = D_EXPERT // MOE_FT
    n_asg = 2 * n_tok
    n_steps = sched[0].shape[0]

    smem_rows = lambda imap: pl.BlockSpec((1, 1, MOE_ROWS), imap, memory_space=pltpu.SMEM)
    grid_spec = pltpu.PrefetchScalarGridSpec(
        num_scalar_prefetch=7,
        grid=(n_steps,),
        in_specs=[smem_rows(lambda s, comp, blk, *_: (blk[jnp.minimum(s + 1, n_steps - 1)], 0, 0)),
                  smem_rows(lambda s, comp, blk, *_: (blk[s], 0, 0)),
                  pl.BlockSpec(memory_space=pl.ANY),
                  pl.BlockSpec((1, dm, MOE_FT), lambda s, c, b, sl, ca, cs, ce, ct: (ce[s], 0, ct[s])),
                  pl.BlockSpec((1, dm, MOE_FT), lambda s, c, b, sl, ca, cs, ce, ct: (ce[s], 0, ct[s])),
                  pl.BlockSpec((1, MOE_FT, dm), lambda s, c, b, sl, ca, cs, ce, ct: (ce[s], ct[s], 0))],
        out_specs=pl.BlockSpec(memory_space=pl.ANY),
        scratch_shapes=[pltpu.VMEM((2, dm, D_EXPERT), BF16), pltpu.VMEM((2, dm, D_EXPERT), BF16),
                        pltpu.VMEM((2, nt, MOE_FT, dm), BF16),
                        pltpu.VMEM((2, MOE_ROWS * ROW_LINES, LANE), jnp.uint32),
                        pltpu.VMEM((MOE_ROWS, dm), BF16),
                        pltpu.VMEM((2, MOE_ROWS * ROW_LINES, LANE), jnp.uint32),
                        pltpu.SemaphoreType.DMA((2,)), pltpu.SemaphoreType.DMA((2,)),
                        pltpu.SMEM((2,), jnp.int32)],
    )
    return pl.pallas_call(
        functools.partial(_moe_kernel, nt=nt),
        grid_spec=grid_spec,
        out_shape=jax.ShapeDtypeStruct(((n_asg + 2 * MOE_ROWS) * ROW_LINES, LANE), jnp.uint32),
        compiler_params=_cparams(("arbitrary",)),
        name="experts",
    )(*sched, gtab, stab, h2, wg, wu, wd)


def _final_kernel(x1_ref, y0_ref, y1_ref, g_ref, m_ref, w_ref, b_ref, o_ref, y_s):
    m = m_ref[0]
    bm, dm = x1_ref.shape
    half = dm // 2
    g0 = g_ref[:, 0:1]
    g1 = g_ref[:, 1:2]
    for c in range(ROW_LINES):
        rows_c = pl.ds(c, bm, stride=ROW_LINES)
        lo0, hi0 = _unpack_bf16_pairs(y0_ref[rows_c, :])
        lo1, hi1 = _unpack_bf16_pairs(y1_ref[rows_c, :])
        y_s[:, c * LANE:(c + 1) * LANE] = g0 * lo0 + g1 * lo1
        y_s[:, half + c * LANE:half + (c + 1) * LANE] = g0 * hi0 + g1 * hi1
    z = DEEPNORM_ALPHA * x1_ref[...] + m[5:6, :] * y_s[...]
    o_ref[...] = _layer_norm_rows(z, w_ref[...], b_ref[...])


def _final(x1, ys, gates, mods3, w, b, seq, bm):
    rows, dm = x1.shape
    blocks_per_batch = seq // bm
    slot_blocks = rows // bm
    row_blk = lambda n: pl.BlockSpec((bm, n), lambda i: (i, 0))
    return pl.pallas_call(
        _final_kernel,
        grid=(rows // bm,),
        in_specs=[row_blk(dm), pl.BlockSpec((bm * ROW_LINES, LANE), lambda i: (i, 0)),
                  pl.BlockSpec((bm * ROW_LINES, LANE), lambda i: (i + slot_blocks, 0)),
                  row_blk(LANE),
                  pl.BlockSpec((1, N_MOD, dm), lambda i: (i // blocks_per_batch, 0, 0)),
                  _resident(w.shape), _resident(b.shape)],
        out_specs=row_blk(dm),
        out_shape=jax.ShapeDtypeStruct((rows, dm), F32),
        scratch_shapes=[pltpu.VMEM((bm, dm), F32)],
        compiler_params=_cparams(("parallel",)),
        name="final_ln",
    )(x1, ys, ys, gates, mods3, w, b)


def _route(route, n_tok):
    e_flat = route[:, 2:4].astype(jnp.int32).reshape(-1)
    n_asg = e_flat.shape[0]
    earange = jnp.arange(N_EXPERTS, dtype=jnp.int32)
    counts = jnp.sum((e_flat[:, None] == earange[None, :]).astype(jnp.int32), 0)
    nblk_e = (counts + MOE_ROWS - 1) // MOE_ROWS
    pad_end = jnp.cumsum(nblk_e * MOE_ROWS)
    pad_start = pad_end - nblk_e * MOE_ROWS
    n_blk = n_asg // MOE_ROWS + N_EXPERTS

    order = jnp.argsort(e_flat, stable=True).astype(jnp.int32)
    starts = jnp.cumsum(counts) - counts
    ridx = jnp.arange(n_blk * MOE_ROWS, dtype=jnp.int32)
    e_row = jnp.minimum(jnp.sum((pad_end[None, :] <= ridx[:, None]).astype(jnp.int32), -1), N_EXPERTS - 1)
    rsel = e_row[:, None] == earange[None, :]
    k_row = ridx - jnp.sum(jnp.where(rsel, pad_start[None, :], 0), -1)
    valid = k_row < jnp.sum(jnp.where(rsel, counts[None, :], 0), -1)
    src = jnp.clip(jnp.sum(jnp.where(rsel, starts[None, :], 0), -1) + k_row, 0, n_asg - 1)
    row_asg = jnp.where(valid, order[src], -1)
    gtab = (jnp.maximum(row_asg, 0) >> 1).reshape(n_blk, 1, MOE_ROWS)
    stab = jnp.where(row_asg < 0, -1, (row_asg & 1) * n_tok + (row_asg >> 1)).reshape(n_blk, 1, MOE_ROWS)

    nt = D_EXPERT // MOE_FT
    n_steps = n_blk + (nt - 1) * N_EXPERTS + nt
    has = nblk_e > 0
    n_visits = jnp.sum(has.astype(jnp.int32))
    e_of_visit = jnp.sort(jnp.where(has, earange, N_EXPERTS))
    vsel = e_of_visit[:, None] == earange[None, :]
    nb_v = jnp.sum(jnp.where(vsel, nblk_e[None, :], 0), -1)
    steps_v = jnp.where(nb_v > 0, jnp.maximum(nb_v, nt), 0)
    end_v = nt + jnp.cumsum(steps_v)
    start_v = end_v - steps_v
    first_blk_v = jnp.cumsum(nb_v) - nb_v
    sidx = jnp.arange(n_steps, dtype=jnp.int32)
    v = jnp.sum((end_v[None, :] <= sidx[:, None]).astype(jnp.int32), -1)
    pick = lambda arr, idx: jnp.sum(jnp.where(idx[:, None] == earange[None, :], arr[None, :], 0), -1)
    in_visit = (sidx >= nt) & (v < n_visits)
    k = sidx - pick(start_v, v)
    comp = in_visit & (k < pick(nb_v, v))
    cnt_v = jnp.sum(jnp.where(vsel, counts[None, :], 0), -1)
    rows_s = jnp.clip(pick(cnt_v, v) - k * MOE_ROWS, 0, MOE_ROWS)
    chunk_rows = GROUP_STEP * SUBLANES
    groups_s = jnp.where(comp, (rows_s + chunk_rows - 1) // chunk_rows * GROUP_STEP, 0)
    blk_s = lax.cummax(jnp.where(comp, pick(first_blk_v, v) + k, 0), axis=0)
    prologue = sidx < nt
    cast = prologue | (in_visit & (k < nt) & (v + 1 < n_visits))
    cexp = jnp.where(prologue, e_of_visit[0], pick(e_of_visit, v + 1))
    ctile = jnp.where(prologue, sidx, k)
    code = lax.cummax(jnp.where(cast, cexp * nt + ctile, 0), axis=0)
    i32 = lambda a: a.astype(jnp.int32)
    sched = (i32(groups_s), i32(blk_s), i32(v % 2), i32(cast), i32(jnp.where(prologue, 0, (v + 1) % 2)),
             i32(jnp.minimum(code // nt, N_EXPERTS - 1)), i32(code % nt))
    return sched, gtab, stab


def _grid_pos_embed(rows):
    quarter = D_MODEL // 4
    omega = 1.0 / (10000.0 ** (jnp.arange(quarter, dtype=F32) / quarter))
    ar = jnp.arange(rows, dtype=F32)[:, None] * omega
    ac = jnp.arange(GRID_W, dtype=F32)[:, None] * omega
    shape = (rows, GRID_W, quarter)
    parts = [jnp.broadcast_to(jnp.sin(ar)[:, None, :], shape), jnp.broadcast_to(jnp.cos(ar)[:, None, :], shape),
             jnp.broadcast_to(jnp.sin(ac)[None, :, :], shape), jnp.broadcast_to(jnp.cos(ac)[None, :, :], shape)]
    return jnp.concatenate(parts, -1).reshape(rows * GRID_W, D_MODEL)


def kernel(x, c, ctx, c_ctx, w_mod, b_mod, w_in, conv_w, conv_b, gate_bias, mlstm_norm_w, cmlp_norm_w,
           w_s, b_s, w_out, ln1_w, ln1_b, router1_w, router1_b, router2_w, router2_b, w_gate, w_up,
           w_down, ln2_w, ln2_b):
    bsz, seq, dm = x.shape
    ctx_len = ctx.shape[1]
    n_tok = bsz * seq
    assert w_mod.shape[0] == 1 and dm == D_MODEL and seq % MCHUNK == 0 and ctx_len == MCHUNK
    pe = _grid_pos_embed(seq // GRID_W).astype(x.dtype)
    x2d = x.reshape(n_tok, dm)
    ctx2d = ctx.reshape(bsz * ctx_len, dm)

    mod_rows = 16
    cc = jnp.concatenate([c, c_ctx[None, :], jnp.zeros((mod_rows - bsz - 1, dm), c.dtype)], 0)
    mods3 = _modulation(cc, w_mod[0], b_mod[0]).reshape(mod_rows, N_MOD, dm)

    dq = D_MLSTM
    wi = w_in[0]
    w_qkvo = wi[:, :4 * dq].astype(BF16)
    w_g = jnp.pad(wi[:, 4 * dq:4 * dq + N_GATE_COLS], ((0, 0), (0, LANE - N_GATE_COLS))).astype(BF16)
    w_uv = wi[:, 4 * dq + N_GATE_COLS:].astype(BF16)
    bm_proj = 512
    cmlp = (cmlp_norm_w[0].reshape(1, -1), w_s[0].astype(BF16), b_s[0].T)
    qkvo, g_x, yc = _projection(x2d, pe, mods3, lambda i: i % bsz,
                                [w_qkvo, w_g, w_uv], [BF16, F32, BF16], [False, False, True], bm_proj,
                                cmlp=cmlp)
    w_kv = wi[:, dq:3 * dq].astype(BF16)
    kv_c, g_c = _projection(ctx2d, None, mods3, lambda i: bsz, [w_kv, w_g], [BF16, F32],
                            [False, False], bm_proj)

    rowq, colq = _gate_stats(g_x, g_c, gate_bias[0], bsz, seq, ctx_len)
    ym = _mlstm(qkvo, kv_c, conv_w[0], conv_b[0].reshape(1, -1), rowq, colq,
                mlstm_norm_w[0].reshape(1, -1), bsz, seq, ctx_len)

    wr = jnp.pad(jnp.concatenate([router2_w[0], router1_w[0]], 1),
                 ((0, 0), (0, LANE - N_GROUPS - N_EXPERTS))).astype(BF16)
    br = jnp.pad(jnp.concatenate([router2_b[0], router1_b[0]], 0),
                 (0, LANE - N_GROUPS - N_EXPERTS)).reshape(1, LANE)
    x1, h2, route = _mixer_out(ym, yc, x2d, pe, mods3, w_out[0].astype(BF16),
                               ln1_w[0].reshape(1, -1), ln1_b[0].reshape(1, -1), wr, br, seq, 512)

    sched, gtab, stab = _route(route, n_tok)
    ys = _experts(h2, gtab, stab, sched, w_gate[0], w_up[0], w_down[0])
    out = _final(x1, ys, route, mods3, ln2_w[0].reshape(1, -1),
                 ln2_b[0].reshape(1, -1), seq, 512)
    return out.reshape(bsz, seq, dm)
```
